```python
import math
import jax
import jax.numpy as jnp
from jax import lax
import numpy as np

D_MODEL = 1024
BATCH = 8
SEQ = 2048
DEPTH = 2
DEC_BATCH = 128
DEC_SEQ = 4
PAST_LEN = 16384
PAGE_SIZE = 128

SSD_HEAD_DIM = 64
SSD_D_INNER = D_MODEL
SSD_HEADS = SSD_D_INNER // SSD_HEAD_DIM
SSD_GROUPS = 2
SSD_HEADS_PER_GROUP = SSD_HEADS // SSD_GROUPS
SSD_STATE = 128
SSD_CONV = 4
SSD_CHUNK = 128
SSD_CONV_DIM = SSD_D_INNER + 2 * SSD_GROUPS * SSD_STATE
S5_WIDTH = D_MODEL // 2
S5_GROUP_CH = 16
S5_GROUPS = S5_WIDTH // S5_GROUP_CH
S5_STATE = 64
N_BRANCHES = 2
IN_Z = SSD_D_INNER
IN_XBC = SSD_CONV_DIM
IN_DT = SSD_HEADS
IN_U = S5_WIDTH
IN_GATE = N_BRANCHES * D_MODEL
IN_TOTAL = IN_Z + IN_XBC + IN_DT + IN_U + IN_GATE
IN_SPLITS = (IN_Z, IN_Z + IN_XBC, IN_Z + IN_XBC + IN_DT, IN_Z + IN_XBC + IN_DT + IN_U)
N_EXPERTS = 16
N_EXPERT_GROUPS = 4
EXPERTS_PER_GROUP = N_EXPERTS // N_EXPERT_GROUPS
TOP_K = 2
D_EXPERT = 512
EPS = 1e-6

kernel_name = 'hybrid_ssd_s5_moe_step'


def rmsnorm(x, w):
    xf = x.astype(jnp.float32)
    y = xf * lax.rsqrt(jnp.mean(xf * xf, axis=-1, keepdims=True) + EPS)
    return (y * w.astype(jnp.float32)).astype(x.dtype)


def causal_dwconv(xbc, conv_buf, w, b):
    L = xbc.shape[1]
    xp = jnp.concatenate([conv_buf.astype(xbc.dtype), xbc], axis=1)
    y = b + xp[:, 0:L] * w[0]
    for k in range(1, SSD_CONV):
        y = y + xp[:, k:k + L] * w[k]
    return y, xp[:, L:]


def segsum_exp(a_cs):
    q = a_cs.shape[-1]
    diff = a_cs[..., :, None] - a_cs[..., None, :]
    mask = jnp.tril(jnp.ones((q, q), dtype=bool))
    return jnp.where(mask, jnp.exp(jnp.where(mask, diff, 0.0)), 0.0)


def ssd_scan(x, dt, a, bm, cm, h0):
    b, L = x.shape[:2]
    Q = SSD_CHUNK if L % SSD_CHUNK == 0 else L
    nc = L // Q
    G, R, P, N = SSD_GROUPS, SSD_HEADS_PER_GROUP, SSD_HEAD_DIM, SSD_STATE
    X = (x * dt[..., None]).reshape(b, nc, Q, G, R, P)
    a_cs = jnp.cumsum(jnp.moveaxis((dt * a).reshape(b, nc, Q, G, R), 2, -1), axis=-1)
    Bc = bm.reshape(b, nc, Q, G, N)
    Cc = cm.reshape(b, nc, Q, G, N)
    CB = jnp.einsum('bcqgn,bcsgn->bcgqs', Cc, Bc)
    y_diag = jnp.einsum('bcgqs,bcgrqs,bcsgrp->bcqgrp', CB, segsum_exp(a_cs), X)
    decay_to_end = jnp.exp(a_cs[..., -1:] - a_cs)
    chunk_states = jnp.einsum('bcsgn,bcgrs,bcsgrp->bcgrpn', Bc, decay_to_end, X)
    chunk_decay = jnp.exp(a_cs[..., -1])

    def step(h, inp):
        s, d = inp
        return h * d[..., None, None] + s, h

    h_final, h_prev = lax.scan(step, h0.reshape(b, G, R, P, N),
                               (jnp.moveaxis(chunk_states, 1, 0), jnp.moveaxis(chunk_decay, 1, 0)))
    h_prev = jnp.moveaxis(h_prev, 0, 1)
    y_off = jnp.einsum('bcqgn,bcgrpn,bcgrq->bcqgrp', Cc, h_prev, jnp.exp(a_cs))
    y = (y_diag + y_off).reshape(b, L, SSD_HEADS, P)
    return y, h_final.reshape(b, SSD_HEADS, P, N)


def ssd_branch(z, xbc, dt_raw, conv_buf, ssm_h0, conv_w, conv_b, dt_bias, a_log, d_skip, norm_w, w_out_a):
    f32 = jnp.float32
    b, L = z.shape[:2]
    xbc, new_conv = causal_dwconv(xbc, conv_buf, conv_w, conv_b)
    xbc = jax.nn.silu(xbc.astype(f32))
    xs, bm, cm = jnp.split(xbc, [SSD_D_INNER, SSD_D_INNER + SSD_GROUPS * SSD_STATE], axis=-1)
    xs = xs.reshape(b, L, SSD_HEADS, SSD_HEAD_DIM)
    dt = jax.nn.softplus(dt_raw.astype(f32) + dt_bias.astype(f32))
    a = -jnp.exp(a_log.astype(f32))
    y, new_ssm = ssd_scan(xs, dt, a, bm.reshape(b, L, SSD_GROUPS, SSD_STATE),
                          cm.reshape(b, L, SSD_GROUPS, SSD_STATE), ssm_h0.astype(f32))
    y = y + d_skip.astype(f32)[:, None] * xs
    y = y.reshape(b, L, SSD_D_INNER) * jax.nn.silu(z.astype(f32))
    yg = y.reshape(b, L, SSD_GROUPS, SSD_D_INNER // SSD_GROUPS)
    yg = yg * lax.rsqrt(jnp.mean(yg * yg, axis=-1, keepdims=True) + EPS)
    y = yg.reshape(b, L, SSD_D_INNER) * norm_w.astype(f32)
    return y.astype(z.dtype) @ w_out_a, new_conv, new_ssm


def s5_branch(u, h0_re, h0_im, a_re, a_im, log_dt, b_re, b_im, c_re, c_im, d_skip, w_glu):
    f32 = jnp.float32
    b, L = u.shape[:2]
    a_re, a_im = a_re.astype(f32), a_im.astype(f32)
    dt = jnp.exp(log_dt.astype(f32))[:, None]
    mag = jnp.exp(dt * a_re)
    ab_re, ab_im = mag * jnp.cos(dt * a_im), mag * jnp.sin(dt * a_im)
    den = a_re * a_re + a_im * a_im
    f_re = ((ab_re - 1.0) * a_re + ab_im * a_im) / den
    f_im = (ab_im * a_re - (ab_re - 1.0) * a_im) / den
    b_re, b_im = b_re.astype(f32), b_im.astype(f32)
    bb_re = f_re[..., None] * b_re - f_im[..., None] * b_im
    bb_im = f_re[..., None] * b_im + f_im[..., None] * b_re
    uf = u.astype(f32)
    ug = uf.reshape(b, L, S5_GROUPS, S5_GROUP_CH)
    x_re = jnp.einsum('gsc,blgc->blgs', bb_re, ug)
    x_im = jnp.einsum('gsc,blgc->blgs', bb_im, ug)
    h0_re, h0_im = h0_re.astype(f32), h0_im.astype(f32)
    x_re = x_re.at[:, 0].add(ab_re * h0_re - ab_im * h0_im)
    x_im = x_im.at[:, 0].add(ab_re * h0_im + ab_im * h0_re)

    def combine(e1, e2):
        a1r, a1i, b1r, b1i = e1
        a2r, a2i, b2r, b2i = e2
        return (a2r * a1r - a2i * a1i, a2r * a1i + a2i * a1r,
                a2r * b1r - a2i * b1i + b2r, a2r * b1i + a2i * b1r + b2i)

    _, _, h_re, h_im = lax.associative_scan(
        combine, (jnp.broadcast_to(ab_re, x_re.shape), jnp.broadcast_to(ab_im, x_im.shape), x_re, x_im), axis=1)
    y = (jnp.einsum('gcs,blgs->blgc', c_re.astype(f32), h_re)
         - jnp.einsum('gcs,blgs->blgc', c_im.astype(f32), h_im)).reshape(b, L, S5_WIDTH)
    y = y + d_skip.astype(f32) * uf
    pre = (jax.nn.gelu(y).astype(u.dtype) @ w_glu).astype(f32)
    out = pre[..., :D_MODEL] * jax.nn.sigmoid(pre[..., D_MODEL:])
    return out.astype(u.dtype), h_re[:, -1], h_im[:, -1]


def moe_ffn(h, w_router, b_router, w1, w3, w2):
    f32 = jnp.float32
    b, L, d = h.shape
    t = h.reshape(b * L, d)
    scores = jax.nn.softmax(t.astype(f32) @ w_router.astype(f32), axis=-1)
    sel = scores + b_router.astype(f32)
    grp_score = lax.top_k(sel.reshape(-1, N_EXPERT_GROUPS, EXPERTS_PER_GROUP), TOP_K)[0].sum(-1)
    best_grp = jnp.argmax(grp_score, axis=-1)
    in_grp = (jnp.arange(N_EXPERTS) // EXPERTS_PER_GROUP)[None, :] == best_grp[:, None]
    _, idx = lax.top_k(jnp.where(in_grp, sel, -jnp.inf), TOP_K)
    wts = jnp.take_along_axis(scores, idx, axis=-1)
    wts = wts / jnp.sum(wts, axis=-1, keepdims=True)
    comb = jnp.sum(jax.nn.one_hot(idx, N_EXPERTS, dtype=f32) * wts[..., None], axis=1).astype(h.dtype)
    y = jnp.zeros_like(t)
    for e in range(N_EXPERTS):
        act = jax.nn.silu(t @ w1[e]) * (t @ w3[e])
        y = y + comb[:, e:e + 1] * (act @ w2[e])
    return y.reshape(b, L, d)


def setup_inputs(seed: int = 0) -> dict:
    key = jax.random.key(seed)
    ks = iter(jax.random.split(key, 48))
    f32 = jnp.float32

    def nrm(shape, s):
        return jax.random.normal(next(ks), shape, f32) * s

    def gain(shape):
        return 1.0 + nrm(shape, 0.02)

    x_prompt = nrm((BATCH, SEQ, D_MODEL), 1.0)
    x_sample = nrm((DEC_BATCH, DEC_SEQ, D_MODEL), 1.0)
    state_ssm = nrm((DEPTH, DEC_BATCH, SSD_HEADS, SSD_HEAD_DIM, SSD_STATE), 0.1)
    state_conv = nrm((DEPTH, DEC_BATCH, SSD_CONV - 1, SSD_CONV_DIM), 1.0)
    state_s5_re = nrm((DEPTH, DEC_BATCH, S5_GROUPS, S5_STATE), 0.5)
    state_s5_im = nrm((DEPTH, DEC_BATCH, S5_GROUPS, S5_STATE), 0.5)
    c_prompt = nrm((BATCH, D_MODEL), 1.0)
    c_sample = nrm((DEC_BATCH, D_MODEL), 1.0)

    w_in = nrm((DEPTH, D_MODEL, IN_TOTAL), D_MODEL ** -0.5)
    conv_w = nrm((DEPTH, SSD_CONV, SSD_CONV_DIM), SSD_CONV ** -0.5)
    conv_b = nrm((DEPTH, SSD_CONV_DIM), 0.01)
    dt0 = jnp.exp(jax.random.uniform(next(ks), (DEPTH, SSD_HEADS), f32, math.log(1e-3), math.log(1e-1)))
    dt_bias = dt0 + jnp.log(-jnp.expm1(-dt0))
    a_log = jnp.log(jax.random.uniform(next(ks), (DEPTH, SSD_HEADS), f32, 1.0, 16.0))
    d_ssd = gain((DEPTH, SSD_HEADS))
    ssd_norm_w = gain((DEPTH, SSD_D_INNER))
    w_out_a = nrm((DEPTH, SSD_D_INNER, D_MODEL), SSD_D_INNER ** -0.5)

    n = jnp.arange(S5_STATE, dtype=f32)
    s5_a_re = -0.5 + nrm((DEPTH, S5_GROUPS, S5_STATE), 0.01)
    s5_a_im = jnp.pi * n + nrm((DEPTH, S5_GROUPS, S5_STATE), 0.01)
    s5_log_dt = jax.random.uniform(next(ks), (DEPTH, S5_GROUPS), f32, math.log(1e-3), math.log(1e-1))
    s5_b_re = nrm((DEPTH, S5_GROUPS, S5_STATE, S5_GROUP_CH), (2 * S5_GROUP_CH) ** -0.5)
    s5_b_im = nrm((DEPTH, S5_GROUPS, S5_STATE, S5_GROUP_CH), (2 * S5_GROUP_CH) ** -0.5)
    s5_c_re = nrm((DEPTH, S5_GROUPS, S5_GROUP_CH, S5_STATE), (2 * S5_STATE) ** -0.5)
    s5_c_im = nrm((DEPTH, S5_GROUPS, S5_GROUP_CH, S5_STATE), (2 * S5_STATE) ** -0.5)
    s5_d = nrm((DEPTH, S5_WIDTH), 1.0)
    w_glu = nrm((DEPTH, S5_WIDTH, 2 * D_MODEL), S5_WIDTH ** -0.5)
    w_out = nrm((DEPTH, D_MODEL, D_MODEL), D_MODEL ** -0.5)

    norm1_w = gain((DEPTH, D_MODEL))
    norm2_w = gain((DEPTH, D_MODEL))
    w_ada = nrm((DEPTH, D_MODEL, 6 * D_MODEL), 0.5 * D_MODEL ** -0.5)
    b_ada = nrm((DEPTH, 6 * D_MODEL), 0.02)
    w_router = nrm((D_MODEL, N_EXPERTS), D_MODEL ** -0.5)
    b_router = nrm((N_EXPERTS,), 0.01)
    w1 = nrm((DEPTH, N_EXPERTS, D_MODEL, D_EXPERT), D_MODEL ** -0.5)
    w3 = nrm((DEPTH, N_EXPERTS, D_MODEL, D_EXPERT), D_MODEL ** -0.5)
    w2 = nrm((DEPTH, N_EXPERTS, D_EXPERT, D_MODEL), D_EXPERT ** -0.5)
    final_norm_w = gain((D_MODEL,))
    return {'x_prompt': x_prompt, 'x_sample': x_sample,
            'state_ssm': state_ssm, 'state_conv': state_conv,
            'state_s5_re': state_s5_re, 'state_s5_im': state_s5_im,
            'c_prompt': c_prompt, 'c_sample': c_sample,
            'w_in': w_in, 'conv_w': conv_w, 'conv_b': conv_b, 'dt_bias': dt_bias, 'a_log': a_log,
            'd_ssd': d_ssd, 'ssd_norm_w': ssd_norm_w, 'w_out_a': w_out_a,
            's5_a_re': s5_a_re, 's5_a_im': s5_a_im, 's5_log_dt': s5_log_dt,
            's5_b_re': s5_b_re, 's5_b_im': s5_b_im, 's5_c_re': s5_c_re, 's5_c_im': s5_c_im,
            's5_d': s5_d, 'w_glu': w_glu, 'w_out': w_out,
            'norm1_w': norm1_w, 'norm2_w': norm2_w, 'w_ada': w_ada, 'b_ada': b_ada,
            'w_router': w_router, 'b_router': b_router, 'w1': w1, 'w3': w3, 'w2': w2,
            'final_norm_w': final_norm_w}


def reference(x_prompt, x_sample, state_ssm, state_conv, state_s5_re, state_s5_im, c_prompt, c_sample,
              w_in, conv_w, conv_b, dt_bias, a_log, d_ssd, ssd_norm_w, w_out_a,
              s5_a_re, s5_a_im, s5_log_dt, s5_b_re, s5_b_im, s5_c_re, s5_c_im, s5_d, w_glu, w_out,
              norm1_w, norm2_w, w_ada, b_ada, w_router, b_router, w1, w3, w2, final_norm_w):
    f32 = jnp.float32

    def mixer(h, l, conv0, ssm0, s5r0, s5i0):
        b, L = h.shape[:2]
        z, xbc, dt_raw, u, gate_raw = jnp.split(h @ w_in[l], IN_SPLITS, axis=-1)
        ya, new_conv, new_ssm = ssd_branch(z, xbc, dt_raw, conv0, ssm0, conv_w[l], conv_b[l], dt_bias[l],
                                           a_log[l], d_ssd[l], ssd_norm_w[l], w_out_a[l])
        yb, new_re, new_im = s5_branch(u, s5r0, s5i0, s5_a_re[l], s5_a_im[l], s5_log_dt[l], s5_b_re[l],
                                       s5_b_im[l], s5_c_re[l], s5_c_im[l], s5_d[l], w_glu[l])
        g = jax.nn.sigmoid(gate_raw.astype(f32)).reshape(b, L, N_BRANCHES, D_MODEL)
        merged = g[..., 0, :] * ya.astype(f32) + g[..., 1, :] * yb.astype(f32)
        return merged.astype(h.dtype) @ w_out[l], new_conv, new_ssm, new_re, new_im

    def trunk(x, c, conv0, ssm0, s5r0, s5i0):
        convs, ssms, res, ims = [], [], [], []
        for l in range(DEPTH):
            mod = (jax.nn.silu(c) @ w_ada[l] + b_ada[l])[:, None, :]
            sh1, sc1, g1, sh2, sc2, g2 = jnp.split(mod, 6, axis=-1)
            h = rmsnorm(x, norm1_w[l]) * (1 + sc1) + sh1
            mix, cv, sm, sr, si = mixer(h, l, conv0[l], ssm0[l], s5r0[l], s5i0[l])
            x = x + g1 * mix
            h = rmsnorm(x, norm2_w[l]) * (1 + sc2) + sh2
            x = x + g2 * moe_ffn(h, w_router, b_router, w1[l], w3[l], w2[l])
            convs.append(cv)
            ssms.append(sm)
            res.append(sr)
            ims.append(si)
        return rmsnorm(x, final_norm_w), jnp.stack(ssms), jnp.stack(convs), jnp.stack(res), jnp.stack(ims)

    zero_conv = jnp.zeros((DEPTH, x_prompt.shape[0], SSD_CONV - 1, SSD_CONV_DIM), x_prompt.dtype)
    zero_ssm = jnp.zeros((DEPTH, x_prompt.shape[0], SSD_HEADS, SSD_HEAD_DIM, SSD_STATE), f32)
    zero_s5 = jnp.zeros((DEPTH, x_prompt.shape[0], S5_GROUPS, S5_STATE), f32)
    y_prompt, ssm_p, conv_p, s5re_p, s5im_p = trunk(x_prompt, c_prompt, zero_conv, zero_ssm, zero_s5, zero_s5)
    y_sample, ssm_s, conv_s, s5re_s, s5im_s = trunk(x_sample, c_sample, state_conv, state_ssm,
                                                    state_s5_re, state_s5_im)
    return (y_prompt, y_sample,
            ssm_p.astype(state_ssm.dtype), ssm_s.astype(state_ssm.dtype),
            conv_p.astype(state_conv.dtype), conv_s.astype(state_conv.dtype),
            s5re_p.astype(state_s5_re.dtype), s5re_s.astype(state_s5_re.dtype),
            s5im_p.astype(state_s5_im.dtype), s5im_s.astype(state_s5_im.dtype))
```

```python
import functools

import jax
import jax.numpy as jnp
from jax import lax
from jax.experimental import pallas as pl
from jax.experimental.pallas import tpu as pltpu

F32 = jnp.float32
BF16 = jnp.bfloat16
EPS = 1e-6

SSD_HEADS = 16
SSD_HEAD_DIM = 64
SSD_GROUPS = 2
SSD_STATE = 128
SSD_CONV = 4
SSD_CHUNK = 128
S5_GROUPS = 32
S5_GROUP_CH = 16
S5_STATE = 64
N_EXPERTS = 16
EXPERTS_PER_GROUP = 4
N_EXPERT_GROUPS = 4

LANES = 128
SUBLANES = 8
VMEM_LIMIT_BYTES = 56 * 1024 * 1024

ROW_TILE = 512
MOE_TILE = 256
S5_STEPS = 64


def _cparams(sem):
    return pltpu.CompilerParams(dimension_semantics=sem, vmem_limit_bytes=VMEM_LIMIT_BYTES)


def _bdot(a, b):
    return jnp.dot(a.astype(BF16), b.astype(BF16), preferred_element_type=F32)


def _bdot_nt(a, b):
    return lax.dot_general(a.astype(BF16), b.astype(BF16), (((1,), (1,)), ((), ())),
                           preferred_element_type=F32)


def _split3(x):
    hi = x.astype(BF16)
    r1 = x - hi.astype(F32)
    mid = r1.astype(BF16)
    lo = (r1 - mid.astype(F32)).astype(BF16)
    return hi, mid, lo


def _dot3_lhs(x, m):
    hi, mid, lo = _split3(x)
    d = functools.partial(jnp.dot, preferred_element_type=F32)
    return d(hi, m) + d(mid, m) + d(lo, m)


def _dot3_rhs(m, x):
    hi, mid, lo = _split3(x)
    d = functools.partial(jnp.dot, preferred_element_type=F32)
    return d(m, hi) + d(m, mid) + d(m, lo)


def _sigmoid(x):
    return jax.nn.sigmoid(x)


def _silu(x):
    return x * _sigmoid(x)


def _softplus(x):
    return jnp.maximum(x, 0.0) + jnp.log1p(jnp.exp(-jnp.abs(x)))


def _gelu_tanh(x):
    c = 0.7978845608028654
    return 0.5 * x * (1.0 + jnp.tanh(c * (x + 0.044715 * (x * x * x))))


def _rms(x):
    return x * lax.rsqrt(jnp.mean(x * x, axis=-1, keepdims=True) + EPS)


def _ada_kernel(c_ref, w_ref, b_ref, o_ref):
    o_ref[0] = _bdot(_silu(c_ref[...]), w_ref[0]) + b_ref[0]


def _ada_mod(c_all, w_ada, b_ada):
    depth, d, n = w_ada.shape
    nseq = c_all.shape[0]
    tn = 1536
    return pl.pallas_call(
        _ada_kernel,
        grid=(depth, n // tn),
        in_specs=[pl.BlockSpec((nseq, d), lambda l, j: (0, 0)),
                  pl.BlockSpec((1, d, tn), lambda l, j: (l, 0, j)),
                  pl.BlockSpec((1, 1, tn), lambda l, j: (l, 0, j))],
        out_specs=pl.BlockSpec((1, nseq, tn), lambda l, j: (l, 0, j)),
        out_shape=jax.ShapeDtypeStruct((depth, nseq, n), F32),
        compiler_params=_cparams(("arbitrary", "arbitrary")),
        name="ada_mod",
    )(c_all, w_ada, b_ada.reshape(depth, 1, n))


def _mod_rows(mod_ref, j, per_token):
    if per_token:
        return mod_ref[j]
    return mod_ref[j:j + 1, :]


def _inproj_kernel(x_ref, mod_ref, nw_ref, wz_ref, wx_ref, wdt_ref, wu_ref, wg_ref,
                   z_ref, xbc_ref, dt_ref, u_ref, g_ref, *, per_token):
    sh = _mod_rows(mod_ref, 0, per_token)
    sc = _mod_rows(mod_ref, 1, per_token)
    h = (_rms(x_ref[...]) * nw_ref[...] * (1.0 + sc) + sh).astype(BF16)
    d = functools.partial(jnp.dot, preferred_element_type=F32)
    z_ref[...] = d(h, wz_ref[...])
    xbc_ref[...] = d(h, wx_ref[...])
    dt_ref[...] = d(h, wdt_ref[...])
    u_ref[...] = d(h, wu_ref[...])
    g_ref[...] = d(h, wg_ref[...])


def _const_spec(shape):
    nd = len(shape)
    return pl.BlockSpec(shape, lambda *_: (0,) * nd)


def _inproj(x, mod, nw, ws, *, tm, per_token, mod_spec, u_shape, u_spec):
    t, d = x.shape
    wz, wx, wdt, wu, wg = ws
    row = lambda n: pl.BlockSpec((tm, n), lambda i: (i, 0))
    return pl.pallas_call(
        functools.partial(_inproj_kernel, per_token=per_token),
        grid=(t // tm,),
        in_specs=[row(d), mod_spec, _const_spec((1, d)),
                  _const_spec(wz.shape), _const_spec(wx.shape), _const_spec(wdt.shape),
                  _const_spec(wu.shape), _const_spec(wg.shape)],
        out_specs=[row(wz.shape[1]), row(wx.shape[1]), row(wdt.shape[1]), u_spec, row(wg.shape[1])],
        out_shape=[jax.ShapeDtypeStruct((t, wz.shape[1]), F32),
                   jax.ShapeDtypeStruct((t, wx.shape[1]), F32),
                   jax.ShapeDtypeStruct((t, wdt.shape[1]), F32),
                   jax.ShapeDtypeStruct(u_shape, F32),
                   jax.ShapeDtypeStruct((t, wg.shape[1]), F32)],
        compiler_params=_cparams(("arbitrary",)),
        name="inproj",
    )(x, mod, nw, wz, wx, wdt, wu, wg)


def _gated_group_norm(y, z, nw):
    y = y * _silu(z)
    half = y.shape[1] // SSD_GROUPS
    parts = [_rms(y[:, g * half:(g + 1) * half]) for g in range(SSD_GROUPS)]
    return jnp.concatenate(parts, axis=1) * nw


def _ssd_prompt_kernel(xbc_ref, dt_ref, z_ref, cw_ref, cb_ref, dtb_ref, alog_ref, dsk_ref, nw_ref,
                       e_ref, yn_ref, conv_ref, ssm_ref, ext_ref, st_ref):
    c = pl.program_id(1)
    q = SSD_CHUNK
    d_in = SSD_HEADS * SSD_HEAD_DIM
    gw = SSD_STATE
    hpg = SSD_HEADS // SSD_GROUPS
    grows = hpg * SSD_HEAD_DIM

    @pl.when(c == 0)
    def _():
        ext_ref[0:SUBLANES, :] = jnp.zeros((SUBLANES, ext_ref.shape[1]), F32)
        st_ref[...] = jnp.zeros(st_ref.shape, F32)

    xbc = xbc_ref[...]
    ext_ref[SUBLANES:SUBLANES + q, :] = xbc
    acc = cb_ref[...] + cw_ref[SSD_CONV - 1:SSD_CONV, :] * xbc
    for j in range(1, SSD_CONV):
        acc = acc + cw_ref[SSD_CONV - 1 - j:SSD_CONV - j, :] * ext_ref[pl.ds(SUBLANES - j, q), :]
    ext_ref[0:SUBLANES, :] = xbc[q - SUBLANES:q, :]
    v = _silu(acc)
    xs = v[:, :d_in]
    bm = v[:, d_in:d_in + SSD_GROUPS * gw]
    cm = v[:, d_in + SSD_GROUPS * gw:]

    dt = _softplus(dt_ref[...] + dtb_ref[...])
    a = -jnp.exp(alog_ref[...])
    da = dt * a
    rid = lax.broadcasted_iota(jnp.int32, (q, q), 0)
    cid = lax.broadcasted_iota(jnp.int32, (q, q), 1)
    causal = rid >= cid
    tri = jnp.where(causal, 1.0, 0.0).astype(BF16)
    a_cs = _dot3_rhs(tri, da)
    a_cs_t = a_cs.T
    a_last = a_cs[q - 1:q, :]
    e = e_ref[...]
    x_dt = xs * _dot3_lhs(dt, e)
    x_end = xs * _dot3_lhs(jnp.exp(a_last - a_cs) * dt, e)
    eacs_e = _dot3_lhs(jnp.exp(a_cs), e)
    x_end_t = x_end.T.astype(BF16)
    lane = lax.broadcasted_iota(jnp.int32, (q, 2 * SSD_HEAD_DIM), 1)
    first = lane < SSD_HEAD_DIM

    y_diag = []
    y_off = []
    for g in range(SSD_GROUPS):
        bg = bm[:, g * gw:(g + 1) * gw].astype(BF16)
        cg = cm[:, g * gw:(g + 1) * gw].astype(BF16)
        cb = _bdot_nt(cg, bg)
        sg = st_ref[g * grows:(g + 1) * grows, :]
        y_off.append(_bdot_nt(cg, sg))
        for pair in range(hpg // 2):
            ms = []
            for r in (2 * pair, 2 * pair + 1):
                h = g * hpg + r
                diff = a_cs[:, h:h + 1] - a_cs_t[h:h + 1, :]
                decay = jnp.where(causal, jnp.exp(jnp.where(causal, diff, 0.0)), 0.0)
                ms.append((cb * decay).astype(BF16))
            h0 = g * hpg + 2 * pair
            xp = x_dt[:, h0 * SSD_HEAD_DIM:(h0 + 2) * SSD_HEAD_DIM]
            rhs = jnp.concatenate([jnp.where(first, xp, 0.0), jnp.where(first, 0.0, xp)], axis=0)
            y_diag.append(jnp.dot(jnp.concatenate(ms, axis=1), rhs.astype(BF16), preferred_element_type=F32))
        contrib = jnp.dot(x_end_t[g * grows:(g + 1) * grows, :], bg, preferred_element_type=F32)
        for r in range(hpg):
            h = g * hpg + r
            rows = slice(h * SSD_HEAD_DIM, (h + 1) * SSD_HEAD_DIM)
            dec = jnp.exp(a_cs_t[h:h + 1, q - 1:q])
            st_ref[rows, :] = st_ref[rows, :] * dec + contrib[r * SSD_HEAD_DIM:(r + 1) * SSD_HEAD_DIM, :]

    y = (jnp.concatenate(y_diag, axis=1) + jnp.concatenate(y_off, axis=1) * eacs_e
         + dsk_ref[...] * xs)
    yn_ref[...] = _gated_group_norm(y, z_ref[...], nw_ref[...]).astype(BF16)

    @pl.when(c == pl.num_programs(1) - 1)
    def _():
        conv_ref[0] = ext_ref[pl.ds(SUBLANES - (SSD_CONV - 1), SSD_CONV - 1), :]
        ssm_ref[0] = st_ref[...].reshape(SSD_HEADS, SSD_HEAD_DIM, SSD_STATE)


def _ssd_prompt(xbc, dt_raw, z, p, nb, seq):
    t, cdim = xbc.shape
    nc = seq // SSD_CHUNK
    q = SSD_CHUNK
    d_in = z.shape[1]
    row = lambda n: pl.BlockSpec((q, n), lambda b, c: (b * nc + c, 0))
    return pl.pallas_call(
        _ssd_prompt_kernel,
        grid=(nb, nc),
        in_specs=[row(cdim), row(LANES), row(d_in),
                  _const_spec((SSD_CONV, cdim)), _const_spec((1, cdim)), _const_spec((1, LANES)),
                  _const_spec((1, LANES)), _const_spec((1, d_in)), _const_spec((1, d_in)),
                  _const_spec((LANES, d_in))],
        out_specs=[row(d_in),
                   pl.BlockSpec((1, SSD_CONV - 1, cdim), lambda b, c: (b, 0, 0)),
                   pl.BlockSpec((1, SSD_HEADS, SSD_HEAD_DIM, SSD_STATE), lambda b, c: (b, 0, 0, 0))],
        out_shape=[jax.ShapeDtypeStruct((t, d_in), BF16),
                   jax.ShapeDtypeStruct((nb, SSD_CONV - 1, cdim), F32),
                   jax.ShapeDtypeStruct((nb, SSD_HEADS, SSD_HEAD_DIM, SSD_STATE), F32)],
        scratch_shapes=[pltpu.VMEM((SUBLANES + q, cdim), F32),
                        pltpu.VMEM((d_in, SSD_STATE), F32)],
        compiler_params=_cparams(("arbitrary", "arbitrary")),
        name="ssd_prompt",
    )(xbc, dt_raw, z, p["conv_w"], p["conv_b"], p["dt_bias"], p["a_log"], p["d_skip_e"], p["ssd_norm_w"],
      p["head_expand"])


def _ssd_sample_a_kernel(xbc_ref, conv0_ref, dt_ref, cw_ref, cb_ref, dtb_ref, alog_ref, dsk_ref, e_ref,
                         ydiag_ref, eacs_ref, wt_ref, bs_ref, c8_ref, dec_ref, nconv_ref, *, steps, nseq):
    d_in = SSD_HEADS * SSD_HEAD_DIM
    gw = SSD_STATE
    hpg = SSD_HEADS // SSD_GROUPS
    k1 = SSD_CONV - 1
    slab = lambda ref, i: ref[i * nseq:(i + 1) * nseq, :]
    xp = [slab(conv0_ref, i) for i in range(k1)] + [slab(xbc_ref, i) for i in range(steps)]
    for i in range(k1):
        nconv_ref[i * nseq:(i + 1) * nseq, :] = xp[steps + i]
    e = e_ref[...]
    a = -jnp.exp(alog_ref[...])
    head = lax.broadcasted_iota(jnp.int32, (nseq, LANES), 1)
    xs, bm, cm, dt, a_cs = [], [], [], [], []
    run = jnp.zeros((nseq, LANES), F32)
    for t in range(steps):
        acc = cb_ref[...]
        for k in range(SSD_CONV):
            acc = acc + cw_ref[k:k + 1, :] * xp[t + k]
        v = _silu(acc)
        xs.append(v[:, :d_in])
        bm.append(v[:, d_in:d_in + SSD_GROUPS * gw])
        cm.append(v[:, d_in + SSD_GROUPS * gw:])
        dt.append(_softplus(slab(dt_ref, t) + dtb_ref[...]))
        run = run + dt[t] * a
        a_cs.append(run)
    dec_ref[...] = jnp.exp(a_cs[steps - 1])
    for t in range(steps):
        y = dsk_ref[...] * xs[t]
        for s in range(t + 1):
            cbs = [jnp.sum(cm[t][:, g * gw:(g + 1) * gw] * bm[s][:, g * gw:(g + 1) * gw],
                           axis=-1, keepdims=True) for g in range(SSD_GROUPS)]
            coef = jnp.exp(a_cs[t] - a_cs[s]) * dt[s] * jnp.where(head < hpg, cbs[0], cbs[1])
            y = y + _dot3_lhs(coef, e) * xs[s]
        ydiag_ref[t * nseq:(t + 1) * nseq, :] = y
        eacs_ref[t * nseq:(t + 1) * nseq, :] = _dot3_lhs(jnp.exp(a_cs[t]), e)
        w = xs[t] * _dot3_lhs(jnp.exp(a_cs[steps - 1] - a_cs[t]) * dt[t], e)
        wt_ref[:, t * nseq:(t + 1) * nseq] = w.T.astype(BF16)
        bs_ref[t * nseq:(t + 1) * nseq, :] = bm[t]
        for g in range(SSD_GROUPS):
            j = g * steps + t
            c8_ref[j * nseq:(j + 1) * nseq, :] = cm[t][:, g * gw:(g + 1) * gw]


def _ssd_sample_b_kernel(dec_ref, h0_ref, wt_ref, bs_ref, c8_ref, yoff_ref, ssm_ref, *, steps, nseq):
    b = pl.program_id(0)
    hpg = SSD_HEADS // SSD_GROUPS
    grows = hpg * SSD_HEAD_DIM
    gw = SSD_STATE
    h0 = h0_ref[0].reshape(SSD_HEADS * SSD_HEAD_DIM, SSD_STATE)
    c8 = c8_ref[pl.ds(b, SSD_GROUPS * steps, stride=nseq), :]
    yoff_ref[0] = _bdot_nt(c8, h0)
    rid = lax.broadcasted_iota(jnp.int32, (steps * nseq, gw), 0)
    mine = rid == b
    for t in range(1, steps):
        mine = jnp.logical_or(mine, rid == b + t * nseq)
    for g in range(SSD_GROUPS):
        bsel = jnp.where(mine, bs_ref[:, g * gw:(g + 1) * gw], 0.0).astype(BF16)
        contrib = jnp.dot(wt_ref[g * grows:(g + 1) * grows, :], bsel, preferred_element_type=F32)
        for r in range(hpg):
            h = g * hpg + r
            rows = slice(h * SSD_HEAD_DIM, (h + 1) * SSD_HEAD_DIM)
            ssm_ref[0, h] = h0[rows, :] * dec_ref[b, h] + contrib[r * SSD_HEAD_DIM:(r + 1) * SSD_HEAD_DIM, :]


def _ssd_sample_c_kernel(ydiag_ref, yoff_ref, eacs_ref, z_ref, nw_ref, yn_ref, *, steps, nseq):
    d_in = SSD_HEADS * SSD_HEAD_DIM
    half = d_in // SSD_GROUPS
    for t in range(steps):
        lo = yoff_ref[:, t * d_in:t * d_in + half]
        hi = yoff_ref[:, (steps + t) * d_in + half:(steps + t + 1) * d_in]
        rows = slice(t * nseq, (t + 1) * nseq)
        y = ydiag_ref[rows, :] + jnp.concatenate([lo, hi], axis=1) * eacs_ref[rows, :]
        yn_ref[rows, :] = _gated_group_norm(y, z_ref[rows, :], nw_ref[...]).astype(BF16)


def _ssd_sample(xbc, dt_raw, z, conv0_tm, ssm0, p, nseq, steps):
    t, cdim = xbc.shape
    d_in = z.shape[1]
    k1 = SSD_CONV - 1
    nj = SSD_GROUPS * steps
    a_out = pl.pallas_call(
        functools.partial(_ssd_sample_a_kernel, steps=steps, nseq=nseq),
        out_shape=[jax.ShapeDtypeStruct((t, d_in), F32),
                   jax.ShapeDtypeStruct((t, d_in), F32),
                   jax.ShapeDtypeStruct((d_in, t), BF16),
                   jax.ShapeDtypeStruct((t, SSD_GROUPS * SSD_STATE), F32),
                   jax.ShapeDtypeStruct((nj * nseq, SSD_STATE), F32),
                   jax.ShapeDtypeStruct((nseq, LANES), F32),
                   jax.ShapeDtypeStruct((k1 * nseq, cdim), F32)],
        compiler_params=pltpu.CompilerParams(vmem_limit_bytes=VMEM_LIMIT_BYTES),
        name="ssd_sample_a",
    )(xbc, conv0_tm, dt_raw, p["conv_w"], p["conv_b"], p["dt_bias"], p["a_log"], p["d_skip_e"],
      p["head_expand"])
    ydiag, eacs_e, wt, bs, c8, dec, nconv = a_out
    yoff, nssm = pl.pallas_call(
        functools.partial(_ssd_sample_b_kernel, steps=steps, nseq=nseq),
        grid=(nseq,),
        in_specs=[pl.BlockSpec(memory_space=pltpu.SMEM),
                  pl.BlockSpec((1, SSD_HEADS, SSD_HEAD_DIM, SSD_STATE), lambda b: (b, 0, 0, 0)),
                  _const_spec(wt.shape), _const_spec(bs.shape), _const_spec(c8.shape)],
        out_specs=[pl.BlockSpec((1, nj, d_in), lambda b: (b, 0, 0)),
                   pl.BlockSpec((1, SSD_HEADS, SSD_HEAD_DIM, SSD_STATE), lambda b: (b, 0, 0, 0))],
        out_shape=[jax.ShapeDtypeStruct((nseq, nj, d_in), F32),
                   jax.ShapeDtypeStruct(ssm0.shape, F32)],
        compiler_params=_cparams(("arbitrary",)),
        name="ssd_sample_b",
    )(dec[:, :SSD_HEADS], ssm0, wt, bs, c8)
    yn = pl.pallas_call(
        functools.partial(_ssd_sample_c_kernel, steps=steps, nseq=nseq),
        out_shape=jax.ShapeDtypeStruct((t, d_in), BF16),
        compiler_params=pltpu.CompilerParams(vmem_limit_bytes=VMEM_LIMIT_BYTES),
        name="ssd_sample_c",
    )(ydiag, yoff.reshape(nseq, nj * d_in), eacs_e, z, p["ssd_norm_w"])
    return yn, nconv, nssm


def _s5_param_kernel(are_ref, aim_ref, ldt_ref, bre_ref, bim_ref, ab_ref, wb_ref):
    n = are_ref.shape[1]
    a_re = are_ref[...]
    a_im = aim_ref[...]
    dt = jnp.exp(ldt_ref[...])
    mag = jnp.exp(dt * a_re)
    ab_re = mag * jnp.cos(dt * a_im)
    ab_im = mag * jnp.sin(dt * a_im)
    den = a_re * a_re + a_im * a_im
    f_re = ((ab_re - 1.0) * a_re + ab_im * a_im) / den
    f_im = (ab_im * a_re - (ab_re - 1.0) * a_im) / den
    ab_ref[0:1, :] = ab_re
    ab_ref[1:2, :] = ab_im
    b_re = bre_ref[...]
    b_im = bim_ref[...]
    wb_ref[:, 0:n] = (f_re * b_re - f_im * b_im).astype(BF16)
    wb_ref[:, n:2 * n] = (f_re * b_im + f_im * b_re).astype(BF16)


def _s5_params(a_re, a_im, ldt, bd_re, bd_im):
    n = a_re.shape[1]
    return pl.pallas_call(
        _s5_param_kernel,
        out_shape=[jax.ShapeDtypeStruct((2, n), F32),
                   jax.ShapeDtypeStruct((bd_re.shape[0], 2 * n), BF16)],
        compiler_params=pltpu.CompilerParams(vmem_limit_bytes=VMEM_LIMIT_BYTES),
        name="s5_params",
    )(a_re, a_im, ldt, bd_re, bd_im)


def _s5_kernel(u_ref, h0re_ref, h0im_ref, ab_ref, wb_ref, cre_ref, cim_ref, d_ref, wglu_ref,
               yb_ref, nre_ref, nim_ref, xh_ref, st_ref, *, rows_per_step, steps):
    i = pl.program_id(0)
    n = ab_ref.shape[1]
    r = rows_per_step

    @pl.when(i == 0)
    def _():
        st_ref[0] = h0re_ref[...]
        st_ref[1] = h0im_ref[...]

    u = u_ref[...]
    xh_ref[...] = _bdot(u, wb_ref[...])
    a_re = ab_ref[0:1, :]
    a_im = ab_ref[1:2, :]

    def step(t, carry):
        h_re, h_im = carry
        rows = pl.ds(pl.multiple_of(t * r, r), r)
        n_re = a_re * h_re - a_im * h_im + xh_ref[rows, 0:n]
        n_im = a_re * h_im + a_im * h_re + xh_ref[rows, n:2 * n]
        xh_ref[rows, 0:n] = n_re
        xh_ref[rows, n:2 * n] = n_im
        return n_re, n_im

    carry = (st_ref[0], st_ref[1])
    if steps <= 8:
        for t in range(steps):
            carry = step(t, carry)
    else:
        carry = lax.fori_loop(0, steps, step, carry, unroll=2)
    st_ref[0] = carry[0]
    st_ref[1] = carry[1]

    y = (_bdot(xh_ref[:, 0:n], cre_ref[...]) - _bdot(xh_ref[:, n:2 * n], cim_ref[...])
         + d_ref[...] * u)
    pre = _bdot(_gelu_tanh(y), wglu_ref[...])
    half = pre.shape[1] // 2
    yb_ref[...] = pre[:, :half] * _sigmoid(pre[:, half:])

    @pl.when(i == pl.num_programs(0) - 1)
    def _():
        nre_ref[...] = carry[0]
        nim_ref[...] = carry[1]


def _s5(u_tm, h0_re, h0_im, sp, rows_per_step, steps):
    t, w = u_tm.shape
    n = sp["ab"].shape[1]
    tm = rows_per_step * steps
    d_model = sp["w_glu"].shape[1] // 2
    return pl.pallas_call(
        functools.partial(_s5_kernel, rows_per_step=rows_per_step, steps=steps),
        grid=(t // tm,),
        in_specs=[pl.BlockSpec((tm, w), lambda i: (i, 0)),
                  _const_spec(h0_re.shape), _const_spec(h0_im.shape), _const_spec(sp["ab"].shape),
                  _const_spec(sp["wb"].shape), _const_spec(sp["c_re"].shape), _const_spec(sp["c_im"].shape),
                  _const_spec((1, w)), _const_spec(sp["w_glu"].shape)],
        out_specs=[pl.BlockSpec((tm, d_model), lambda i: (i, 0)),
                   _const_spec(h0_re.shape), _const_spec(h0_im.shape)],
        out_shape=[jax.ShapeDtypeStruct((t, d_model), F32),
                   jax.ShapeDtypeStruct(h0_re.shape, F32),
                   jax.ShapeDtypeStruct(h0_im.shape, F32)],
        scratch_shapes=[pltpu.VMEM((tm, 2 * n), F32),
                        pltpu.VMEM((2, rows_per_step, n), F32)],
        compiler_params=_cparams(("arbitrary",)),
        name="s5_scan",
    )(u_tm, h0_re, h0_im, sp["ab"], sp["wb"], sp["c_re"], sp["c_im"], sp["d"], sp["w_glu"])


def _merge_router_kernel(yn_ref, yb_ref, gate_ref, x_ref, mod_ref, woa_ref, wo_ref, nw_ref, wr_ref, br_ref,
                         x1_ref, h2_ref, idx_ref, rank_ref, wts_ref, cnt_ref, run_ref, *, per_token):
    i = pl.program_id(0)
    tm, d = x_ref.shape

    @pl.when(i == 0)
    def _():
        run_ref[...] = jnp.zeros(run_ref.shape, F32)

    ya = jnp.dot(yn_ref[...], woa_ref[...], preferred_element_type=F32)
    gate = _sigmoid(gate_ref[...])
    merged = gate[:, :d] * ya + gate[:, d:] * yb_ref[...]
    mix = _bdot(merged, wo_ref[...])
    g1 = _mod_rows(mod_ref, 2, per_token)
    sh2 = _mod_rows(mod_ref, 3, per_token)
    sc2 = _mod_rows(mod_ref, 4, per_token)
    x1 = x_ref[...] + g1 * mix
    x1_ref[...] = x1
    h2 = _rms(x1) * nw_ref[...] * (1.0 + sc2) + sh2
    h2_ref[...] = h2

    logits = jnp.dot(h2, wr_ref[...], preferred_element_type=F32, precision=lax.Precision.HIGHEST)
    lt = logits.T
    erow = lax.broadcasted_iota(jnp.int32, lt.shape, 0)
    lt = jnp.where(erow < N_EXPERTS, lt, -jnp.inf)
    ex = jnp.exp(lt - jnp.max(lt, axis=0, keepdims=True))
    scores = ex / jnp.sum(ex, axis=0, keepdims=True)
    sel = scores + br_ref[...]
    s = [sel[e:e + 1, :] for e in range(N_EXPERTS)]
    p = [scores[e:e + 1, :] for e in range(N_EXPERTS)]

    def group_top2_sum(vals):
        best = None
        for a in range(len(vals)):
            for b in range(a + 1, len(vals)):
                pair = vals[a] + vals[b]
                best = pair if best is None else jnp.maximum(best, pair)
        return best

    gs = [group_top2_sum(s[EXPERTS_PER_GROUP * g:EXPERTS_PER_GROUP * (g + 1)]) for g in range(N_EXPERT_GROUPS)]
    best = gs[0]
    bg = jnp.zeros(best.shape, jnp.int32)
    for g in range(1, N_EXPERT_GROUPS):
        better = gs[g] > best
        bg = jnp.where(better, g, bg)
        best = jnp.where(better, gs[g], best)

    def pick(rows, j):
        out = rows[j]
        for g in range(1, N_EXPERT_GROUPS):
            out = jnp.where(bg == g, rows[EXPERTS_PER_GROUP * g + j], out)
        return out

    cs = [pick(s, j) for j in range(EXPERTS_PER_GROUP)]
    cp = [pick(p, j) for j in range(EXPERTS_PER_GROUP)]

    def argmax_first(vals, skip):
        bv = None
        bi = None
        bw = None
        for j in range(len(vals)):
            v = vals[j] if skip is None else jnp.where(skip == j, -jnp.inf, vals[j])
            if bv is None:
                bv, bi, bw = v, jnp.zeros(v.shape, jnp.int32), cp[j]
            else:
                better = v > bv
                bi = jnp.where(better, j, bi)
                bw = jnp.where(better, cp[j], bw)
                bv = jnp.where(better, v, bv)
        return bi, bw

    i1, w1 = argmax_first(cs, None)
    i2, w2 = argmax_first(cs, i1)
    wsum = w1 + w2
    e1 = bg * EXPERTS_PER_GROUP + i1
    e2 = bg * EXPERTS_PER_GROUP + i2
    idx_ref[0:1, :] = e1
    idx_ref[1:2, :] = e2
    wrow = lax.broadcasted_iota(jnp.int32, lt.shape, 0)
    wmat = jnp.where(wrow == 0, w1 / wsum, jnp.where(wrow == 1, w2 / wsum, 0.0))
    wts_ref[...] = wmat.T

    oh1 = jnp.where(erow == e1, 1.0, 0.0)
    oh2 = jnp.where(erow == e2, 1.0, 0.0)
    both = oh1 + oh2
    ta = lax.broadcasted_iota(jnp.int32, (tm, tm), 0)
    tb = lax.broadcasted_iota(jnp.int32, (tm, tm), 1)
    earlier = jnp.where(ta < tb, 1.0, 0.0).astype(BF16)
    before = jnp.dot(both.astype(BF16), earlier, preferred_element_type=F32) + run_ref[...]
    rank_ref[0:1, :] = jnp.sum(oh1 * before, axis=0, keepdims=True).astype(jnp.int32)
    rank_ref[1:2, :] = jnp.sum(oh2 * before, axis=0, keepdims=True).astype(jnp.int32)
    run_ref[...] = run_ref[...] + jnp.sum(both, axis=1, keepdims=True)

    @pl.when(i == pl.num_programs(0) - 1)
    def _():
        cnt_ref[...] = run_ref[...]


def _merge_router(yn, yb, gate, x, mod, lw, rw, *, tm, per_token, mod_spec, yb_spec):
    t, d = x.shape
    row = lambda n: pl.BlockSpec((tm, n), lambda i: (i, 0))
    pair = pl.BlockSpec((2, tm), lambda i: (0, i))
    return pl.pallas_call(
        functools.partial(_merge_router_kernel, per_token=per_token),
        grid=(t // tm,),
        in_specs=[row(d), yb_spec, row(2 * d), row(d), mod_spec,
                  _const_spec((d, d)), _const_spec((d, d)), _const_spec((1, d)),
                  _const_spec((d, LANES)), _const_spec((LANES, 1))],
        out_specs=[row(d), row(d), pair, pair, row(LANES), _const_spec((LANES, 1))],
        out_shape=[jax.ShapeDtypeStruct((t, d), F32),
                   jax.ShapeDtypeStruct((t, d), F32),
                   jax.ShapeDtypeStruct((2, t), jnp.int32),
                   jax.ShapeDtypeStruct((2, t), jnp.int32),
                   jax.ShapeDtypeStruct((t, LANES), F32),
                   jax.ShapeDtypeStruct((LANES, 1), F32)],
        scratch_shapes=[pltpu.VMEM((LANES, 1), F32)],
        compiler_params=_cparams(("arbitrary",)),
        name="merge_router",
    )(yn, yb, gate, x, mod, lw["w_out_a"], lw["w_out"], lw["norm2_w"], rw["w_router"], rw["b_router"])


def _row_copy(src_ref, src_row, dst_ref, dst_row, sem):
    return pltpu.make_async_copy(src_ref.at[pl.ds(src_row, 1), :], dst_ref.at[pl.ds(dst_row, 1), :], sem)


def _dispatch_kernel(dest_ref, h_ref, init_ref, out_ref, sem):
    del init_ref
    tm = h_ref.shape[0]

    def issue(t, carry):
        for k in range(2):
            _row_copy(h_ref, t, out_ref, dest_ref[k, t], sem).start()
        return carry

    def drain(t, carry):
        for k in range(2):
            _row_copy(h_ref, t, out_ref, dest_ref[k, t], sem).wait()
        return carry

    lax.fori_loop(0, tm, issue, 0, unroll=8)
    lax.fori_loop(0, tm, drain, 0, unroll=8)


def _dispatch(dest, h2, n_rows, tm):
    t, d = h2.shape
    return pl.pallas_call(
        _dispatch_kernel,
        grid=(t // tm,),
        in_specs=[pl.BlockSpec((2, tm), lambda i: (0, i), memory_space=pltpu.SMEM),
                  pl.BlockSpec((tm, d), lambda i: (i, 0)),
                  pl.BlockSpec(memory_space=pl.ANY)],
        out_specs=pl.BlockSpec(memory_space=pl.ANY),
        out_shape=jax.ShapeDtypeStruct((n_rows, d), F32),
        scratch_shapes=[pltpu.SemaphoreType.DMA(())],
        input_output_aliases={2: 0},
        compiler_params=_cparams(("arbitrary",)),
        name="moe_dispatch",
    )(dest, h2, jnp.zeros((n_rows, d), F32))


def _expert_kernel(te_ref, nv_ref, x_ref, w1_ref, w3_ref, w2_ref, y_ref):
    j = pl.program_id(0)

    @pl.when(j < nv_ref[0])
    def _():
        xb = x_ref[...].astype(BF16)
        a = jnp.dot(xb, w1_ref[...], preferred_element_type=F32)
        b = jnp.dot(xb, w3_ref[...], preferred_element_type=F32)
        y_ref[...] = _bdot(_silu(a) * b, w2_ref[...])

    @pl.when(j >= nv_ref[0])
    def _():
        y_ref[...] = jnp.zeros(y_ref.shape, F32)


def _experts(tile_expert, n_valid, xs, w1, w3, w2):
    n_rows, d = xs.shape
    de = w1.shape[2]
    tm = MOE_TILE
    last = lambda j, te, nv: jnp.minimum(j, nv[0] - 1)
    grid_spec = pltpu.PrefetchScalarGridSpec(
        num_scalar_prefetch=2,
        grid=(n_rows // tm,),
        in_specs=[pl.BlockSpec((tm, d), lambda j, te, nv: (last(j, te, nv), 0)),
                  pl.BlockSpec((None, d, de), lambda j, te, nv: (te[j], 0, 0)),
                  pl.BlockSpec((None, d, de), lambda j, te, nv: (te[j], 0, 0)),
                  pl.BlockSpec((None, de, d), lambda j, te, nv: (te[j], 0, 0))],
        out_specs=pl.BlockSpec((tm, d), lambda j, te, nv: (j, 0)),
    )
    return pl.pallas_call(
        _expert_kernel,
        grid_spec=grid_spec,
        out_shape=jax.ShapeDtypeStruct((n_rows, d), F32),
        compiler_params=_cparams(("arbitrary",)),
        name="moe_experts",
    )(tile_expert, n_valid, xs, w1, w3, w2)


def _combine_kernel(dest_ref, ys_ref, x1_ref, wts_ref, mod_ref, fw_ref, out_ref, buf_ref, sem,
                    *, per_token, final_norm):
    tm = x1_ref.shape[0]

    def issue(t, carry):
        for k in range(2):
            _row_copy(ys_ref, dest_ref[k, t], buf_ref.at[k], t, sem).start()
        return carry

    def drain(t, carry):
        for k in range(2):
            _row_copy(ys_ref, dest_ref[k, t], buf_ref.at[k], t, sem).wait()
        return carry

    lax.fori_loop(0, tm, issue, 0, unroll=8)
    lax.fori_loop(0, tm, drain, 0, unroll=8)
    w = wts_ref[...]
    moe = w[:, 0:1] * buf_ref[0] + w[:, 1:2] * buf_ref[1]
    x2 = x1_ref[...] + _mod_rows(mod_ref, 5, per_token) * moe
    if final_norm:
        x2 = _rms(x2) * fw_ref[...]
    out_ref[...] = x2


def _combine(dest, ys, x1, wts, mod, fw, *, tm, per_token, mod_spec, final_norm):
    t, d = x1.shape
    row = lambda n: pl.BlockSpec((tm, n), lambda i: (i, 0))
    return pl.pallas_call(
        functools.partial(_combine_kernel, per_token=per_token, final_norm=final_norm),
        grid=(t // tm,),
        in_specs=[pl.BlockSpec((2, tm), lambda i: (0, i), memory_space=pltpu.SMEM),
                  pl.BlockSpec(memory_space=pl.ANY),
                  row(d), row(LANES), mod_spec, _const_spec((1, d))],
        out_specs=row(d),
        out_shape=jax.ShapeDtypeStruct((t, d), F32),
        scratch_shapes=[pltpu.VMEM((2, tm, d), F32), pltpu.SemaphoreType.DMA(())],
        compiler_params=_cparams(("arbitrary",)),
        name="moe_combine",
    )(dest, ys, x1, wts, mod, fw)


def _moe(h2, idx, rank, counts, x1, wts, mod, fw, ew, *, tok_tile, per_token, mod_spec_fn, final_norm):
    t, d = h2.shape
    tm = MOE_TILE
    n_rows = ((2 * t + N_EXPERTS * (tm - 1)) // tm + 1) * tm
    n_tiles = n_rows // tm
    cnt = counts[:N_EXPERTS, 0].astype(jnp.int32)
    padded = ((cnt + tm - 1) // tm) * tm
    ends = jnp.cumsum(padded)
    starts = ends - padded
    n_valid = (ends[-1] // tm).astype(jnp.int32)
    tiles = jnp.minimum(jnp.arange(n_tiles, dtype=jnp.int32), n_valid - 1)
    tile_expert = jnp.minimum(jnp.searchsorted(ends, tiles * tm, side="right"), N_EXPERTS - 1).astype(jnp.int32)
    dest = starts[idx] + rank

    xs = _dispatch(dest, h2, n_rows, tok_tile)
    ys = _experts(tile_expert, n_valid.reshape(1), xs, ew["w1"], ew["w3"], ew["w2"])
    return _combine(dest, ys, x1, wts, mod, fw, tm=tok_tile, per_token=per_token,
                    mod_spec=mod_spec_fn(tok_tile), final_norm=final_norm)


def _pad_lanes(a, n=LANES):
    return jnp.pad(a, [(0, 0)] * (a.ndim - 1) + [(0, n - a.shape[-1])])


def _block_diag(blocks):
    g, r, c = blocks.shape
    out = jnp.zeros((g, r, g, c), blocks.dtype)
    ar = jnp.arange(g)
    out = out.at[ar, :, ar, :].set(blocks)
    return out.reshape(g * r, g * c)


def kernel(x_prompt, x_sample, state_ssm, state_conv, state_s5_re, state_s5_im, c_prompt, c_sample, w_in, conv_w, conv_b, dt_bias, a_log, d_ssd, ssd_norm_w, w_out_a, s5_a_re, s5_a_im, s5_log_dt, s5_b_re, s5_b_im, s5_c_re, s5_c_im, s5_d, w_glu, w_out, norm1_w, norm2_w, w_ada, b_ada, w_router, b_router, w1, w3, w2, final_norm_w):
    nb, seq, d = x_prompt.shape
    ns, steps, _ = x_sample.shape
    depth = w_in.shape[0]
    d_in = SSD_HEADS * SSD_HEAD_DIM
    cdim = conv_w.shape[2]
    s5w = S5_GROUPS * S5_GROUP_CH
    s5n = S5_GROUPS * S5_STATE
    k1 = SSD_CONV - 1
    tp = nb * seq
    ts = ns * steps
    tiles_per_seq = seq // ROW_TILE

    mod = _ada_mod(jnp.concatenate([c_prompt, c_sample], axis=0), w_ada, b_ada)
    head_expand = jnp.repeat(jnp.eye(LANES, SSD_HEADS, dtype=BF16), SSD_HEAD_DIM, axis=1)
    rw = {"w_router": _pad_lanes(w_router), "b_router": _pad_lanes(b_router[None, :]).reshape(LANES, 1)}

    xp = x_prompt.reshape(tp, d)
    xsm = jnp.transpose(x_sample, (1, 0, 2)).reshape(ts, d)
    outs = {k: [] for k in ("ssm_p", "ssm_s", "conv_p", "conv_s", "re_p", "re_s", "im_p", "im_s")}
    zero_state = jnp.zeros((nb, s5n), F32)

    for l in range(depth):
        o0 = 0
        o1 = d_in
        o2 = o1 + cdim
        o3 = o2 + SSD_HEADS
        o4 = o3 + s5w
        wl = w_in[l].astype(BF16)
        ws = (wl[:, o0:o1], wl[:, o1:o2], _pad_lanes(wl[:, o2:o3]), wl[:, o3:o4], wl[:, o4:])
        p = {"conv_w": conv_w[l], "conv_b": conv_b[l][None, :], "dt_bias": _pad_lanes(dt_bias[l][None, :]),
             "a_log": _pad_lanes(a_log[l][None, :]),
             "d_skip_e": jnp.repeat(d_ssd[l], SSD_HEAD_DIM)[None, :],
             "ssd_norm_w": ssd_norm_w[l][None, :], "head_expand": head_expand}
        ab, wb = _s5_params(s5_a_re[l].reshape(1, s5n), s5_a_im[l].reshape(1, s5n),
                            jnp.repeat(s5_log_dt[l], S5_STATE)[None, :],
                            _block_diag(jnp.transpose(s5_b_re[l], (0, 2, 1))),
                            _block_diag(jnp.transpose(s5_b_im[l], (0, 2, 1))))
        sp = {"ab": ab, "wb": wb,
              "c_re": _block_diag(jnp.transpose(s5_c_re[l], (0, 2, 1))).astype(BF16),
              "c_im": _block_diag(jnp.transpose(s5_c_im[l], (0, 2, 1))).astype(BF16),
              "d": s5_d[l][None, :], "w_glu": w_glu[l].astype(BF16)}
        lw = {"w_out_a": w_out_a[l].astype(BF16), "w_out": w_out[l].astype(BF16), "norm2_w": norm2_w[l][None, :]}
        ew = {"w1": w1[l].astype(BF16), "w3": w3[l].astype(BF16), "w2": w2[l].astype(BF16)}
        final = l == depth - 1
        fw = final_norm_w[None, :]

        mod_p = mod[l, :nb].reshape(nb, 6, d)
        mod_p_spec = lambda tm: pl.BlockSpec((None, 6, d), lambda i: (i // (seq // tm), 0, 0))
        z, xbc, dtr, u_tm, gate = _inproj(
            xp, mod_p, norm1_w[l][None, :], ws, tm=ROW_TILE, per_token=False, mod_spec=mod_p_spec(ROW_TILE),
            u_shape=(seq, nb * s5w),
            u_spec=pl.BlockSpec((ROW_TILE, s5w), lambda i: (i % tiles_per_seq, i // tiles_per_seq)))
        yn, nconv, nssm = _ssd_prompt(xbc, dtr, z, p, nb, seq)
        yb_tm, nre, nim = _s5(u_tm.reshape(seq * nb, s5w), zero_state, zero_state, sp, nb, S5_STEPS)
        x1, h2, idx, rank, wts, counts = _merge_router(
            yn, yb_tm.reshape(seq, nb * d), gate, xp, mod_p, lw, rw, tm=ROW_TILE, per_token=False,
            mod_spec=mod_p_spec(ROW_TILE),
            yb_spec=pl.BlockSpec((ROW_TILE, d), lambda i: (i % tiles_per_seq, i // tiles_per_seq)))
        xp = _moe(h2, idx, rank, counts, x1, wts, mod_p, fw, ew, tok_tile=MOE_TILE, per_token=False,
                  mod_spec_fn=mod_p_spec, final_norm=final)
        outs["ssm_p"].append(nssm)
        outs["conv_p"].append(nconv)
        outs["re_p"].append(nre.reshape(nb, S5_GROUPS, S5_STATE))
        outs["im_p"].append(nim.reshape(nb, S5_GROUPS, S5_STATE))

        mod_s = jnp.transpose(mod[l, nb:].reshape(ns, 6, d), (1, 0, 2))
        mod_s_spec = lambda tm: pl.BlockSpec((6, tm, d), lambda i: (0, i % (ns // tm), 0))
        z, xbc, dtr, u_s, gate = _inproj(
            xsm, mod_s, norm1_w[l][None, :], ws, tm=ns, per_token=True, mod_spec=mod_s_spec(ns),
            u_shape=(ts, s5w), u_spec=pl.BlockSpec((ns, s5w), lambda i: (i, 0)))
        conv0_tm = jnp.transpose(state_conv[l], (1, 0, 2)).reshape(k1 * ns, cdim)
        yn, nconv_tm, nssm = _ssd_sample(xbc, dtr, z, conv0_tm, state_ssm[l], p, ns, steps)
        yb, nre, nim = _s5(u_s, state_s5_re[l].reshape(ns, s5n), state_s5_im[l].reshape(ns, s5n), sp, ns, steps)
        x1, h2, idx, rank, wts, counts = _merge_router(
            yn, yb, gate, xsm, mod_s, lw, rw, tm=ns, per_token=True, mod_spec=mod_s_spec(ns),
            yb_spec=pl.BlockSpec((ns, d), lambda i: (i, 0)))
        xsm = _moe(h2, idx, rank, counts, x1, wts, mod_s, fw, ew, tok_tile=ns, per_token=True,
                   mod_spec_fn=mod_s_spec, final_norm=final)
        outs["ssm_s"].append(nssm)
        outs["conv_s"].append(jnp.transpose(nconv_tm.reshape(k1, ns, cdim), (1, 0, 2)))
        outs["re_s"].append(nre.reshape(ns, S5_GROUPS, S5_STATE))
        outs["im_s"].append(nim.reshape(ns, S5_GROUPS, S5_STATE))

    y_prompt = xp.reshape(nb, seq, d)
    y_sample = jnp.transpose(xsm.reshape(steps, ns, d), (1, 0, 2))
    st = lambda k: jnp.stack(outs[k])
    return (y_prompt, y_sample, st("ssm_p"), st("ssm_s"), st("conv_p"), st("conv_s"),
            st("re_p"), st("re_s"), st("im_p"), st("im_s"))
```

```python
import functools

import jax
import jax.numpy as jnp
from jax import lax
from jax.experimental import pallas as pl
from jax.experimental.pallas import tpu as pltpu

F32 = jnp.float32
BF16 = jnp.bfloat16
EPS = 1e-6

SSD_HEADS = 16
SSD_HEAD_DIM = 64
SSD_GROUPS = 2
SSD_STATE = 128
SSD_CONV = 4
SSD_CHUNK = 128
S5_GROUPS = 32
S5_GROUP_CH = 16
S5_STATE = 64
N_EXPERTS = 16
EXPERTS_PER_GROUP = 4
N_EXPERT_GROUPS = 4

LANES = 128
SUBLANES = 8
MXU_TILE = 256
VMEM_LIMIT_BYTES = 56 * 1024 * 1024

ROW_TILE = 512
MOE_TILE = 256
S5_STEPS = 64


def _cparams(sem):
    return pltpu.CompilerParams(dimension_semantics=sem, vmem_limit_bytes=VMEM_LIMIT_BYTES)


def _bdot(a, b):
    return jnp.dot(a.astype(BF16), b.astype(BF16), preferred_element_type=F32)


def _bdot_nt(a, b):
    return lax.dot_general(a.astype(BF16), b.astype(BF16), (((1,), (1,)), ((), ())),
                           preferred_element_type=F32)


def _split3(x):
    hi = x.astype(BF16)
    r1 = x - hi.astype(F32)
    mid = r1.astype(BF16)
    lo = (r1 - mid.astype(F32)).astype(BF16)
    return hi, mid, lo


def _dot3_lhs(x, m):
    hi, mid, lo = _split3(x)
    d = functools.partial(jnp.dot, preferred_element_type=F32)
    return d(hi, m) + d(mid, m) + d(lo, m)


def _dot3_rhs(m, x):
    hi, mid, lo = _split3(x)
    d = functools.partial(jnp.dot, preferred_element_type=F32)
    return d(m, hi) + d(m, mid) + d(m, lo)


def _sigmoid(x):
    return jax.nn.sigmoid(x)


def _silu(x):
    return x * _sigmoid(x)


def _softplus(x):
    return jnp.maximum(x, 0.0) + jnp.log1p(jnp.exp(-jnp.abs(x)))


def _gelu_tanh(x):
    c = 0.7978845608028654
    return 0.5 * x * (1.0 + jnp.tanh(c * (x + 0.044715 * (x * x * x))))


def _rms(x):
    return x * lax.rsqrt(jnp.mean(x * x, axis=-1, keepdims=True) + EPS)


def _ada_kernel(c_ref, w_ref, b_ref, o_ref):
    o_ref[0] = _bdot(_silu(c_ref[...]), w_ref[0]) + b_ref[0]


def _ada_mod(c_all, w_ada, b_ada):
    depth, d, n = w_ada.shape
    nseq = c_all.shape[0]
    tn = 1536
    return pl.pallas_call(
        _ada_kernel,
        grid=(depth, n // tn),
        in_specs=[pl.BlockSpec((nseq, d), lambda l, j: (0, 0)),
                  pl.BlockSpec((1, d, tn), lambda l, j: (l, 0, j)),
                  pl.BlockSpec((1, 1, tn), lambda l, j: (l, 0, j))],
        out_specs=pl.BlockSpec((1, nseq, tn), lambda l, j: (l, 0, j)),
        out_shape=jax.ShapeDtypeStruct((depth, nseq, n), F32),
        compiler_params=_cparams(("arbitrary", "arbitrary")),
        name="ada_mod",
    )(c_all, w_ada, b_ada.reshape(depth, 1, n))


def _mod_rows(mod_ref, j, per_token):
    if per_token:
        return mod_ref[j]
    return mod_ref[j:j + 1, :]


def _inproj_kernel(x_ref, mod_ref, nw_ref, wz_ref, wx_ref, wdt_ref, wu_ref, wg_ref,
                   z_ref, xbc_ref, dt_ref, u_ref, g_ref, *, per_token):
    sh = _mod_rows(mod_ref, 0, per_token)
    sc = _mod_rows(mod_ref, 1, per_token)
    h = (_rms(x_ref[...]) * nw_ref[...] * (1.0 + sc) + sh).astype(BF16)
    d = functools.partial(jnp.dot, preferred_element_type=F32)
    z_ref[...] = d(h, wz_ref[...])
    xbc_ref[...] = d(h, wx_ref[...])
    dt_ref[...] = d(h, wdt_ref[...])
    u_ref[...] = d(h, wu_ref[...])
    g_ref[...] = d(h, wg_ref[...])


def _const_spec(shape):
    nd = len(shape)
    return pl.BlockSpec(shape, lambda *_: (0,) * nd)


def _inproj(x, mod, nw, ws, *, tm, per_token, mod_spec):
    t, d = x.shape
    wz, wx, wdt, wu, wg = ws
    row = lambda n: pl.BlockSpec((tm, n), lambda i: (i, 0))
    return pl.pallas_call(
        functools.partial(_inproj_kernel, per_token=per_token),
        grid=(t // tm,),
        in_specs=[row(d), mod_spec, _const_spec((1, d)),
                  _const_spec(wz.shape), _const_spec(wx.shape), _const_spec(wdt.shape),
                  _const_spec(wu.shape), _const_spec(wg.shape)],
        out_specs=[row(wz.shape[1]), row(wx.shape[1]), row(wdt.shape[1]), row(wu.shape[1]), row(wg.shape[1])],
        out_shape=[jax.ShapeDtypeStruct((t, wz.shape[1]), F32),
                   jax.ShapeDtypeStruct((t, wx.shape[1]), F32),
                   jax.ShapeDtypeStruct((t, wdt.shape[1]), F32),
                   jax.ShapeDtypeStruct((t, wu.shape[1]), F32),
                   jax.ShapeDtypeStruct((t, wg.shape[1]), F32)],
        compiler_params=_cparams(("arbitrary",)),
        name="inproj",
    )(x, mod, nw, wz, wx, wdt, wu, wg)


def _gated_group_norm(y, z, nw):
    y = y * _silu(z)
    half = y.shape[1] // SSD_GROUPS
    parts = [_rms(y[:, g * half:(g + 1) * half]) for g in range(SSD_GROUPS)]
    return jnp.concatenate(parts, axis=1) * nw


def _ssd_prompt_kernel(xbc_ref, dt_ref, z_ref, cw_ref, cb_ref, dtb_ref, alog_ref, dsk_ref, nw_ref,
                       e_ref, yn_ref, conv_ref, ssm_ref, ext_ref, st_ref):
    c = pl.program_id(1)
    q = SSD_CHUNK
    d_in = SSD_HEADS * SSD_HEAD_DIM
    gw = SSD_STATE
    hpg = SSD_HEADS // SSD_GROUPS
    grows = hpg * SSD_HEAD_DIM

    @pl.when(c == 0)
    def _():
        ext_ref[0:SUBLANES, :] = jnp.zeros((SUBLANES, ext_ref.shape[1]), F32)
        st_ref[...] = jnp.zeros(st_ref.shape, F32)

    xbc = xbc_ref[...]
    ext_ref[SUBLANES:SUBLANES + q, :] = xbc
    acc = cb_ref[...] + cw_ref[SSD_CONV - 1:SSD_CONV, :] * xbc
    for j in range(1, SSD_CONV):
        acc = acc + cw_ref[SSD_CONV - 1 - j:SSD_CONV - j, :] * ext_ref[pl.ds(SUBLANES - j, q), :]
    ext_ref[0:SUBLANES, :] = xbc[q - SUBLANES:q, :]
    v = _silu(acc)
    xs = v[:, :d_in]
    bm = v[:, d_in:d_in + SSD_GROUPS * gw]
    cm = v[:, d_in + SSD_GROUPS * gw:]

    dt = _softplus(dt_ref[...] + dtb_ref[...])
    a = -jnp.exp(alog_ref[...])
    da = dt * a
    rid = lax.broadcasted_iota(jnp.int32, (q, q), 0)
    cid = lax.broadcasted_iota(jnp.int32, (q, q), 1)
    causal = rid >= cid
    tri = jnp.where(causal, 1.0, 0.0).astype(BF16)
    a_cs = _dot3_rhs(tri, da)
    a_cs_t = a_cs.T
    a_last = a_cs[q - 1:q, :]
    e = e_ref[...]
    x_dt = xs * _dot3_lhs(dt, e)
    x_end = xs * _dot3_lhs(jnp.exp(a_last - a_cs) * dt, e)
    eacs_e = _dot3_lhs(jnp.exp(a_cs), e)
    x_end_t = x_end.T.astype(BF16)
    lane = lax.broadcasted_iota(jnp.int32, (q, 2 * SSD_HEAD_DIM), 1)
    first = lane < SSD_HEAD_DIM

    y_diag = []
    y_off = []
    for g in range(SSD_GROUPS):
        bg = bm[:, g * gw:(g + 1) * gw].astype(BF16)
        cg = cm[:, g * gw:(g + 1) * gw].astype(BF16)
        cb = _bdot_nt(cg, bg)
        sg = st_ref[g * grows:(g + 1) * grows, :]
        y_off.append(_bdot_nt(cg, sg))
        for pair in range(hpg // 2):
            ms = []
            for r in (2 * pair, 2 * pair + 1):
                h = g * hpg + r
                diff = a_cs[:, h:h + 1] - a_cs_t[h:h + 1, :]
                decay = jnp.where(causal, jnp.exp(jnp.where(causal, diff, 0.0)), 0.0)
                ms.append((cb * decay).astype(BF16))
            h0 = g * hpg + 2 * pair
            xp = x_dt[:, h0 * SSD_HEAD_DIM:(h0 + 2) * SSD_HEAD_DIM]
            rhs = jnp.concatenate([jnp.where(first, xp, 0.0), jnp.where(first, 0.0, xp)], axis=0)
            y_diag.append(jnp.dot(jnp.concatenate(ms, axis=1), rhs.astype(BF16), preferred_element_type=F32))
        contrib = jnp.dot(x_end_t[g * grows:(g + 1) * grows, :], bg, preferred_element_type=F32)
        for r in range(hpg):
            h = g * hpg + r
            rows = slice(h * SSD_HEAD_DIM, (h + 1) * SSD_HEAD_DIM)
            dec = jnp.exp(a_cs_t[h:h + 1, q - 1:q])
            st_ref[rows, :] = st_ref[rows, :] * dec + contrib[r * SSD_HEAD_DIM:(r + 1) * SSD_HEAD_DIM, :]

    y = (jnp.concatenate(y_diag, axis=1) + jnp.concatenate(y_off, axis=1) * eacs_e
         + dsk_ref[...] * xs)
    yn_ref[...] = _gated_group_norm(y, z_ref[...], nw_ref[...]).astype(BF16)

    @pl.when(c == pl.num_programs(1) - 1)
    def _():
        conv_ref[0] = ext_ref[pl.ds(SUBLANES - (SSD_CONV - 1), SSD_CONV - 1), :]
        ssm_ref[0] = st_ref[...].reshape(SSD_HEADS, SSD_HEAD_DIM, SSD_STATE)


def _ssd_prompt(xbc, dt_raw, z, p, nb, seq):
    t, cdim = xbc.shape
    nc = seq // SSD_CHUNK
    q = SSD_CHUNK
    d_in = z.shape[1]
    row = lambda n: pl.BlockSpec((q, n), lambda b, c: (b * nc + c, 0))
    return pl.pallas_call(
        _ssd_prompt_kernel,
        grid=(nb, nc),
        in_specs=[row(cdim), row(LANES), row(d_in),
                  _const_spec((SSD_CONV, cdim)), _const_spec((1, cdim)), _const_spec((1, LANES)),
                  _const_spec((1, LANES)), _const_spec((1, d_in)), _const_spec((1, d_in)),
                  _const_spec((LANES, d_in))],
        out_specs=[row(d_in),
                   pl.BlockSpec((1, SSD_CONV - 1, cdim), lambda b, c: (b, 0, 0)),
                   pl.BlockSpec((1, SSD_HEADS, SSD_HEAD_DIM, SSD_STATE), lambda b, c: (b, 0, 0, 0))],
        out_shape=[jax.ShapeDtypeStruct((t, d_in), BF16),
                   jax.ShapeDtypeStruct((nb, SSD_CONV - 1, cdim), F32),
                   jax.ShapeDtypeStruct((nb, SSD_HEADS, SSD_HEAD_DIM, SSD_STATE), F32)],
        scratch_shapes=[pltpu.VMEM((SUBLANES + q, cdim), F32),
                        pltpu.VMEM((d_in, SSD_STATE), F32)],
        compiler_params=_cparams(("arbitrary", "arbitrary")),
        name="ssd_prompt",
    )(xbc, dt_raw, z, p["conv_w"], p["conv_b"], p["dt_bias"], p["a_log"], p["d_skip_e"], p["ssd_norm_w"],
      p["head_expand"])


def _ssd_sample_a_kernel(xbc_ref, conv0_ref, dt_ref, cw_ref, cb_ref, dtb_ref, alog_ref, dsk_ref, e_ref,
                         ydiag_ref, eacs_ref, wt_ref, bs_ref, c8_ref, dec_ref, nconv_ref, *, steps, nseq):
    d_in = SSD_HEADS * SSD_HEAD_DIM
    gw = SSD_STATE
    hpg = SSD_HEADS // SSD_GROUPS
    k1 = SSD_CONV - 1
    slab = lambda ref, i: ref[i * nseq:(i + 1) * nseq, :]
    xp = [slab(conv0_ref, i) for i in range(k1)] + [slab(xbc_ref, i) for i in range(steps)]
    for i in range(k1):
        nconv_ref[i * nseq:(i + 1) * nseq, :] = xp[steps + i]
    e = e_ref[...]
    a = -jnp.exp(alog_ref[...])
    head = lax.broadcasted_iota(jnp.int32, (nseq, LANES), 1)
    xs, bm, cm, dt, a_cs = [], [], [], [], []
    run = jnp.zeros((nseq, LANES), F32)
    for t in range(steps):
        acc = cb_ref[...]
        for k in range(SSD_CONV):
            acc = acc + cw_ref[k:k + 1, :] * xp[t + k]
        v = _silu(acc)
        xs.append(v[:, :d_in])
        bm.append(v[:, d_in:d_in + SSD_GROUPS * gw])
        cm.append(v[:, d_in + SSD_GROUPS * gw:])
        dt.append(_softplus(slab(dt_ref, t) + dtb_ref[...]))
        run = run + dt[t] * a
        a_cs.append(run)
    dec_ref[...] = jnp.exp(a_cs[steps - 1])
    for t in range(steps):
        y = dsk_ref[...] * xs[t]
        for s in range(t + 1):
            cbs = [jnp.sum(cm[t][:, g * gw:(g + 1) * gw] * bm[s][:, g * gw:(g + 1) * gw],
                           axis=-1, keepdims=True) for g in range(SSD_GROUPS)]
            coef = jnp.exp(a_cs[t] - a_cs[s]) * dt[s] * jnp.where(head < hpg, cbs[0], cbs[1])
            y = y + _dot3_lhs(coef, e) * xs[s]
        ydiag_ref[t * nseq:(t + 1) * nseq, :] = y
        eacs_ref[t * nseq:(t + 1) * nseq, :] = _dot3_lhs(jnp.exp(a_cs[t]), e)
        w = xs[t] * _dot3_lhs(jnp.exp(a_cs[steps - 1] - a_cs[t]) * dt[t], e)
        wt_ref[:, t * nseq:(t + 1) * nseq] = w.T.astype(BF16)
        bs_ref[t * nseq:(t + 1) * nseq, :] = bm[t]
        for g in range(SSD_GROUPS):
            j = g * steps + t
            c8_ref[j * nseq:(j + 1) * nseq, :] = cm[t][:, g * gw:(g + 1) * gw]


def _ssd_sample_b_kernel(dec_ref, h0_ref, wt_ref, bs_ref, c8_ref, *rest, steps, nseq, layer):
    yoff_ref, ssm_ref = rest[-2:]
    if layer:
        ssm_ref[0:layer] = rest[0][...]
    b = pl.program_id(0)
    hpg = SSD_HEADS // SSD_GROUPS
    grows = hpg * SSD_HEAD_DIM
    gw = SSD_STATE
    h0 = h0_ref[0].reshape(SSD_HEADS * SSD_HEAD_DIM, SSD_STATE)
    c8 = c8_ref[pl.ds(b, SSD_GROUPS * steps, stride=nseq), :]
    yoff_ref[0] = _bdot_nt(c8, h0)
    rid = lax.broadcasted_iota(jnp.int32, (steps * nseq, gw), 0)
    mine = rid == b
    for t in range(1, steps):
        mine = jnp.logical_or(mine, rid == b + t * nseq)
    for g in range(SSD_GROUPS):
        bsel = jnp.where(mine, bs_ref[:, g * gw:(g + 1) * gw], 0.0).astype(BF16)
        contrib = jnp.dot(wt_ref[g * grows:(g + 1) * grows, :], bsel, preferred_element_type=F32)
        for r in range(hpg):
            h = g * hpg + r
            rows = slice(h * SSD_HEAD_DIM, (h + 1) * SSD_HEAD_DIM)
            ssm_ref[layer, 0, h] = (h0[rows, :] * dec_ref[b, h]
                                    + contrib[r * SSD_HEAD_DIM:(r + 1) * SSD_HEAD_DIM, :])


def _ssd_sample_c_kernel(ydiag_ref, yoff_ref, eacs_ref, z_ref, nw_ref, yn_ref, *, steps, nseq):
    d_in = SSD_HEADS * SSD_HEAD_DIM
    half = d_in // SSD_GROUPS
    for t in range(steps):
        lo = yoff_ref[:, t * d_in:t * d_in + half]
        hi = yoff_ref[:, (steps + t) * d_in + half:(steps + t + 1) * d_in]
        rows = slice(t * nseq, (t + 1) * nseq)
        y = ydiag_ref[rows, :] + jnp.concatenate([lo, hi], axis=1) * eacs_ref[rows, :]
        yn_ref[rows, :] = _gated_group_norm(y, z_ref[rows, :], nw_ref[...]).astype(BF16)


def _ssd_sample(xbc, dt_raw, z, conv0_tm, ssm_all, ssm_done, layer, p, nseq, steps):
    t, cdim = xbc.shape
    d_in = z.shape[1]
    k1 = SSD_CONV - 1
    nj = SSD_GROUPS * steps
    a_out = pl.pallas_call(
        functools.partial(_ssd_sample_a_kernel, steps=steps, nseq=nseq),
        out_shape=[jax.ShapeDtypeStruct((t, d_in), F32),
                   jax.ShapeDtypeStruct((t, d_in), F32),
                   jax.ShapeDtypeStruct((d_in, t), BF16),
                   jax.ShapeDtypeStruct((t, SSD_GROUPS * SSD_STATE), F32),
                   jax.ShapeDtypeStruct((nj * nseq, SSD_STATE), F32),
                   jax.ShapeDtypeStruct((nseq, LANES), F32),
                   jax.ShapeDtypeStruct((k1 * nseq, cdim), F32)],
        compiler_params=pltpu.CompilerParams(vmem_limit_bytes=VMEM_LIMIT_BYTES),
        name="ssd_sample_a",
    )(xbc, conv0_tm, dt_raw, p["conv_w"], p["conv_b"], p["dt_bias"], p["a_log"], p["d_skip_e"],
      p["head_expand"])
    ydiag, eacs_e, wt, bs, c8, dec, nconv = a_out
    state_blk = (SSD_HEADS, SSD_HEAD_DIM, SSD_STATE)
    operands = [dec[:, :SSD_HEADS], ssm_all, wt, bs, c8]
    in_specs = [pl.BlockSpec(memory_space=pltpu.SMEM),
                pl.BlockSpec((None, 1) + state_blk, lambda b: (layer, b, 0, 0, 0)),
                _const_spec(wt.shape), _const_spec(bs.shape), _const_spec(c8.shape)]
    if layer:
        operands.append(ssm_done)
        in_specs.append(pl.BlockSpec((layer, 1) + state_blk, lambda b: (0, b, 0, 0, 0)))
    yoff, nssm = pl.pallas_call(
        functools.partial(_ssd_sample_b_kernel, steps=steps, nseq=nseq, layer=layer),
        grid=(nseq,),
        in_specs=in_specs,
        out_specs=[pl.BlockSpec((1, nj, d_in), lambda b: (b, 0, 0)),
                   pl.BlockSpec((layer + 1, 1) + state_blk, lambda b: (0, b, 0, 0, 0))],
        out_shape=[jax.ShapeDtypeStruct((nseq, nj, d_in), F32),
                   jax.ShapeDtypeStruct((layer + 1, nseq) + state_blk, F32)],
        compiler_params=_cparams(("arbitrary",)),
        name="ssd_sample_b",
    )(*operands)
    yn = pl.pallas_call(
        functools.partial(_ssd_sample_c_kernel, steps=steps, nseq=nseq),
        out_shape=jax.ShapeDtypeStruct((t, d_in), BF16),
        compiler_params=pltpu.CompilerParams(vmem_limit_bytes=VMEM_LIMIT_BYTES),
        name="ssd_sample_c",
    )(ydiag, yoff.reshape(nseq, nj * d_in), eacs_e, z, p["ssd_norm_w"])
    return yn, nconv, nssm


def _s5_param_kernel(are_ref, aim_ref, ldt_ref, bre_ref, bim_ref, cre_ref, cim_ref,
                     ab_ref, wb_ref, cret_ref, cimt_ref):
    n = are_ref.shape[1]
    w = wb_ref.shape[0]
    a_re = are_ref[...]
    a_im = aim_ref[...]
    dt = jnp.exp(ldt_ref[...])
    mag = jnp.exp(dt * a_re)
    ab_re = mag * jnp.cos(dt * a_im)
    ab_im = mag * jnp.sin(dt * a_im)
    den = a_re * a_re + a_im * a_im
    f_re = ((ab_re - 1.0) * a_re + ab_im * a_im) / den
    f_im = (ab_im * a_re - (ab_re - 1.0) * a_im) / den
    ab_ref[0:1, :] = ab_re
    ab_ref[1:2, :] = ab_im
    b_re = bre_ref[...]
    b_im = bim_ref[...]
    row_g = lax.shift_right_logical(lax.broadcasted_iota(jnp.int32, (w, n), 0), S5_GROUP_CH.bit_length() - 1)
    col_g = lax.shift_right_logical(lax.broadcasted_iota(jnp.int32, (w, n), 1), S5_STATE.bit_length() - 1)
    same = row_g == col_g

    def spread(rows):
        return jnp.where(same, jnp.concatenate([rows] * S5_GROUPS, axis=0), 0.0).astype(BF16)

    wb_ref[:, 0:n] = spread(f_re * b_re - f_im * b_im)
    wb_ref[:, n:2 * n] = spread(f_re * b_im + f_im * b_re)
    cret_ref[...] = spread(cre_ref[...])
    cimt_ref[...] = spread(cim_ref[...])


def _s5_params(a_re, a_im, ldt, bt_re, bt_im, ct_re, ct_im):
    n = a_re.shape[1]
    w = S5_GROUPS * S5_GROUP_CH
    return pl.pallas_call(
        _s5_param_kernel,
        out_shape=[jax.ShapeDtypeStruct((2, n), F32),
                   jax.ShapeDtypeStruct((w, 2 * n), BF16),
                   jax.ShapeDtypeStruct((w, n), BF16),
                   jax.ShapeDtypeStruct((w, n), BF16)],
        compiler_params=pltpu.CompilerParams(vmem_limit_bytes=VMEM_LIMIT_BYTES),
        name="s5_params",
    )(a_re, a_im, ldt, bt_re, bt_im, ct_re, ct_im)


def _s5_kernel(u_ref, h0re_ref, h0im_ref, ab_ref, wb_ref, cret_ref, cimt_ref, d_ref, wglu_ref,
               yb_ref, nre_ref, nim_ref, xh_ref, st_ref, *, rows_per_step, steps, batch_major):
    i = pl.program_id(0)
    n = ab_ref.shape[1]
    w = wb_ref.shape[0]
    r = rows_per_step

    @pl.when(i == 0)
    def _():
        st_ref[0] = h0re_ref[...]
        st_ref[1] = h0im_ref[...]

    if batch_major:
        u = pltpu.einshape("btd->(tb)d", u_ref[...])
    else:
        u = u_ref[...]
    ub = u.astype(BF16)
    for j in range(2 * n // MXU_TILE):
        c0 = (j * MXU_TILE) % n
        k0 = (c0 // S5_STATE * S5_GROUP_CH) // LANES * LANES
        cols = slice(j * MXU_TILE, (j + 1) * MXU_TILE)
        xh_ref[:, cols] = jnp.dot(ub[:, k0:k0 + LANES], wb_ref[k0:k0 + LANES, cols], preferred_element_type=F32)
    a_re = ab_ref[0:1, :]
    a_im = ab_ref[1:2, :]

    def step(t, carry):
        h_re, h_im = carry
        rows = pl.ds(pl.multiple_of(t * r, r), r)
        n_re = a_re * h_re - a_im * h_im + xh_ref[rows, 0:n]
        n_im = a_re * h_im + a_im * h_re + xh_ref[rows, n:2 * n]
        xh_ref[rows, 0:n] = n_re
        xh_ref[rows, n:2 * n] = n_im
        return n_re, n_im

    carry = (st_ref[0], st_ref[1])
    if steps <= 8:
        for t in range(steps):
            carry = step(t, carry)
    else:
        carry = lax.fori_loop(0, steps, step, carry, unroll=2)
    st_ref[0] = carry[0]
    st_ref[1] = carry[1]

    n_ct = w // MXU_TILE
    ys = []
    for j in range(n_ct):
        rows = slice(j * MXU_TILE, (j + 1) * MXU_TILE)
        ks = slice(j * (n // n_ct), (j + 1) * (n // n_ct))
        ks_im = slice(n + j * (n // n_ct), n + (j + 1) * (n // n_ct))
        ys.append(_bdot_nt(xh_ref[:, ks], cret_ref[rows, ks]) - _bdot_nt(xh_ref[:, ks_im], cimt_ref[rows, ks]))
    y = jnp.concatenate(ys, axis=1) + d_ref[...] * u
    pre = _bdot(_gelu_tanh(y), wglu_ref[...])
    half = pre.shape[1] // 2
    yb = pre[:, :half] * _sigmoid(pre[:, half:])
    if batch_major:
        yb_ref[...] = pltpu.einshape("(tb)d->btd", yb, b=r)
    else:
        yb_ref[...] = yb

    @pl.when(i == pl.num_programs(0) - 1)
    def _():
        nre_ref[...] = carry[0]
        nim_ref[...] = carry[1]


def _s5(u, h0_re, h0_im, sp, rows_per_step, steps, batch_major):
    n = sp["ab"].shape[1]
    w = u.shape[-1]
    tm = rows_per_step * steps
    d_model = sp["w_glu"].shape[1] // 2
    if batch_major:
        nb, seq, _ = u.shape
        grid = (seq // steps,)
        u_spec = pl.BlockSpec((nb, steps, w), lambda i: (0, i, 0))
        yb_spec = pl.BlockSpec((nb, steps, d_model), lambda i: (0, i, 0))
        yb_shape = (nb, seq, d_model)
    else:
        grid = (u.shape[0] // tm,)
        u_spec = pl.BlockSpec((tm, w), lambda i: (i, 0))
        yb_spec = pl.BlockSpec((tm, d_model), lambda i: (i, 0))
        yb_shape = (u.shape[0], d_model)
    return pl.pallas_call(
        functools.partial(_s5_kernel, rows_per_step=rows_per_step, steps=steps, batch_major=batch_major),
        grid=grid,
        in_specs=[u_spec,
                  _const_spec(h0_re.shape), _const_spec(h0_im.shape), _const_spec(sp["ab"].shape),
                  _const_spec(sp["wb"].shape), _const_spec(sp["c_re_t"].shape), _const_spec(sp["c_im_t"].shape),
                  _const_spec((1, w)), _const_spec(sp["w_glu"].shape)],
        out_specs=[yb_spec, _const_spec(h0_re.shape), _const_spec(h0_im.shape)],
        out_shape=[jax.ShapeDtypeStruct(yb_shape, F32),
                   jax.ShapeDtypeStruct(h0_re.shape, F32),
                   jax.ShapeDtypeStruct(h0_im.shape, F32)],
        scratch_shapes=[pltpu.VMEM((tm, 2 * n), F32),
                        pltpu.VMEM((2, rows_per_step, n), F32)],
        compiler_params=_cparams(("arbitrary",)),
        name="s5_scan",
    )(u, h0_re, h0_im, sp["ab"], sp["wb"], sp["c_re_t"], sp["c_im_t"], sp["d"], sp["w_glu"])


def _merge_router_kernel(yn_ref, yb_ref, gate_ref, x_ref, mod_ref, woa_ref, wo_ref, nw_ref, wr_ref, br_ref,
                         x1_ref, h2_ref, idx_ref, rank_ref, wts_ref, cnt_ref, run_ref, *, per_token):
    i = pl.program_id(0)
    tm, d = x_ref.shape

    @pl.when(i == 0)
    def _():
        run_ref[...] = jnp.zeros(run_ref.shape, F32)

    ya = jnp.dot(yn_ref[...], woa_ref[...], preferred_element_type=F32)
    gate = _sigmoid(gate_ref[...])
    merged = gate[:, :d] * ya + gate[:, d:] * yb_ref[...]
    mix = _bdot(merged, wo_ref[...])
    g1 = _mod_rows(mod_ref, 2, per_token)
    sh2 = _mod_rows(mod_ref, 3, per_token)
    sc2 = _mod_rows(mod_ref, 4, per_token)
    x1 = x_ref[...] + g1 * mix
    x1_ref[...] = x1
    h2 = _rms(x1) * nw_ref[...] * (1.0 + sc2) + sh2
    h2_ref[...] = h2

    wr = wr_ref[...]
    wr_hi = wr.astype(BF16)
    wr_lo = (wr - wr_hi.astype(F32)).astype(BF16)
    h2_hi = h2.astype(BF16)
    h2_lo = (h2 - h2_hi.astype(F32)).astype(BF16)
    d = functools.partial(jnp.dot, preferred_element_type=F32)
    logits = d(h2_hi, wr_hi) + d(h2_lo, wr_hi) + d(h2_hi, wr_lo)
    lt = logits.T
    erow = lax.broadcasted_iota(jnp.int32, lt.shape, 0)
    lt = jnp.where(erow < N_EXPERTS, lt, -jnp.inf)
    ex = jnp.exp(lt - jnp.max(lt, axis=0, keepdims=True))
    scores = ex / jnp.sum(ex, axis=0, keepdims=True)
    sel = scores + br_ref[...]
    s = [sel[e:e + 1, :] for e in range(N_EXPERTS)]
    p = [scores[e:e + 1, :] for e in range(N_EXPERTS)]

    def group_top2_sum(vals):
        best = None
        for a in range(len(vals)):
            for b in range(a + 1, len(vals)):
                pair = vals[a] + vals[b]
                best = pair if best is None else jnp.maximum(best, pair)
        return best

    gs = [group_top2_sum(s[EXPERTS_PER_GROUP * g:EXPERTS_PER_GROUP * (g + 1)]) for g in range(N_EXPERT_GROUPS)]
    best = gs[0]
    bg = jnp.zeros(best.shape, jnp.int32)
    for g in range(1, N_EXPERT_GROUPS):
        better = gs[g] > best
        bg = jnp.where(better, g, bg)
        best = jnp.where(better, gs[g], best)

    def pick(rows, j):
        out = rows[j]
        for g in range(1, N_EXPERT_GROUPS):
            out = jnp.where(bg == g, rows[EXPERTS_PER_GROUP * g + j], out)
        return out

    cs = [pick(s, j) for j in range(EXPERTS_PER_GROUP)]
    cp = [pick(p, j) for j in range(EXPERTS_PER_GROUP)]

    def argmax_first(vals, skip):
        bv = None
        bi = None
        bw = None
        for j in range(len(vals)):
            v = vals[j] if skip is None else jnp.where(skip == j, -jnp.inf, vals[j])
            if bv is None:
                bv, bi, bw = v, jnp.zeros(v.shape, jnp.int32), cp[j]
            else:
                better = v > bv
                bi = jnp.where(better, j, bi)
                bw = jnp.where(better, cp[j], bw)
                bv = jnp.where(better, v, bv)
        return bi, bw

    i1, w1 = argmax_first(cs, None)
    i2, w2 = argmax_first(cs, i1)
    wsum = w1 + w2
    e1 = bg * EXPERTS_PER_GROUP + i1
    e2 = bg * EXPERTS_PER_GROUP + i2
    idx_ref[0:1, :] = e1
    idx_ref[1:2, :] = e2
    wrow = lax.broadcasted_iota(jnp.int32, lt.shape, 0)
    wmat = jnp.where(wrow == 0, w1 / wsum, jnp.where(wrow == 1, w2 / wsum, 0.0))
    wts_ref[...] = wmat.T

    oh1 = jnp.where(erow == e1, 1.0, 0.0)
    oh2 = jnp.where(erow == e2, 1.0, 0.0)
    both = oh1 + oh2
    ta = lax.broadcasted_iota(jnp.int32, (tm, tm), 0)
    tb = lax.broadcasted_iota(jnp.int32, (tm, tm), 1)
    earlier = jnp.where(ta < tb, 1.0, 0.0).astype(BF16)
    before = jnp.dot(both.astype(BF16), earlier, preferred_element_type=F32) + run_ref[...]
    rank_ref[0:1, :] = jnp.sum(oh1 * before, axis=0, keepdims=True).astype(jnp.int32)
    rank_ref[1:2, :] = jnp.sum(oh2 * before, axis=0, keepdims=True).astype(jnp.int32)
    run_ref[...] = run_ref[...] + jnp.sum(both, axis=1, keepdims=True)

    @pl.when(i == pl.num_programs(0) - 1)
    def _():
        cnt_ref[...] = run_ref[...]


def _merge_router(yn, yb, gate, x, mod, lw, rw, *, tm, per_token, mod_spec):
    t, d = x.shape
    row = lambda n: pl.BlockSpec((tm, n), lambda i: (i, 0))
    pair = pl.BlockSpec((2, tm), lambda i: (0, i))
    return pl.pallas_call(
        functools.partial(_merge_router_kernel, per_token=per_token),
        grid=(t // tm,),
        in_specs=[row(d), row(d), row(2 * d), row(d), mod_spec,
                  _const_spec((d, d)), _const_spec((d, d)), _const_spec((1, d)),
                  _const_spec((d, LANES)), _const_spec((LANES, 1))],
        out_specs=[row(d), row(d), pair, pair, row(LANES), _const_spec((LANES, 1))],
        out_shape=[jax.ShapeDtypeStruct((t, d), F32),
                   jax.ShapeDtypeStruct((t, d), F32),
                   jax.ShapeDtypeStruct((2, t), jnp.int32),
                   jax.ShapeDtypeStruct((2, t), jnp.int32),
                   jax.ShapeDtypeStruct((t, LANES), F32),
                   jax.ShapeDtypeStruct((LANES, 1), F32)],
        scratch_shapes=[pltpu.VMEM((LANES, 1), F32)],
        compiler_params=_cparams(("arbitrary",)),
        name="merge_router",
    )(yn, yb, gate, x, mod, lw["w_out_a"], lw["w_out"], lw["norm2_w"], rw["w_router"], rw["b_router"])


def _plan_kernel(cnt_ref, idx_ref, rank_ref, dest_ref, te_ref, ends_ref, nv_ref, *, tile, n_tiles):
    shift = tile.bit_length() - 1
    run = jnp.int32(0)
    starts = []
    for e in range(N_EXPERTS):
        starts.append(run)
        run = run + (((cnt_ref[e] + (tile - 1)) >> shift) << shift)
        ends_ref[e] = run
    n_valid = run >> shift
    nv_ref[0] = n_valid
    idx = idx_ref[...]
    dest = rank_ref[...]
    for e in range(N_EXPERTS):
        dest = dest + jnp.where(idx == e, starts[e], 0)
    dest_ref[...] = dest

    def tile_owner(j, carry):
        pos = jnp.minimum(j, n_valid - 1) * tile
        owner = jnp.int32(0)
        for e in range(N_EXPERTS - 1):
            owner = owner + jnp.where(ends_ref[e] <= pos, 1, 0)
        te_ref[j] = owner
        return carry

    lax.fori_loop(0, n_tiles, tile_owner, 0)


def _plan(cnt, idx, rank, tile, n_tiles):
    smem = pl.BlockSpec(memory_space=pltpu.SMEM)
    vmem = pl.BlockSpec(memory_space=pltpu.VMEM)
    return pl.pallas_call(
        functools.partial(_plan_kernel, tile=tile, n_tiles=n_tiles),
        in_specs=[smem, vmem, vmem],
        out_specs=[vmem, smem, smem, smem],
        out_shape=[jax.ShapeDtypeStruct(idx.shape, jnp.int32),
                   jax.ShapeDtypeStruct((n_tiles,), jnp.int32),
                   jax.ShapeDtypeStruct((N_EXPERTS,), jnp.int32),
                   jax.ShapeDtypeStruct((1,), jnp.int32)],
        name="moe_plan",
    )(cnt, idx, rank)


def _row_copy(src_ref, src_row, dst_ref, dst_row, sem):
    return pltpu.make_async_copy(src_ref.at[pl.ds(src_row, 1), :], dst_ref.at[pl.ds(dst_row, 1), :], sem)


def _dispatch_kernel(ends_ref, dest_ref, h_ref, out_ref, zero_ref, sem, zsem, *, tile):
    tm = h_ref.shape[0]

    @pl.when(pl.program_id(0) == 0)
    def _():
        zero_ref[...] = jnp.zeros(zero_ref.shape, F32)

        def clear(start):
            return pltpu.make_async_copy(zero_ref, out_ref.at[pl.ds(pl.multiple_of(start, tile), tile), :], zsem)

        def used(e):
            return ends_ref[e] > (ends_ref[e - 1] if e else 0)

        def unused_tiles(fn):
            n_valid = ends_ref[N_EXPERTS - 1] >> (tile.bit_length() - 1)
            lax.fori_loop(n_valid, out_ref.shape[0] // tile, lambda j, c: (fn(clear(j * tile)), c)[1], 0)

        for e in range(N_EXPERTS):
            pl.when(used(e))(lambda e=e: clear(ends_ref[e] - tile).start())
        unused_tiles(lambda cp: cp.start())
        for e in range(N_EXPERTS):
            pl.when(used(e))(lambda e=e: clear(ends_ref[e] - tile).wait())
        unused_tiles(lambda cp: cp.wait())

    def issue(t, carry):
        for k in range(2):
            _row_copy(h_ref, t, out_ref, dest_ref[k, t], sem).start()
        return carry

    def drain(t, carry):
        for k in range(2):
            _row_copy(h_ref, t, out_ref, dest_ref[k, t], sem).wait()
        return carry

    lax.fori_loop(0, tm, issue, 0, unroll=8)
    lax.fori_loop(0, tm, drain, 0, unroll=8)


def _dispatch(ends, dest, h2, n_rows, tm):
    t, d = h2.shape
    return pl.pallas_call(
        functools.partial(_dispatch_kernel, tile=MOE_TILE),
        grid=(t // tm,),
        in_specs=[pl.BlockSpec(memory_space=pltpu.SMEM),
                  pl.BlockSpec((2, tm), lambda i: (0, i), memory_space=pltpu.SMEM),
                  pl.BlockSpec((tm, d), lambda i: (i, 0))],
        out_specs=pl.BlockSpec(memory_space=pl.ANY),
        out_shape=jax.ShapeDtypeStruct((n_rows, d), F32),
        scratch_shapes=[pltpu.VMEM((MOE_TILE, d), F32), pltpu.SemaphoreType.DMA(()), pltpu.SemaphoreType.DMA(())],
        compiler_params=_cparams(("arbitrary",)),
        name="moe_dispatch",
    )(ends, dest, h2)


def _expert_kernel(te_ref, nv_ref, x_ref, w1_ref, w3_ref, w2_ref, y_ref, w1b_ref, w3b_ref, w2b_ref):
    j = pl.program_id(0)
    fresh = jnp.logical_or(j == 0, te_ref[j] != te_ref[jnp.maximum(j - 1, 0)])

    @pl.when(fresh)
    def _():
        w1b_ref[...] = w1_ref[...].astype(BF16)
        w3b_ref[...] = w3_ref[...].astype(BF16)
        w2b_ref[...] = w2_ref[...].astype(BF16)

    @pl.when(j < nv_ref[0])
    def _():
        xb = x_ref[...].astype(BF16)
        a = jnp.dot(xb, w1b_ref[...], preferred_element_type=F32)
        b = jnp.dot(xb, w3b_ref[...], preferred_element_type=F32)
        y_ref[...] = _bdot(_silu(a) * b, w2b_ref[...])

    @pl.when(j >= nv_ref[0])
    def _():
        y_ref[...] = jnp.zeros(y_ref.shape, F32)


def _experts(tile_expert, n_valid, xs, w1, w3, w2, layer):
    n_rows, d = xs.shape
    de = w1.shape[3]
    tm = MOE_TILE
    wspec = lambda r, c: pl.BlockSpec((None, None, r, c), lambda j, te, nv: (layer, te[j], 0, 0))
    grid_spec = pltpu.PrefetchScalarGridSpec(
        num_scalar_prefetch=2,
        grid=(n_rows // tm,),
        in_specs=[pl.BlockSpec((tm, d), lambda j, te, nv: (jnp.minimum(j, nv[0] - 1), 0)),
                  wspec(d, de), wspec(d, de), wspec(de, d)],
        out_specs=pl.BlockSpec((tm, d), lambda j, te, nv: (j, 0)),
        scratch_shapes=[pltpu.VMEM((d, de), BF16), pltpu.VMEM((d, de), BF16), pltpu.VMEM((de, d), BF16)],
    )
    return pl.pallas_call(
        _expert_kernel,
        grid_spec=grid_spec,
        out_shape=jax.ShapeDtypeStruct((n_rows, d), F32),
        compiler_params=_cparams(("arbitrary",)),
        name="moe_experts",
    )(tile_expert, n_valid, xs, w1, w3, w2)


def _combine_kernel(dest_ref, ys_ref, x1_ref, wts_ref, mod_ref, fw_ref, out_ref, buf_ref, sem,
                    *, per_token, final_norm):
    tm = x1_ref.shape[0]

    def issue(t, carry):
        for k in range(2):
            _row_copy(ys_ref, dest_ref[k, t], buf_ref.at[k], t, sem).start()
        return carry

    def drain(t, carry):
        for k in range(2):
            _row_copy(ys_ref, dest_ref[k, t], buf_ref.at[k], t, sem).wait()
        return carry

    lax.fori_loop(0, tm, issue, 0, unroll=8)
    lax.fori_loop(0, tm, drain, 0, unroll=8)
    w = wts_ref[...]
    moe = w[:, 0:1] * buf_ref[0] + w[:, 1:2] * buf_ref[1]
    x2 = x1_ref[...] + _mod_rows(mod_ref, 5, per_token) * moe
    if final_norm:
        x2 = _rms(x2) * fw_ref[...]
    out_ref[...] = x2


def _combine(dest, ys, x1, wts, mod, fw, *, tm, per_token, mod_spec, final_norm):
    t, d = x1.shape
    row = lambda n: pl.BlockSpec((tm, n), lambda i: (i, 0))
    return pl.pallas_call(
        functools.partial(_combine_kernel, per_token=per_token, final_norm=final_norm),
        grid=(t // tm,),
        in_specs=[pl.BlockSpec((2, tm), lambda i: (0, i), memory_space=pltpu.SMEM),
                  pl.BlockSpec(memory_space=pl.ANY),
                  row(d), row(LANES), mod_spec, _const_spec((1, d))],
        out_specs=row(d),
        out_shape=jax.ShapeDtypeStruct((t, d), F32),
        scratch_shapes=[pltpu.VMEM((2, tm, d), F32), pltpu.SemaphoreType.DMA(())],
        compiler_params=_cparams(("arbitrary",)),
        name="moe_combine",
    )(dest, ys, x1, wts, mod, fw)


def _moe(h2, idx, rank, counts, x1, wts, mod, fw, ew, layer, *, tok_tile, per_token, mod_spec_fn, final_norm):
    t, d = h2.shape
    tm = MOE_TILE
    n_rows = ((2 * t + N_EXPERTS * (tm - 1)) // tm + 1) * tm
    cnt = counts[:N_EXPERTS, 0].astype(jnp.int32)
    dest, tile_expert, ends, n_valid = _plan(cnt, idx, rank, tm, n_rows // tm)
    xs = _dispatch(ends, dest, h2, n_rows, tok_tile)
    ys = _experts(tile_expert, n_valid, xs, ew["w1"], ew["w3"], ew["w2"], layer)
    return _combine(dest, ys, x1, wts, mod, fw, tm=tok_tile, per_token=per_token,
                    mod_spec=mod_spec_fn(tok_tile), final_norm=final_norm)


def _pad_lanes(a, n=LANES):
    return jnp.pad(a, [(0, 0)] * (a.ndim - 1) + [(0, n - a.shape[-1])])


def kernel(x_prompt, x_sample, state_ssm, state_conv, state_s5_re, state_s5_im, c_prompt, c_sample, w_in, conv_w, conv_b, dt_bias, a_log, d_ssd, ssd_norm_w, w_out_a, s5_a_re, s5_a_im, s5_log_dt, s5_b_re, s5_b_im, s5_c_re, s5_c_im, s5_d, w_glu, w_out, norm1_w, norm2_w, w_ada, b_ada, w_router, b_router, w1, w3, w2, final_norm_w):
    nb, seq, d = x_prompt.shape
    ns, steps, _ = x_sample.shape
    depth = w_in.shape[0]
    d_in = SSD_HEADS * SSD_HEAD_DIM
    cdim = conv_w.shape[2]
    s5w = S5_GROUPS * S5_GROUP_CH
    s5n = S5_GROUPS * S5_STATE
    k1 = SSD_CONV - 1
    tp = nb * seq
    ts = ns * steps

    mod = _ada_mod(jnp.concatenate([c_prompt, c_sample], axis=0), w_ada, b_ada)
    head_expand = jnp.repeat(jnp.eye(LANES, SSD_HEADS, dtype=BF16), SSD_HEAD_DIM, axis=1)
    rw = {"w_router": _pad_lanes(w_router), "b_router": _pad_lanes(b_router[None, :]).reshape(LANES, 1)}

    xp = x_prompt.reshape(tp, d)
    xsm = jnp.transpose(x_sample, (1, 0, 2)).reshape(ts, d)
    outs = {k: [] for k in ("ssm_p", "conv_p", "conv_s", "re_p", "re_s", "im_p", "im_s")}
    ssm_s = None
    zero_state = jnp.zeros((nb, s5n), F32)

    for l in range(depth):
        o0 = 0
        o1 = d_in
        o2 = o1 + cdim
        o3 = o2 + SSD_HEADS
        o4 = o3 + s5w
        wl = w_in[l].astype(BF16)
        ws = (wl[:, o0:o1], wl[:, o1:o2], _pad_lanes(wl[:, o2:o3]), wl[:, o3:o4], wl[:, o4:])
        p = {"conv_w": conv_w[l], "conv_b": conv_b[l][None, :], "dt_bias": _pad_lanes(dt_bias[l][None, :]),
             "a_log": _pad_lanes(a_log[l][None, :]),
             "d_skip_e": jnp.repeat(d_ssd[l], SSD_HEAD_DIM)[None, :],
             "ssd_norm_w": ssd_norm_w[l][None, :], "head_expand": head_expand}
        chan_rows = lambda a, perm: jnp.transpose(a, perm).reshape(S5_GROUP_CH, s5n)
        ab, wb, c_re_t, c_im_t = _s5_params(
            s5_a_re[l].reshape(1, s5n), s5_a_im[l].reshape(1, s5n), jnp.repeat(s5_log_dt[l], S5_STATE)[None, :],
            chan_rows(s5_b_re[l], (2, 0, 1)), chan_rows(s5_b_im[l], (2, 0, 1)),
            chan_rows(s5_c_re[l], (1, 0, 2)), chan_rows(s5_c_im[l], (1, 0, 2)))
        sp = {"ab": ab, "wb": wb, "c_re_t": c_re_t, "c_im_t": c_im_t,
              "d": s5_d[l][None, :], "w_glu": w_glu[l].astype(BF16)}
        lw = {"w_out_a": w_out_a[l].astype(BF16), "w_out": w_out[l].astype(BF16), "norm2_w": norm2_w[l][None, :]}
        ew = {"w1": w1, "w3": w3, "w2": w2}
        final = l == depth - 1
        fw = final_norm_w[None, :]

        mod_p = mod[l, :nb].reshape(nb, 6, d)
        mod_p_spec = lambda tm: pl.BlockSpec((None, 6, d), lambda i: (i // (seq // tm), 0, 0))
        z, xbc, dtr, u, gate = _inproj(
            xp, mod_p, norm1_w[l][None, :], ws, tm=ROW_TILE, per_token=False, mod_spec=mod_p_spec(ROW_TILE))
        yn, nconv, nssm = _ssd_prompt(xbc, dtr, z, p, nb, seq)
        yb, nre, nim = _s5(u.reshape(nb, seq, s5w), zero_state, zero_state, sp, nb, S5_STEPS, True)
        x1, h2, idx, rank, wts, counts = _merge_router(
            yn, yb.reshape(tp, d), gate, xp, mod_p, lw, rw, tm=ROW_TILE, per_token=False,
            mod_spec=mod_p_spec(ROW_TILE))
        xp = _moe(h2, idx, rank, counts, x1, wts, mod_p, fw, ew, l, tok_tile=MOE_TILE, per_token=False,
                  mod_spec_fn=mod_p_spec, final_norm=final)
        outs["ssm_p"].append(nssm)
        outs["conv_p"].append(nconv)
        outs["re_p"].append(nre.reshape(nb, S5_GROUPS, S5_STATE))
        outs["im_p"].append(nim.reshape(nb, S5_GROUPS, S5_STATE))

        mod_s = jnp.transpose(mod[l, nb:].reshape(ns, 6, d), (1, 0, 2))
        mod_s_spec = lambda tm: pl.BlockSpec((6, tm, d), lambda i: (0, i % (ns // tm), 0))
        z, xbc, dtr, u, gate = _inproj(
            xsm, mod_s, norm1_w[l][None, :], ws, tm=ns, per_token=True, mod_spec=mod_s_spec(ns))
        conv0_tm = jnp.transpose(state_conv[l], (1, 0, 2)).reshape(k1 * ns, cdim)
        yn, nconv_tm, ssm_s = _ssd_sample(xbc, dtr, z, conv0_tm, state_ssm, ssm_s, l, p, ns, steps)
        yb, nre, nim = _s5(u, state_s5_re[l].reshape(ns, s5n), state_s5_im[l].reshape(ns, s5n), sp, ns, steps,
                           False)
        x1, h2, idx, rank, wts, counts = _merge_router(
            yn, yb, gate, xsm, mod_s, lw, rw, tm=ns, per_token=True, mod_spec=mod_s_spec(ns))
        xsm = _moe(h2, idx, rank, counts, x1, wts, mod_s, fw, ew, l, tok_tile=ns, per_token=True,
                   mod_spec_fn=mod_s_spec, final_norm=final)
        outs["conv_s"].append(jnp.transpose(nconv_tm.reshape(k1, ns, cdim), (1, 0, 2)))
        outs["re_s"].append(nre.reshape(ns, S5_GROUPS, S5_STATE))
        outs["im_s"].append(nim.reshape(ns, S5_GROUPS, S5_STATE))

    y_prompt = xp.reshape(nb, seq, d)
    y_sample = jnp.transpose(xsm.reshape(steps, ns, d), (1, 0, 2))
    st = lambda k: jnp.stack(outs[k])
    return (y_prompt, y_sample, st("ssm_p"), ssm_s, st("conv_p"), st("conv_s"),
            st("re_p"), st("re_s"), st("im_p"), st("im_s"))
```

```python
import functools

import jax
import jax.numpy as jnp
from jax import lax
from jax.experimental import pallas as pl
from jax.experimental.pallas import tpu as pltpu

F32 = jnp.float32
BF16 = jnp.bfloat16
EPS = 1e-6

SSD_HEADS = 16
SSD_HEAD_DIM = 64
SSD_GROUPS = 2
SSD_STATE = 128
SSD_CONV = 4
SSD_CHUNK = 128
S5_GROUPS = 32
S5_GROUP_CH = 16
S5_STATE = 64
N_EXPERTS = 16
EXPERTS_PER_GROUP = 4
N_EXPERT_GROUPS = 4

LANES = 128
SUBLANES = 8
MXU_TILE = 256
VMEM_LIMIT_BYTES = 56 * 1024 * 1024

ROW_TILE = 512
MOE_TILE = 256
S5_STEPS = 64


def _cparams(sem):
    return pltpu.CompilerParams(dimension_semantics=sem, vmem_limit_bytes=VMEM_LIMIT_BYTES)


def _bdot(a, b):
    return jnp.dot(a.astype(BF16), b.astype(BF16), preferred_element_type=F32)


def _bdot_nt(a, b):
    return lax.dot_general(a.astype(BF16), b.astype(BF16), (((1,), (1,)), ((), ())),
                           preferred_element_type=F32)


def _split3(x):
    hi = x.astype(BF16)
    r1 = x - hi.astype(F32)
    mid = r1.astype(BF16)
    lo = (r1 - mid.astype(F32)).astype(BF16)
    return hi, mid, lo


def _dot3_lhs(x, m):
    hi, mid, lo = _split3(x)
    d = functools.partial(jnp.dot, preferred_element_type=F32)
    return d(hi, m) + d(mid, m) + d(lo, m)


def _dot3_rhs(m, x):
    hi, mid, lo = _split3(x)
    d = functools.partial(jnp.dot, preferred_element_type=F32)
    return d(m, hi) + d(m, mid) + d(m, lo)


def _sigmoid(x):
    return jax.nn.sigmoid(x)


def _silu(x):
    return x * _sigmoid(x)


def _softplus(x):
    return jnp.maximum(x, 0.0) + jnp.log1p(jnp.exp(-jnp.abs(x)))


def _gelu_tanh(x):
    c = 0.7978845608028654
    return 0.5 * x * (1.0 + jnp.tanh(c * (x + 0.044715 * (x * x * x))))


def _rms(x):
    return x * lax.rsqrt(jnp.mean(x * x, axis=-1, keepdims=True) + EPS)


def _ada_kernel(c_ref, w_ref, b_ref, o_ref):
    o_ref[0] = _bdot(_silu(c_ref[...]), w_ref[0]) + b_ref[0]


def _ada_mod(c_all, w_ada, b_ada):
    depth, d, n = w_ada.shape
    nseq = c_all.shape[0]
    tn = 1536
    return pl.pallas_call(
        _ada_kernel,
        grid=(depth, n // tn),
        in_specs=[pl.BlockSpec((nseq, d), lambda l, j: (0, 0)),
                  pl.BlockSpec((1, d, tn), lambda l, j: (l, 0, j)),
                  pl.BlockSpec((1, 1, tn), lambda l, j: (l, 0, j))],
        out_specs=pl.BlockSpec((1, nseq, tn), lambda l, j: (l, 0, j)),
        out_shape=jax.ShapeDtypeStruct((depth, nseq, n), F32),
        compiler_params=_cparams(("arbitrary", "arbitrary")),
        name="ada_mod",
    )(c_all, w_ada, b_ada.reshape(depth, 1, n))


def _mod_rows(mod_ref, j, per_token):
    if per_token:
        return mod_ref[j]
    return mod_ref[j:j + 1, :]


def _inproj_kernel(x_ref, mod_ref, nw_ref, wz_ref, wx_ref, wdt_ref, wu_ref, wg_ref,
                   z_ref, xbc_ref, dt_ref, u_ref, g_ref, *, per_token):
    sh = _mod_rows(mod_ref, 0, per_token)
    sc = _mod_rows(mod_ref, 1, per_token)
    h = (_rms(x_ref[...]) * nw_ref[...] * (1.0 + sc) + sh).astype(BF16)
    d = functools.partial(jnp.dot, preferred_element_type=F32)
    z_ref[...] = d(h, wz_ref[...]).astype(BF16)
    xbc_ref[...] = d(h, wx_ref[...])
    dt_ref[...] = d(h, wdt_ref[...])
    u_ref[...] = d(h, wu_ref[...])
    g_ref[...] = d(h, wg_ref[...]).astype(BF16)


def _const_spec(shape):
    nd = len(shape)
    return pl.BlockSpec(shape, lambda *_: (0,) * nd)


def _inproj(x, mod, nw, ws, *, tm, per_token, mod_spec):
    t, d = x.shape
    wz, wx, wdt, wu, wg = ws
    row = lambda n: pl.BlockSpec((tm, n), lambda i: (i, 0))
    return pl.pallas_call(
        functools.partial(_inproj_kernel, per_token=per_token),
        grid=(t // tm,),
        in_specs=[row(d), mod_spec, _const_spec((1, d)),
                  _const_spec(wz.shape), _const_spec(wx.shape), _const_spec(wdt.shape),
                  _const_spec(wu.shape), _const_spec(wg.shape)],
        out_specs=[row(wz.shape[1]), row(wx.shape[1]), row(wdt.shape[1]), row(wu.shape[1]), row(wg.shape[1])],
        out_shape=[jax.ShapeDtypeStruct((t, wz.shape[1]), BF16),
                   jax.ShapeDtypeStruct((t, wx.shape[1]), F32),
                   jax.ShapeDtypeStruct((t, wdt.shape[1]), F32),
                   jax.ShapeDtypeStruct((t, wu.shape[1]), F32),
                   jax.ShapeDtypeStruct((t, wg.shape[1]), BF16)],
        compiler_params=_cparams(("arbitrary",)),
        name="inproj",
    )(x, mod, nw, wz, wx, wdt, wu, wg)


def _gated_group_norm(y, z, nw):
    y = y * _silu(z)
    half = y.shape[1] // SSD_GROUPS
    parts = [_rms(y[:, g * half:(g + 1) * half]) for g in range(SSD_GROUPS)]
    return jnp.concatenate(parts, axis=1) * nw


def _ssd_prompt_kernel(xbc_ref, dt_ref, z_ref, cw_ref, cb_ref, dtb_ref, alog_ref, dsk_ref, nw_ref,
                       e_ref, yn_ref, conv_ref, ssm_ref, ext_ref, st_ref):
    c = pl.program_id(1)
    q = SSD_CHUNK
    d_in = SSD_HEADS * SSD_HEAD_DIM
    gw = SSD_STATE
    hpg = SSD_HEADS // SSD_GROUPS
    grows = hpg * SSD_HEAD_DIM

    @pl.when(c == 0)
    def _():
        ext_ref[0:SUBLANES, :] = jnp.zeros((SUBLANES, ext_ref.shape[1]), F32)
        st_ref[...] = jnp.zeros(st_ref.shape, F32)

    xbc = xbc_ref[...]
    ext_ref[SUBLANES:SUBLANES + q, :] = xbc
    acc = cb_ref[...] + cw_ref[SSD_CONV - 1:SSD_CONV, :] * xbc
    for j in range(1, SSD_CONV):
        acc = acc + cw_ref[SSD_CONV - 1 - j:SSD_CONV - j, :] * ext_ref[pl.ds(SUBLANES - j, q), :]
    ext_ref[0:SUBLANES, :] = xbc[q - SUBLANES:q, :]
    v = _silu(acc)
    xs = v[:, :d_in]
    bm = v[:, d_in:d_in + SSD_GROUPS * gw]
    cm = v[:, d_in + SSD_GROUPS * gw:]

    dt = _softplus(dt_ref[...] + dtb_ref[...])
    a = -jnp.exp(alog_ref[...])
    da = dt * a
    rid = lax.broadcasted_iota(jnp.int32, (q, q), 0)
    cid = lax.broadcasted_iota(jnp.int32, (q, q), 1)
    causal = rid >= cid
    tri = jnp.where(causal, 1.0, 0.0).astype(BF16)
    a_cs = _dot3_rhs(tri, da)
    a_cs_t = a_cs.T
    a_last = a_cs[q - 1:q, :]
    e = e_ref[...]
    x_dt = xs * _dot3_lhs(dt, e)
    x_end = xs * _dot3_lhs(jnp.exp(a_last - a_cs) * dt, e)
    eacs_e = _dot3_lhs(jnp.exp(a_cs), e)
    x_end_t = x_end.T.astype(BF16)
    lane = lax.broadcasted_iota(jnp.int32, (q, 2 * SSD_HEAD_DIM), 1)
    first = lane < SSD_HEAD_DIM

    y_diag = []
    y_off = []
    for g in range(SSD_GROUPS):
        bg = bm[:, g * gw:(g + 1) * gw].astype(BF16)
        cg = cm[:, g * gw:(g + 1) * gw].astype(BF16)
        cb = _bdot_nt(cg, bg)
        sg = st_ref[g * grows:(g + 1) * grows, :]
        y_off.append(_bdot_nt(cg, sg))
        for pair in range(hpg // 2):
            ms = []
            for r in (2 * pair, 2 * pair + 1):
                h = g * hpg + r
                diff = a_cs[:, h:h + 1] - a_cs_t[h:h + 1, :]
                decay = jnp.where(causal, jnp.exp(jnp.where(causal, diff, 0.0)), 0.0)
                ms.append((cb * decay).astype(BF16))
            h0 = g * hpg + 2 * pair
            xp = x_dt[:, h0 * SSD_HEAD_DIM:(h0 + 2) * SSD_HEAD_DIM]
            rhs = jnp.concatenate([jnp.where(first, xp, 0.0), jnp.where(first, 0.0, xp)], axis=0)
            y_diag.append(jnp.dot(jnp.concatenate(ms, axis=1), rhs.astype(BF16), preferred_element_type=F32))
        contrib = jnp.dot(x_end_t[g * grows:(g + 1) * grows, :], bg, preferred_element_type=F32)
        for r in range(hpg):
            h = g * hpg + r
            rows = slice(h * SSD_HEAD_DIM, (h + 1) * SSD_HEAD_DIM)
            dec = jnp.exp(a_cs_t[h:h + 1, q - 1:q])
            st_ref[rows, :] = st_ref[rows, :] * dec + contrib[r * SSD_HEAD_DIM:(r + 1) * SSD_HEAD_DIM, :]

    y = (jnp.concatenate(y_diag, axis=1) + jnp.concatenate(y_off, axis=1) * eacs_e
         + dsk_ref[...] * xs)
    yn_ref[...] = _gated_group_norm(y, z_ref[...].astype(F32), nw_ref[...]).astype(BF16)

    @pl.when(c == pl.num_programs(1) - 1)
    def _():
        conv_ref[0] = ext_ref[pl.ds(SUBLANES - (SSD_CONV - 1), SSD_CONV - 1), :]
        ssm_ref[0] = st_ref[...].reshape(SSD_HEADS, SSD_HEAD_DIM, SSD_STATE)


def _ssd_prompt(xbc, dt_raw, z, p, nb, seq):
    t, cdim = xbc.shape
    nc = seq // SSD_CHUNK
    q = SSD_CHUNK
    d_in = z.shape[1]
    row = lambda n: pl.BlockSpec((q, n), lambda b, c: (b * nc + c, 0))
    return pl.pallas_call(
        _ssd_prompt_kernel,
        grid=(nb, nc),
        in_specs=[row(cdim), row(LANES), row(d_in),
                  _const_spec((SSD_CONV, cdim)), _const_spec((1, cdim)), _const_spec((1, LANES)),
                  _const_spec((1, LANES)), _const_spec((1, d_in)), _const_spec((1, d_in)),
                  _const_spec((LANES, d_in))],
        out_specs=[row(d_in),
                   pl.BlockSpec((1, SSD_CONV - 1, cdim), lambda b, c: (b, 0, 0)),
                   pl.BlockSpec((1, SSD_HEADS, SSD_HEAD_DIM, SSD_STATE), lambda b, c: (b, 0, 0, 0))],
        out_shape=[jax.ShapeDtypeStruct((t, d_in), BF16),
                   jax.ShapeDtypeStruct((nb, SSD_CONV - 1, cdim), F32),
                   jax.ShapeDtypeStruct((nb, SSD_HEADS, SSD_HEAD_DIM, SSD_STATE), F32)],
        scratch_shapes=[pltpu.VMEM((SUBLANES + q, cdim), F32),
                        pltpu.VMEM((d_in, SSD_STATE), F32)],
        compiler_params=_cparams(("arbitrary", "arbitrary")),
        name="ssd_prompt",
    )(xbc, dt_raw, z, p["conv_w"], p["conv_b"], p["dt_bias"], p["a_log"], p["d_skip_e"], p["ssd_norm_w"],
      p["head_expand"])


def _ssd_sample_a_kernel(xbc_ref, conv0_ref, dt_ref, cw_ref, cb_ref, dtb_ref, alog_ref, dsk_ref, e_ref,
                         ydiag_ref, eacs_ref, wt_ref, bs_ref, c8_ref, dec_ref, nconv_ref, *, steps, nseq):
    d_in = SSD_HEADS * SSD_HEAD_DIM
    gw = SSD_STATE
    hpg = SSD_HEADS // SSD_GROUPS
    k1 = SSD_CONV - 1
    slab = lambda ref, i: ref[i * nseq:(i + 1) * nseq, :]
    xp = [slab(conv0_ref, i) for i in range(k1)] + [slab(xbc_ref, i) for i in range(steps)]
    for i in range(k1):
        nconv_ref[i * nseq:(i + 1) * nseq, :] = xp[steps + i]
    e = e_ref[...]
    a = -jnp.exp(alog_ref[...])
    head = lax.broadcasted_iota(jnp.int32, (nseq, LANES), 1)
    xs, bm, cm, dt, a_cs = [], [], [], [], []
    run = jnp.zeros((nseq, LANES), F32)
    for t in range(steps):
        acc = cb_ref[...]
        for k in range(SSD_CONV):
            acc = acc + cw_ref[k:k + 1, :] * xp[t + k]
        v = _silu(acc)
        xs.append(v[:, :d_in])
        bm.append(v[:, d_in:d_in + SSD_GROUPS * gw])
        cm.append(v[:, d_in + SSD_GROUPS * gw:])
        dt.append(_softplus(slab(dt_ref, t) + dtb_ref[...]))
        run = run + dt[t] * a
        a_cs.append(run)
    dec_ref[...] = jnp.exp(a_cs[steps - 1])
    for t in range(steps):
        y = dsk_ref[...] * xs[t]
        for s in range(t + 1):
            cbs = [jnp.sum(cm[t][:, g * gw:(g + 1) * gw] * bm[s][:, g * gw:(g + 1) * gw],
                           axis=-1, keepdims=True) for g in range(SSD_GROUPS)]
            coef = jnp.exp(a_cs[t] - a_cs[s]) * dt[s] * jnp.where(head < hpg, cbs[0], cbs[1])
            y = y + _dot3_lhs(coef, e) * xs[s]
        ydiag_ref[t * nseq:(t + 1) * nseq, :] = y
        eacs_ref[t * nseq:(t + 1) * nseq, :] = _dot3_lhs(jnp.exp(a_cs[t]), e)
        w = xs[t] * _dot3_lhs(jnp.exp(a_cs[steps - 1] - a_cs[t]) * dt[t], e)
        wt_ref[:, t * nseq:(t + 1) * nseq] = w.T.astype(BF16)
        bs_ref[t * nseq:(t + 1) * nseq, :] = bm[t]
        for g in range(SSD_GROUPS):
            j = g * steps + t
            c8_ref[j * nseq:(j + 1) * nseq, :] = cm[t][:, g * gw:(g + 1) * gw]


def _ssd_sample_b_kernel(dec_ref, h0_ref, wt_ref, bs_ref, c8_ref, *rest, steps, nseq, layer):
    yoff_ref, ssm_ref = rest[-2:]
    if layer:
        ssm_ref[0:layer] = rest[0][...]
    b = pl.program_id(0)
    hpg = SSD_HEADS // SSD_GROUPS
    grows = hpg * SSD_HEAD_DIM
    gw = SSD_STATE
    h0 = h0_ref[0].reshape(SSD_HEADS * SSD_HEAD_DIM, SSD_STATE)
    c8 = c8_ref[pl.ds(b, SSD_GROUPS * steps, stride=nseq), :]
    yoff_ref[0] = _bdot_nt(c8, h0)
    rid = lax.broadcasted_iota(jnp.int32, (steps * nseq, gw), 0)
    mine = rid == b
    for t in range(1, steps):
        mine = jnp.logical_or(mine, rid == b + t * nseq)
    for g in range(SSD_GROUPS):
        bsel = jnp.where(mine, bs_ref[:, g * gw:(g + 1) * gw], 0.0).astype(BF16)
        contrib = jnp.dot(wt_ref[g * grows:(g + 1) * grows, :], bsel, preferred_element_type=F32)
        for r in range(hpg):
            h = g * hpg + r
            rows = slice(h * SSD_HEAD_DIM, (h + 1) * SSD_HEAD_DIM)
            ssm_ref[layer, 0, h] = (h0[rows, :] * dec_ref[b, h]
                                    + contrib[r * SSD_HEAD_DIM:(r + 1) * SSD_HEAD_DIM, :])


def _ssd_sample_c_kernel(ydiag_ref, yoff_ref, eacs_ref, z_ref, nw_ref, yn_ref, *, steps, nseq):
    d_in = SSD_HEADS * SSD_HEAD_DIM
    half = d_in // SSD_GROUPS
    for t in range(steps):
        lo = yoff_ref[:, t * d_in:t * d_in + half]
        hi = yoff_ref[:, (steps + t) * d_in + half:(steps + t + 1) * d_in]
        rows = slice(t * nseq, (t + 1) * nseq)
        y = ydiag_ref[rows, :] + jnp.concatenate([lo, hi], axis=1) * eacs_ref[rows, :]
        yn_ref[rows, :] = _gated_group_norm(y, z_ref[rows, :].astype(F32), nw_ref[...]).astype(BF16)


def _ssd_sample(xbc, dt_raw, z, conv0_tm, ssm_all, ssm_done, layer, p, nseq, steps):
    t, cdim = xbc.shape
    d_in = z.shape[1]
    k1 = SSD_CONV - 1
    nj = SSD_GROUPS * steps
    a_out = pl.pallas_call(
        functools.partial(_ssd_sample_a_kernel, steps=steps, nseq=nseq),
        out_shape=[jax.ShapeDtypeStruct((t, d_in), F32),
                   jax.ShapeDtypeStruct((t, d_in), F32),
                   jax.ShapeDtypeStruct((d_in, t), BF16),
                   jax.ShapeDtypeStruct((t, SSD_GROUPS * SSD_STATE), F32),
                   jax.ShapeDtypeStruct((nj * nseq, SSD_STATE), F32),
                   jax.ShapeDtypeStruct((nseq, LANES), F32),
                   jax.ShapeDtypeStruct((k1 * nseq, cdim), F32)],
        compiler_params=pltpu.CompilerParams(vmem_limit_bytes=VMEM_LIMIT_BYTES),
        name="ssd_sample_a",
    )(xbc, conv0_tm, dt_raw, p["conv_w"], p["conv_b"], p["dt_bias"], p["a_log"], p["d_skip_e"],
      p["head_expand"])
    ydiag, eacs_e, wt, bs, c8, dec, nconv = a_out
    state_blk = (SSD_HEADS, SSD_HEAD_DIM, SSD_STATE)
    operands = [dec[:, :SSD_HEADS], ssm_all, wt, bs, c8]
    in_specs = [pl.BlockSpec(memory_space=pltpu.SMEM),
                pl.BlockSpec((None, 1) + state_blk, lambda b: (layer, b, 0, 0, 0)),
                _const_spec(wt.shape), _const_spec(bs.shape), _const_spec(c8.shape)]
    if layer:
        operands.append(ssm_done)
        in_specs.append(pl.BlockSpec((layer, 1) + state_blk, lambda b: (0, b, 0, 0, 0)))
    yoff, nssm = pl.pallas_call(
        functools.partial(_ssd_sample_b_kernel, steps=steps, nseq=nseq, layer=layer),
        grid=(nseq,),
        in_specs=in_specs,
        out_specs=[pl.BlockSpec((1, nj, d_in), lambda b: (b, 0, 0)),
                   pl.BlockSpec((layer + 1, 1) + state_blk, lambda b: (0, b, 0, 0, 0))],
        out_shape=[jax.ShapeDtypeStruct((nseq, nj, d_in), F32),
                   jax.ShapeDtypeStruct((layer + 1, nseq) + state_blk, F32)],
        compiler_params=_cparams(("arbitrary",)),
        name="ssd_sample_b",
    )(*operands)
    yn = pl.pallas_call(
        functools.partial(_ssd_sample_c_kernel, steps=steps, nseq=nseq),
        out_shape=jax.ShapeDtypeStruct((t, d_in), BF16),
        compiler_params=pltpu.CompilerParams(vmem_limit_bytes=VMEM_LIMIT_BYTES),
        name="ssd_sample_c",
    )(ydiag, yoff.reshape(nseq, nj * d_in), eacs_e, z, p["ssd_norm_w"])
    return yn, nconv, nssm


def _s5_param_kernel(are_ref, aim_ref, ldt_ref, bre_ref, bim_ref, cre_ref, cim_ref,
                     ab_ref, wb_ref, cret_ref, cimt_ref):
    n = are_ref.shape[1]
    w = wb_ref.shape[0]
    a_re = are_ref[...]
    a_im = aim_ref[...]
    dt = jnp.exp(ldt_ref[...])
    mag = jnp.exp(dt * a_re)
    ab_re = mag * jnp.cos(dt * a_im)
    ab_im = mag * jnp.sin(dt * a_im)
    den = a_re * a_re + a_im * a_im
    f_re = ((ab_re - 1.0) * a_re + ab_im * a_im) / den
    f_im = (ab_im * a_re - (ab_re - 1.0) * a_im) / den
    ab_ref[0:1, :] = ab_re
    ab_ref[1:2, :] = ab_im
    b_re = bre_ref[...]
    b_im = bim_ref[...]
    row_g = lax.shift_right_logical(lax.broadcasted_iota(jnp.int32, (w, n), 0), S5_GROUP_CH.bit_length() - 1)
    col_g = lax.shift_right_logical(lax.broadcasted_iota(jnp.int32, (w, n), 1), S5_STATE.bit_length() - 1)
    same = row_g == col_g

    def spread(rows):
        return jnp.where(same, jnp.concatenate([rows] * S5_GROUPS, axis=0), 0.0).astype(BF16)

    wb_ref[:, 0:n] = spread(f_re * b_re - f_im * b_im)
    wb_ref[:, n:2 * n] = spread(f_re * b_im + f_im * b_re)
    cret_ref[...] = spread(cre_ref[...])
    cimt_ref[...] = spread(cim_ref[...])


def _s5_params(a_re, a_im, ldt, bt_re, bt_im, ct_re, ct_im):
    n = a_re.shape[1]
    w = S5_GROUPS * S5_GROUP_CH
    return pl.pallas_call(
        _s5_param_kernel,
        out_shape=[jax.ShapeDtypeStruct((2, n), F32),
                   jax.ShapeDtypeStruct((w, 2 * n), BF16),
                   jax.ShapeDtypeStruct((w, n), BF16),
                   jax.ShapeDtypeStruct((w, n), BF16)],
        compiler_params=pltpu.CompilerParams(vmem_limit_bytes=VMEM_LIMIT_BYTES),
        name="s5_params",
    )(a_re, a_im, ldt, bt_re, bt_im, ct_re, ct_im)


def _s5_kernel(u_ref, h0re_ref, h0im_ref, ab_ref, wb_ref, cret_ref, cimt_ref, d_ref, wglu_ref,
               yb_ref, nre_ref, nim_ref, xh_ref, st_ref, *, rows_per_step, steps, batch_major):
    i = pl.program_id(0)
    n = ab_ref.shape[1]
    w = wb_ref.shape[0]
    r = rows_per_step

    @pl.when(i == 0)
    def _():
        st_ref[0] = h0re_ref[...]
        st_ref[1] = h0im_ref[...]

    if batch_major:
        u = pltpu.einshape("btd->(tb)d", u_ref[...])
    else:
        u = u_ref[...]
    ub = u.astype(BF16)
    for j in range(2 * n // MXU_TILE):
        c0 = (j * MXU_TILE) % n
        k0 = (c0 // S5_STATE * S5_GROUP_CH) // LANES * LANES
        cols = slice(j * MXU_TILE, (j + 1) * MXU_TILE)
        xh_ref[:, cols] = jnp.dot(ub[:, k0:k0 + LANES], wb_ref[k0:k0 + LANES, cols], preferred_element_type=F32)
    a_re = ab_ref[0:1, :]
    a_im = ab_ref[1:2, :]

    def step(t, carry):
        h_re, h_im = carry
        rows = pl.ds(pl.multiple_of(t * r, r), r)
        n_re = a_re * h_re - a_im * h_im + xh_ref[rows, 0:n]
        n_im = a_re * h_im + a_im * h_re + xh_ref[rows, n:2 * n]
        xh_ref[rows, 0:n] = n_re
        xh_ref[rows, n:2 * n] = n_im
        return n_re, n_im

    carry = (st_ref[0], st_ref[1])
    if steps <= 8:
        for t in range(steps):
            carry = step(t, carry)
    else:
        carry = lax.fori_loop(0, steps, step, carry, unroll=2)
    st_ref[0] = carry[0]
    st_ref[1] = carry[1]

    n_ct = w // MXU_TILE
    ys = []
    for j in range(n_ct):
        rows = slice(j * MXU_TILE, (j + 1) * MXU_TILE)
        ks = slice(j * (n // n_ct), (j + 1) * (n // n_ct))
        ks_im = slice(n + j * (n // n_ct), n + (j + 1) * (n // n_ct))
        ys.append(_bdot_nt(xh_ref[:, ks], cret_ref[rows, ks]) - _bdot_nt(xh_ref[:, ks_im], cimt_ref[rows, ks]))
    y = jnp.concatenate(ys, axis=1) + d_ref[...] * u
    pre = _bdot(_gelu_tanh(y), wglu_ref[...])
    half = pre.shape[1] // 2
    yb = pre[:, :half] * _sigmoid(pre[:, half:])
    if batch_major:
        yb_ref[...] = pltpu.einshape("(tb)d->btd", yb, b=r).astype(BF16)
    else:
        yb_ref[...] = yb.astype(BF16)

    @pl.when(i == pl.num_programs(0) - 1)
    def _():
        nre_ref[...] = carry[0]
        nim_ref[...] = carry[1]


def _s5(u, h0_re, h0_im, sp, rows_per_step, steps, batch_major):
    n = sp["ab"].shape[1]
    w = u.shape[-1]
    tm = rows_per_step * steps
    d_model = sp["w_glu"].shape[1] // 2
    if batch_major:
        nb, seq, _ = u.shape
        grid = (seq // steps,)
        u_spec = pl.BlockSpec((nb, steps, w), lambda i: (0, i, 0))
        yb_spec = pl.BlockSpec((nb, steps, d_model), lambda i: (0, i, 0))
        yb_shape = (nb, seq, d_model)
    else:
        grid = (u.shape[0] // tm,)
        u_spec = pl.BlockSpec((tm, w), lambda i: (i, 0))
        yb_spec = pl.BlockSpec((tm, d_model), lambda i: (i, 0))
        yb_shape = (u.shape[0], d_model)
    return pl.pallas_call(
        functools.partial(_s5_kernel, rows_per_step=rows_per_step, steps=steps, batch_major=batch_major),
        grid=grid,
        in_specs=[u_spec,
                  _const_spec(h0_re.shape), _const_spec(h0_im.shape), _const_spec(sp["ab"].shape),
                  _const_spec(sp["wb"].shape), _const_spec(sp["c_re_t"].shape), _const_spec(sp["c_im_t"].shape),
                  _const_spec((1, w)), _const_spec(sp["w_glu"].shape)],
        out_specs=[yb_spec, _const_spec(h0_re.shape), _const_spec(h0_im.shape)],
        out_shape=[jax.ShapeDtypeStruct(yb_shape, BF16),
                   jax.ShapeDtypeStruct(h0_re.shape, F32),
                   jax.ShapeDtypeStruct(h0_im.shape, F32)],
        scratch_shapes=[pltpu.VMEM((tm, 2 * n), F32),
                        pltpu.VMEM((2, rows_per_step, n), F32)],
        compiler_params=_cparams(("arbitrary",)),
        name="s5_scan",
    )(u, h0_re, h0_im, sp["ab"], sp["wb"], sp["c_re_t"], sp["c_im_t"], sp["d"], sp["w_glu"])


def _merge_router_kernel(yn_ref, yb_ref, gate_ref, x_ref, mod_ref, woa_ref, wo_ref, nw_ref, wr_ref, br_ref,
                         x1_ref, h2_ref, idx_ref, rank_ref, wts_ref, cnt_ref, run_ref, *, per_token):
    i = pl.program_id(0)
    tm, d = x_ref.shape

    @pl.when(i == 0)
    def _():
        run_ref[...] = jnp.zeros(run_ref.shape, F32)

    ya = jnp.dot(yn_ref[...], woa_ref[...], preferred_element_type=F32)
    gate = _sigmoid(gate_ref[...].astype(F32))
    merged = gate[:, :d] * ya + gate[:, d:] * yb_ref[...]
    mix = _bdot(merged, wo_ref[...])
    g1 = _mod_rows(mod_ref, 2, per_token)
    sh2 = _mod_rows(mod_ref, 3, per_token)
    sc2 = _mod_rows(mod_ref, 4, per_token)
    x1 = x_ref[...] + g1 * mix
    x1_ref[...] = x1
    h2 = _rms(x1) * nw_ref[...] * (1.0 + sc2) + sh2
    h2_ref[...] = h2.reshape(tm, d // LANES, LANES)

    wr = wr_ref[...]
    wr_hi = wr.astype(BF16)
    wr_lo = (wr - wr_hi.astype(F32)).astype(BF16)
    h2_hi = h2.astype(BF16)
    h2_lo = (h2 - h2_hi.astype(F32)).astype(BF16)
    d = functools.partial(jnp.dot, preferred_element_type=F32)
    logits = d(h2_hi, wr_hi) + d(h2_lo, wr_hi) + d(h2_hi, wr_lo)
    lt = logits.T
    erow = lax.broadcasted_iota(jnp.int32, lt.shape, 0)
    lt = jnp.where(erow < N_EXPERTS, lt, -jnp.inf)
    ex = jnp.exp(lt - jnp.max(lt, axis=0, keepdims=True))
    scores = ex / jnp.sum(ex, axis=0, keepdims=True)
    sel = scores + br_ref[...]
    s = [sel[e:e + 1, :] for e in range(N_EXPERTS)]
    p = [scores[e:e + 1, :] for e in range(N_EXPERTS)]

    def group_top2_sum(vals):
        best = None
        for a in range(len(vals)):
            for b in range(a + 1, len(vals)):
                pair = vals[a] + vals[b]
                best = pair if best is None else jnp.maximum(best, pair)
        return best

    gs = [group_top2_sum(s[EXPERTS_PER_GROUP * g:EXPERTS_PER_GROUP * (g + 1)]) for g in range(N_EXPERT_GROUPS)]
    best = gs[0]
    bg = jnp.zeros(best.shape, jnp.int32)
    for g in range(1, N_EXPERT_GROUPS):
        better = gs[g] > best
        bg = jnp.where(better, g, bg)
        best = jnp.where(better, gs[g], best)

    def pick(rows, j):
        out = rows[j]
        for g in range(1, N_EXPERT_GROUPS):
            out = jnp.where(bg == g, rows[EXPERTS_PER_GROUP * g + j], out)
        return out

    cs = [pick(s, j) for j in range(EXPERTS_PER_GROUP)]
    cp = [pick(p, j) for j in range(EXPERTS_PER_GROUP)]

    def argmax_first(vals, skip):
        bv = None
        bi = None
        bw = None
        for j in range(len(vals)):
            v = vals[j] if skip is None else jnp.where(skip == j, -jnp.inf, vals[j])
            if bv is None:
                bv, bi, bw = v, jnp.zeros(v.shape, jnp.int32), cp[j]
            else:
                better = v > bv
                bi = jnp.where(better, j, bi)
                bw = jnp.where(better, cp[j], bw)
                bv = jnp.where(better, v, bv)
        return bi, bw

    i1, w1 = argmax_first(cs, None)
    i2, w2 = argmax_first(cs, i1)
    wsum = w1 + w2
    e1 = bg * EXPERTS_PER_GROUP + i1
    e2 = bg * EXPERTS_PER_GROUP + i2
    idx_ref[0:1, :] = e1
    idx_ref[1:2, :] = e2
    wrow = lax.broadcasted_iota(jnp.int32, lt.shape, 0)
    wmat = jnp.where(wrow == 0, w1 / wsum, jnp.where(wrow == 1, w2 / wsum, 0.0))
    wts_ref[...] = wmat.T

    oh1 = jnp.where(erow == e1, 1.0, 0.0)
    oh2 = jnp.where(erow == e2, 1.0, 0.0)
    both = oh1 + oh2
    ta = lax.broadcasted_iota(jnp.int32, (tm, tm), 0)
    tb = lax.broadcasted_iota(jnp.int32, (tm, tm), 1)
    earlier = jnp.where(ta < tb, 1.0, 0.0).astype(BF16)
    before = jnp.dot(both.astype(BF16), earlier, preferred_element_type=F32) + run_ref[...]
    rank_ref[0:1, :] = jnp.sum(oh1 * before, axis=0, keepdims=True).astype(jnp.int32)
    rank_ref[1:2, :] = jnp.sum(oh2 * before, axis=0, keepdims=True).astype(jnp.int32)
    run_ref[...] = run_ref[...] + jnp.sum(both, axis=1, keepdims=True)

    @pl.when(i == pl.num_programs(0) - 1)
    def _():
        cnt_ref[...] = run_ref[...]


def _merge_router(yn, yb, gate, x, mod, lw, rw, *, tm, per_token, mod_spec):
    t, d = x.shape
    row = lambda n: pl.BlockSpec((tm, n), lambda i: (i, 0))
    pair = pl.BlockSpec((2, tm), lambda i: (0, i))
    return pl.pallas_call(
        functools.partial(_merge_router_kernel, per_token=per_token),
        grid=(t // tm,),
        in_specs=[row(d), row(d), row(2 * d), row(d), mod_spec,
                  _const_spec((d, d)), _const_spec((d, d)), _const_spec((1, d)),
                  _const_spec((d, LANES)), _const_spec((LANES, 1))],
        out_specs=[row(d), _row_tiles(tm, d), pair, pair, row(LANES), _const_spec((LANES, 1))],
        out_shape=[jax.ShapeDtypeStruct((t, d), F32),
                   jax.ShapeDtypeStruct((t, d // LANES, LANES), F32),
                   jax.ShapeDtypeStruct((2, t), jnp.int32),
                   jax.ShapeDtypeStruct((2, t), jnp.int32),
                   jax.ShapeDtypeStruct((t, LANES), F32),
                   jax.ShapeDtypeStruct((LANES, 1), F32)],
        scratch_shapes=[pltpu.VMEM((LANES, 1), F32)],
        compiler_params=_cparams(("arbitrary",)),
        name="merge_router",
    )(yn, yb, gate, x, mod, lw["w_out_a"], lw["w_out"], lw["norm2_w"], rw["w_router"], rw["b_router"])


def _plan_kernel(cnt_ref, idx_ref, rank_ref, dest_ref, te_ref, ends_ref, nv_ref, *, tile, n_tiles):
    shift = tile.bit_length() - 1
    run = jnp.int32(0)
    starts = []
    for e in range(N_EXPERTS):
        starts.append(run)
        run = run + (((cnt_ref[e] + (tile - 1)) >> shift) << shift)
        ends_ref[e] = run
    n_valid = run >> shift
    nv_ref[0] = n_valid
    idx = idx_ref[...]
    dest = rank_ref[...]
    for e in range(N_EXPERTS):
        dest = dest + jnp.where(idx == e, starts[e], 0)
    dest_ref[...] = dest

    def tile_owner(j, carry):
        pos = jnp.minimum(j, n_valid - 1) * tile
        owner = jnp.int32(0)
        for e in range(N_EXPERTS - 1):
            owner = owner + jnp.where(ends_ref[e] <= pos, 1, 0)
        te_ref[j] = owner
        return carry

    lax.fori_loop(0, n_tiles, tile_owner, 0)


def _plan(cnt, idx, rank, tile, n_tiles):
    smem = pl.BlockSpec(memory_space=pltpu.SMEM)
    vmem = pl.BlockSpec(memory_space=pltpu.VMEM)
    return pl.pallas_call(
        functools.partial(_plan_kernel, tile=tile, n_tiles=n_tiles),
        in_specs=[smem, vmem, vmem],
        out_specs=[vmem, smem, smem, smem],
        out_shape=[jax.ShapeDtypeStruct(idx.shape, jnp.int32),
                   jax.ShapeDtypeStruct((n_tiles,), jnp.int32),
                   jax.ShapeDtypeStruct((N_EXPERTS,), jnp.int32),
                   jax.ShapeDtypeStruct((1,), jnp.int32)],
        name="moe_plan",
    )(cnt, idx, rank)


def _row_tiles(tm, d):
    return pl.BlockSpec((tm, d // LANES, LANES), lambda i: (i, 0, 0))


def _row_copy(src_ref, src_row, dst_ref, dst_row, sem):
    return pltpu.make_async_copy(src_ref.at[src_row], dst_ref.at[dst_row], sem)


def _each_row(tm, fn):
    def body(t, carry):
        for k in range(2):
            fn(t, k)
        return carry

    lax.fori_loop(0, tm, body, 0, unroll=8)


def _dispatch_kernel(ends_ref, dest_ref, prev_dest_ref, h_ref, out_ref, zero_ref, stage_ref, sem, zsem, *, tile):
    i = pl.program_id(0)
    tm = h_ref.shape[0]
    slot = i % 2

    @pl.when(i == 0)
    def _():
        zero_ref[...] = jnp.zeros(zero_ref.shape, F32)

        def clear(start):
            return pltpu.make_async_copy(zero_ref, out_ref.at[pl.ds(pl.multiple_of(start, tile), tile)], zsem)

        def used(e):
            return ends_ref[e] > (ends_ref[e - 1] if e else 0)

        def unused_tiles(fn):
            n_valid = ends_ref[N_EXPERTS - 1] >> (tile.bit_length() - 1)
            lax.fori_loop(n_valid, out_ref.shape[0] // tile, lambda j, c: (fn(clear(j * tile)), c)[1], 0)

        for e in range(N_EXPERTS):
            pl.when(used(e))(lambda e=e: clear(ends_ref[e] - tile).start())
        unused_tiles(lambda cp: cp.start())
        for e in range(N_EXPERTS):
            pl.when(used(e))(lambda e=e: clear(ends_ref[e] - tile).wait())
        unused_tiles(lambda cp: cp.wait())

    stage_ref[slot] = h_ref[...]
    _each_row(tm, lambda t, k: _row_copy(stage_ref.at[slot], t, out_ref, dest_ref[k, t], sem.at[slot]).start())

    @pl.when(i > 0)
    def _():
        _each_row(tm, lambda t, k: _row_copy(stage_ref.at[1 - slot], t, out_ref, prev_dest_ref[k, t],
                                             sem.at[1 - slot]).wait())

    @pl.when(i == pl.num_programs(0) - 1)
    def _():
        _each_row(tm, lambda t, k: _row_copy(stage_ref.at[slot], t, out_ref, dest_ref[k, t], sem.at[slot]).wait())


def _dispatch(ends, dest, h2, n_rows, tm):
    t, nt, lanes = h2.shape
    idx_spec = lambda fn: pl.BlockSpec((2, tm), fn, memory_space=pltpu.SMEM)
    return pl.pallas_call(
        functools.partial(_dispatch_kernel, tile=MOE_TILE),
        grid=(t // tm,),
        in_specs=[pl.BlockSpec(memory_space=pltpu.SMEM),
                  idx_spec(lambda i: (0, i)), idx_spec(lambda i: (0, jnp.maximum(i - 1, 0))),
                  _row_tiles(tm, nt * lanes)],
        out_specs=pl.BlockSpec(memory_space=pl.ANY),
        out_shape=jax.ShapeDtypeStruct((n_rows, nt, lanes), F32),
        scratch_shapes=[pltpu.VMEM((MOE_TILE, nt, lanes), F32), pltpu.VMEM((2, tm, nt, lanes), F32),
                        pltpu.SemaphoreType.DMA((2,)), pltpu.SemaphoreType.DMA(())],
        compiler_params=_cparams(("arbitrary",)),
        name="moe_dispatch",
    )(ends, dest, dest, h2)


def _expert_kernel(te_ref, nv_ref, x_ref, w1_ref, w3_ref, w2_ref, y_ref, w1b_ref, w3b_ref, w2b_ref):
    j = pl.program_id(0)
    fresh = jnp.logical_or(j == 0, te_ref[j] != te_ref[jnp.maximum(j - 1, 0)])

    @pl.when(fresh)
    def _():
        w1b_ref[...] = w1_ref[...].astype(BF16)
        w3b_ref[...] = w3_ref[...].astype(BF16)
        w2b_ref[...] = w2_ref[...].astype(BF16)

    @pl.when(j < nv_ref[0])
    def _():
        tm, nt, lanes = x_ref.shape
        xb = x_ref[...].reshape(tm, nt * lanes).astype(BF16)
        a = jnp.dot(xb, w1b_ref[...], preferred_element_type=F32)
        b = jnp.dot(xb, w3b_ref[...], preferred_element_type=F32)
        y_ref[...] = _bdot(_silu(a) * b, w2b_ref[...]).reshape(tm, nt, lanes)

    @pl.when(j >= nv_ref[0])
    def _():
        y_ref[...] = jnp.zeros(y_ref.shape, F32)


def _experts(tile_expert, n_valid, xs, w1, w3, w2, layer):
    n_rows, nt, lanes = xs.shape
    d = nt * lanes
    de = w1.shape[3]
    tm = MOE_TILE
    wspec = lambda r, c: pl.BlockSpec((None, None, r, c), lambda j, te, nv: (layer, te[j], 0, 0))
    grid_spec = pltpu.PrefetchScalarGridSpec(
        num_scalar_prefetch=2,
        grid=(n_rows // tm,),
        in_specs=[pl.BlockSpec((tm, nt, lanes), lambda j, te, nv: (jnp.minimum(j, nv[0] - 1), 0, 0)),
                  wspec(d, de), wspec(d, de), wspec(de, d)],
        out_specs=pl.BlockSpec((tm, nt, lanes), lambda j, te, nv: (j, 0, 0)),
        scratch_shapes=[pltpu.VMEM((d, de), BF16), pltpu.VMEM((d, de), BF16), pltpu.VMEM((de, d), BF16)],
    )
    return pl.pallas_call(
        _expert_kernel,
        grid_spec=grid_spec,
        out_shape=jax.ShapeDtypeStruct((n_rows, nt, lanes), F32),
        compiler_params=_cparams(("arbitrary",)),
        name="moe_experts",
    )(tile_expert, n_valid, xs, w1, w3, w2)


def _combine_kernel(dest_ref, next_dest_ref, ys_ref, x1_ref, wts_ref, mod_ref, fw_ref, out_ref, buf_ref, sem,
                    *, per_token, final_norm):
    i = pl.program_id(0)
    tm, d = x1_ref.shape
    slot = i % 2

    def gather(idx_ref, s):
        return lambda t, k: _row_copy(ys_ref, idx_ref[k, t], buf_ref.at[s, k], t, sem.at[s])

    @pl.when(i == 0)
    def _():
        _each_row(tm, lambda t, k: gather(dest_ref, slot)(t, k).start())

    @pl.when(i + 1 < pl.num_programs(0))
    def _():
        _each_row(tm, lambda t, k: gather(next_dest_ref, 1 - slot)(t, k).start())

    _each_row(tm, lambda t, k: gather(dest_ref, slot)(t, k).wait())
    w = wts_ref[...]
    rows = lambda k: buf_ref[slot, k].reshape(tm, d)
    moe = w[:, 0:1] * rows(0) + w[:, 1:2] * rows(1)
    x2 = x1_ref[...] + _mod_rows(mod_ref, 5, per_token) * moe
    if final_norm:
        x2 = _rms(x2) * fw_ref[...]
    out_ref[...] = x2


def _combine(dest, ys, x1, wts, mod, fw, *, tm, per_token, mod_spec, final_norm):
    t, d = x1.shape
    row = lambda n: pl.BlockSpec((tm, n), lambda i: (i, 0))
    n_steps = t // tm
    idx_spec = lambda fn: pl.BlockSpec((2, tm), fn, memory_space=pltpu.SMEM)
    return pl.pallas_call(
        functools.partial(_combine_kernel, per_token=per_token, final_norm=final_norm),
        grid=(n_steps,),
        in_specs=[idx_spec(lambda i: (0, i)), idx_spec(lambda i: (0, jnp.minimum(i + 1, n_steps - 1))),
                  pl.BlockSpec(memory_space=pl.ANY),
                  row(d), row(LANES), mod_spec, _const_spec((1, d))],
        out_specs=row(d),
        out_shape=jax.ShapeDtypeStruct((t, d), F32),
        scratch_shapes=[pltpu.VMEM((2, 2, tm) + ys.shape[1:], F32), pltpu.SemaphoreType.DMA((2,))],
        compiler_params=_cparams(("arbitrary",)),
        name="moe_combine",
    )(dest, dest, ys, x1, wts, mod, fw)


def _moe(h2, idx, rank, counts, x1, wts, mod, fw, ew, layer, *, tok_tile, per_token, mod_spec_fn, final_norm):
    t = h2.shape[0]
    tm = MOE_TILE
    n_rows = ((2 * t + N_EXPERTS * (tm - 1)) // tm + 1) * tm
    cnt = counts[:N_EXPERTS, 0].astype(jnp.int32)
    dest, tile_expert, ends, n_valid = _plan(cnt, idx, rank, tm, n_rows // tm)
    xs = _dispatch(ends, dest, h2, n_rows, tok_tile)
    ys = _experts(tile_expert, n_valid, xs, ew["w1"], ew["w3"], ew["w2"], layer)
    return _combine(dest, ys, x1, wts, mod, fw, tm=tok_tile, per_token=per_token,
                    mod_spec=mod_spec_fn(tok_tile), final_norm=final_norm)


def _pad_lanes(a, n=LANES):
    return jnp.pad(a, [(0, 0)] * (a.ndim - 1) + [(0, n - a.shape[-1])])


def kernel(x_prompt, x_sample, state_ssm, state_conv, state_s5_re, state_s5_im, c_prompt, c_sample, w_in, conv_w, conv_b, dt_bias, a_log, d_ssd, ssd_norm_w, w_out_a, s5_a_re, s5_a_im, s5_log_dt, s5_b_re, s5_b_im, s5_c_re, s5_c_im, s5_d, w_glu, w_out, norm1_w, norm2_w, w_ada, b_ada, w_router, b_router, w1, w3, w2, final_norm_w):
    nb, seq, d = x_prompt.shape
    ns, steps, _ = x_sample.shape
    depth = w_in.shape[0]
    d_in = SSD_HEADS * SSD_HEAD_DIM
    cdim = conv_w.shape[2]
    s5w = S5_GROUPS * S5_GROUP_CH
    s5n = S5_GROUPS * S5_STATE
    k1 = SSD_CONV - 1
    tp = nb * seq
    ts = ns * steps

    mod = _ada_mod(jnp.concatenate([c_prompt, c_sample], axis=0), w_ada, b_ada)
    head_expand = jnp.repeat(jnp.eye(LANES, SSD_HEADS, dtype=BF16), SSD_HEAD_DIM, axis=1)
    rw = {"w_router": _pad_lanes(w_router), "b_router": _pad_lanes(b_router[None, :]).reshape(LANES, 1)}

    xp = x_prompt.reshape(tp, d)
    xsm = jnp.transpose(x_sample, (1, 0, 2)).reshape(ts, d)
    outs = {k: [] for k in ("ssm_p", "conv_p", "conv_s", "re_p", "re_s", "im_p", "im_s")}
    ssm_s = None
    zero_state = jnp.zeros((nb, s5n), F32)

    for l in range(depth):
        o0 = 0
        o1 = d_in
        o2 = o1 + cdim
        o3 = o2 + SSD_HEADS
        o4 = o3 + s5w
        wl = w_in[l].astype(BF16)
        ws = (wl[:, o0:o1], wl[:, o1:o2], _pad_lanes(wl[:, o2:o3]), wl[:, o3:o4], wl[:, o4:])
        p = {"conv_w": conv_w[l], "conv_b": conv_b[l][None, :], "dt_bias": _pad_lanes(dt_bias[l][None, :]),
             "a_log": _pad_lanes(a_log[l][None, :]),
             "d_skip_e": jnp.repeat(d_ssd[l], SSD_HEAD_DIM)[None, :],
             "ssd_norm_w": ssd_norm_w[l][None, :], "head_expand": head_expand}
        chan_rows = lambda a, perm: jnp.transpose(a, perm).reshape(S5_GROUP_CH, s5n)
        ab, wb, c_re_t, c_im_t = _s5_params(
            s5_a_re[l].reshape(1, s5n), s5_a_im[l].reshape(1, s5n), jnp.repeat(s5_log_dt[l], S5_STATE)[None, :],
            chan_rows(s5_b_re[l], (2, 0, 1)), chan_rows(s5_b_im[l], (2, 0, 1)),
            chan_rows(s5_c_re[l], (1, 0, 2)), chan_rows(s5_c_im[l], (1, 0, 2)))
        sp = {"ab": ab, "wb": wb, "c_re_t": c_re_t, "c_im_t": c_im_t,
              "d": s5_d[l][None, :], "w_glu": w_glu[l].astype(BF16)}
        lw = {"w_out_a": w_out_a[l].astype(BF16), "w_out": w_out[l].astype(BF16), "norm2_w": norm2_w[l][None, :]}
        ew = {"w1": w1, "w3": w3, "w2": w2}
        final = l == depth - 1
        fw = final_norm_w[None, :]

        mod_p = mod[l, :nb].reshape(nb, 6, d)
        mod_p_spec = lambda tm: pl.BlockSpec((None, 6, d), lambda i: (i // (seq // tm), 0, 0))
        z, xbc, dtr, u, gate = _inproj(
            xp, mod_p, norm1_w[l][None, :], ws, tm=ROW_TILE, per_token=False, mod_spec=mod_p_spec(ROW_TILE))
        yn, nconv, nssm = _ssd_prompt(xbc, dtr, z, p, nb, seq)
        yb, nre, nim = _s5(u.reshape(nb, seq, s5w), zero_state, zero_state, sp, nb, S5_STEPS, True)
        x1, h2, idx, rank, wts, counts = _merge_router(
            yn, yb.reshape(tp, d), gate, xp, mod_p, lw, rw, tm=ROW_TILE, per_token=False,
            mod_spec=mod_p_spec(ROW_TILE))
        xp = _moe(h2, idx, rank, counts, x1, wts, mod_p, fw, ew, l, tok_tile=MOE_TILE, per_token=False,
                  mod_spec_fn=mod_p_spec, final_norm=final)
        outs["ssm_p"].append(nssm)
        outs["conv_p"].append(nconv)
        outs["re_p"].append(nre.reshape(nb, S5_GROUPS, S5_STATE))
        outs["im_p"].append(nim.reshape(nb, S5_GROUPS, S5_STATE))

        mod_s = jnp.transpose(mod[l, nb:].reshape(ns, 6, d), (1, 0, 2))
        mod_s_spec = lambda tm: pl.BlockSpec((6, tm, d), lambda i: (0, i % (ns // tm), 0))
        z, xbc, dtr, u, gate = _inproj(
            xsm, mod_s, norm1_w[l][None, :], ws, tm=ns, per_token=True, mod_spec=mod_s_spec(ns))
        conv0_tm = jnp.transpose(state_conv[l], (1, 0, 2)).reshape(k1 * ns, cdim)
        yn, nconv_tm, ssm_s = _ssd_sample(xbc, dtr, z, conv0_tm, state_ssm, ssm_s, l, p, ns, steps)
        yb, nre, nim = _s5(u, state_s5_re[l].reshape(ns, s5n), state_s5_im[l].reshape(ns, s5n), sp, ns, steps,
                           False)
        x1, h2, idx, rank, wts, counts = _merge_router(
            yn, yb, gate, xsm, mod_s, lw, rw, tm=ns, per_token=True, mod_spec=mod_s_spec(ns))
        xsm = _moe(h2, idx, rank, counts, x1, wts, mod_s, fw, ew, l, tok_tile=ns, per_token=True,
                   mod_spec_fn=mod_s_spec, final_norm=final)
        outs["conv_s"].append(jnp.transpose(nconv_tm.reshape(k1, ns, cdim), (1, 0, 2)))
        outs["re_s"].append(nre.reshape(ns, S5_GROUPS, S5_STATE))
        outs["im_s"].append(nim.reshape(ns, S5_GROUPS, S5_STATE))

    y_prompt = xp.reshape(nb, seq, d)
    y_sample = jnp.transpose(xsm.reshape(steps, ns, d), (1, 0, 2))
    st = lambda k: jnp.stack(outs[k])
    return (y_prompt, y_sample, st("ssm_p"), ssm_s, st("conv_p"), st("conv_s"),
            st("re_p"), st("re_s"), st("im_p"), st("im_s"))
```

```python
import functools

import jax
import jax.numpy as jnp
from jax import lax
from jax.experimental import pallas as pl
from jax.experimental.pallas import tpu as pltpu

F32 = jnp.float32
BF16 = jnp.bfloat16
EPS = 1e-6

SSD_HEADS = 16
SSD_HEAD_DIM = 64
SSD_GROUPS = 2
SSD_STATE = 128
SSD_CONV = 4
SSD_CHUNK = 128
S5_GROUPS = 32
S5_GROUP_CH = 16
S5_STATE = 64
N_EXPERTS = 16
EXPERTS_PER_GROUP = 4
N_EXPERT_GROUPS = 4

LANES = 128
SUBLANES = 8
MXU_TILE = 256
VMEM_LIMIT_BYTES = 56 * 1024 * 1024

ROW_TILE = 512
MOE_TILE = 256
EXPERT_TILE = 512
S5_STEPS = 64
SSD_SAMPLE_SEQS = 4


def _cparams(sem):
    return pltpu.CompilerParams(dimension_semantics=sem, vmem_limit_bytes=VMEM_LIMIT_BYTES)


def _bdot(a, b):
    return jnp.dot(a.astype(BF16), b.astype(BF16), preferred_element_type=F32)


def _bdot_nt(a, b):
    return lax.dot_general(a.astype(BF16), b.astype(BF16), (((1,), (1,)), ((), ())),
                           preferred_element_type=F32)


def _split3(x):
    hi = x.astype(BF16)
    r1 = x - hi.astype(F32)
    mid = r1.astype(BF16)
    lo = (r1 - mid.astype(F32)).astype(BF16)
    return hi, mid, lo


def _dot3_lhs(x, m3):
    return jnp.dot(jnp.concatenate(_split3(x), axis=1), m3, preferred_element_type=F32)


def _dot3_rhs(m, x):
    return jnp.dot(jnp.concatenate([m] * 3, axis=1), jnp.concatenate(_split3(x), axis=0),
                   preferred_element_type=F32)


def _sigmoid(x):
    return jax.nn.sigmoid(x)


def _silu(x):
    return x * _sigmoid(x)


def _softplus(x):
    return jnp.maximum(x, 0.0) + jnp.log1p(jnp.exp(-jnp.abs(x)))


def _gelu_tanh(x):
    c = 0.7978845608028654
    return 0.5 * x * (1.0 + jnp.tanh(c * (x + 0.044715 * (x * x * x))))


def _rms(x):
    return x * lax.rsqrt(jnp.mean(x * x, axis=-1, keepdims=True) + EPS)


def _ada_kernel(c_ref, w_ref, b_ref, o_ref):
    o_ref[0] = _bdot(_silu(c_ref[...]), w_ref[0]) + b_ref[0]


def _ada_mod(c_all, w_ada, b_ada):
    depth, d, n = w_ada.shape
    nseq = c_all.shape[0]
    tn = 1536
    return pl.pallas_call(
        _ada_kernel,
        grid=(depth, n // tn),
        in_specs=[pl.BlockSpec((nseq, d), lambda l, j: (0, 0)),
                  pl.BlockSpec((1, d, tn), lambda l, j: (l, 0, j)),
                  pl.BlockSpec((1, 1, tn), lambda l, j: (l, 0, j))],
        out_specs=pl.BlockSpec((1, nseq, tn), lambda l, j: (l, 0, j)),
        out_shape=jax.ShapeDtypeStruct((depth, nseq, n), F32),
        compiler_params=_cparams(("arbitrary", "arbitrary")),
        name="ada_mod",
    )(c_all, w_ada, b_ada.reshape(depth, 1, n))


def _mod_rows(mod_ref, j, per_token):
    if per_token:
        return mod_ref[j]
    return mod_ref[j:j + 1, :]


def _inproj_kernel(x_ref, mod_ref, nw_ref, wz_ref, wx_ref, wdt_ref, wu_ref, wg_ref,
                   z_ref, xbc_ref, dt_ref, u_ref, g_ref, *, per_token):
    sh = _mod_rows(mod_ref, 0, per_token)
    sc = _mod_rows(mod_ref, 1, per_token)
    h = (_rms(x_ref[...]) * nw_ref[...] * (1.0 + sc) + sh).astype(BF16)
    d = functools.partial(jnp.dot, preferred_element_type=F32)
    z_ref[...] = d(h, wz_ref[...]).astype(BF16)
    xbc_ref[...] = d(h, wx_ref[...])
    dt_ref[...] = d(h, wdt_ref[...])
    u_ref[...] = d(h, wu_ref[...])
    g_ref[...] = d(h, wg_ref[...]).astype(BF16)


def _const_spec(shape):
    nd = len(shape)
    return pl.BlockSpec(shape, lambda *_: (0,) * nd)


def _inproj(x, mod, nw, ws, *, tm, per_token, mod_spec):
    t, d = x.shape
    wz, wx, wdt, wu, wg = ws
    row = lambda n: pl.BlockSpec((tm, n), lambda i: (i, 0))
    return pl.pallas_call(
        functools.partial(_inproj_kernel, per_token=per_token),
        grid=(t // tm,),
        in_specs=[row(d), mod_spec, _const_spec((1, d)),
                  _const_spec(wz.shape), _const_spec(wx.shape), _const_spec(wdt.shape),
                  _const_spec(wu.shape), _const_spec(wg.shape)],
        out_specs=[row(wz.shape[1]), row(wx.shape[1]), row(wdt.shape[1]), row(wu.shape[1]), row(wg.shape[1])],
        out_shape=[jax.ShapeDtypeStruct((t, wz.shape[1]), BF16),
                   jax.ShapeDtypeStruct((t, wx.shape[1]), F32),
                   jax.ShapeDtypeStruct((t, wdt.shape[1]), F32),
                   jax.ShapeDtypeStruct((t, wu.shape[1]), F32),
                   jax.ShapeDtypeStruct((t, wg.shape[1]), BF16)],
        compiler_params=_cparams(("arbitrary",)),
        name="inproj",
    )(x, mod, nw, wz, wx, wdt, wu, wg)


def _gated_group_norm(y, z, nw):
    y = y * _silu(z)
    half = y.shape[1] // SSD_GROUPS
    parts = [_rms(y[:, g * half:(g + 1) * half]) for g in range(SSD_GROUPS)]
    return jnp.concatenate(parts, axis=1) * nw


def _ssd_prompt_kernel(xbc_ref, dt_ref, z_ref, cw_ref, cb_ref, dtb_ref, alog_ref, dsk_ref, nw_ref,
                       e_ref, yn_ref, conv_ref, ssm_ref, ext_ref, st_ref):
    c = pl.program_id(1)
    q = SSD_CHUNK
    d_in = SSD_HEADS * SSD_HEAD_DIM
    gw = SSD_STATE
    hpg = SSD_HEADS // SSD_GROUPS
    grows = hpg * SSD_HEAD_DIM

    @pl.when(c == 0)
    def _():
        ext_ref[0:SUBLANES, :] = jnp.zeros((SUBLANES, ext_ref.shape[1]), F32)
        st_ref[...] = jnp.zeros(st_ref.shape, F32)

    xbc = xbc_ref[...]
    ext_ref[SUBLANES:SUBLANES + q, :] = xbc
    acc = cb_ref[...] + cw_ref[SSD_CONV - 1:SSD_CONV, :] * xbc
    for j in range(1, SSD_CONV):
        acc = acc + cw_ref[SSD_CONV - 1 - j:SSD_CONV - j, :] * ext_ref[pl.ds(SUBLANES - j, q), :]
    ext_ref[0:SUBLANES, :] = xbc[q - SUBLANES:q, :]
    v = _silu(acc)
    xs = v[:, :d_in]
    bm = v[:, d_in:d_in + SSD_GROUPS * gw]
    cm = v[:, d_in + SSD_GROUPS * gw:]

    dt = _softplus(dt_ref[...] + dtb_ref[...])
    a = -jnp.exp(alog_ref[...])
    da = dt * a
    rid = lax.broadcasted_iota(jnp.int32, (q, q), 0)
    cid = lax.broadcasted_iota(jnp.int32, (q, q), 1)
    causal = rid >= cid
    tri = jnp.where(causal, 1.0, 0.0).astype(BF16)
    a_cs = _dot3_rhs(tri, da)
    a_cs_t = a_cs.T
    a_last = a_cs[q - 1:q, :]
    per_head = jnp.concatenate([dt, jnp.exp(a_last - a_cs) * dt, jnp.exp(a_cs)], axis=0)
    per_chan = _dot3_lhs(per_head, e_ref[...])
    x_dt = xs * per_chan[0:q]
    x_end = xs * per_chan[q:2 * q]
    eacs_e = per_chan[2 * q:3 * q]
    x_end_t = x_end.T.astype(BF16)
    lane = lax.broadcasted_iota(jnp.int32, (q, 2 * SSD_HEAD_DIM), 1)
    first = lane < SSD_HEAD_DIM

    y_diag = []
    y_off = []
    for g in range(SSD_GROUPS):
        bg = bm[:, g * gw:(g + 1) * gw].astype(BF16)
        cg = cm[:, g * gw:(g + 1) * gw].astype(BF16)
        cb = _bdot_nt(cg, bg)
        sg = st_ref[g * grows:(g + 1) * grows, :]
        y_off.append(_bdot_nt(cg, sg))
        for pair in range(hpg // 2):
            ms = []
            for r in (2 * pair, 2 * pair + 1):
                h = g * hpg + r
                diff = a_cs[:, h:h + 1] - a_cs_t[h:h + 1, :]
                decay = jnp.where(causal, jnp.exp(jnp.where(causal, diff, 0.0)), 0.0)
                ms.append((cb * decay).astype(BF16))
            h0 = g * hpg + 2 * pair
            xp = x_dt[:, h0 * SSD_HEAD_DIM:(h0 + 2) * SSD_HEAD_DIM]
            rhs = jnp.concatenate([jnp.where(first, xp, 0.0), jnp.where(first, 0.0, xp)], axis=0)
            y_diag.append(jnp.dot(jnp.concatenate(ms, axis=1), rhs.astype(BF16), preferred_element_type=F32))
        contrib = jnp.dot(x_end_t[g * grows:(g + 1) * grows, :], bg, preferred_element_type=F32)
        for r in range(hpg):
            h = g * hpg + r
            rows = slice(h * SSD_HEAD_DIM, (h + 1) * SSD_HEAD_DIM)
            dec = jnp.exp(a_cs_t[h:h + 1, q - 1:q])
            st_ref[rows, :] = st_ref[rows, :] * dec + contrib[r * SSD_HEAD_DIM:(r + 1) * SSD_HEAD_DIM, :]

    y = (jnp.concatenate(y_diag, axis=1) + jnp.concatenate(y_off, axis=1) * eacs_e
         + dsk_ref[...] * xs)
    yn_ref[...] = _gated_group_norm(y, z_ref[...].astype(F32), nw_ref[...]).astype(BF16)

    @pl.when(c == pl.num_programs(1) - 1)
    def _():
        conv_ref[0] = ext_ref[pl.ds(SUBLANES - (SSD_CONV - 1), SSD_CONV - 1), :]
        ssm_ref[0] = st_ref[...].reshape(SSD_HEADS, SSD_HEAD_DIM, SSD_STATE)


def _ssd_prompt(xbc, dt_raw, z, p, nb, seq):
    t, cdim = xbc.shape
    nc = seq // SSD_CHUNK
    q = SSD_CHUNK
    d_in = z.shape[1]
    row = lambda n: pl.BlockSpec((q, n), lambda b, c: (b * nc + c, 0))
    return pl.pallas_call(
        _ssd_prompt_kernel,
        grid=(nb, nc),
        in_specs=[row(cdim), row(LANES), row(d_in),
                  _const_spec((SSD_CONV, cdim)), _const_spec((1, cdim)), _const_spec((1, LANES)),
                  _const_spec((1, LANES)), _const_spec((1, d_in)), _const_spec((1, d_in)),
                  _const_spec((3 * LANES, d_in))],
        out_specs=[row(d_in),
                   pl.BlockSpec((1, SSD_CONV - 1, cdim), lambda b, c: (b, 0, 0)),
                   pl.BlockSpec((1, SSD_HEADS, SSD_HEAD_DIM, SSD_STATE), lambda b, c: (b, 0, 0, 0))],
        out_shape=[jax.ShapeDtypeStruct((t, d_in), BF16),
                   jax.ShapeDtypeStruct((nb, SSD_CONV - 1, cdim), F32),
                   jax.ShapeDtypeStruct((nb, SSD_HEADS, SSD_HEAD_DIM, SSD_STATE), F32)],
        scratch_shapes=[pltpu.VMEM((SUBLANES + q, cdim), F32),
                        pltpu.VMEM((d_in, SSD_STATE), F32)],
        compiler_params=_cparams(("arbitrary", "arbitrary")),
        name="ssd_prompt",
    )(xbc, dt_raw, z, p["conv_w"], p["conv_b"], p["dt_bias"], p["a_log"], p["d_skip_e"], p["ssd_norm_w"],
      p["head_expand"])


def _ssd_sample_a_kernel(xbc_ref, conv0_ref, dt_ref, cw_ref, cb_ref, dtb_ref, alog_ref, dsk_ref, e_ref,
                         ydiag_ref, eacs_ref, wt_ref, bs_ref, c8_ref, dec_ref, nconv_ref, *, steps, nseq):
    d_in = SSD_HEADS * SSD_HEAD_DIM
    gw = SSD_STATE
    hpg = SSD_HEADS // SSD_GROUPS
    k1 = SSD_CONV - 1
    slab = lambda ref, i: ref[i * nseq:(i + 1) * nseq, :]
    xp = [slab(conv0_ref, i) for i in range(k1)] + [slab(xbc_ref, i) for i in range(steps)]
    for i in range(k1):
        nconv_ref[i * nseq:(i + 1) * nseq, :] = xp[steps + i]
    e = e_ref[...]
    a = -jnp.exp(alog_ref[...])
    head = lax.broadcasted_iota(jnp.int32, (nseq, LANES), 1)
    xs, bm, cm, dt, a_cs = [], [], [], [], []
    run = jnp.zeros((nseq, LANES), F32)
    for t in range(steps):
        acc = cb_ref[...]
        for k in range(SSD_CONV):
            acc = acc + cw_ref[k:k + 1, :] * xp[t + k]
        v = _silu(acc)
        xs.append(v[:, :d_in])
        bm.append(v[:, d_in:d_in + SSD_GROUPS * gw])
        cm.append(v[:, d_in + SSD_GROUPS * gw:])
        dt.append(_softplus(slab(dt_ref, t) + dtb_ref[...]))
        run = run + dt[t] * a
        a_cs.append(run)
    dec_ref[...] = jnp.exp(a_cs[steps - 1])
    for t in range(steps):
        y = dsk_ref[...] * xs[t]
        for s in range(t + 1):
            cbs = [jnp.sum(cm[t][:, g * gw:(g + 1) * gw] * bm[s][:, g * gw:(g + 1) * gw],
                           axis=-1, keepdims=True) for g in range(SSD_GROUPS)]
            coef = jnp.exp(a_cs[t] - a_cs[s]) * dt[s] * jnp.where(head < hpg, cbs[0], cbs[1])
            y = y + _dot3_lhs(coef, e) * xs[s]
        ydiag_ref[t * nseq:(t + 1) * nseq, :] = y
        eacs_ref[t * nseq:(t + 1) * nseq, :] = _dot3_lhs(jnp.exp(a_cs[t]), e)
        w = xs[t] * _dot3_lhs(jnp.exp(a_cs[steps - 1] - a_cs[t]) * dt[t], e)
        wt_ref[:, t * nseq:(t + 1) * nseq] = w.T.astype(BF16)
        bs_ref[t * nseq:(t + 1) * nseq, :] = bm[t]
        for g in range(SSD_GROUPS):
            j = g * steps + t
            c8_ref[j * nseq:(j + 1) * nseq, :] = cm[t][:, g * gw:(g + 1) * gw]


def _ssd_sample_b_kernel(dec_ref, h0_ref, wt_ref, bs_ref, c8_ref, *rest, steps, nseq, layer):
    yoff_ref, ssm_ref = rest[-2:]
    if layer:
        ssm_ref[0:layer] = rest[0][...]
    hpg = SSD_HEADS // SSD_GROUPS
    grows = hpg * SSD_HEAD_DIM
    gw = SSD_STATE
    rid = lax.broadcasted_iota(jnp.int32, (steps * nseq, gw), 0)
    for s in range(h0_ref.shape[0]):
        b = pl.program_id(0) * h0_ref.shape[0] + s
        h0 = h0_ref[s].reshape(SSD_HEADS * SSD_HEAD_DIM, SSD_STATE)
        c8 = c8_ref[pl.ds(b, SSD_GROUPS * steps, stride=nseq), :]
        yoff_ref[s] = _bdot_nt(c8, h0)
        mine = rid == b
        for t in range(1, steps):
            mine = jnp.logical_or(mine, rid == b + t * nseq)
        for g in range(SSD_GROUPS):
            bsel = jnp.where(mine, bs_ref[:, g * gw:(g + 1) * gw], 0.0).astype(BF16)
            contrib = jnp.dot(wt_ref[g * grows:(g + 1) * grows, :], bsel, preferred_element_type=F32)
            for r in range(hpg):
                h = g * hpg + r
                rows = slice(h * SSD_HEAD_DIM, (h + 1) * SSD_HEAD_DIM)
                ssm_ref[layer, s, h] = (h0[rows, :] * dec_ref[b, h]
                                        + contrib[r * SSD_HEAD_DIM:(r + 1) * SSD_HEAD_DIM, :])


def _ssd_sample_c_kernel(ydiag_ref, yoff_ref, eacs_ref, z_ref, nw_ref, yn_ref, *, steps, nseq):
    d_in = SSD_HEADS * SSD_HEAD_DIM
    half = d_in // SSD_GROUPS
    for t in range(steps):
        lo = yoff_ref[:, t * d_in:t * d_in + half]
        hi = yoff_ref[:, (steps + t) * d_in + half:(steps + t + 1) * d_in]
        rows = slice(t * nseq, (t + 1) * nseq)
        y = ydiag_ref[rows, :] + jnp.concatenate([lo, hi], axis=1) * eacs_ref[rows, :]
        yn_ref[rows, :] = _gated_group_norm(y, z_ref[rows, :].astype(F32), nw_ref[...]).astype(BF16)


def _ssd_sample(xbc, dt_raw, z, conv0_tm, ssm_all, ssm_done, layer, p, nseq, steps):
    t, cdim = xbc.shape
    d_in = z.shape[1]
    k1 = SSD_CONV - 1
    nj = SSD_GROUPS * steps
    a_out = pl.pallas_call(
        functools.partial(_ssd_sample_a_kernel, steps=steps, nseq=nseq),
        out_shape=[jax.ShapeDtypeStruct((t, d_in), F32),
                   jax.ShapeDtypeStruct((t, d_in), F32),
                   jax.ShapeDtypeStruct((d_in, t), BF16),
                   jax.ShapeDtypeStruct((t, SSD_GROUPS * SSD_STATE), F32),
                   jax.ShapeDtypeStruct((nj * nseq, SSD_STATE), F32),
                   jax.ShapeDtypeStruct((nseq, LANES), F32),
                   jax.ShapeDtypeStruct((k1 * nseq, cdim), F32)],
        compiler_params=pltpu.CompilerParams(vmem_limit_bytes=VMEM_LIMIT_BYTES),
        name="ssd_sample_a",
    )(xbc, conv0_tm, dt_raw, p["conv_w"], p["conv_b"], p["dt_bias"], p["a_log"], p["d_skip_e"],
      p["head_expand"])
    ydiag, eacs_e, wt, bs, c8, dec, nconv = a_out
    state_blk = (SSD_HEADS, SSD_HEAD_DIM, SSD_STATE)
    operands = [dec[:, :SSD_HEADS], ssm_all, wt, bs, c8]
    sb = SSD_SAMPLE_SEQS
    in_specs = [pl.BlockSpec(memory_space=pltpu.SMEM),
                pl.BlockSpec((None, sb) + state_blk, lambda b: (layer, b, 0, 0, 0)),
                _const_spec(wt.shape), _const_spec(bs.shape), _const_spec(c8.shape)]
    if layer:
        operands.append(ssm_done)
        in_specs.append(pl.BlockSpec((layer, sb) + state_blk, lambda b: (0, b, 0, 0, 0)))
    yoff, nssm = pl.pallas_call(
        functools.partial(_ssd_sample_b_kernel, steps=steps, nseq=nseq, layer=layer),
        grid=(nseq // sb,),
        in_specs=in_specs,
        out_specs=[pl.BlockSpec((sb, nj, d_in), lambda b: (b, 0, 0)),
                   pl.BlockSpec((layer + 1, sb) + state_blk, lambda b: (0, b, 0, 0, 0))],
        out_shape=[jax.ShapeDtypeStruct((nseq, nj, d_in), F32),
                   jax.ShapeDtypeStruct((layer + 1, nseq) + state_blk, F32)],
        compiler_params=_cparams(("arbitrary",)),
        name="ssd_sample_b",
    )(*operands)
    yn = pl.pallas_call(
        functools.partial(_ssd_sample_c_kernel, steps=steps, nseq=nseq),
        out_shape=jax.ShapeDtypeStruct((t, d_in), BF16),
        compiler_params=pltpu.CompilerParams(vmem_limit_bytes=VMEM_LIMIT_BYTES),
        name="ssd_sample_c",
    )(ydiag, yoff.reshape(nseq, nj * d_in), eacs_e, z, p["ssd_norm_w"])
    return yn, nconv, nssm


def _s5_param_kernel(are_ref, aim_ref, ldt_ref, bre_ref, bim_ref, cre_ref, cim_ref,
                     ab_ref, wb_ref, cret_ref, cimt_ref):
    n = are_ref.shape[1]
    w = wb_ref.shape[0]
    a_re = are_ref[...]
    a_im = aim_ref[...]
    dt = jnp.exp(ldt_ref[...])
    mag = jnp.exp(dt * a_re)
    ab_re = mag * jnp.cos(dt * a_im)
    ab_im = mag * jnp.sin(dt * a_im)
    den = a_re * a_re + a_im * a_im
    f_re = ((ab_re - 1.0) * a_re + ab_im * a_im) / den
    f_im = (ab_im * a_re - (ab_re - 1.0) * a_im) / den
    ab_ref[0:1, :] = ab_re
    ab_ref[1:2, :] = ab_im
    b_re = bre_ref[...]
    b_im = bim_ref[...]
    row_g = lax.shift_right_logical(lax.broadcasted_iota(jnp.int32, (w, n), 0), S5_GROUP_CH.bit_length() - 1)
    col_g = lax.shift_right_logical(lax.broadcasted_iota(jnp.int32, (w, n), 1), S5_STATE.bit_length() - 1)
    same = row_g == col_g

    def spread(rows):
        return jnp.where(same, jnp.concatenate([rows] * S5_GROUPS, axis=0), 0.0).astype(BF16)

    wb_ref[:, 0:n] = spread(f_re * b_re - f_im * b_im)
    wb_ref[:, n:2 * n] = spread(f_re * b_im + f_im * b_re)
    cret_ref[...] = spread(cre_ref[...])
    cimt_ref[...] = spread(cim_ref[...])


def _s5_params(a_re, a_im, ldt, bt_re, bt_im, ct_re, ct_im):
    n = a_re.shape[1]
    w = S5_GROUPS * S5_GROUP_CH
    return pl.pallas_call(
        _s5_param_kernel,
        out_shape=[jax.ShapeDtypeStruct((2, n), F32),
                   jax.ShapeDtypeStruct((w, 2 * n), BF16),
                   jax.ShapeDtypeStruct((w, n), BF16),
                   jax.ShapeDtypeStruct((w, n), BF16)],
        compiler_params=pltpu.CompilerParams(vmem_limit_bytes=VMEM_LIMIT_BYTES),
        name="s5_params",
    )(a_re, a_im, ldt, bt_re, bt_im, ct_re, ct_im)


def _s5_kernel(u_ref, h0re_ref, h0im_ref, ab_ref, wb_ref, cret_ref, cimt_ref, d_ref, wglu_ref,
               yb_ref, nre_ref, nim_ref, xh_ref, st_ref, *, rows_per_step, steps, batch_major):
    i = pl.program_id(0)
    n = ab_ref.shape[1]
    w = wb_ref.shape[0]
    r = rows_per_step

    @pl.when(i == 0)
    def _():
        st_ref[0] = h0re_ref[...]
        st_ref[1] = h0im_ref[...]

    if batch_major:
        u = pltpu.einshape("btd->(tb)d", u_ref[...])
    else:
        u = u_ref[...]
    ub = u.astype(BF16)
    for j in range(2 * n // MXU_TILE):
        c0 = (j * MXU_TILE) % n
        k0 = (c0 // S5_STATE * S5_GROUP_CH) // LANES * LANES
        cols = slice(j * MXU_TILE, (j + 1) * MXU_TILE)
        xh_ref[:, cols] = jnp.dot(ub[:, k0:k0 + LANES], wb_ref[k0:k0 + LANES, cols], preferred_element_type=F32)
    a_re = ab_ref[0:1, :]
    a_im = ab_ref[1:2, :]

    def step(t, carry):
        h_re, h_im = carry
        rows = pl.ds(pl.multiple_of(t * r, r), r)
        n_re = a_re * h_re - a_im * h_im + xh_ref[rows, 0:n]
        n_im = a_re * h_im + a_im * h_re + xh_ref[rows, n:2 * n]
        xh_ref[rows, 0:n] = n_re
        xh_ref[rows, n:2 * n] = n_im
        return n_re, n_im

    carry = (st_ref[0], st_ref[1])
    if steps <= 8:
        for t in range(steps):
            carry = step(t, carry)
    else:
        carry = lax.fori_loop(0, steps, step, carry, unroll=2)
    st_ref[0] = carry[0]
    st_ref[1] = carry[1]

    n_ct = w // MXU_TILE
    ys = []
    for j in range(n_ct):
        rows = slice(j * MXU_TILE, (j + 1) * MXU_TILE)
        ks = slice(j * (n // n_ct), (j + 1) * (n // n_ct))
        ks_im = slice(n + j * (n // n_ct), n + (j + 1) * (n // n_ct))
        ys.append(_bdot_nt(xh_ref[:, ks], cret_ref[rows, ks]) - _bdot_nt(xh_ref[:, ks_im], cimt_ref[rows, ks]))
    y = jnp.concatenate(ys, axis=1) + d_ref[...] * u
    pre = _bdot(_gelu_tanh(y), wglu_ref[...])
    half = pre.shape[1] // 2
    yb = pre[:, :half] * _sigmoid(pre[:, half:])
    if batch_major:
        yb_ref[...] = pltpu.einshape("(tb)d->btd", yb, b=r).astype(BF16)
    else:
        yb_ref[...] = yb.astype(BF16)

    @pl.when(i == pl.num_programs(0) - 1)
    def _():
        nre_ref[...] = carry[0]
        nim_ref[...] = carry[1]


def _s5(u, h0_re, h0_im, sp, rows_per_step, steps, batch_major):
    n = sp["ab"].shape[1]
    w = u.shape[-1]
    tm = rows_per_step * steps
    d_model = sp["w_glu"].shape[1] // 2
    if batch_major:
        nb, seq, _ = u.shape
        grid = (seq // steps,)
        u_spec = pl.BlockSpec((nb, steps, w), lambda i: (0, i, 0))
        yb_spec = pl.BlockSpec((nb, steps, d_model), lambda i: (0, i, 0))
        yb_shape = (nb, seq, d_model)
    else:
        grid = (u.shape[0] // tm,)
        u_spec = pl.BlockSpec((tm, w), lambda i: (i, 0))
        yb_spec = pl.BlockSpec((tm, d_model), lambda i: (i, 0))
        yb_shape = (u.shape[0], d_model)
    return pl.pallas_call(
        functools.partial(_s5_kernel, rows_per_step=rows_per_step, steps=steps, batch_major=batch_major),
        grid=grid,
        in_specs=[u_spec,
                  _const_spec(h0_re.shape), _const_spec(h0_im.shape), _const_spec(sp["ab"].shape),
                  _const_spec(sp["wb"].shape), _const_spec(sp["c_re_t"].shape), _const_spec(sp["c_im_t"].shape),
                  _const_spec((1, w)), _const_spec(sp["w_glu"].shape)],
        out_specs=[yb_spec, _const_spec(h0_re.shape), _const_spec(h0_im.shape)],
        out_shape=[jax.ShapeDtypeStruct(yb_shape, BF16),
                   jax.ShapeDtypeStruct(h0_re.shape, F32),
                   jax.ShapeDtypeStruct(h0_im.shape, F32)],
        scratch_shapes=[pltpu.VMEM((tm, 2 * n), F32),
                        pltpu.VMEM((2, rows_per_step, n), F32)],
        compiler_params=_cparams(("arbitrary",)),
        name="s5_scan",
    )(u, h0_re, h0_im, sp["ab"], sp["wb"], sp["c_re_t"], sp["c_im_t"], sp["d"], sp["w_glu"])


def _merge_router_kernel(yn_ref, yb_ref, gate_ref, x_ref, mod_ref, woa_ref, wo_ref, nw_ref, wr_ref, br_ref,
                         x1_ref, h2_ref, idx_ref, rank_ref, wts_ref, cnt_ref, run_ref, *, per_token):
    i = pl.program_id(0)
    tm, d = x_ref.shape

    @pl.when(i == 0)
    def _():
        run_ref[...] = jnp.zeros(run_ref.shape, F32)

    ya = jnp.dot(yn_ref[...], woa_ref[...], preferred_element_type=F32)
    gate = _sigmoid(gate_ref[...].astype(F32))
    merged = gate[:, :d] * ya + gate[:, d:] * yb_ref[...]
    mix = _bdot(merged, wo_ref[...])
    g1 = _mod_rows(mod_ref, 2, per_token)
    sh2 = _mod_rows(mod_ref, 3, per_token)
    sc2 = _mod_rows(mod_ref, 4, per_token)
    x1 = x_ref[...] + g1 * mix
    x1_ref[...] = x1
    h2 = _rms(x1) * nw_ref[...] * (1.0 + sc2) + sh2
    h2_ref[...] = h2.reshape(tm, d // LANES, LANES)

    wr = wr_ref[...]
    wr_hi = wr.astype(BF16)
    wr_lo = (wr - wr_hi.astype(F32)).astype(BF16)
    h2_hi = h2.astype(BF16)
    h2_lo = (h2 - h2_hi.astype(F32)).astype(BF16)
    d = functools.partial(jnp.dot, preferred_element_type=F32)
    logits = d(h2_hi, wr_hi) + d(h2_lo, wr_hi) + d(h2_hi, wr_lo)
    lt = logits.T
    erow = lax.broadcasted_iota(jnp.int32, lt.shape, 0)
    lt = jnp.where(erow < N_EXPERTS, lt, -jnp.inf)
    ex = jnp.exp(lt - jnp.max(lt, axis=0, keepdims=True))
    scores = ex / jnp.sum(ex, axis=0, keepdims=True)
    sel = scores + br_ref[...]
    s = [sel[e:e + 1, :] for e in range(N_EXPERTS)]
    p = [scores[e:e + 1, :] for e in range(N_EXPERTS)]

    def group_top2_sum(vals):
        best = None
        for a in range(len(vals)):
            for b in range(a + 1, len(vals)):
                pair = vals[a] + vals[b]
                best = pair if best is None else jnp.maximum(best, pair)
        return best

    gs = [group_top2_sum(s[EXPERTS_PER_GROUP * g:EXPERTS_PER_GROUP * (g + 1)]) for g in range(N_EXPERT_GROUPS)]
    best = gs[0]
    bg = jnp.zeros(best.shape, jnp.int32)
    for g in range(1, N_EXPERT_GROUPS):
        better = gs[g] > best
        bg = jnp.where(better, g, bg)
        best = jnp.where(better, gs[g], best)

    def pick(rows, j):
        out = rows[j]
        for g in range(1, N_EXPERT_GROUPS):
            out = jnp.where(bg == g, rows[EXPERTS_PER_GROUP * g + j], out)
        return out

    cs = [pick(s, j) for j in range(EXPERTS_PER_GROUP)]
    cp = [pick(p, j) for j in range(EXPERTS_PER_GROUP)]

    def argmax_first(vals, skip):
        bv = None
        bi = None
        bw = None
        for j in range(len(vals)):
            v = vals[j] if skip is None else jnp.where(skip == j, -jnp.inf, vals[j])
            if bv is None:
                bv, bi, bw = v, jnp.zeros(v.shape, jnp.int32), cp[j]
            else:
                better = v > bv
                bi = jnp.where(better, j, bi)
                bw = jnp.where(better, cp[j], bw)
                bv = jnp.where(better, v, bv)
        return bi, bw

    i1, w1 = argmax_first(cs, None)
    i2, w2 = argmax_first(cs, i1)
    wsum = w1 + w2
    e1 = bg * EXPERTS_PER_GROUP + i1
    e2 = bg * EXPERTS_PER_GROUP + i2
    idx_ref[0:1, :] = e1
    idx_ref[1:2, :] = e2
    wrow = lax.broadcasted_iota(jnp.int32, lt.shape, 0)
    wmat = jnp.where(wrow == 0, w1 / wsum, jnp.where(wrow == 1, w2 / wsum, 0.0))
    wts_ref[...] = wmat.T

    oh1 = jnp.where(erow == e1, 1.0, 0.0)
    oh2 = jnp.where(erow == e2, 1.0, 0.0)
    both = oh1 + oh2
    ta = lax.broadcasted_iota(jnp.int32, (tm, tm), 0)
    tb = lax.broadcasted_iota(jnp.int32, (tm, tm), 1)
    earlier = jnp.where(ta < tb, 1.0, 0.0).astype(BF16)
    before = jnp.dot(both.astype(BF16), earlier, preferred_element_type=F32) + run_ref[...]
    rank_ref[0:1, :] = jnp.sum(oh1 * before, axis=0, keepdims=True).astype(jnp.int32)
    rank_ref[1:2, :] = jnp.sum(oh2 * before, axis=0, keepdims=True).astype(jnp.int32)
    run_ref[...] = run_ref[...] + jnp.sum(both, axis=1, keepdims=True)

    @pl.when(i == pl.num_programs(0) - 1)
    def _():
        cnt_ref[...] = run_ref[...]


def _merge_router(yn, yb, gate, x, mod, lw, rw, *, tm, per_token, mod_spec):
    t, d = x.shape
    row = lambda n: pl.BlockSpec((tm, n), lambda i: (i, 0))
    pair = pl.BlockSpec((2, tm), lambda i: (0, i))
    return pl.pallas_call(
        functools.partial(_merge_router_kernel, per_token=per_token),
        grid=(t // tm,),
        in_specs=[row(d), row(d), row(2 * d), row(d), mod_spec,
                  _const_spec((d, d)), _const_spec((d, d)), _const_spec((1, d)),
                  _const_spec((d, LANES)), _const_spec((LANES, 1))],
        out_specs=[row(d), _row_tiles(tm, d), pair, pair, row(LANES), _const_spec((LANES, 1))],
        out_shape=[jax.ShapeDtypeStruct((t, d), F32),
                   jax.ShapeDtypeStruct((t, d // LANES, LANES), F32),
                   jax.ShapeDtypeStruct((2, t), jnp.int32),
                   jax.ShapeDtypeStruct((2, t), jnp.int32),
                   jax.ShapeDtypeStruct((t, LANES), F32),
                   jax.ShapeDtypeStruct((LANES, 1), F32)],
        scratch_shapes=[pltpu.VMEM((LANES, 1), F32)],
        compiler_params=_cparams(("arbitrary",)),
        name="merge_router",
    )(yn, yb, gate, x, mod, lw["w_out_a"], lw["w_out"], lw["norm2_w"], rw["w_router"], rw["b_router"])


def _plan_kernel(cnt_ref, idx_ref, rank_ref, dest_ref, te_ref, ends_ref, nv_ref, *, tile, n_tiles):
    shift = tile.bit_length() - 1
    run = jnp.int32(0)
    starts = []
    for e in range(N_EXPERTS):
        starts.append(run)
        run = run + (((cnt_ref[e] + (tile - 1)) >> shift) << shift)
        ends_ref[e] = run
    n_valid = run >> shift
    nv_ref[0] = n_valid
    idx = idx_ref[...]
    dest = rank_ref[...]
    for e in range(N_EXPERTS):
        dest = dest + jnp.where(idx == e, starts[e], 0)
    dest_ref[...] = dest

    def tile_owner(j, carry):
        pos = jnp.minimum(j, n_valid - 1) * tile
        owner = jnp.int32(0)
        for e in range(N_EXPERTS - 1):
            owner = owner + jnp.where(ends_ref[e] <= pos, 1, 0)
        te_ref[j] = owner
        return carry

    lax.fori_loop(0, n_tiles, tile_owner, 0)


def _plan(cnt, idx, rank, tile, n_tiles):
    smem = pl.BlockSpec(memory_space=pltpu.SMEM)
    vmem = pl.BlockSpec(memory_space=pltpu.VMEM)
    return pl.pallas_call(
        functools.partial(_plan_kernel, tile=tile, n_tiles=n_tiles),
        in_specs=[smem, vmem, vmem],
        out_specs=[vmem, smem, smem, smem],
        out_shape=[jax.ShapeDtypeStruct(idx.shape, jnp.int32),
                   jax.ShapeDtypeStruct((n_tiles,), jnp.int32),
                   jax.ShapeDtypeStruct((N_EXPERTS,), jnp.int32),
                   jax.ShapeDtypeStruct((1,), jnp.int32)],
        name="moe_plan",
    )(cnt, idx, rank)


def _row_tiles(tm, d):
    return pl.BlockSpec((tm, d // LANES, LANES), lambda i: (i, 0, 0))


def _row_copy(src_ref, src_row, dst_ref, dst_row, sem):
    return pltpu.make_async_copy(src_ref.at[src_row], dst_ref.at[dst_row], sem)


def _each_row(tm, fn):
    def body(t, carry):
        for k in range(2):
            fn(t, k)
        return carry

    lax.fori_loop(0, tm, body, 0, unroll=8)


def _dispatch_kernel(ends_ref, dest_ref, prev_dest_ref, h_ref, out_ref, zero_ref, stage_ref, sem, zsem, *, tile):
    i = pl.program_id(0)
    tm = h_ref.shape[0]
    slot = i % 2

    @pl.when(i == 0)
    def _():
        zero_ref[...] = jnp.zeros(zero_ref.shape, F32)

        def clear(start):
            return pltpu.make_async_copy(zero_ref, out_ref.at[pl.ds(pl.multiple_of(start, tile), tile)], zsem)

        def used(e):
            return ends_ref[e] > (ends_ref[e - 1] if e else 0)

        def unused_tiles(fn):
            n_valid = ends_ref[N_EXPERTS - 1] >> (tile.bit_length() - 1)
            lax.fori_loop(n_valid, out_ref.shape[0] // tile, lambda j, c: (fn(clear(j * tile)), c)[1], 0)

        for e in range(N_EXPERTS):
            pl.when(used(e))(lambda e=e: clear(ends_ref[e] - tile).start())
        unused_tiles(lambda cp: cp.start())
        for e in range(N_EXPERTS):
            pl.when(used(e))(lambda e=e: clear(ends_ref[e] - tile).wait())
        unused_tiles(lambda cp: cp.wait())

    stage_ref[slot] = h_ref[...]
    _each_row(tm, lambda t, k: _row_copy(stage_ref.at[slot], t, out_ref, dest_ref[k, t], sem.at[slot]).start(priority=k))

    @pl.when(i > 0)
    def _():
        _each_row(tm, lambda t, k: _row_copy(stage_ref.at[1 - slot], t, out_ref, prev_dest_ref[k, t],
                                             sem.at[1 - slot]).wait())

    @pl.when(i == pl.num_programs(0) - 1)
    def _():
        _each_row(tm, lambda t, k: _row_copy(stage_ref.at[slot], t, out_ref, dest_ref[k, t], sem.at[slot]).wait())


def _dispatch(ends, dest, h2, n_rows, tm, tile):
    t, nt, lanes = h2.shape
    idx_spec = lambda fn: pl.BlockSpec((2, tm), fn, memory_space=pltpu.SMEM)
    return pl.pallas_call(
        functools.partial(_dispatch_kernel, tile=tile),
        grid=(t // tm,),
        in_specs=[pl.BlockSpec(memory_space=pltpu.SMEM),
                  idx_spec(lambda i: (0, i)), idx_spec(lambda i: (0, jnp.maximum(i - 1, 0))),
                  _row_tiles(tm, nt * lanes)],
        out_specs=pl.BlockSpec(memory_space=pl.ANY),
        out_shape=jax.ShapeDtypeStruct((n_rows, nt, lanes), F32),
        scratch_shapes=[pltpu.VMEM((tile, nt, lanes), F32), pltpu.VMEM((2, tm, nt, lanes), F32),
                        pltpu.SemaphoreType.DMA((2,)), pltpu.SemaphoreType.DMA(())],
        compiler_params=_cparams(("arbitrary",)),
        name="moe_dispatch",
    )(ends, dest, dest, h2)


def _expert_kernel(te_ref, nv_ref, x_ref, w1_ref, w3_ref, w2_ref, y_ref, w1b_ref, w3b_ref, w2b_ref):
    j = pl.program_id(0)
    fresh = jnp.logical_or(j == 0, te_ref[j] != te_ref[jnp.maximum(j - 1, 0)])

    @pl.when(fresh)
    def _():
        w1b_ref[...] = w1_ref[...].astype(BF16)
        w3b_ref[...] = w3_ref[...].astype(BF16)
        w2b_ref[...] = w2_ref[...].astype(BF16)

    @pl.when(j < nv_ref[0])
    def _():
        tm, nt, lanes = x_ref.shape
        xb = x_ref[...].reshape(tm, nt * lanes).astype(BF16)
        a = jnp.dot(xb, w1b_ref[...], preferred_element_type=F32)
        b = jnp.dot(xb, w3b_ref[...], preferred_element_type=F32)
        y_ref[...] = _bdot(_silu(a) * b, w2b_ref[...]).reshape(tm, nt, lanes)

    @pl.when(j >= nv_ref[0])
    def _():
        y_ref[...] = jnp.zeros(y_ref.shape, F32)


def _experts(tile_expert, n_valid, xs, w1, w3, w2, layer, tm):
    n_rows, nt, lanes = xs.shape
    d = nt * lanes
    de = w1.shape[3]
    wspec = lambda r, c: pl.BlockSpec((None, None, r, c), lambda j, te, nv: (layer, te[j], 0, 0))
    grid_spec = pltpu.PrefetchScalarGridSpec(
        num_scalar_prefetch=2,
        grid=(n_rows // tm,),
        in_specs=[pl.BlockSpec((tm, nt, lanes), lambda j, te, nv: (jnp.minimum(j, nv[0] - 1), 0, 0)),
                  wspec(d, de), wspec(d, de), wspec(de, d)],
        out_specs=pl.BlockSpec((tm, nt, lanes), lambda j, te, nv: (j, 0, 0)),
        scratch_shapes=[pltpu.VMEM((d, de), BF16), pltpu.VMEM((d, de), BF16), pltpu.VMEM((de, d), BF16)],
    )
    return pl.pallas_call(
        _expert_kernel,
        grid_spec=grid_spec,
        out_shape=jax.ShapeDtypeStruct((n_rows, nt, lanes), F32),
        compiler_params=_cparams(("arbitrary",)),
        name="moe_experts",
    )(tile_expert, n_valid, xs, w1, w3, w2)


def _combine_kernel(dest_ref, next_dest_ref, ys_ref, x1_ref, wts_ref, mod_ref, fw_ref, out_ref, buf_ref, sem,
                    *, per_token, final_norm):
    i = pl.program_id(0)
    tm, d = x1_ref.shape
    slot = i % 2

    def gather(idx_ref, s):
        return lambda t, k: _row_copy(ys_ref, idx_ref[k, t], buf_ref.at[s, k], t, sem.at[s])

    @pl.when(i == 0)
    def _():
        _each_row(tm, lambda t, k: gather(dest_ref, slot)(t, k).start(priority=k))

    @pl.when(i + 1 < pl.num_programs(0))
    def _():
        _each_row(tm, lambda t, k: gather(next_dest_ref, 1 - slot)(t, k).start(priority=k))

    _each_row(tm, lambda t, k: gather(dest_ref, slot)(t, k).wait())
    w = wts_ref[...]
    rows = lambda k: buf_ref[slot, k].reshape(tm, d)
    moe = w[:, 0:1] * rows(0) + w[:, 1:2] * rows(1)
    x2 = x1_ref[...] + _mod_rows(mod_ref, 5, per_token) * moe
    if final_norm:
        x2 = _rms(x2) * fw_ref[...]
    out_ref[...] = x2


def _combine(dest, ys, x1, wts, mod, fw, *, tm, per_token, mod_spec, final_norm):
    t, d = x1.shape
    row = lambda n: pl.BlockSpec((tm, n), lambda i: (i, 0))
    n_steps = t // tm
    idx_spec = lambda fn: pl.BlockSpec((2, tm), fn, memory_space=pltpu.SMEM)
    return pl.pallas_call(
        functools.partial(_combine_kernel, per_token=per_token, final_norm=final_norm),
        grid=(n_steps,),
        in_specs=[idx_spec(lambda i: (0, i)), idx_spec(lambda i: (0, jnp.minimum(i + 1, n_steps - 1))),
                  pl.BlockSpec(memory_space=pl.ANY),
                  row(d), row(LANES), mod_spec, _const_spec((1, d))],
        out_specs=row(d),
        out_shape=jax.ShapeDtypeStruct((t, d), F32),
        scratch_shapes=[pltpu.VMEM((2, 2, tm) + ys.shape[1:], F32), pltpu.SemaphoreType.DMA((2,))],
        compiler_params=_cparams(("arbitrary",)),
        name="moe_combine",
    )(dest, dest, ys, x1, wts, mod, fw)


def _moe(h2, idx, rank, counts, x1, wts, mod, fw, ew, layer, *, tok_tile, expert_tile, per_token, mod_spec_fn,
         final_norm):
    t = h2.shape[0]
    tm = expert_tile
    n_rows = ((2 * t + N_EXPERTS * (tm - 1)) // tm + 1) * tm
    cnt = counts[:N_EXPERTS, 0].astype(jnp.int32)
    dest, tile_expert, ends, n_valid = _plan(cnt, idx, rank, tm, n_rows // tm)
    xs = _dispatch(ends, dest, h2, n_rows, tok_tile, tm)
    ys = _experts(tile_expert, n_valid, xs, ew["w1"], ew["w3"], ew["w2"], layer, tm)
    return _combine(dest, ys, x1, wts, mod, fw, tm=tok_tile, per_token=per_token,
                    mod_spec=mod_spec_fn(tok_tile), final_norm=final_norm)


def _pad_lanes(a, n=LANES):
    return jnp.pad(a, [(0, 0)] * (a.ndim - 1) + [(0, n - a.shape[-1])])


def kernel(x_prompt, x_sample, state_ssm, state_conv, state_s5_re, state_s5_im, c_prompt, c_sample, w_in, conv_w, conv_b, dt_bias, a_log, d_ssd, ssd_norm_w, w_out_a, s5_a_re, s5_a_im, s5_log_dt, s5_b_re, s5_b_im, s5_c_re, s5_c_im, s5_d, w_glu, w_out, norm1_w, norm2_w, w_ada, b_ada, w_router, b_router, w1, w3, w2, final_norm_w):
    nb, seq, d = x_prompt.shape
    ns, steps, _ = x_sample.shape
    depth = w_in.shape[0]
    d_in = SSD_HEADS * SSD_HEAD_DIM
    cdim = conv_w.shape[2]
    s5w = S5_GROUPS * S5_GROUP_CH
    s5n = S5_GROUPS * S5_STATE
    k1 = SSD_CONV - 1
    tp = nb * seq
    ts = ns * steps

    mod = _ada_mod(jnp.concatenate([c_prompt, c_sample], axis=0), w_ada, b_ada)
    head_expand = jnp.tile(jnp.repeat(jnp.eye(LANES, SSD_HEADS, dtype=BF16), SSD_HEAD_DIM, axis=1), (3, 1))
    rw = {"w_router": _pad_lanes(w_router), "b_router": _pad_lanes(b_router[None, :]).reshape(LANES, 1)}

    xp = x_prompt.reshape(tp, d)
    xsm = jnp.transpose(x_sample, (1, 0, 2)).reshape(ts, d)
    outs = {k: [] for k in ("ssm_p", "conv_p", "conv_s", "re_p", "re_s", "im_p", "im_s")}
    ssm_s = None
    zero_state = jnp.zeros((nb, s5n), F32)

    for l in range(depth):
        o0 = 0
        o1 = d_in
        o2 = o1 + cdim
        o3 = o2 + SSD_HEADS
        o4 = o3 + s5w
        wl = w_in[l].astype(BF16)
        ws = (wl[:, o0:o1], wl[:, o1:o2], _pad_lanes(wl[:, o2:o3]), wl[:, o3:o4], wl[:, o4:])
        p = {"conv_w": conv_w[l], "conv_b": conv_b[l][None, :], "dt_bias": _pad_lanes(dt_bias[l][None, :]),
             "a_log": _pad_lanes(a_log[l][None, :]),
             "d_skip_e": jnp.repeat(d_ssd[l], SSD_HEAD_DIM)[None, :],
             "ssd_norm_w": ssd_norm_w[l][None, :], "head_expand": head_expand}
        chan_rows = lambda a, perm: jnp.transpose(a, perm).reshape(S5_GROUP_CH, s5n)
        ab, wb, c_re_t, c_im_t = _s5_params(
            s5_a_re[l].reshape(1, s5n), s5_a_im[l].reshape(1, s5n), jnp.repeat(s5_log_dt[l], S5_STATE)[None, :],
            chan_rows(s5_b_re[l], (2, 0, 1)), chan_rows(s5_b_im[l], (2, 0, 1)),
            chan_rows(s5_c_re[l], (1, 0, 2)), chan_rows(s5_c_im[l], (1, 0, 2)))
        sp = {"ab": ab, "wb": wb, "c_re_t": c_re_t, "c_im_t": c_im_t,
              "d": s5_d[l][None, :], "w_glu": w_glu[l].astype(BF16)}
        lw = {"w_out_a": w_out_a[l].astype(BF16), "w_out": w_out[l].astype(BF16), "norm2_w": norm2_w[l][None, :]}
        ew = {"w1": w1, "w3": w3, "w2": w2}
        final = l == depth - 1
        fw = final_norm_w[None, :]

        mod_p = mod[l, :nb].reshape(nb, 6, d)
        mod_p_spec = lambda tm: pl.BlockSpec((None, 6, d), lambda i: (i // (seq // tm), 0, 0))
        z, xbc, dtr, u, gate = _inproj(
            xp, mod_p, norm1_w[l][None, :], ws, tm=ROW_TILE, per_token=False, mod_spec=mod_p_spec(ROW_TILE))
        yn, nconv, nssm = _ssd_prompt(xbc, dtr, z, p, nb, seq)
        yb, nre, nim = _s5(u.reshape(nb, seq, s5w), zero_state, zero_state, sp, nb, S5_STEPS, True)
        x1, h2, idx, rank, wts, counts = _merge_router(
            yn, yb.reshape(tp, d), gate, xp, mod_p, lw, rw, tm=ROW_TILE, per_token=False,
            mod_spec=mod_p_spec(ROW_TILE))
        xp = _moe(h2, idx, rank, counts, x1, wts, mod_p, fw, ew, l, tok_tile=MOE_TILE, expert_tile=EXPERT_TILE, per_token=False,
                  mod_spec_fn=mod_p_spec, final_norm=final)
        outs["ssm_p"].append(nssm)
        outs["conv_p"].append(nconv)
        outs["re_p"].append(nre.reshape(nb, S5_GROUPS, S5_STATE))
        outs["im_p"].append(nim.reshape(nb, S5_GROUPS, S5_STATE))

        mod_s = jnp.transpose(mod[l, nb:].reshape(ns, 6, d), (1, 0, 2))
        mod_s_spec = lambda tm: pl.BlockSpec((6, tm, d), lambda i: (0, i % (ns // tm), 0))
        z, xbc, dtr, u, gate = _inproj(
            xsm, mod_s, norm1_w[l][None, :], ws, tm=ns, per_token=True, mod_spec=mod_s_spec(ns))
        conv0_tm = jnp.transpose(state_conv[l], (1, 0, 2)).reshape(k1 * ns, cdim)
        yn, nconv_tm, ssm_s = _ssd_sample(xbc, dtr, z, conv0_tm, state_ssm, ssm_s, l, p, ns, steps)
        yb, nre, nim = _s5(u, state_s5_re[l].reshape(ns, s5n), state_s5_im[l].reshape(ns, s5n), sp, ns, steps,
                           False)
        x1, h2, idx, rank, wts, counts = _merge_router(
            yn, yb, gate, xsm, mod_s, lw, rw, tm=ns, per_token=True, mod_spec=mod_s_spec(ns))
        xsm = _moe(h2, idx, rank, counts, x1, wts, mod_s, fw, ew, l, tok_tile=ns, expert_tile=ns, per_token=True,
                   mod_spec_fn=mod_s_spec, final_norm=final)
        outs["conv_s"].append(jnp.transpose(nconv_tm.reshape(k1, ns, cdim), (1, 0, 2)))
        outs["re_s"].append(nre.reshape(ns, S5_GROUPS, S5_STATE))
        outs["im_s"].append(nim.reshape(ns, S5_GROUPS, S5_STATE))

    y_prompt = xp.reshape(nb, seq, d)
    y_sample = jnp.transpose(xsm.reshape(steps, ns, d), (1, 0, 2))
    st = lambda k: jnp.stack(outs[k])
    return (y_prompt, y_sample, st("ssm_p"), ssm_s, st("conv_p"), st("conv_s"),
            st("re_p"), st("re_s"), st("im_p"), st("im_s"))
```

```python
import functools

import jax
import jax.numpy as jnp
from jax import lax
from jax.experimental import pallas as pl
from jax.experimental.pallas import tpu as pltpu

F32 = jnp.float32
BF16 = jnp.bfloat16
EPS = 1e-6

SSD_HEADS = 16
SSD_HEAD_DIM = 64
SSD_GROUPS = 2
SSD_STATE = 128
SSD_CONV = 4
SSD_CHUNK = 128
S5_GROUPS = 32
S5_GROUP_CH = 16
S5_STATE = 64
N_EXPERTS = 16
EXPERTS_PER_GROUP = 4
N_EXPERT_GROUPS = 4

LANES = 128
SUBLANES = 8
MXU_TILE = 256
VMEM_LIMIT_BYTES = 56 * 1024 * 1024

ROW_TILE = 512
MOE_TILE = 256
EXPERT_TILE = 512
S5_STEPS = 64
S5_SCAN_LANES = 512
ROUTER_SUBTILES = 2
SSD_SAMPLE_SEQS = 4


def _cparams(sem):
    return pltpu.CompilerParams(dimension_semantics=sem, vmem_limit_bytes=VMEM_LIMIT_BYTES)


def _bdot(a, b):
    return jnp.dot(a.astype(BF16), b.astype(BF16), preferred_element_type=F32)


def _bdot_nt(a, b):
    return lax.dot_general(a.astype(BF16), b.astype(BF16), (((1,), (1,)), ((), ())),
                           preferred_element_type=F32)


def _split3(x):
    hi = x.astype(BF16)
    r1 = x - hi.astype(F32)
    mid = r1.astype(BF16)
    lo = (r1 - mid.astype(F32)).astype(BF16)
    return hi, mid, lo


def _dot3_lhs(x, m3):
    return jnp.dot(jnp.concatenate(_split3(x), axis=1), m3, preferred_element_type=F32)


def _dot3_rhs(m, x):
    return jnp.dot(jnp.concatenate([m] * 3, axis=1), jnp.concatenate(_split3(x), axis=0),
                   preferred_element_type=F32)


def _sigmoid(x):
    return jax.nn.sigmoid(x)


def _silu(x):
    return x * _sigmoid(x)


def _softplus(x):
    return jnp.maximum(x, 0.0) + jnp.log1p(jnp.exp(-jnp.abs(x)))


def _gelu_tanh(x):
    c = 0.7978845608028654
    return 0.5 * x * (1.0 + jnp.tanh(c * (x + 0.044715 * (x * x * x))))


def _rms(x):
    return x * lax.rsqrt(jnp.mean(x * x, axis=-1, keepdims=True) + EPS)


def _ada_kernel(c_ref, w_ref, b_ref, o_ref):
    o_ref[0] = _bdot(_silu(c_ref[...]), w_ref[0]) + b_ref[0]


def _ada_mod(c_all, w_ada, b_ada):
    depth, d, n = w_ada.shape
    nseq = c_all.shape[0]
    tn = 1536
    return pl.pallas_call(
        _ada_kernel,
        grid=(depth, n // tn),
        in_specs=[pl.BlockSpec((nseq, d), lambda l, j: (0, 0)),
                  pl.BlockSpec((1, d, tn), lambda l, j: (l, 0, j)),
                  pl.BlockSpec((1, 1, tn), lambda l, j: (l, 0, j))],
        out_specs=pl.BlockSpec((1, nseq, tn), lambda l, j: (l, 0, j)),
        out_shape=jax.ShapeDtypeStruct((depth, nseq, n), F32),
        compiler_params=_cparams(("arbitrary", "arbitrary")),
        name="ada_mod",
    )(c_all, w_ada, b_ada.reshape(depth, 1, n))


def _mod_rows(mod_ref, j, per_token):
    if per_token:
        return mod_ref[j]
    return mod_ref[j:j + 1, :]


def _inproj_kernel(x_ref, mod_ref, nw_ref, wz_ref, wx_ref, wdt_ref, wu_ref, wg_ref,
                   z_ref, xbc_ref, dt_ref, u_ref, g_ref, *, per_token):
    sh = _mod_rows(mod_ref, 0, per_token)
    sc = _mod_rows(mod_ref, 1, per_token)
    h = (_rms(x_ref[...]) * nw_ref[...] * (1.0 + sc) + sh).astype(BF16)
    d = functools.partial(jnp.dot, preferred_element_type=F32)
    z_ref[...] = d(h, wz_ref[...])
    xbc_ref[...] = d(h, wx_ref[...])
    dt_ref[...] = d(h, wdt_ref[...])
    u_ref[...] = d(h, wu_ref[...])
    g_ref[...] = d(h, wg_ref[...])


def _const_spec(shape):
    nd = len(shape)
    return pl.BlockSpec(shape, lambda *_: (0,) * nd)


def _inproj(x, mod, nw, ws, *, tm, per_token, mod_spec):
    t, d = x.shape
    wz, wx, wdt, wu, wg = ws
    row = lambda n: pl.BlockSpec((tm, n), lambda i: (i, 0))
    return pl.pallas_call(
        functools.partial(_inproj_kernel, per_token=per_token),
        grid=(t // tm,),
        in_specs=[row(d), mod_spec, _const_spec((1, d)),
                  _const_spec(wz.shape), _const_spec(wx.shape), _const_spec(wdt.shape),
                  _const_spec(wu.shape), _const_spec(wg.shape)],
        out_specs=[row(wz.shape[1]), row(wx.shape[1]), row(wdt.shape[1]), row(wu.shape[1]), row(wg.shape[1])],
        out_shape=[jax.ShapeDtypeStruct((t, wz.shape[1]), F32),
                   jax.ShapeDtypeStruct((t, wx.shape[1]), F32),
                   jax.ShapeDtypeStruct((t, wdt.shape[1]), F32),
                   jax.ShapeDtypeStruct((t, wu.shape[1]), F32),
                   jax.ShapeDtypeStruct((t, wg.shape[1]), F32)],
        compiler_params=_cparams(("arbitrary",)),
        name="inproj",
    )(x, mod, nw, wz, wx, wdt, wu, wg)


def _gated_group_norm(y, z, nw):
    y = y * _silu(z)
    half = y.shape[1] // SSD_GROUPS
    parts = [_rms(y[:, g * half:(g + 1) * half]) for g in range(SSD_GROUPS)]
    return jnp.concatenate(parts, axis=1) * nw


def _ssd_prompt_kernel(xbc_ref, dt_ref, z_ref, cw_ref, cb_ref, dtb_ref, alog_ref, dsk_ref, nw_ref,
                       e_ref, yn_ref, conv_ref, ssm_ref, ext_ref, st_ref):
    c = pl.program_id(1)
    q = SSD_CHUNK
    d_in = SSD_HEADS * SSD_HEAD_DIM
    gw = SSD_STATE
    hpg = SSD_HEADS // SSD_GROUPS
    grows = hpg * SSD_HEAD_DIM

    @pl.when(c == 0)
    def _():
        ext_ref[...] = jnp.zeros(ext_ref.shape, F32)
        st_ref[...] = jnp.zeros(st_ref.shape, F32)

    xbc = xbc_ref[...]
    sub = lax.broadcasted_iota(jnp.int32, (SUBLANES, xbc.shape[1]), 0)
    taps = [cw_ref[k:k + 1, :] for k in range(SSD_CONV)]
    prev_rolled = [pltpu.roll(ext_ref[...], j, axis=0) for j in range(1, SSD_CONV)]
    acc_tiles = []
    for r0 in range(0, q, SUBLANES):
        cur = xbc[r0:r0 + SUBLANES, :]
        rolled = [pltpu.roll(cur, j, axis=0) for j in range(1, SSD_CONV)]
        acc = cb_ref[...] + taps[SSD_CONV - 1] * cur
        for j in range(1, SSD_CONV):
            acc = acc + taps[SSD_CONV - 1 - j] * jnp.where(sub < j, prev_rolled[j - 1], rolled[j - 1])
        acc_tiles.append(acc)
        prev_rolled = rolled
    ext_ref[...] = xbc[q - SUBLANES:q, :]
    v = _silu(jnp.concatenate(acc_tiles, axis=0))
    xs = v[:, :d_in]
    bm = v[:, d_in:d_in + SSD_GROUPS * gw]
    cm = v[:, d_in + SSD_GROUPS * gw:]

    dt = _softplus(dt_ref[...] + dtb_ref[...])
    a = -jnp.exp(alog_ref[...])
    da = dt * a
    rid = lax.broadcasted_iota(jnp.int32, (q, q), 0)
    cid = lax.broadcasted_iota(jnp.int32, (q, q), 1)
    causal = rid >= cid
    tri = jnp.where(causal, 1.0, 0.0).astype(BF16)
    a_cs = _dot3_rhs(tri, da)
    a_cs_t = a_cs.T
    a_last = a_cs[q - 1:q, :]
    per_head = jnp.concatenate([dt, jnp.exp(a_last - a_cs) * dt, jnp.exp(a_cs)], axis=0)
    per_chan = _dot3_lhs(per_head, e_ref[...])
    x_dt = xs * per_chan[0:q]
    x_end = xs * per_chan[q:2 * q]
    eacs_e = per_chan[2 * q:3 * q]
    x_end_t = x_end.T.astype(BF16)
    lane = lax.broadcasted_iota(jnp.int32, (q, 2 * SSD_HEAD_DIM), 1)
    first = lane < SSD_HEAD_DIM

    y_diag = []
    y_off = []
    for g in range(SSD_GROUPS):
        bg = bm[:, g * gw:(g + 1) * gw].astype(BF16)
        cg = cm[:, g * gw:(g + 1) * gw].astype(BF16)
        cb = _bdot_nt(cg, bg)
        sg = st_ref[g * grows:(g + 1) * grows, :]
        y_off.append(_bdot_nt(cg, sg))
        for pair in range(hpg // 2):
            ms = []
            for r in (2 * pair, 2 * pair + 1):
                h = g * hpg + r
                diff = a_cs[:, h:h + 1] - a_cs_t[h:h + 1, :]
                decay = jnp.where(causal, jnp.exp(jnp.where(causal, diff, 0.0)), 0.0)
                ms.append((cb * decay).astype(BF16))
            h0 = g * hpg + 2 * pair
            xp = x_dt[:, h0 * SSD_HEAD_DIM:(h0 + 2) * SSD_HEAD_DIM]
            rhs = jnp.concatenate([jnp.where(first, xp, 0.0), jnp.where(first, 0.0, xp)], axis=0)
            y_diag.append(jnp.dot(jnp.concatenate(ms, axis=1), rhs.astype(BF16), preferred_element_type=F32))
        contrib = jnp.dot(x_end_t[g * grows:(g + 1) * grows, :], bg, preferred_element_type=F32)
        for r in range(hpg):
            h = g * hpg + r
            rows = slice(h * SSD_HEAD_DIM, (h + 1) * SSD_HEAD_DIM)
            dec = jnp.exp(a_cs_t[h:h + 1, q - 1:q])
            st_ref[rows, :] = st_ref[rows, :] * dec + contrib[r * SSD_HEAD_DIM:(r + 1) * SSD_HEAD_DIM, :]

    y = (jnp.concatenate(y_diag, axis=1) + jnp.concatenate(y_off, axis=1) * eacs_e
         + dsk_ref[...] * xs)
    yn_ref[...] = _gated_group_norm(y, z_ref[...], nw_ref[...]).astype(BF16)

    @pl.when(c == pl.num_programs(1) - 1)
    def _():
        conv_ref[0] = ext_ref[pl.ds(SUBLANES - (SSD_CONV - 1), SSD_CONV - 1), :]
        ssm_ref[0] = st_ref[...].reshape(SSD_HEADS, SSD_HEAD_DIM, SSD_STATE)


def _ssd_prompt(xbc, dt_raw, z, p, nb, seq):
    t, cdim = xbc.shape
    nc = seq // SSD_CHUNK
    q = SSD_CHUNK
    d_in = z.shape[1]
    row = lambda n: pl.BlockSpec((q, n), lambda b, c: (b * nc + c, 0))
    return pl.pallas_call(
        _ssd_prompt_kernel,
        grid=(nb, nc),
        in_specs=[row(cdim), row(LANES), row(d_in),
                  _const_spec((SSD_CONV, cdim)), _const_spec((1, cdim)), _const_spec((1, LANES)),
                  _const_spec((1, LANES)), _const_spec((1, d_in)), _const_spec((1, d_in)),
                  _const_spec((3 * LANES, d_in))],
        out_specs=[row(d_in),
                   pl.BlockSpec((1, SSD_CONV - 1, cdim), lambda b, c: (b, 0, 0)),
                   pl.BlockSpec((1, SSD_HEADS, SSD_HEAD_DIM, SSD_STATE), lambda b, c: (b, 0, 0, 0))],
        out_shape=[jax.ShapeDtypeStruct((t, d_in), BF16),
                   jax.ShapeDtypeStruct((nb, SSD_CONV - 1, cdim), F32),
                   jax.ShapeDtypeStruct((nb, SSD_HEADS, SSD_HEAD_DIM, SSD_STATE), F32)],
        scratch_shapes=[pltpu.VMEM((SUBLANES, cdim), F32),
                        pltpu.VMEM((d_in, SSD_STATE), F32)],
        compiler_params=_cparams(("arbitrary", "arbitrary")),
        name="ssd_prompt",
    )(xbc, dt_raw, z, p["conv_w"], p["conv_b"], p["dt_bias"], p["a_log"], p["d_skip_e"], p["ssd_norm_w"],
      p["head_expand"])


def _ssd_sample_a_kernel(xbc_ref, conv0_ref, dt_ref, cw_ref, cb_ref, dtb_ref, alog_ref, dsk_ref, e_ref,
                         ydiag_ref, eacs_ref, wt_ref, bs_ref, c8_ref, dec_ref, nconv_ref, *, steps, nseq):
    d_in = SSD_HEADS * SSD_HEAD_DIM
    gw = SSD_STATE
    hpg = SSD_HEADS // SSD_GROUPS
    k1 = SSD_CONV - 1
    slab = lambda ref, i: ref[i * nseq:(i + 1) * nseq, :]
    xp = [slab(conv0_ref, i) for i in range(k1)] + [slab(xbc_ref, i) for i in range(steps)]
    for i in range(k1):
        nconv_ref[i * nseq:(i + 1) * nseq, :] = xp[steps + i]
    e = e_ref[...]
    a = -jnp.exp(alog_ref[...])
    head = lax.broadcasted_iota(jnp.int32, (nseq, LANES), 1)
    xs, bm, cm, dt, a_cs = [], [], [], [], []
    run = jnp.zeros((nseq, LANES), F32)
    for t in range(steps):
        acc = cb_ref[...]
        for k in range(SSD_CONV):
            acc = acc + cw_ref[k:k + 1, :] * xp[t + k]
        v = _silu(acc)
        xs.append(v[:, :d_in])
        bm.append(v[:, d_in:d_in + SSD_GROUPS * gw])
        cm.append(v[:, d_in + SSD_GROUPS * gw:])
        dt.append(_softplus(slab(dt_ref, t) + dtb_ref[...]))
        run = run + dt[t] * a
        a_cs.append(run)
    dec_ref[...] = jnp.exp(a_cs[steps - 1])
    for t in range(steps):
        y = dsk_ref[...] * xs[t]
        for s in range(t + 1):
            cbs = [jnp.sum(cm[t][:, g * gw:(g + 1) * gw] * bm[s][:, g * gw:(g + 1) * gw],
                           axis=-1, keepdims=True) for g in range(SSD_GROUPS)]
            coef = jnp.exp(a_cs[t] - a_cs[s]) * dt[s] * jnp.where(head < hpg, cbs[0], cbs[1])
            y = y + _dot3_lhs(coef, e) * xs[s]
        ydiag_ref[t * nseq:(t + 1) * nseq, :] = y
        eacs_ref[t * nseq:(t + 1) * nseq, :] = _dot3_lhs(jnp.exp(a_cs[t]), e)
        w = xs[t] * _dot3_lhs(jnp.exp(a_cs[steps - 1] - a_cs[t]) * dt[t], e)
        wt_ref[:, t * nseq:(t + 1) * nseq] = w.T.astype(BF16)
        bs_ref[t * nseq:(t + 1) * nseq, :] = bm[t]
        for g in range(SSD_GROUPS):
            j = g * steps + t
            c8_ref[j * nseq:(j + 1) * nseq, :] = cm[t][:, g * gw:(g + 1) * gw]


def _ssd_sample_b_kernel(dec_ref, h0_ref, wt_ref, bs_ref, c8_ref, *rest, steps, nseq, layer):
    yoff_ref, ssm_ref = rest[-2:]
    if layer:
        ssm_ref[0:layer] = rest[0][...]
    hpg = SSD_HEADS // SSD_GROUPS
    grows = hpg * SSD_HEAD_DIM
    gw = SSD_STATE
    rid = lax.broadcasted_iota(jnp.int32, (steps * nseq, gw), 0)
    for s in range(h0_ref.shape[0]):
        b = pl.program_id(0) * h0_ref.shape[0] + s
        h0 = h0_ref[s].reshape(SSD_HEADS * SSD_HEAD_DIM, SSD_STATE)
        c8 = c8_ref[pl.ds(b, SSD_GROUPS * steps, stride=nseq), :]
        yoff_ref[s] = _bdot_nt(c8, h0)
        mine = rid == b
        for t in range(1, steps):
            mine = jnp.logical_or(mine, rid == b + t * nseq)
        for g in range(SSD_GROUPS):
            bsel = jnp.where(mine, bs_ref[:, g * gw:(g + 1) * gw], 0.0).astype(BF16)
            contrib = jnp.dot(wt_ref[g * grows:(g + 1) * grows, :], bsel, preferred_element_type=F32)
            for r in range(hpg):
                h = g * hpg + r
                rows = slice(h * SSD_HEAD_DIM, (h + 1) * SSD_HEAD_DIM)
                ssm_ref[layer, s, h] = (h0[rows, :] * dec_ref[b, h]
                                        + contrib[r * SSD_HEAD_DIM:(r + 1) * SSD_HEAD_DIM, :])


def _ssd_sample_c_kernel(ydiag_ref, yoff_ref, eacs_ref, z_ref, nw_ref, yn_ref, *, steps, nseq):
    d_in = SSD_HEADS * SSD_HEAD_DIM
    half = d_in // SSD_GROUPS
    for t in range(steps):
        lo = yoff_ref[:, t * d_in:t * d_in + half]
        hi = yoff_ref[:, (steps + t) * d_in + half:(steps + t + 1) * d_in]
        rows = slice(t * nseq, (t + 1) * nseq)
        y = ydiag_ref[rows, :] + jnp.concatenate([lo, hi], axis=1) * eacs_ref[rows, :]
        yn_ref[rows, :] = _gated_group_norm(y, z_ref[rows, :], nw_ref[...]).astype(BF16)


def _ssd_sample(xbc, dt_raw, z, conv0_tm, ssm_all, ssm_done, layer, p, nseq, steps):
    t, cdim = xbc.shape
    d_in = z.shape[1]
    k1 = SSD_CONV - 1
    nj = SSD_GROUPS * steps
    a_out = pl.pallas_call(
        functools.partial(_ssd_sample_a_kernel, steps=steps, nseq=nseq),
        out_shape=[jax.ShapeDtypeStruct((t, d_in), F32),
                   jax.ShapeDtypeStruct((t, d_in), F32),
                   jax.ShapeDtypeStruct((d_in, t), BF16),
                   jax.ShapeDtypeStruct((t, SSD_GROUPS * SSD_STATE), F32),
                   jax.ShapeDtypeStruct((nj * nseq, SSD_STATE), F32),
                   jax.ShapeDtypeStruct((nseq, LANES), F32),
                   jax.ShapeDtypeStruct((k1 * nseq, cdim), F32)],
        compiler_params=pltpu.CompilerParams(vmem_limit_bytes=VMEM_LIMIT_BYTES),
        name="ssd_sample_a",
    )(xbc, conv0_tm, dt_raw, p["conv_w"], p["conv_b"], p["dt_bias"], p["a_log"], p["d_skip_e"],
      p["head_expand"])
    ydiag, eacs_e, wt, bs, c8, dec, nconv = a_out
    state_blk = (SSD_HEADS, SSD_HEAD_DIM, SSD_STATE)
    operands = [dec[:, :SSD_HEADS], ssm_all, wt, bs, c8]
    sb = SSD_SAMPLE_SEQS
    in_specs = [pl.BlockSpec(memory_space=pltpu.SMEM),
                pl.BlockSpec((None, sb) + state_blk, lambda b: (layer, b, 0, 0, 0)),
                _const_spec(wt.shape), _const_spec(bs.shape), _const_spec(c8.shape)]
    if layer:
        operands.append(ssm_done)
        in_specs.append(pl.BlockSpec((layer, sb) + state_blk, lambda b: (0, b, 0, 0, 0)))
    yoff, nssm = pl.pallas_call(
        functools.partial(_ssd_sample_b_kernel, steps=steps, nseq=nseq, layer=layer),
        grid=(nseq // sb,),
        in_specs=in_specs,
        out_specs=[pl.BlockSpec((sb, nj, d_in), lambda b: (b, 0, 0)),
                   pl.BlockSpec((layer + 1, sb) + state_blk, lambda b: (0, b, 0, 0, 0))],
        out_shape=[jax.ShapeDtypeStruct((nseq, nj, d_in), F32),
                   jax.ShapeDtypeStruct((layer + 1, nseq) + state_blk, F32)],
        compiler_params=_cparams(("arbitrary",)),
        name="ssd_sample_b",
    )(*operands)
    yn = pl.pallas_call(
        functools.partial(_ssd_sample_c_kernel, steps=steps, nseq=nseq),
        out_shape=jax.ShapeDtypeStruct((t, d_in), BF16),
        compiler_params=pltpu.CompilerParams(vmem_limit_bytes=VMEM_LIMIT_BYTES),
        name="ssd_sample_c",
    )(ydiag, yoff.reshape(nseq, nj * d_in), eacs_e, z, p["ssd_norm_w"])
    return yn, nconv, nssm


def _s5_param_kernel(are_ref, aim_ref, ldt_ref, bre_ref, bim_ref, cre_ref, cim_ref,
                     ab_ref, wb_ref, cret_ref, cimt_ref):
    n = are_ref.shape[1]
    w = wb_ref.shape[0]
    a_re = are_ref[...]
    a_im = aim_ref[...]
    dt = jnp.exp(ldt_ref[...])
    mag = jnp.exp(dt * a_re)
    ab_re = mag * jnp.cos(dt * a_im)
    ab_im = mag * jnp.sin(dt * a_im)
    den = a_re * a_re + a_im * a_im
    f_re = ((ab_re - 1.0) * a_re + ab_im * a_im) / den
    f_im = (ab_im * a_re - (ab_re - 1.0) * a_im) / den
    ab_ref[0:1, :] = ab_re
    ab_ref[1:2, :] = ab_im
    b_re = bre_ref[...]
    b_im = bim_ref[...]
    row_g = lax.shift_right_logical(lax.broadcasted_iota(jnp.int32, (w, n), 0), S5_GROUP_CH.bit_length() - 1)
    col_g = lax.shift_right_logical(lax.broadcasted_iota(jnp.int32, (w, n), 1), S5_STATE.bit_length() - 1)
    same = row_g == col_g

    def spread(rows):
        return jnp.where(same, jnp.concatenate([rows] * S5_GROUPS, axis=0), 0.0).astype(BF16)

    wb_ref[:, 0:n] = spread(f_re * b_re - f_im * b_im)
    wb_ref[:, n:2 * n] = spread(f_re * b_im + f_im * b_re)
    cret_ref[...] = spread(cre_ref[...])
    cimt_ref[...] = spread(cim_ref[...])


def _s5_params(a_re, a_im, ldt, bt_re, bt_im, ct_re, ct_im):
    n = a_re.shape[1]
    w = S5_GROUPS * S5_GROUP_CH
    return pl.pallas_call(
        _s5_param_kernel,
        out_shape=[jax.ShapeDtypeStruct((2, n), F32),
                   jax.ShapeDtypeStruct((w, 2 * n), BF16),
                   jax.ShapeDtypeStruct((w, n), BF16),
                   jax.ShapeDtypeStruct((w, n), BF16)],
        compiler_params=pltpu.CompilerParams(vmem_limit_bytes=VMEM_LIMIT_BYTES),
        name="s5_params",
    )(a_re, a_im, ldt, bt_re, bt_im, ct_re, ct_im)


def _s5_kernel(u_ref, h0re_ref, h0im_ref, ab_ref, wb_ref, cret_ref, cimt_ref, d_ref, wglu_ref,
               yb_ref, nre_ref, nim_ref, xh_ref, st_ref, *, rows_per_step, steps, batch_major):
    i = pl.program_id(0)
    n = ab_ref.shape[1]
    w = wb_ref.shape[0]
    r = rows_per_step

    @pl.when(i == 0)
    def _():
        st_ref[0] = h0re_ref[...]
        st_ref[1] = h0im_ref[...]

    if batch_major:
        u = pltpu.einshape("btd->(tb)d", u_ref[...])
    else:
        u = u_ref[...]
    ub = u.astype(BF16)
    for j in range(2 * n // MXU_TILE):
        c0 = (j * MXU_TILE) % n
        k0 = (c0 // S5_STATE * S5_GROUP_CH) // LANES * LANES
        cols = slice(j * MXU_TILE, (j + 1) * MXU_TILE)
        xh_ref[:, cols] = jnp.dot(ub[:, k0:k0 + LANES], wb_ref[k0:k0 + LANES, cols], preferred_element_type=F32)
    for c0 in range(0, n, S5_SCAN_LANES):
        re = slice(c0, c0 + S5_SCAN_LANES)
        im = slice(n + c0, n + c0 + S5_SCAN_LANES)
        a_re = jnp.broadcast_to(ab_ref[0:1, re], (r, S5_SCAN_LANES))
        a_im = jnp.broadcast_to(ab_ref[1:2, re], (r, S5_SCAN_LANES))

        def step(t, carry, re=re, im=im, a_re=a_re, a_im=a_im):
            h_re, h_im = carry
            rows = pl.ds(pl.multiple_of(t * r, r), r)
            n_re = a_re * h_re - a_im * h_im + xh_ref[rows, re]
            n_im = a_re * h_im + a_im * h_re + xh_ref[rows, im]
            xh_ref[rows, re] = n_re
            xh_ref[rows, im] = n_im
            return n_re, n_im

        carry = (st_ref[0, :, re], st_ref[1, :, re])
        if steps <= 8:
            for t in range(steps):
                carry = step(t, carry)
        else:
            carry = lax.fori_loop(0, steps, step, carry, unroll=4)
        st_ref[0, :, re] = carry[0]
        st_ref[1, :, re] = carry[1]

    n_ct = w // MXU_TILE
    ys = []
    for j in range(n_ct):
        rows = slice(j * MXU_TILE, (j + 1) * MXU_TILE)
        ks = slice(j * (n // n_ct), (j + 1) * (n // n_ct))
        ks_im = slice(n + j * (n // n_ct), n + (j + 1) * (n // n_ct))
        ys.append(_bdot_nt(xh_ref[:, ks], cret_ref[rows, ks]) - _bdot_nt(xh_ref[:, ks_im], cimt_ref[rows, ks]))
    y = jnp.concatenate(ys, axis=1) + d_ref[...] * u
    pre = _bdot(_gelu_tanh(y), wglu_ref[...])
    half = pre.shape[1] // 2
    yb = pre[:, :half] * _sigmoid(pre[:, half:])
    if batch_major:
        yb_ref[...] = pltpu.einshape("(tb)d->btd", yb, b=r)
    else:
        yb_ref[...] = yb

    @pl.when(i == pl.num_programs(0) - 1)
    def _():
        nre_ref[...] = st_ref[0]
        nim_ref[...] = st_ref[1]


def _s5(u, h0_re, h0_im, sp, rows_per_step, steps, batch_major):
    n = sp["ab"].shape[1]
    w = u.shape[-1]
    tm = rows_per_step * steps
    d_model = sp["w_glu"].shape[1] // 2
    if batch_major:
        nb, seq, _ = u.shape
        grid = (seq // steps,)
        u_spec = pl.BlockSpec((nb, steps, w), lambda i: (0, i, 0))
        yb_spec = pl.BlockSpec((nb, steps, d_model), lambda i: (0, i, 0))
        yb_shape = (nb, seq, d_model)
    else:
        grid = (u.shape[0] // tm,)
        u_spec = pl.BlockSpec((tm, w), lambda i: (i, 0))
        yb_spec = pl.BlockSpec((tm, d_model), lambda i: (i, 0))
        yb_shape = (u.shape[0], d_model)
    return pl.pallas_call(
        functools.partial(_s5_kernel, rows_per_step=rows_per_step, steps=steps, batch_major=batch_major),
        grid=grid,
        in_specs=[u_spec,
                  _const_spec(h0_re.shape), _const_spec(h0_im.shape), _const_spec(sp["ab"].shape),
                  _const_spec(sp["wb"].shape), _const_spec(sp["c_re_t"].shape), _const_spec(sp["c_im_t"].shape),
                  _const_spec((1, w)), _const_spec(sp["w_glu"].shape)],
        out_specs=[yb_spec, _const_spec(h0_re.shape), _const_spec(h0_im.shape)],
        out_shape=[jax.ShapeDtypeStruct(yb_shape, F32),
                   jax.ShapeDtypeStruct(h0_re.shape, F32),
                   jax.ShapeDtypeStruct(h0_im.shape, F32)],
        scratch_shapes=[pltpu.VMEM((tm, 2 * n), F32),
                        pltpu.VMEM((2, rows_per_step, n), F32)],
        compiler_params=_cparams(("arbitrary",)),
        name="s5_scan",
    )(u, h0_re, h0_im, sp["ab"], sp["wb"], sp["c_re_t"], sp["c_im_t"], sp["d"], sp["w_glu"])


def _merge_router_kernel(yn_ref, yb_ref, gate_ref, x_ref, mod_ref, woa_ref, wo_ref, nw_ref, wr_ref, br_ref,
                         x1_ref, h2_ref, idx_ref, rank_ref, wts_ref, cnt_ref, run_ref, *, per_token):
    i = pl.program_id(0)
    tm = x_ref.shape[0]

    @pl.when(i == 0)
    def _():
        run_ref[...] = jnp.zeros(run_ref.shape, F32)

    sub = tm // ROUTER_SUBTILES if tm % (ROUTER_SUBTILES * LANES) == 0 else tm
    for r0 in range(0, tm, sub):
        rows = pl.ds(r0, sub)
        _route_rows(yn_ref.at[rows], yb_ref.at[rows], gate_ref.at[rows], x_ref.at[rows],
                    mod_ref.at[:, rows] if per_token else mod_ref, woa_ref, wo_ref, nw_ref, wr_ref, br_ref,
                    x1_ref.at[rows], h2_ref.at[rows], idx_ref.at[:, rows], rank_ref.at[:, rows], wts_ref.at[rows],
                    run_ref, per_token=per_token)

    @pl.when(i == pl.num_programs(0) - 1)
    def _():
        cnt_ref[...] = run_ref[...]


def _route_rows(yn_ref, yb_ref, gate_ref, x_ref, mod_ref, woa_ref, wo_ref, nw_ref, wr_ref, br_ref,
                x1_ref, h2_ref, idx_ref, rank_ref, wts_ref, run_ref, *, per_token):
    tm, d = x_ref.shape
    ya = jnp.dot(yn_ref[...], woa_ref[...], preferred_element_type=F32)
    gate = _sigmoid(gate_ref[...])
    merged = gate[:, :d] * ya + gate[:, d:] * yb_ref[...]
    mix = _bdot(merged, wo_ref[...])
    g1 = _mod_rows(mod_ref, 2, per_token)
    sh2 = _mod_rows(mod_ref, 3, per_token)
    sc2 = _mod_rows(mod_ref, 4, per_token)
    x1 = x_ref[...] + g1 * mix
    x1_ref[...] = x1
    h2 = _rms(x1) * nw_ref[...] * (1.0 + sc2) + sh2
    h2_ref[...] = h2.reshape(tm, d // LANES, LANES)

    wr = wr_ref[...]
    wr_hi = wr.astype(BF16)
    wr_lo = (wr - wr_hi.astype(F32)).astype(BF16)
    h2_hi = h2.astype(BF16)
    h2_lo = (h2 - h2_hi.astype(F32)).astype(BF16)
    d = functools.partial(jnp.dot, preferred_element_type=F32)
    logits = d(h2_hi, wr_hi) + d(h2_lo, wr_hi) + d(h2_hi, wr_lo)
    lt = logits.T
    erow = lax.broadcasted_iota(jnp.int32, lt.shape, 0)
    lt = jnp.where(erow < N_EXPERTS, lt, -jnp.inf)
    ex = jnp.exp(lt - jnp.max(lt, axis=0, keepdims=True))
    scores = ex / jnp.sum(ex, axis=0, keepdims=True)
    sel = scores + br_ref[...]
    s = [sel[e:e + 1, :] for e in range(N_EXPERTS)]
    p = [scores[e:e + 1, :] for e in range(N_EXPERTS)]

    def group_top2_sum(vals):
        best = None
        for a in range(len(vals)):
            for b in range(a + 1, len(vals)):
                pair = vals[a] + vals[b]
                best = pair if best is None else jnp.maximum(best, pair)
        return best

    gs = [group_top2_sum(s[EXPERTS_PER_GROUP * g:EXPERTS_PER_GROUP * (g + 1)]) for g in range(N_EXPERT_GROUPS)]
    best = gs[0]
    bg = jnp.zeros(best.shape, jnp.int32)
    for g in range(1, N_EXPERT_GROUPS):
        better = gs[g] > best
        bg = jnp.where(better, g, bg)
        best = jnp.where(better, gs[g], best)

    def pick(rows, j):
        out = rows[j]
        for g in range(1, N_EXPERT_GROUPS):
            out = jnp.where(bg == g, rows[EXPERTS_PER_GROUP * g + j], out)
        return out

    cs = [pick(s, j) for j in range(EXPERTS_PER_GROUP)]
    cp = [pick(p, j) for j in range(EXPERTS_PER_GROUP)]

    def argmax_first(vals, skip):
        bv = None
        bi = None
        bw = None
        for j in range(len(vals)):
            v = vals[j] if skip is None else jnp.where(skip == j, -jnp.inf, vals[j])
            if bv is None:
                bv, bi, bw = v, jnp.zeros(v.shape, jnp.int32), cp[j]
            else:
                better = v > bv
                bi = jnp.where(better, j, bi)
                bw = jnp.where(better, cp[j], bw)
                bv = jnp.where(better, v, bv)
        return bi, bw

    i1, w1 = argmax_first(cs, None)
    i2, w2 = argmax_first(cs, i1)
    wsum = w1 + w2
    e1 = bg * EXPERTS_PER_GROUP + i1
    e2 = bg * EXPERTS_PER_GROUP + i2
    idx_ref[0:1, :] = e1
    idx_ref[1:2, :] = e2
    wrow = lax.broadcasted_iota(jnp.int32, lt.shape, 0)
    wmat = jnp.where(wrow == 0, w1 / wsum, jnp.where(wrow == 1, w2 / wsum, 0.0))
    wts_ref[...] = wmat.T

    oh1 = jnp.where(erow == e1, 1.0, 0.0)
    oh2 = jnp.where(erow == e2, 1.0, 0.0)
    both = oh1 + oh2
    ta = lax.broadcasted_iota(jnp.int32, (tm, tm), 0)
    tb = lax.broadcasted_iota(jnp.int32, (tm, tm), 1)
    earlier = jnp.where(ta < tb, 1.0, 0.0).astype(BF16)
    before = jnp.dot(both.astype(BF16), earlier, preferred_element_type=F32) + run_ref[...]
    rank_ref[0:1, :] = jnp.sum(oh1 * before, axis=0, keepdims=True).astype(jnp.int32)
    rank_ref[1:2, :] = jnp.sum(oh2 * before, axis=0, keepdims=True).astype(jnp.int32)
    run_ref[...] = run_ref[...] + jnp.sum(both, axis=1, keepdims=True)


def _merge_router(yn, yb, gate, x, mod, lw, rw, *, tm, per_token, mod_spec):
    t, d = x.shape
    row = lambda n: pl.BlockSpec((tm, n), lambda i: (i, 0))
    pair = pl.BlockSpec((2, tm), lambda i: (0, i))
    return pl.pallas_call(
        functools.partial(_merge_router_kernel, per_token=per_token),
        grid=(t // tm,),
        in_specs=[row(d), row(d), row(2 * d), row(d), mod_spec,
                  _const_spec((d, d)), _const_spec((d, d)), _const_spec((1, d)),
                  _const_spec((d, LANES)), _const_spec((LANES, 1))],
        out_specs=[row(d), _row_tiles(tm, d), pair, pair, row(LANES), _const_spec((LANES, 1))],
        out_shape=[jax.ShapeDtypeStruct((t, d), F32),
                   jax.ShapeDtypeStruct((t, d // LANES, LANES), F32),
                   jax.ShapeDtypeStruct((2, t), jnp.int32),
                   jax.ShapeDtypeStruct((2, t), jnp.int32),
                   jax.ShapeDtypeStruct((t, LANES), F32),
                   jax.ShapeDtypeStruct((LANES, 1), F32)],
        scratch_shapes=[pltpu.VMEM((LANES, 1), F32)],
        compiler_params=_cparams(("arbitrary",)),
        name="merge_router",
    )(yn, yb, gate, x, mod, lw["w_out_a"], lw["w_out"], lw["norm2_w"], rw["w_router"], rw["b_router"])


def _plan_kernel(cnt_ref, idx_ref, rank_ref, dest_ref, te_ref, ends_ref, nv_ref, *, tile, n_tiles):
    shift = tile.bit_length() - 1
    run = jnp.int32(0)
    starts = []
    for e in range(N_EXPERTS):
        starts.append(run)
        run = run + (((cnt_ref[e] + (tile - 1)) >> shift) << shift)
        ends_ref[e] = run
    n_valid = run >> shift
    nv_ref[0] = n_valid
    idx = idx_ref[...]
    dest = rank_ref[...]
    for e in range(N_EXPERTS):
        dest = dest + jnp.where(idx == e, starts[e], 0)
    dest_ref[...] = dest

    def tile_owner(j, carry):
        pos = jnp.minimum(j, n_valid - 1) * tile
        owner = jnp.int32(0)
        for e in range(N_EXPERTS - 1):
            owner = owner + jnp.where(ends_ref[e] <= pos, 1, 0)
        te_ref[j] = owner
        return carry

    lax.fori_loop(0, n_tiles, tile_owner, 0)


def _plan(cnt, idx, rank, tile, n_tiles):
    smem = pl.BlockSpec(memory_space=pltpu.SMEM)
    vmem = pl.BlockSpec(memory_space=pltpu.VMEM)
    return pl.pallas_call(
        functools.partial(_plan_kernel, tile=tile, n_tiles=n_tiles),
        in_specs=[smem, vmem, vmem],
        out_specs=[vmem, smem, smem, smem],
        out_shape=[jax.ShapeDtypeStruct(idx.shape, jnp.int32),
                   jax.ShapeDtypeStruct((n_tiles,), jnp.int32),
                   jax.ShapeDtypeStruct((N_EXPERTS,), jnp.int32),
                   jax.ShapeDtypeStruct((1,), jnp.int32)],
        name="moe_plan",
    )(cnt, idx, rank)


def _row_tiles(tm, d):
    return pl.BlockSpec((tm, d // LANES, LANES), lambda i: (i, 0, 0))


def _row_copy(src_ref, src_row, dst_ref, dst_row, sem):
    return pltpu.make_async_copy(src_ref.at[src_row], dst_ref.at[dst_row], sem)


def _each_row(tm, fn):
    def body(t, carry):
        for k in range(2):
            fn(t, k)
        return carry

    lax.fori_loop(0, tm, body, 0, unroll=8)


def _dispatch_kernel(ends_ref, dest_ref, prev_dest_ref, h_ref, out_ref, zero_ref, stage_ref, sem, zsem, *, tile):
    i = pl.program_id(0)
    tm = h_ref.shape[0]
    slot = i % 2

    @pl.when(i == 0)
    def _():
        zero_ref[...] = jnp.zeros(zero_ref.shape, F32)

        def clear(start):
            return pltpu.make_async_copy(zero_ref, out_ref.at[pl.ds(pl.multiple_of(start, tile), tile)], zsem)

        def used(e):
            return ends_ref[e] > (ends_ref[e - 1] if e else 0)

        def unused_tiles(fn):
            n_valid = ends_ref[N_EXPERTS - 1] >> (tile.bit_length() - 1)
            lax.fori_loop(n_valid, out_ref.shape[0] // tile, lambda j, c: (fn(clear(j * tile)), c)[1], 0)

        for e in range(N_EXPERTS):
            pl.when(used(e))(lambda e=e: clear(ends_ref[e] - tile).start())
        unused_tiles(lambda cp: cp.start())
        for e in range(N_EXPERTS):
            pl.when(used(e))(lambda e=e: clear(ends_ref[e] - tile).wait())
        unused_tiles(lambda cp: cp.wait())

    stage_ref[slot] = h_ref[...]
    _each_row(tm, lambda t, k: _row_copy(stage_ref.at[slot], t, out_ref, dest_ref[k, t], sem.at[slot]).start(priority=k))

    @pl.when(i > 0)
    def _():
        _each_row(tm, lambda t, k: _row_copy(stage_ref.at[1 - slot], t, out_ref, prev_dest_ref[k, t],
                                             sem.at[1 - slot]).wait())

    @pl.when(i == pl.num_programs(0) - 1)
    def _():
        _each_row(tm, lambda t, k: _row_copy(stage_ref.at[slot], t, out_ref, dest_ref[k, t], sem.at[slot]).wait())


def _dispatch(ends, dest, h2, n_rows, tm, tile):
    t, nt, lanes = h2.shape
    idx_spec = lambda fn: pl.BlockSpec((2, tm), fn, memory_space=pltpu.SMEM)
    return pl.pallas_call(
        functools.partial(_dispatch_kernel, tile=tile),
        grid=(t // tm,),
        in_specs=[pl.BlockSpec(memory_space=pltpu.SMEM),
                  idx_spec(lambda i: (0, i)), idx_spec(lambda i: (0, jnp.maximum(i - 1, 0))),
                  _row_tiles(tm, nt * lanes)],
        out_specs=pl.BlockSpec(memory_space=pl.ANY),
        out_shape=jax.ShapeDtypeStruct((n_rows, nt, lanes), F32),
        scratch_shapes=[pltpu.VMEM((tile, nt, lanes), F32), pltpu.VMEM((2, tm, nt, lanes), F32),
                        pltpu.SemaphoreType.DMA((2,)), pltpu.SemaphoreType.DMA(())],
        compiler_params=_cparams(("arbitrary",)),
        name="moe_dispatch",
    )(ends, dest, dest, h2)


def _expert_kernel(te_ref, nv_ref, x_ref, w1_ref, w3_ref, w2_ref, y_ref, w1b_ref, w3b_ref, w2b_ref):
    j = pl.program_id(0)
    fresh = jnp.logical_or(j == 0, te_ref[j] != te_ref[jnp.maximum(j - 1, 0)])

    @pl.when(fresh)
    def _():
        w1b_ref[...] = w1_ref[...].astype(BF16)
        w3b_ref[...] = w3_ref[...].astype(BF16)
        w2b_ref[...] = w2_ref[...].astype(BF16)

    @pl.when(j < nv_ref[0])
    def _():
        tm, nt, lanes = x_ref.shape
        xb = x_ref[...].reshape(tm, nt * lanes).astype(BF16)
        a = jnp.dot(xb, w1b_ref[...], preferred_element_type=F32)
        b = jnp.dot(xb, w3b_ref[...], preferred_element_type=F32)
        y_ref[...] = _bdot(_silu(a) * b, w2b_ref[...]).reshape(tm, nt, lanes)

    @pl.when(j >= nv_ref[0])
    def _():
        y_ref[...] = jnp.zeros(y_ref.shape, F32)


def _experts(tile_expert, n_valid, xs, w1, w3, w2, layer, tm):
    n_rows, nt, lanes = xs.shape
    d = nt * lanes
    de = w1.shape[3]
    wspec = lambda r, c: pl.BlockSpec((None, None, r, c), lambda j, te, nv: (layer, te[j], 0, 0))
    grid_spec = pltpu.PrefetchScalarGridSpec(
        num_scalar_prefetch=2,
        grid=(n_rows // tm,),
        in_specs=[pl.BlockSpec((tm, nt, lanes), lambda j, te, nv: (jnp.minimum(j, nv[0] - 1), 0, 0)),
                  wspec(d, de), wspec(d, de), wspec(de, d)],
        out_specs=pl.BlockSpec((tm, nt, lanes), lambda j, te, nv: (j, 0, 0)),
        scratch_shapes=[pltpu.VMEM((d, de), BF16), pltpu.VMEM((d, de), BF16), pltpu.VMEM((de, d), BF16)],
    )
    return pl.pallas_call(
        _expert_kernel,
        grid_spec=grid_spec,
        out_shape=jax.ShapeDtypeStruct((n_rows, nt, lanes), F32),
        compiler_params=_cparams(("arbitrary",)),
        name="moe_experts",
    )(tile_expert, n_valid, xs, w1, w3, w2)


def _combine_kernel(dest_ref, next_dest_ref, ys_ref, x1_ref, wts_ref, mod_ref, fw_ref, out_ref, buf_ref, sem,
                    *, per_token, final_norm):
    i = pl.program_id(0)
    tm, d = x1_ref.shape
    slot = i % 2

    def gather(idx_ref, s):
        return lambda t, k: _row_copy(ys_ref, idx_ref[k, t], buf_ref.at[s, k], t, sem.at[s])

    @pl.when(i == 0)
    def _():
        _each_row(tm, lambda t, k: gather(dest_ref, slot)(t, k).start(priority=k))

    @pl.when(i + 1 < pl.num_programs(0))
    def _():
        _each_row(tm, lambda t, k: gather(next_dest_ref, 1 - slot)(t, k).start(priority=k))

    _each_row(tm, lambda t, k: gather(dest_ref, slot)(t, k).wait())
    w = wts_ref[...]
    rows = lambda k: buf_ref[slot, k].reshape(tm, d)
    moe = w[:, 0:1] * rows(0) + w[:, 1:2] * rows(1)
    x2 = x1_ref[...] + _mod_rows(mod_ref, 5, per_token) * moe
    if final_norm:
        x2 = _rms(x2) * fw_ref[...]
    out_ref[...] = x2


def _combine(dest, ys, x1, wts, mod, fw, *, tm, per_token, mod_spec, final_norm):
    t, d = x1.shape
    row = lambda n: pl.BlockSpec((tm, n), lambda i: (i, 0))
    n_steps = t // tm
    idx_spec = lambda fn: pl.BlockSpec((2, tm), fn, memory_space=pltpu.SMEM)
    return pl.pallas_call(
        functools.partial(_combine_kernel, per_token=per_token, final_norm=final_norm),
        grid=(n_steps,),
        in_specs=[idx_spec(lambda i: (0, i)), idx_spec(lambda i: (0, jnp.minimum(i + 1, n_steps - 1))),
                  pl.BlockSpec(memory_space=pl.ANY),
                  row(d), row(LANES), mod_spec, _const_spec((1, d))],
        out_specs=row(d),
        out_shape=jax.ShapeDtypeStruct((t, d), F32),
        scratch_shapes=[pltpu.VMEM((2, 2, tm) + ys.shape[1:], F32), pltpu.SemaphoreType.DMA((2,))],
        compiler_params=_cparams(("arbitrary",)),
        name="moe_combine",
    )(dest, dest, ys, x1, wts, mod, fw)


def _moe(h2, idx, rank, counts, x1, wts, mod, fw, ew, layer, *, tok_tile, expert_tile, per_token, mod_spec_fn,
         final_norm):
    t = h2.shape[0]
    tm = expert_tile
    n_rows = ((2 * t + N_EXPERTS * (tm - 1)) // tm + 1) * tm
    cnt = counts[:N_EXPERTS, 0].astype(jnp.int32)
    dest, tile_expert, ends, n_valid = _plan(cnt, idx, rank, tm, n_rows // tm)
    xs = _dispatch(ends, dest, h2, n_rows, tok_tile, tm)
    ys = _experts(tile_expert, n_valid, xs, ew["w1"], ew["w3"], ew["w2"], layer, tm)
    return _combine(dest, ys, x1, wts, mod, fw, tm=tok_tile, per_token=per_token,
                    mod_spec=mod_spec_fn(tok_tile), final_norm=final_norm)


def _pad_lanes(a, n=LANES):
    return jnp.pad(a, [(0, 0)] * (a.ndim - 1) + [(0, n - a.shape[-1])])


def kernel(x_prompt, x_sample, state_ssm, state_conv, state_s5_re, state_s5_im, c_prompt, c_sample, w_in, conv_w, conv_b, dt_bias, a_log, d_ssd, ssd_norm_w, w_out_a, s5_a_re, s5_a_im, s5_log_dt, s5_b_re, s5_b_im, s5_c_re, s5_c_im, s5_d, w_glu, w_out, norm1_w, norm2_w, w_ada, b_ada, w_router, b_router, w1, w3, w2, final_norm_w):
    nb, seq, d = x_prompt.shape
    ns, steps, _ = x_sample.shape
    depth = w_in.shape[0]
    d_in = SSD_HEADS * SSD_HEAD_DIM
    cdim = conv_w.shape[2]
    s5w = S5_GROUPS * S5_GROUP_CH
    s5n = S5_GROUPS * S5_STATE
    k1 = SSD_CONV - 1
    tp = nb * seq
    ts = ns * steps

    mod = _ada_mod(jnp.concatenate([c_prompt, c_sample], axis=0), w_ada, b_ada)
    head_expand = jnp.tile(jnp.repeat(jnp.eye(LANES, SSD_HEADS, dtype=BF16), SSD_HEAD_DIM, axis=1), (3, 1))
    rw = {"w_router": _pad_lanes(w_router), "b_router": _pad_lanes(b_router[None, :]).reshape(LANES, 1)}

    xp = x_prompt.reshape(tp, d)
    xsm = jnp.transpose(x_sample, (1, 0, 2)).reshape(ts, d)
    outs = {k: [] for k in ("ssm_p", "conv_p", "conv_s", "re_p", "re_s", "im_p", "im_s")}
    ssm_s = None
    zero_state = jnp.zeros((nb, s5n), F32)

    for l in range(depth):
        o0 = 0
        o1 = d_in
        o2 = o1 + cdim
        o3 = o2 + SSD_HEADS
        o4 = o3 + s5w
        wl = w_in[l].astype(BF16)
        ws = (wl[:, o0:o1], wl[:, o1:o2], _pad_lanes(wl[:, o2:o3]), wl[:, o3:o4], wl[:, o4:])
        p = {"conv_w": conv_w[l], "conv_b": conv_b[l][None, :], "dt_bias": _pad_lanes(dt_bias[l][None, :]),
             "a_log": _pad_lanes(a_log[l][None, :]),
             "d_skip_e": jnp.repeat(d_ssd[l], SSD_HEAD_DIM)[None, :],
             "ssd_norm_w": ssd_norm_w[l][None, :], "head_expand": head_expand}
        chan_rows = lambda a, perm: jnp.transpose(a, perm).reshape(S5_GROUP_CH, s5n)
        ab, wb, c_re_t, c_im_t = _s5_params(
            s5_a_re[l].reshape(1, s5n), s5_a_im[l].reshape(1, s5n), jnp.repeat(s5_log_dt[l], S5_STATE)[None, :],
            chan_rows(s5_b_re[l], (2, 0, 1)), chan_rows(s5_b_im[l], (2, 0, 1)),
            chan_rows(s5_c_re[l], (1, 0, 2)), chan_rows(s5_c_im[l], (1, 0, 2)))
        sp = {"ab": ab, "wb": wb, "c_re_t": c_re_t, "c_im_t": c_im_t,
              "d": s5_d[l][None, :], "w_glu": w_glu[l].astype(BF16)}
        lw = {"w_out_a": w_out_a[l].astype(BF16), "w_out": w_out[l].astype(BF16), "norm2_w": norm2_w[l][None, :]}
        ew = {"w1": w1, "w3": w3, "w2": w2}
        final = l == depth - 1
        fw = final_norm_w[None, :]

        mod_p = mod[l, :nb].reshape(nb, 6, d)
        mod_p_spec = lambda tm: pl.BlockSpec((None, 6, d), lambda i: (i // (seq // tm), 0, 0))
        z, xbc, dtr, u, gate = _inproj(
            xp, mod_p, norm1_w[l][None, :], ws, tm=ROW_TILE, per_token=False, mod_spec=mod_p_spec(ROW_TILE))
        yn, nconv, nssm = _ssd_prompt(xbc, dtr, z, p, nb, seq)
        yb, nre, nim = _s5(u.reshape(nb, seq, s5w), zero_state, zero_state, sp, nb, S5_STEPS, True)
        x1, h2, idx, rank, wts, counts = _merge_router(
            yn, yb.reshape(tp, d), gate, xp, mod_p, lw, rw, tm=ROW_TILE, per_token=False,
            mod_spec=mod_p_spec(ROW_TILE))
        xp = _moe(h2, idx, rank, counts, x1, wts, mod_p, fw, ew, l, tok_tile=MOE_TILE, expert_tile=EXPERT_TILE, per_token=False,
                  mod_spec_fn=mod_p_spec, final_norm=final)
        outs["ssm_p"].append(nssm)
        outs["conv_p"].append(nconv)
        outs["re_p"].append(nre.reshape(nb, S5_GROUPS, S5_STATE))
        outs["im_p"].append(nim.reshape(nb, S5_GROUPS, S5_STATE))

        mod_s = jnp.transpose(mod[l, nb:].reshape(ns, 6, d), (1, 0, 2))
        mod_s_spec = lambda tm: pl.BlockSpec((6, tm, d), lambda i: (0, i % (ns // tm), 0))
        z, xbc, dtr, u, gate = _inproj(
            xsm, mod_s, norm1_w[l][None, :], ws, tm=ns, per_token=True, mod_spec=mod_s_spec(ns))
        conv0_tm = jnp.transpose(state_conv[l], (1, 0, 2)).reshape(k1 * ns, cdim)
        yn, nconv_tm, ssm_s = _ssd_sample(xbc, dtr, z, conv0_tm, state_ssm, ssm_s, l, p, ns, steps)
        yb, nre, nim = _s5(u, state_s5_re[l].reshape(ns, s5n), state_s5_im[l].reshape(ns, s5n), sp, ns, steps,
                           False)
        x1, h2, idx, rank, wts, counts = _merge_router(
            yn, yb, gate, xsm, mod_s, lw, rw, tm=ns, per_token=True, mod_spec=mod_s_spec(ns))
        xsm = _moe(h2, idx, rank, counts, x1, wts, mod_s, fw, ew, l, tok_tile=ns, expert_tile=ns, per_token=True,
                   mod_spec_fn=mod_s_spec, final_norm=final)
        outs["conv_s"].append(jnp.transpose(nconv_tm.reshape(k1, ns, cdim), (1, 0, 2)))
        outs["re_s"].append(nre.reshape(ns, S5_GROUPS, S5_STATE))
        outs["im_s"].append(nim.reshape(ns, S5_GROUPS, S5_STATE))

    y_prompt = xp.reshape(nb, seq, d)
    y_sample = jnp.transpose(xsm.reshape(steps, ns, d), (1, 0, 2))
    st = lambda k: jnp.stack(outs[k])
    return (y_prompt, y_sample, st("ssm_p"), ssm_s, st("conv_p"), st("conv_s"),
            st("re_p"), st("re_s"), st("im_p"), st("im_s"))
```

```python
import functools

import jax
import jax.numpy as jnp
from jax import lax
from jax.experimental import pallas as pl
from jax.experimental.pallas import tpu as pltpu

F32 = jnp.float32
BF16 = jnp.bfloat16
EPS = 1e-6

SSD_HEADS = 16
SSD_HEAD_DIM = 64
SSD_GROUPS = 2
SSD_STATE = 128
SSD_CONV = 4
SSD_CHUNK = 128
S5_GROUPS = 32
S5_GROUP_CH = 16
S5_STATE = 64
N_EXPERTS = 16
EXPERTS_PER_GROUP = 4
N_EXPERT_GROUPS = 4

LANES = 128
SUBLANES = 8
MXU_TILE = 256
VMEM_LIMIT_BYTES = 56 * 1024 * 1024

ROW_TILE = 512
MOE_TILE = 256
EXPERT_TILE = 512
S5_STEPS = 64
S5_SCAN_LANES = 512
ROUTER_SUBTILES = 2
SSD_SAMPLE_SEQS = 4


def _cparams(sem):
    return pltpu.CompilerParams(dimension_semantics=sem, vmem_limit_bytes=VMEM_LIMIT_BYTES)


def _bdot(a, b):
    return jnp.dot(a.astype(BF16), b.astype(BF16), preferred_element_type=F32)


def _bdot_nt(a, b):
    return lax.dot_general(a.astype(BF16), b.astype(BF16), (((1,), (1,)), ((), ())),
                           preferred_element_type=F32)


def _split3(x):
    hi = x.astype(BF16)
    r1 = x - hi.astype(F32)
    mid = r1.astype(BF16)
    lo = (r1 - mid.astype(F32)).astype(BF16)
    return hi, mid, lo


def _dot3_lhs(x, m3):
    return jnp.dot(jnp.concatenate(_split3(x), axis=1), m3, preferred_element_type=F32)


def _dot3_rhs(m, x):
    return jnp.dot(jnp.concatenate([m] * 3, axis=1), jnp.concatenate(_split3(x), axis=0),
                   preferred_element_type=F32)


def _sigmoid(x):
    return jax.nn.sigmoid(x)


def _silu(x):
    return x * _sigmoid(x)


def _softplus(x):
    return jnp.maximum(x, 0.0) + jnp.log1p(jnp.exp(-jnp.abs(x)))


def _gelu_tanh(x):
    c = 0.7978845608028654
    return 0.5 * x * (1.0 + jnp.tanh(c * (x + 0.044715 * (x * x * x))))


def _rms(x):
    return x * lax.rsqrt(jnp.mean(x * x, axis=-1, keepdims=True) + EPS)


def _ada_kernel(c_ref, w_ref, b_ref, o_ref):
    o_ref[0] = _bdot(_silu(c_ref[...]), w_ref[0]) + b_ref[0]


def _ada_mod(c_all, w_ada, b_ada):
    depth, d, n = w_ada.shape
    nseq = c_all.shape[0]
    tn = 1536
    return pl.pallas_call(
        _ada_kernel,
        grid=(depth, n // tn),
        in_specs=[pl.BlockSpec((nseq, d), lambda l, j: (0, 0)),
                  pl.BlockSpec((1, d, tn), lambda l, j: (l, 0, j)),
                  pl.BlockSpec((1, 1, tn), lambda l, j: (l, 0, j))],
        out_specs=pl.BlockSpec((1, nseq, tn), lambda l, j: (l, 0, j)),
        out_shape=jax.ShapeDtypeStruct((depth, nseq, n), F32),
        compiler_params=_cparams(("arbitrary", "arbitrary")),
        name="ada_mod",
    )(c_all, w_ada, b_ada.reshape(depth, 1, n))


def _mod_rows(mod_ref, j, per_token, rows_per_seq=None):
    if per_token:
        return mod_ref[j]
    if rows_per_seq is not None:
        m = mod_ref[:, j, :]
        return jnp.broadcast_to(m[:, None, :], (m.shape[0], rows_per_seq, m.shape[1])).reshape(-1, m.shape[1])
    return mod_ref[j:j + 1, :]


def _inproj_kernel(x_ref, mod_ref, nw_ref, wz_ref, wx_ref, wdt_ref, wu_ref, wg_ref,
                   z_ref, xbc_ref, dt_ref, u_ref, g_ref, *, per_token):
    sh = _mod_rows(mod_ref, 0, per_token)
    sc = _mod_rows(mod_ref, 1, per_token)
    h = (_rms(x_ref[...]) * nw_ref[...] * (1.0 + sc) + sh).astype(BF16)
    d = functools.partial(jnp.dot, preferred_element_type=F32)
    z_ref[...] = d(h, wz_ref[...])
    xbc_ref[...] = d(h, wx_ref[...])
    dt_ref[...] = d(h, wdt_ref[...])
    u_ref[...] = d(h, wu_ref[...])
    g_ref[...] = d(h, wg_ref[...])


def _const_spec(shape):
    nd = len(shape)
    return pl.BlockSpec(shape, lambda *_: (0,) * nd)


def _inproj(x, mod, nw, ws, *, tm, per_token, mod_spec):
    t, d = x.shape
    wz, wx, wdt, wu, wg = ws
    row = lambda n: pl.BlockSpec((tm, n), lambda i: (i, 0))
    return pl.pallas_call(
        functools.partial(_inproj_kernel, per_token=per_token),
        grid=(t // tm,),
        in_specs=[row(d), mod_spec, _const_spec((1, d)),
                  _const_spec(wz.shape), _const_spec(wx.shape), _const_spec(wdt.shape),
                  _const_spec(wu.shape), _const_spec(wg.shape)],
        out_specs=[row(wz.shape[1]), row(wx.shape[1]), row(wdt.shape[1]), row(wu.shape[1]), row(wg.shape[1])],
        out_shape=[jax.ShapeDtypeStruct((t, wz.shape[1]), F32),
                   jax.ShapeDtypeStruct((t, wx.shape[1]), F32),
                   jax.ShapeDtypeStruct((t, wdt.shape[1]), F32),
                   jax.ShapeDtypeStruct((t, wu.shape[1]), F32),
                   jax.ShapeDtypeStruct((t, wg.shape[1]), F32)],
        compiler_params=_cparams(("arbitrary",)),
        name="inproj",
    )(x, mod, nw, wz, wx, wdt, wu, wg)


def _gated_group_norm(y, z, nw):
    y = y * _silu(z)
    half = y.shape[1] // SSD_GROUPS
    parts = [_rms(y[:, g * half:(g + 1) * half]) for g in range(SSD_GROUPS)]
    return jnp.concatenate(parts, axis=1) * nw


def _ssd_prompt_kernel(xbc_ref, dt_ref, z_ref, cw_ref, cb_ref, dtb_ref, alog_ref, dsk_ref, nw_ref,
                       e_ref, yn_ref, conv_ref, ssm_ref, ext_ref, st_ref):
    c = pl.program_id(1)
    q = SSD_CHUNK
    d_in = SSD_HEADS * SSD_HEAD_DIM
    gw = SSD_STATE
    hpg = SSD_HEADS // SSD_GROUPS
    grows = hpg * SSD_HEAD_DIM

    @pl.when(c == 0)
    def _():
        ext_ref[...] = jnp.zeros(ext_ref.shape, F32)
        st_ref[...] = jnp.zeros(st_ref.shape, F32)

    xbc = xbc_ref[...]
    sub = lax.broadcasted_iota(jnp.int32, (SUBLANES, xbc.shape[1]), 0)
    taps = [cw_ref[k:k + 1, :] for k in range(SSD_CONV)]
    prev_rolled = [pltpu.roll(ext_ref[...], j, axis=0) for j in range(1, SSD_CONV)]
    acc_tiles = []
    for r0 in range(0, q, SUBLANES):
        cur = xbc[r0:r0 + SUBLANES, :]
        rolled = [pltpu.roll(cur, j, axis=0) for j in range(1, SSD_CONV)]
        acc = cb_ref[...] + taps[SSD_CONV - 1] * cur
        for j in range(1, SSD_CONV):
            acc = acc + taps[SSD_CONV - 1 - j] * jnp.where(sub < j, prev_rolled[j - 1], rolled[j - 1])
        acc_tiles.append(acc)
        prev_rolled = rolled
    ext_ref[...] = xbc[q - SUBLANES:q, :]
    v = _silu(jnp.concatenate(acc_tiles, axis=0))
    xs = v[:, :d_in]
    bm = v[:, d_in:d_in + SSD_GROUPS * gw]
    cm = v[:, d_in + SSD_GROUPS * gw:]

    dt = _softplus(dt_ref[...] + dtb_ref[...])
    a = -jnp.exp(alog_ref[...])
    da = dt * a
    rid = lax.broadcasted_iota(jnp.int32, (q, q), 0)
    cid = lax.broadcasted_iota(jnp.int32, (q, q), 1)
    causal = rid >= cid
    tri = jnp.where(causal, 1.0, 0.0).astype(BF16)
    a_cs = _dot3_rhs(tri, da)
    a_cs_t = a_cs.T
    a_last = a_cs[q - 1:q, :]
    per_head = jnp.concatenate([dt, jnp.exp(a_last - a_cs) * dt, jnp.exp(a_cs)], axis=0)
    per_chan = _dot3_lhs(per_head, e_ref[...])
    x_dt = xs * per_chan[0:q]
    x_end = xs * per_chan[q:2 * q]
    eacs_e = per_chan[2 * q:3 * q]
    x_end_t = x_end.T.astype(BF16)
    lane = lax.broadcasted_iota(jnp.int32, (q, 2 * SSD_HEAD_DIM), 1)
    first = lane < SSD_HEAD_DIM

    y_diag = []
    y_off = []
    for g in range(SSD_GROUPS):
        bg = bm[:, g * gw:(g + 1) * gw].astype(BF16)
        cg = cm[:, g * gw:(g + 1) * gw].astype(BF16)
        cb = _bdot_nt(cg, bg)
        sg = st_ref[g * grows:(g + 1) * grows, :]
        y_off.append(_bdot_nt(cg, sg))
        for pair in range(hpg // 2):
            ms = []
            for r in (2 * pair, 2 * pair + 1):
                h = g * hpg + r
                diff = a_cs[:, h:h + 1] - a_cs_t[h:h + 1, :]
                decay = jnp.where(causal, jnp.exp(jnp.where(causal, diff, 0.0)), 0.0)
                ms.append((cb * decay).astype(BF16))
            h0 = g * hpg + 2 * pair
            xp = x_dt[:, h0 * SSD_HEAD_DIM:(h0 + 2) * SSD_HEAD_DIM]
            rhs = jnp.concatenate([jnp.where(first, xp, 0.0), jnp.where(first, 0.0, xp)], axis=0)
            y_diag.append(jnp.dot(jnp.concatenate(ms, axis=1), rhs.astype(BF16), preferred_element_type=F32))
        contrib = jnp.dot(x_end_t[g * grows:(g + 1) * grows, :], bg, preferred_element_type=F32)
        for r in range(hpg):
            h = g * hpg + r
            rows = slice(h * SSD_HEAD_DIM, (h + 1) * SSD_HEAD_DIM)
            dec = jnp.exp(a_cs_t[h:h + 1, q - 1:q])
            st_ref[rows, :] = st_ref[rows, :] * dec + contrib[r * SSD_HEAD_DIM:(r + 1) * SSD_HEAD_DIM, :]

    y = (jnp.concatenate(y_diag, axis=1) + jnp.concatenate(y_off, axis=1) * eacs_e
         + dsk_ref[...] * xs)
    yn_ref[...] = _gated_group_norm(y, z_ref[...], nw_ref[...]).astype(BF16)

    @pl.when(c == pl.num_programs(1) - 1)
    def _():
        conv_ref[0] = ext_ref[pl.ds(SUBLANES - (SSD_CONV - 1), SSD_CONV - 1), :]
        ssm_ref[0] = st_ref[...].reshape(SSD_HEADS, SSD_HEAD_DIM, SSD_STATE)


def _ssd_prompt(xbc, dt_raw, z, p, nb, seq):
    t, cdim = xbc.shape
    nc = seq // SSD_CHUNK
    q = SSD_CHUNK
    d_in = z.shape[1]
    row = lambda n: pl.BlockSpec((q, n), lambda b, c: (b * nc + c, 0))
    return pl.pallas_call(
        _ssd_prompt_kernel,
        grid=(nb, nc),
        in_specs=[row(cdim), row(LANES), row(d_in),
                  _const_spec((SSD_CONV, cdim)), _const_spec((1, cdim)), _const_spec((1, LANES)),
                  _const_spec((1, LANES)), _const_spec((1, d_in)), _const_spec((1, d_in)),
                  _const_spec((3 * LANES, d_in))],
        out_specs=[row(d_in),
                   pl.BlockSpec((1, SSD_CONV - 1, cdim), lambda b, c: (b, 0, 0)),
                   pl.BlockSpec((1, SSD_HEADS, SSD_HEAD_DIM, SSD_STATE), lambda b, c: (b, 0, 0, 0))],
        out_shape=[jax.ShapeDtypeStruct((t, d_in), BF16),
                   jax.ShapeDtypeStruct((nb, SSD_CONV - 1, cdim), F32),
                   jax.ShapeDtypeStruct((nb, SSD_HEADS, SSD_HEAD_DIM, SSD_STATE), F32)],
        scratch_shapes=[pltpu.VMEM((SUBLANES, cdim), F32),
                        pltpu.VMEM((d_in, SSD_STATE), F32)],
        compiler_params=_cparams(("arbitrary", "arbitrary")),
        name="ssd_prompt",
    )(xbc, dt_raw, z, p["conv_w"], p["conv_b"], p["dt_bias"], p["a_log"], p["d_skip_e"], p["ssd_norm_w"],
      p["head_expand"])


def _ssd_sample_a_kernel(xbc_ref, conv0_ref, dt_ref, cw_ref, cb_ref, dtb_ref, alog_ref, dsk_ref, e_ref,
                         ydiag_ref, eacs_ref, wt_ref, bs_ref, c8_ref, dec_ref, nconv_ref, *, steps, nseq):
    d_in = SSD_HEADS * SSD_HEAD_DIM
    gw = SSD_STATE
    hpg = SSD_HEADS // SSD_GROUPS
    k1 = SSD_CONV - 1
    slab = lambda ref, i: ref[i * nseq:(i + 1) * nseq, :]
    xp = [slab(conv0_ref, i) for i in range(k1)] + [slab(xbc_ref, i) for i in range(steps)]
    for i in range(k1):
        nconv_ref[i * nseq:(i + 1) * nseq, :] = xp[steps + i]
    e = e_ref[...]
    a = -jnp.exp(alog_ref[...])
    head = lax.broadcasted_iota(jnp.int32, (nseq, LANES), 1)
    xs, bm, cm, dt, a_cs = [], [], [], [], []
    run = jnp.zeros((nseq, LANES), F32)
    for t in range(steps):
        acc = cb_ref[...]
        for k in range(SSD_CONV):
            acc = acc + cw_ref[k:k + 1, :] * xp[t + k]
        v = _silu(acc)
        xs.append(v[:, :d_in])
        bm.append(v[:, d_in:d_in + SSD_GROUPS * gw])
        cm.append(v[:, d_in + SSD_GROUPS * gw:])
        dt.append(_softplus(slab(dt_ref, t) + dtb_ref[...]))
        run = run + dt[t] * a
        a_cs.append(run)
    dec_ref[...] = jnp.exp(a_cs[steps - 1])
    for t in range(steps):
        y = dsk_ref[...] * xs[t]
        for s in range(t + 1):
            cbs = [jnp.sum(cm[t][:, g * gw:(g + 1) * gw] * bm[s][:, g * gw:(g + 1) * gw],
                           axis=-1, keepdims=True) for g in range(SSD_GROUPS)]
            coef = jnp.exp(a_cs[t] - a_cs[s]) * dt[s] * jnp.where(head < hpg, cbs[0], cbs[1])
            y = y + _dot3_lhs(coef, e) * xs[s]
        ydiag_ref[t * nseq:(t + 1) * nseq, :] = y
        eacs_ref[t * nseq:(t + 1) * nseq, :] = _dot3_lhs(jnp.exp(a_cs[t]), e)
        w = xs[t] * _dot3_lhs(jnp.exp(a_cs[steps - 1] - a_cs[t]) * dt[t], e)
        wt_ref[:, t * nseq:(t + 1) * nseq] = w.T.astype(BF16)
        bs_ref[t * nseq:(t + 1) * nseq, :] = bm[t]
        for g in range(SSD_GROUPS):
            j = g * steps + t
            c8_ref[j * nseq:(j + 1) * nseq, :] = cm[t][:, g * gw:(g + 1) * gw]


def _ssd_sample_b_kernel(dec_ref, h0_ref, wt_ref, bs_ref, c8_ref, *rest, steps, nseq, layer):
    yoff_ref, ssm_ref = rest[-2:]
    if layer:
        ssm_ref[0:layer] = rest[0][...]
    hpg = SSD_HEADS // SSD_GROUPS
    grows = hpg * SSD_HEAD_DIM
    gw = SSD_STATE
    rid = lax.broadcasted_iota(jnp.int32, (steps * nseq, gw), 0)
    for s in range(h0_ref.shape[0]):
        b = pl.program_id(0) * h0_ref.shape[0] + s
        h0 = h0_ref[s].reshape(SSD_HEADS * SSD_HEAD_DIM, SSD_STATE)
        c8 = c8_ref[pl.ds(b, SSD_GROUPS * steps, stride=nseq), :]
        yoff_ref[s] = _bdot_nt(c8, h0)
        mine = rid == b
        for t in range(1, steps):
            mine = jnp.logical_or(mine, rid == b + t * nseq)
        for g in range(SSD_GROUPS):
            bsel = jnp.where(mine, bs_ref[:, g * gw:(g + 1) * gw], 0.0).astype(BF16)
            contrib = jnp.dot(wt_ref[g * grows:(g + 1) * grows, :], bsel, preferred_element_type=F32)
            for r in range(hpg):
                h = g * hpg + r
                rows = slice(h * SSD_HEAD_DIM, (h + 1) * SSD_HEAD_DIM)
                ssm_ref[layer, s, h] = (h0[rows, :] * dec_ref[b, h]
                                        + contrib[r * SSD_HEAD_DIM:(r + 1) * SSD_HEAD_DIM, :])


def _ssd_sample_c_kernel(ydiag_ref, yoff_ref, eacs_ref, z_ref, nw_ref, yn_ref, *, steps, nseq):
    d_in = SSD_HEADS * SSD_HEAD_DIM
    half = d_in // SSD_GROUPS
    for t in range(steps):
        lo = yoff_ref[:, t * d_in:t * d_in + half]
        hi = yoff_ref[:, (steps + t) * d_in + half:(steps + t + 1) * d_in]
        rows = slice(t * nseq, (t + 1) * nseq)
        y = ydiag_ref[rows, :] + jnp.concatenate([lo, hi], axis=1) * eacs_ref[rows, :]
        yn_ref[rows, :] = _gated_group_norm(y, z_ref[rows, :], nw_ref[...]).astype(BF16)


def _ssd_sample(xbc, dt_raw, z, conv0_tm, ssm_all, ssm_done, layer, p, nseq, steps):
    t, cdim = xbc.shape
    d_in = z.shape[1]
    k1 = SSD_CONV - 1
    nj = SSD_GROUPS * steps
    a_out = pl.pallas_call(
        functools.partial(_ssd_sample_a_kernel, steps=steps, nseq=nseq),
        out_shape=[jax.ShapeDtypeStruct((t, d_in), F32),
                   jax.ShapeDtypeStruct((t, d_in), F32),
                   jax.ShapeDtypeStruct((d_in, t), BF16),
                   jax.ShapeDtypeStruct((t, SSD_GROUPS * SSD_STATE), F32),
                   jax.ShapeDtypeStruct((nj * nseq, SSD_STATE), F32),
                   jax.ShapeDtypeStruct((nseq, LANES), F32),
                   jax.ShapeDtypeStruct((k1 * nseq, cdim), F32)],
        compiler_params=pltpu.CompilerParams(vmem_limit_bytes=VMEM_LIMIT_BYTES),
        name="ssd_sample_a",
    )(xbc, conv0_tm, dt_raw, p["conv_w"], p["conv_b"], p["dt_bias"], p["a_log"], p["d_skip_e"],
      p["head_expand"])
    ydiag, eacs_e, wt, bs, c8, dec, nconv = a_out
    state_blk = (SSD_HEADS, SSD_HEAD_DIM, SSD_STATE)
    operands = [dec[:, :SSD_HEADS], ssm_all, wt, bs, c8]
    sb = SSD_SAMPLE_SEQS
    in_specs = [pl.BlockSpec(memory_space=pltpu.SMEM),
                pl.BlockSpec((None, sb) + state_blk, lambda b: (layer, b, 0, 0, 0)),
                _const_spec(wt.shape), _const_spec(bs.shape), _const_spec(c8.shape)]
    if layer:
        operands.append(ssm_done)
        in_specs.append(pl.BlockSpec((layer, sb) + state_blk, lambda b: (0, b, 0, 0, 0)))
    yoff, nssm = pl.pallas_call(
        functools.partial(_ssd_sample_b_kernel, steps=steps, nseq=nseq, layer=layer),
        grid=(nseq // sb,),
        in_specs=in_specs,
        out_specs=[pl.BlockSpec((sb, nj, d_in), lambda b: (b, 0, 0)),
                   pl.BlockSpec((layer + 1, sb) + state_blk, lambda b: (0, b, 0, 0, 0))],
        out_shape=[jax.ShapeDtypeStruct((nseq, nj, d_in), F32),
                   jax.ShapeDtypeStruct((layer + 1, nseq) + state_blk, F32)],
        compiler_params=_cparams(("arbitrary",)),
        name="ssd_sample_b",
    )(*operands)
    yn = pl.pallas_call(
        functools.partial(_ssd_sample_c_kernel, steps=steps, nseq=nseq),
        out_shape=jax.ShapeDtypeStruct((t, d_in), BF16),
        compiler_params=pltpu.CompilerParams(vmem_limit_bytes=VMEM_LIMIT_BYTES),
        name="ssd_sample_c",
    )(ydiag, yoff.reshape(nseq, nj * d_in), eacs_e, z, p["ssd_norm_w"])
    return yn, nconv, nssm


def _s5_param_kernel(are_ref, aim_ref, ldt_ref, bre_ref, bim_ref, cre_ref, cim_ref,
                     ab_ref, wb_ref, cret_ref, cimt_ref):
    n = are_ref.shape[1]
    w = wb_ref.shape[0]
    a_re = are_ref[...]
    a_im = aim_ref[...]
    dt = jnp.exp(ldt_ref[...])
    mag = jnp.exp(dt * a_re)
    ab_re = mag * jnp.cos(dt * a_im)
    ab_im = mag * jnp.sin(dt * a_im)
    den = a_re * a_re + a_im * a_im
    f_re = ((ab_re - 1.0) * a_re + ab_im * a_im) / den
    f_im = (ab_im * a_re - (ab_re - 1.0) * a_im) / den
    ab_ref[0:1, :] = ab_re
    ab_ref[1:2, :] = ab_im
    b_re = bre_ref[...]
    b_im = bim_ref[...]
    row_g = lax.shift_right_logical(lax.broadcasted_iota(jnp.int32, (w, n), 0), S5_GROUP_CH.bit_length() - 1)
    col_g = lax.shift_right_logical(lax.broadcasted_iota(jnp.int32, (w, n), 1), S5_STATE.bit_length() - 1)
    same = row_g == col_g

    def spread(rows):
        return jnp.where(same, jnp.concatenate([rows] * S5_GROUPS, axis=0), 0.0).astype(BF16)

    wb_ref[:, 0:n] = spread(f_re * b_re - f_im * b_im)
    wb_ref[:, n:2 * n] = spread(f_re * b_im + f_im * b_re)
    cret_ref[...] = spread(cre_ref[...])
    cimt_ref[...] = spread(cim_ref[...])


def _s5_params(a_re, a_im, ldt, bt_re, bt_im, ct_re, ct_im):
    n = a_re.shape[1]
    w = S5_GROUPS * S5_GROUP_CH
    return pl.pallas_call(
        _s5_param_kernel,
        out_shape=[jax.ShapeDtypeStruct((2, n), F32),
                   jax.ShapeDtypeStruct((w, 2 * n), BF16),
                   jax.ShapeDtypeStruct((w, n), BF16),
                   jax.ShapeDtypeStruct((w, n), BF16)],
        compiler_params=pltpu.CompilerParams(vmem_limit_bytes=VMEM_LIMIT_BYTES),
        name="s5_params",
    )(a_re, a_im, ldt, bt_re, bt_im, ct_re, ct_im)


def _s5_kernel(u_ref, h0re_ref, h0im_ref, ab_ref, wb_ref, cret_ref, cimt_ref, d_ref, wglu_ref,
               yb_ref, nre_ref, nim_ref, xh_ref, st_ref, *, rows_per_step, steps, batch_major):
    i = pl.program_id(0)
    n = ab_ref.shape[1]
    w = wb_ref.shape[0]
    r = rows_per_step

    @pl.when(i == 0)
    def _():
        st_ref[0] = h0re_ref[...]
        st_ref[1] = h0im_ref[...]

    if batch_major:
        u = pltpu.einshape("btd->(tb)d", u_ref[...])
    else:
        u = u_ref[...]
    ub = u.astype(BF16)
    for j in range(2 * n // MXU_TILE):
        c0 = (j * MXU_TILE) % n
        k0 = (c0 // S5_STATE * S5_GROUP_CH) // LANES * LANES
        cols = slice(j * MXU_TILE, (j + 1) * MXU_TILE)
        xh_ref[:, cols] = jnp.dot(ub[:, k0:k0 + LANES], wb_ref[k0:k0 + LANES, cols], preferred_element_type=F32)
    for c0 in range(0, n, S5_SCAN_LANES):
        re = slice(c0, c0 + S5_SCAN_LANES)
        im = slice(n + c0, n + c0 + S5_SCAN_LANES)
        a_re = jnp.broadcast_to(ab_ref[0:1, re], (r, S5_SCAN_LANES))
        a_im = jnp.broadcast_to(ab_ref[1:2, re], (r, S5_SCAN_LANES))

        def step(t, carry, re=re, im=im, a_re=a_re, a_im=a_im):
            h_re, h_im = carry
            rows = pl.ds(pl.multiple_of(t * r, r), r)
            n_re = a_re * h_re - a_im * h_im + xh_ref[rows, re]
            n_im = a_re * h_im + a_im * h_re + xh_ref[rows, im]
            xh_ref[rows, re] = n_re
            xh_ref[rows, im] = n_im
            return n_re, n_im

        carry = (st_ref[0, :, re], st_ref[1, :, re])
        if steps <= 8:
            for t in range(steps):
                carry = step(t, carry)
        else:
            carry = lax.fori_loop(0, steps, step, carry, unroll=4)
        st_ref[0, :, re] = carry[0]
        st_ref[1, :, re] = carry[1]

    n_ct = w // MXU_TILE
    ys = []
    for j in range(n_ct):
        rows = slice(j * MXU_TILE, (j + 1) * MXU_TILE)
        ks = slice(j * (n // n_ct), (j + 1) * (n // n_ct))
        ks_im = slice(n + j * (n // n_ct), n + (j + 1) * (n // n_ct))
        ys.append(_bdot_nt(xh_ref[:, ks], cret_ref[rows, ks]) - _bdot_nt(xh_ref[:, ks_im], cimt_ref[rows, ks]))
    y = jnp.concatenate(ys, axis=1) + d_ref[...] * u
    pre = _bdot(_gelu_tanh(y), wglu_ref[...])
    half = pre.shape[1] // 2
    yb = pre[:, :half] * _sigmoid(pre[:, half:])
    if batch_major:
        yb_ref[...] = pltpu.einshape("(tb)d->btd", yb, b=r)
    else:
        yb_ref[...] = yb

    @pl.when(i == pl.num_programs(0) - 1)
    def _():
        nre_ref[...] = st_ref[0]
        nim_ref[...] = st_ref[1]


def _s5(u, h0_re, h0_im, sp, rows_per_step, steps, batch_major):
    n = sp["ab"].shape[1]
    w = u.shape[-1]
    tm = rows_per_step * steps
    d_model = sp["w_glu"].shape[1] // 2
    if batch_major:
        nb, seq, _ = u.shape
        grid = (seq // steps,)
        u_spec = pl.BlockSpec((nb, steps, w), lambda i: (0, i, 0))
        yb_spec = pl.BlockSpec((nb, steps, d_model), lambda i: (0, i, 0))
        yb_shape = (nb, seq, d_model)
    else:
        grid = (u.shape[0] // tm,)
        u_spec = pl.BlockSpec((tm, w), lambda i: (i, 0))
        yb_spec = pl.BlockSpec((tm, d_model), lambda i: (i, 0))
        yb_shape = (u.shape[0], d_model)
    return pl.pallas_call(
        functools.partial(_s5_kernel, rows_per_step=rows_per_step, steps=steps, batch_major=batch_major),
        grid=grid,
        in_specs=[u_spec,
                  _const_spec(h0_re.shape), _const_spec(h0_im.shape), _const_spec(sp["ab"].shape),
                  _const_spec(sp["wb"].shape), _const_spec(sp["c_re_t"].shape), _const_spec(sp["c_im_t"].shape),
                  _const_spec((1, w)), _const_spec(sp["w_glu"].shape)],
        out_specs=[yb_spec, _const_spec(h0_re.shape), _const_spec(h0_im.shape)],
        out_shape=[jax.ShapeDtypeStruct(yb_shape, F32),
                   jax.ShapeDtypeStruct(h0_re.shape, F32),
                   jax.ShapeDtypeStruct(h0_im.shape, F32)],
        scratch_shapes=[pltpu.VMEM((tm, 2 * n), F32),
                        pltpu.VMEM((2, rows_per_step, n), F32)],
        compiler_params=_cparams(("arbitrary",)),
        name="s5_scan",
    )(u, h0_re, h0_im, sp["ab"], sp["wb"], sp["c_re_t"], sp["c_im_t"], sp["d"], sp["w_glu"])


def _merge_router_steps(i, row_refs, mod_ref, weight_refs, cnt_ref, run_ref, *, per_token, rows_per_seq):
    @pl.when(i == 0)
    def _():
        run_ref[...] = jnp.zeros(run_ref.shape, F32)

    lead = row_refs[3].shape[0]
    rows_each = 1 if rows_per_seq is None else rows_per_seq
    sub = lead // ROUTER_SUBTILES if (lead * rows_each) % (ROUTER_SUBTILES * LANES) == 0 else lead
    for r0 in range(0, lead, sub):
        rows = pl.ds(r0, sub)
        if per_token:
            mod = mod_ref.at[:, rows]
        elif rows_per_seq is not None:
            mod = mod_ref.at[rows]
        else:
            mod = mod_ref
        _route_rows(*[ref.at[rows] for ref in row_refs], mod, *weight_refs, run_ref,
                    per_token=per_token, rows_per_seq=rows_per_seq)

    @pl.when(i == pl.num_programs(0) - 1)
    def _():
        cnt_ref[...] = run_ref[...]


def _merge_router_kernel(yn_ref, yb_ref, gate_ref, x_ref, mod_ref, woa_ref, wo_ref, nw_ref, wr_ref, br_ref,
                         x1_ref, h2_ref, meta_ref, cnt_ref, run_ref, *, per_token):
    _merge_router_steps(pl.program_id(0), (yn_ref, yb_ref, gate_ref, x_ref, x1_ref, h2_ref, meta_ref), mod_ref,
                        (woa_ref, wo_ref, nw_ref, wr_ref, br_ref), cnt_ref, run_ref,
                        per_token=per_token, rows_per_seq=None)


def _rows2d(ref):
    v = ref[...]
    return v.reshape(-1, v.shape[-1]) if v.ndim == 3 else v


def _route_rows(yn_ref, yb_ref, gate_ref, x_ref, x1_ref, h2_ref, meta_ref, mod_ref,
                woa_ref, wo_ref, nw_ref, wr_ref, br_ref, run_ref, *, per_token, rows_per_seq):
    x = _rows2d(x_ref)
    tm, d = x.shape
    ya = jnp.dot(_rows2d(yn_ref), woa_ref[...], preferred_element_type=F32)
    gate = _sigmoid(_rows2d(gate_ref))
    merged = gate[:, :d] * ya + gate[:, d:] * _rows2d(yb_ref)
    mix = _bdot(merged, wo_ref[...])
    g1 = _mod_rows(mod_ref, 2, per_token, rows_per_seq)
    sh2 = _mod_rows(mod_ref, 3, per_token, rows_per_seq)
    sc2 = _mod_rows(mod_ref, 4, per_token, rows_per_seq)
    x1 = x + g1 * mix
    x1_ref[...] = x1.reshape(x1_ref.shape)
    h2 = _rms(x1) * nw_ref[...] * (1.0 + sc2) + sh2
    h2_ref[...] = h2.reshape(tm, d // LANES, LANES).reshape(h2_ref.shape)

    wr = wr_ref[...]
    wr_hi = wr.astype(BF16)
    wr_lo = (wr - wr_hi.astype(F32)).astype(BF16)
    h2_hi = h2.astype(BF16)
    h2_lo = (h2 - h2_hi.astype(F32)).astype(BF16)
    d = functools.partial(jnp.dot, preferred_element_type=F32)
    logits = d(h2_hi, wr_hi) + d(h2_lo, wr_hi) + d(h2_hi, wr_lo)
    lt = logits.T
    erow = lax.broadcasted_iota(jnp.int32, lt.shape, 0)
    lt = jnp.where(erow < N_EXPERTS, lt, -jnp.inf)
    ex = jnp.exp(lt - jnp.max(lt, axis=0, keepdims=True))
    scores = ex / jnp.sum(ex, axis=0, keepdims=True)
    sel = scores + br_ref[...]
    s = [sel[e:e + 1, :] for e in range(N_EXPERTS)]
    p = [scores[e:e + 1, :] for e in range(N_EXPERTS)]

    def group_top2_sum(vals):
        best = None
        for a in range(len(vals)):
            for b in range(a + 1, len(vals)):
                pair = vals[a] + vals[b]
                best = pair if best is None else jnp.maximum(best, pair)
        return best

    gs = [group_top2_sum(s[EXPERTS_PER_GROUP * g:EXPERTS_PER_GROUP * (g + 1)]) for g in range(N_EXPERT_GROUPS)]
    best = gs[0]
    bg = jnp.zeros(best.shape, jnp.int32)
    for g in range(1, N_EXPERT_GROUPS):
        better = gs[g] > best
        bg = jnp.where(better, g, bg)
        best = jnp.where(better, gs[g], best)

    def pick(rows, j):
        out = rows[j]
        for g in range(1, N_EXPERT_GROUPS):
            out = jnp.where(bg == g, rows[EXPERTS_PER_GROUP * g + j], out)
        return out

    cs = [pick(s, j) for j in range(EXPERTS_PER_GROUP)]
    cp = [pick(p, j) for j in range(EXPERTS_PER_GROUP)]

    def argmax_first(vals, skip):
        bv = None
        bi = None
        bw = None
        for j in range(len(vals)):
            v = vals[j] if skip is None else jnp.where(skip == j, -jnp.inf, vals[j])
            if bv is None:
                bv, bi, bw = v, jnp.zeros(v.shape, jnp.int32), cp[j]
            else:
                better = v > bv
                bi = jnp.where(better, j, bi)
                bw = jnp.where(better, cp[j], bw)
                bv = jnp.where(better, v, bv)
        return bi, bw

    i1, w1 = argmax_first(cs, None)
    i2, w2 = argmax_first(cs, i1)
    wsum = w1 + w2
    e1 = bg * EXPERTS_PER_GROUP + i1
    e2 = bg * EXPERTS_PER_GROUP + i2
    oh1 = jnp.where(erow == e1, 1.0, 0.0)
    oh2 = jnp.where(erow == e2, 1.0, 0.0)
    both = oh1 + oh2
    ta = lax.broadcasted_iota(jnp.int32, (tm, tm), 0)
    tb = lax.broadcasted_iota(jnp.int32, (tm, tm), 1)
    earlier = jnp.where(ta < tb, 1.0, 0.0).astype(BF16)
    before = jnp.dot(both.astype(BF16), earlier, preferred_element_type=F32) + run_ref[...]
    r1 = jnp.sum(oh1 * before, axis=0, keepdims=True)
    r2 = jnp.sum(oh2 * before, axis=0, keepdims=True)
    run_ref[...] = run_ref[...] + jnp.sum(both, axis=1, keepdims=True)

    record = [w1 / wsum, w2 / wsum, e1.astype(F32), e2.astype(F32), r1, r2]
    wrow = lax.broadcasted_iota(jnp.int32, lt.shape, 0)
    wmat = jnp.zeros(lt.shape, F32)
    for k, v in enumerate(record):
        wmat = jnp.where(wrow == k, v, wmat)
    meta_ref[...] = wmat.T.reshape(meta_ref.shape)


def _merge_router(yn, yb, gate, x, mod, lw, rw, *, tm, per_token, mod_spec):
    t, d = x.shape
    row = lambda n: pl.BlockSpec((tm, n), lambda i: (i, 0))
    return pl.pallas_call(
        functools.partial(_merge_router_kernel, per_token=per_token),
        grid=(t // tm,),
        in_specs=[row(d), row(d), row(2 * d), row(d), mod_spec,
                  _const_spec((d, d)), _const_spec((d, d)), _const_spec((1, d)),
                  _const_spec((d, LANES)), _const_spec((LANES, 1))],
        out_specs=[row(d), _row_tiles(tm, d), row(LANES), _const_spec((LANES, 1))],
        out_shape=[jax.ShapeDtypeStruct((t, d), F32),
                   jax.ShapeDtypeStruct((t, d // LANES, LANES), F32),
                   jax.ShapeDtypeStruct((t, LANES), F32),
                   jax.ShapeDtypeStruct((LANES, 1), F32)],
        scratch_shapes=[pltpu.VMEM((LANES, 1), F32)],
        compiler_params=_cparams(("arbitrary",)),
        name="merge_router",
    )(yn, yb, gate, x, mod, lw["w_out_a"], lw["w_out"], lw["norm2_w"], rw["w_router"], rw["b_router"])


def _s5_merge_kernel(u_ref, h0re_ref, h0im_ref, ab_ref, wb_ref, cret_ref, cimt_ref, d_ref, wglu_ref,
                     yn_ref, gate_ref, x_ref, mod_ref, woa_ref, wo_ref, nw_ref, wr_ref, br_ref,
                     nre_ref, nim_ref, x1_ref, h2_ref, meta_ref, cnt_ref,
                     xh_ref, st_ref, yb_ref, run_ref, *, steps):
    _s5_kernel(u_ref, h0re_ref, h0im_ref, ab_ref, wb_ref, cret_ref, cimt_ref, d_ref, wglu_ref,
               yb_ref, nre_ref, nim_ref, xh_ref, st_ref, rows_per_step=u_ref.shape[0], steps=steps, batch_major=True)
    _merge_router_steps(pl.program_id(0), (yn_ref, yb_ref, gate_ref, x_ref, x1_ref, h2_ref, meta_ref), mod_ref,
                        (woa_ref, wo_ref, nw_ref, wr_ref, br_ref), cnt_ref, run_ref,
                        per_token=False, rows_per_seq=steps)


def _resident_spec(shape):
    nd = len(shape)
    return pl.BlockSpec(shape, lambda *_: (0,) * nd, pipeline_mode=pl.Buffered(1))


def _s5_merge(u, yn, gate, x, mod, zero_state, sp, lw, rw, steps):
    nb, seq, w = u.shape
    d = x.shape[-1]
    n = sp["ab"].shape[1]
    blk = lambda *tail: pl.BlockSpec((nb, steps) + tail, lambda i: (0, i) + (0,) * len(tail))
    weights = [sp["ab"], sp["wb"], sp["c_re_t"], sp["c_im_t"], sp["d"], sp["w_glu"]]
    merge_w = [lw["w_out_a"], lw["w_out"], lw["norm2_w"], rw["w_router"], rw["b_router"]]
    return pl.pallas_call(
        functools.partial(_s5_merge_kernel, steps=steps),
        grid=(seq // steps,),
        in_specs=[blk(w), _resident_spec(zero_state.shape), _resident_spec(zero_state.shape)]
                 + [_resident_spec(a.shape) for a in weights]
                 + [blk(d), blk(2 * d), blk(d), _resident_spec(mod.shape)]
                 + [_resident_spec(a.shape) for a in merge_w],
        out_specs=[_const_spec(zero_state.shape), _const_spec(zero_state.shape),
                   blk(d), blk(d // LANES, LANES), blk(LANES), _const_spec((LANES, 1))],
        out_shape=[jax.ShapeDtypeStruct(zero_state.shape, F32),
                   jax.ShapeDtypeStruct(zero_state.shape, F32),
                   jax.ShapeDtypeStruct((nb, seq, d), F32),
                   jax.ShapeDtypeStruct((nb, seq, d // LANES, LANES), F32),
                   jax.ShapeDtypeStruct((nb, seq, LANES), F32),
                   jax.ShapeDtypeStruct((LANES, 1), F32)],
        scratch_shapes=[pltpu.VMEM((nb * steps, 2 * n), F32), pltpu.VMEM((2, nb, n), F32),
                        pltpu.VMEM((nb, steps, d), F32), pltpu.VMEM((LANES, 1), F32)],
        compiler_params=_cparams(("arbitrary",)),
        name="s5_merge_router",
    )(u, zero_state, zero_state, *weights, yn, gate, x, mod, *merge_w)


def _meta_rows_kernel(meta_ref, idx_ref, rank_ref):
    m = meta_ref[...].T
    idx_ref[...] = m[2:4, :].astype(jnp.int32)
    rank_ref[...] = m[4:6, :].astype(jnp.int32)


def _meta_rows(meta):
    t = meta.shape[0]
    tm = min(t, ROW_TILE)
    pair = pl.BlockSpec((2, tm), lambda i: (0, i))
    return pl.pallas_call(
        _meta_rows_kernel,
        grid=(t // tm,),
        in_specs=[pl.BlockSpec((tm, LANES), lambda i: (i, 0))],
        out_specs=[pair, pair],
        out_shape=[jax.ShapeDtypeStruct((2, t), jnp.int32), jax.ShapeDtypeStruct((2, t), jnp.int32)],
        compiler_params=_cparams(("arbitrary",)),
        name="moe_meta_rows",
    )(meta)


def _plan_kernel(cnt_ref, idx_ref, rank_ref, dest_ref, te_ref, ends_ref, nv_ref, *, tile, n_tiles):
    shift = tile.bit_length() - 1
    run = jnp.int32(0)
    starts = []
    for e in range(N_EXPERTS):
        starts.append(run)
        run = run + (((cnt_ref[e] + (tile - 1)) >> shift) << shift)
        ends_ref[e] = run
    n_valid = run >> shift
    nv_ref[0] = n_valid
    idx = idx_ref[...]
    dest = rank_ref[...]
    for e in range(N_EXPERTS):
        dest = dest + jnp.where(idx == e, starts[e], 0)
    dest_ref[...] = dest

    def tile_owner(j, carry):
        pos = jnp.minimum(j, n_valid - 1) * tile
        owner = jnp.int32(0)
        for e in range(N_EXPERTS - 1):
            owner = owner + jnp.where(ends_ref[e] <= pos, 1, 0)
        te_ref[j] = owner
        return carry

    lax.fori_loop(0, n_tiles, tile_owner, 0)


def _plan(cnt, idx, rank, tile, n_tiles):
    smem = pl.BlockSpec(memory_space=pltpu.SMEM)
    vmem = pl.BlockSpec(memory_space=pltpu.VMEM)
    return pl.pallas_call(
        functools.partial(_plan_kernel, tile=tile, n_tiles=n_tiles),
        in_specs=[smem, vmem, vmem],
        out_specs=[vmem, smem, smem, smem],
        out_shape=[jax.ShapeDtypeStruct(idx.shape, jnp.int32),
                   jax.ShapeDtypeStruct((n_tiles,), jnp.int32),
                   jax.ShapeDtypeStruct((N_EXPERTS,), jnp.int32),
                   jax.ShapeDtypeStruct((1,), jnp.int32)],
        name="moe_plan",
    )(cnt, idx, rank)


def _row_tiles(tm, d):
    return pl.BlockSpec((tm, d // LANES, LANES), lambda i: (i, 0, 0))


def _row_copy(src_ref, src_row, dst_ref, dst_row, sem):
    return pltpu.make_async_copy(src_ref.at[src_row], dst_ref.at[dst_row], sem)


def _each_row(tm, fn):
    def body(t, carry):
        for k in range(2):
            fn(t, k)
        return carry

    lax.fori_loop(0, tm, body, 0, unroll=8)


def _dispatch_kernel(ends_ref, dest_ref, prev_dest_ref, h_ref, out_ref, zero_ref, stage_ref, sem, zsem, *, tile):
    i = pl.program_id(0)
    tm = h_ref.shape[0]
    slot = i % 2

    @pl.when(i == 0)
    def _():
        zero_ref[...] = jnp.zeros(zero_ref.shape, F32)

        def clear(start):
            return pltpu.make_async_copy(zero_ref, out_ref.at[pl.ds(pl.multiple_of(start, tile), tile)], zsem)

        def used(e):
            return ends_ref[e] > (ends_ref[e - 1] if e else 0)

        def unused_tiles(fn):
            n_valid = ends_ref[N_EXPERTS - 1] >> (tile.bit_length() - 1)
            lax.fori_loop(n_valid, out_ref.shape[0] // tile, lambda j, c: (fn(clear(j * tile)), c)[1], 0)

        for e in range(N_EXPERTS):
            pl.when(used(e))(lambda e=e: clear(ends_ref[e] - tile).start())
        unused_tiles(lambda cp: cp.start())
        for e in range(N_EXPERTS):
            pl.when(used(e))(lambda e=e: clear(ends_ref[e] - tile).wait())
        unused_tiles(lambda cp: cp.wait())

    stage_ref[slot] = h_ref[...]
    _each_row(tm, lambda t, k: _row_copy(stage_ref.at[slot], t, out_ref, dest_ref[k, t], sem.at[slot]).start(priority=k))

    @pl.when(i > 0)
    def _():
        _each_row(tm, lambda t, k: _row_copy(stage_ref.at[1 - slot], t, out_ref, prev_dest_ref[k, t],
                                             sem.at[1 - slot]).wait())

    @pl.when(i == pl.num_programs(0) - 1)
    def _():
        _each_row(tm, lambda t, k: _row_copy(stage_ref.at[slot], t, out_ref, dest_ref[k, t], sem.at[slot]).wait())


def _dispatch(ends, dest, h2, n_rows, tm, tile):
    t, nt, lanes = h2.shape
    idx_spec = lambda fn: pl.BlockSpec((2, tm), fn, memory_space=pltpu.SMEM)
    return pl.pallas_call(
        functools.partial(_dispatch_kernel, tile=tile),
        grid=(t // tm,),
        in_specs=[pl.BlockSpec(memory_space=pltpu.SMEM),
                  idx_spec(lambda i: (0, i)), idx_spec(lambda i: (0, jnp.maximum(i - 1, 0))),
                  _row_tiles(tm, nt * lanes)],
        out_specs=pl.BlockSpec(memory_space=pl.ANY),
        out_shape=jax.ShapeDtypeStruct((n_rows, nt, lanes), F32),
        scratch_shapes=[pltpu.VMEM((tile, nt, lanes), F32), pltpu.VMEM((2, tm, nt, lanes), F32),
                        pltpu.SemaphoreType.DMA((2,)), pltpu.SemaphoreType.DMA(())],
        compiler_params=_cparams(("arbitrary",)),
        name="moe_dispatch",
    )(ends, dest, dest, h2)


def _expert_kernel(te_ref, nv_ref, x_ref, w1_ref, w3_ref, w2_ref, y_ref, w1b_ref, w3b_ref, w2b_ref):
    j = pl.program_id(0)
    fresh = jnp.logical_or(j == 0, te_ref[j] != te_ref[jnp.maximum(j - 1, 0)])

    @pl.when(fresh)
    def _():
        w1b_ref[...] = w1_ref[...].astype(BF16)
        w3b_ref[...] = w3_ref[...].astype(BF16)
        w2b_ref[...] = w2_ref[...].astype(BF16)

    @pl.when(j < nv_ref[0])
    def _():
        tm, nt, lanes = x_ref.shape
        xb = x_ref[...].reshape(tm, nt * lanes).astype(BF16)
        a = jnp.dot(xb, w1b_ref[...], preferred_element_type=F32)
        b = jnp.dot(xb, w3b_ref[...], preferred_element_type=F32)
        y_ref[...] = _bdot(_silu(a) * b, w2b_ref[...]).reshape(tm, nt, lanes)

    @pl.when(j >= nv_ref[0])
    def _():
        y_ref[...] = jnp.zeros(y_ref.shape, F32)


def _experts(tile_expert, n_valid, xs, w1, w3, w2, layer, tm):
    n_rows, nt, lanes = xs.shape
    d = nt * lanes
    de = w1.shape[3]
    wspec = lambda r, c: pl.BlockSpec((None, None, r, c), lambda j, te, nv: (layer, te[j], 0, 0))
    grid_spec = pltpu.PrefetchScalarGridSpec(
        num_scalar_prefetch=2,
        grid=(n_rows // tm,),
        in_specs=[pl.BlockSpec((tm, nt, lanes), lambda j, te, nv: (jnp.minimum(j, nv[0] - 1), 0, 0)),
                  wspec(d, de), wspec(d, de), wspec(de, d)],
        out_specs=pl.BlockSpec((tm, nt, lanes), lambda j, te, nv: (j, 0, 0)),
        scratch_shapes=[pltpu.VMEM((d, de), BF16), pltpu.VMEM((d, de), BF16), pltpu.VMEM((de, d), BF16)],
    )
    return pl.pallas_call(
        _expert_kernel,
        grid_spec=grid_spec,
        out_shape=jax.ShapeDtypeStruct((n_rows, nt, lanes), F32),
        compiler_params=_cparams(("arbitrary",)),
        name="moe_experts",
    )(tile_expert, n_valid, xs, w1, w3, w2)


def _combine_kernel(dest_ref, next_dest_ref, ys_ref, x1_ref, wts_ref, mod_ref, fw_ref, out_ref, buf_ref, sem,
                    *, per_token, final_norm):
    i = pl.program_id(0)
    tm, d = x1_ref.shape
    slot = i % 2

    def gather(idx_ref, s):
        return lambda t, k: _row_copy(ys_ref, idx_ref[k, t], buf_ref.at[s, k], t, sem.at[s])

    @pl.when(i == 0)
    def _():
        _each_row(tm, lambda t, k: gather(dest_ref, slot)(t, k).start(priority=k))

    @pl.when(i + 1 < pl.num_programs(0))
    def _():
        _each_row(tm, lambda t, k: gather(next_dest_ref, 1 - slot)(t, k).start(priority=k))

    _each_row(tm, lambda t, k: gather(dest_ref, slot)(t, k).wait())
    w = wts_ref[...]
    rows = lambda k: buf_ref[slot, k].reshape(tm, d)
    moe = w[:, 0:1] * rows(0) + w[:, 1:2] * rows(1)
    x2 = x1_ref[...] + _mod_rows(mod_ref, 5, per_token) * moe
    if final_norm:
        x2 = _rms(x2) * fw_ref[...]
    out_ref[...] = x2


def _combine(dest, ys, x1, wts, mod, fw, *, tm, per_token, mod_spec, final_norm):
    t, d = x1.shape
    row = lambda n: pl.BlockSpec((tm, n), lambda i: (i, 0))
    n_steps = t // tm
    idx_spec = lambda fn: pl.BlockSpec((2, tm), fn, memory_space=pltpu.SMEM)
    return pl.pallas_call(
        functools.partial(_combine_kernel, per_token=per_token, final_norm=final_norm),
        grid=(n_steps,),
        in_specs=[idx_spec(lambda i: (0, i)), idx_spec(lambda i: (0, jnp.minimum(i + 1, n_steps - 1))),
                  pl.BlockSpec(memory_space=pl.ANY),
                  row(d), row(LANES), mod_spec, _const_spec((1, d))],
        out_specs=row(d),
        out_shape=jax.ShapeDtypeStruct((t, d), F32),
        scratch_shapes=[pltpu.VMEM((2, 2, tm) + ys.shape[1:], F32), pltpu.SemaphoreType.DMA((2,))],
        compiler_params=_cparams(("arbitrary",)),
        name="moe_combine",
    )(dest, dest, ys, x1, wts, mod, fw)


def _moe(h2, idx, rank, counts, x1, wts, mod, fw, ew, layer, *, tok_tile, expert_tile, per_token, mod_spec_fn,
         final_norm):
    t = h2.shape[0]
    tm = expert_tile
    n_rows = ((2 * t + N_EXPERTS * (tm - 1)) // tm + 1) * tm
    cnt = counts[:N_EXPERTS, 0].astype(jnp.int32)
    dest, tile_expert, ends, n_valid = _plan(cnt, idx, rank, tm, n_rows // tm)
    xs = _dispatch(ends, dest, h2, n_rows, tok_tile, tm)
    ys = _experts(tile_expert, n_valid, xs, ew["w1"], ew["w3"], ew["w2"], layer, tm)
    return _combine(dest, ys, x1, wts, mod, fw, tm=tok_tile, per_token=per_token,
                    mod_spec=mod_spec_fn(tok_tile), final_norm=final_norm)


def _pad_lanes(a, n=LANES):
    return jnp.pad(a, [(0, 0)] * (a.ndim - 1) + [(0, n - a.shape[-1])])


def kernel(x_prompt, x_sample, state_ssm, state_conv, state_s5_re, state_s5_im, c_prompt, c_sample, w_in, conv_w, conv_b, dt_bias, a_log, d_ssd, ssd_norm_w, w_out_a, s5_a_re, s5_a_im, s5_log_dt, s5_b_re, s5_b_im, s5_c_re, s5_c_im, s5_d, w_glu, w_out, norm1_w, norm2_w, w_ada, b_ada, w_router, b_router, w1, w3, w2, final_norm_w):
    nb, seq, d = x_prompt.shape
    ns, steps, _ = x_sample.shape
    depth = w_in.shape[0]
    d_in = SSD_HEADS * SSD_HEAD_DIM
    cdim = conv_w.shape[2]
    s5w = S5_GROUPS * S5_GROUP_CH
    s5n = S5_GROUPS * S5_STATE
    k1 = SSD_CONV - 1
    tp = nb * seq
    ts = ns * steps

    mod = _ada_mod(jnp.concatenate([c_prompt, c_sample], axis=0), w_ada, b_ada)
    head_expand = jnp.tile(jnp.repeat(jnp.eye(LANES, SSD_HEADS, dtype=BF16), SSD_HEAD_DIM, axis=1), (3, 1))
    rw = {"w_router": _pad_lanes(w_router), "b_router": _pad_lanes(b_router[None, :]).reshape(LANES, 1)}

    xp = x_prompt.reshape(tp, d)
    xsm = jnp.transpose(x_sample, (1, 0, 2)).reshape(ts, d)
    outs = {k: [] for k in ("ssm_p", "conv_p", "conv_s", "re_p", "re_s", "im_p", "im_s")}
    ssm_s = None
    zero_state = jnp.zeros((nb, s5n), F32)

    for l in range(depth):
        o0 = 0
        o1 = d_in
        o2 = o1 + cdim
        o3 = o2 + SSD_HEADS
        o4 = o3 + s5w
        wl = w_in[l].astype(BF16)
        ws = (wl[:, o0:o1], wl[:, o1:o2], _pad_lanes(wl[:, o2:o3]), wl[:, o3:o4], wl[:, o4:])
        p = {"conv_w": conv_w[l], "conv_b": conv_b[l][None, :], "dt_bias": _pad_lanes(dt_bias[l][None, :]),
             "a_log": _pad_lanes(a_log[l][None, :]),
             "d_skip_e": jnp.repeat(d_ssd[l], SSD_HEAD_DIM)[None, :],
             "ssd_norm_w": ssd_norm_w[l][None, :], "head_expand": head_expand}
        chan_rows = lambda a, perm: jnp.transpose(a, perm).reshape(S5_GROUP_CH, s5n)
        ab, wb, c_re_t, c_im_t = _s5_params(
            s5_a_re[l].reshape(1, s5n), s5_a_im[l].reshape(1, s5n), jnp.repeat(s5_log_dt[l], S5_STATE)[None, :],
            chan_rows(s5_b_re[l], (2, 0, 1)), chan_rows(s5_b_im[l], (2, 0, 1)),
            chan_rows(s5_c_re[l], (1, 0, 2)), chan_rows(s5_c_im[l], (1, 0, 2)))
        sp = {"ab": ab, "wb": wb, "c_re_t": c_re_t, "c_im_t": c_im_t,
              "d": s5_d[l][None, :], "w_glu": w_glu[l].astype(BF16)}
        lw = {"w_out_a": w_out_a[l].astype(BF16), "w_out": w_out[l].astype(BF16), "norm2_w": norm2_w[l][None, :]}
        ew = {"w1": w1, "w3": w3, "w2": w2}
        final = l == depth - 1
        fw = final_norm_w[None, :]

        mod_p = mod[l, :nb].reshape(nb, 6, d)
        mod_p_spec = lambda tm: pl.BlockSpec((None, 6, d), lambda i: (i // (seq // tm), 0, 0))
        z, xbc, dtr, u, gate = _inproj(
            xp, mod_p, norm1_w[l][None, :], ws, tm=ROW_TILE, per_token=False, mod_spec=mod_p_spec(ROW_TILE))
        yn, nconv, nssm = _ssd_prompt(xbc, dtr, z, p, nb, seq)
        seq3 = lambda a: a.reshape((nb, seq) + a.shape[1:])
        nre, nim, x1, h2, meta, counts = _s5_merge(seq3(u), seq3(yn), seq3(gate), seq3(xp), mod_p, zero_state,
                                                   sp, lw, rw, S5_STEPS)
        meta = meta.reshape(tp, LANES)
        idx, rank = _meta_rows(meta)
        xp = _moe(h2.reshape(tp, d // LANES, LANES), idx, rank, counts, x1.reshape(tp, d), meta, mod_p, fw, ew, l,
                  tok_tile=MOE_TILE, expert_tile=EXPERT_TILE, per_token=False, mod_spec_fn=mod_p_spec,
                  final_norm=final)
        outs["ssm_p"].append(nssm)
        outs["conv_p"].append(nconv)
        outs["re_p"].append(nre.reshape(nb, S5_GROUPS, S5_STATE))
        outs["im_p"].append(nim.reshape(nb, S5_GROUPS, S5_STATE))

        mod_s = jnp.transpose(mod[l, nb:].reshape(ns, 6, d), (1, 0, 2))
        mod_s_spec = lambda tm: pl.BlockSpec((6, tm, d), lambda i: (0, i % (ns // tm), 0))
        z, xbc, dtr, u, gate = _inproj(
            xsm, mod_s, norm1_w[l][None, :], ws, tm=ns, per_token=True, mod_spec=mod_s_spec(ns))
        conv0_tm = jnp.transpose(state_conv[l], (1, 0, 2)).reshape(k1 * ns, cdim)
        yn, nconv_tm, ssm_s = _ssd_sample(xbc, dtr, z, conv0_tm, state_ssm, ssm_s, l, p, ns, steps)
        yb, nre, nim = _s5(u, state_s5_re[l].reshape(ns, s5n), state_s5_im[l].reshape(ns, s5n), sp, ns, steps,
                           False)
        x1, h2, meta, counts = _merge_router(
            yn, yb, gate, xsm, mod_s, lw, rw, tm=ns, per_token=True, mod_spec=mod_s_spec(ns))
        idx, rank = _meta_rows(meta)
        xsm = _moe(h2, idx, rank, counts, x1, meta, mod_s, fw, ew, l, tok_tile=ns, expert_tile=ns, per_token=True,
                   mod_spec_fn=mod_s_spec, final_norm=final)
        outs["conv_s"].append(jnp.transpose(nconv_tm.reshape(k1, ns, cdim), (1, 0, 2)))
        outs["re_s"].append(nre.reshape(ns, S5_GROUPS, S5_STATE))
        outs["im_s"].append(nim.reshape(ns, S5_GROUPS, S5_STATE))

    y_prompt = xp.reshape(nb, seq, d)
    y_sample = jnp.transpose(xsm.reshape(steps, ns, d), (1, 0, 2))
    st = lambda k: jnp.stack(outs[k])
    return (y_prompt, y_sample, st("ssm_p"), ssm_s, st("conv_p"), st("conv_s"),
            st("re_p"), st("re_s"), st("im_p"), st("im_s"))
```

```python
import functools

import jax
import jax.numpy as jnp
from jax import lax
from jax.experimental import pallas as pl
from jax.experimental.pallas import tpu as pltpu

F32 = jnp.float32
BF16 = jnp.bfloat16
EPS = 1e-6

SSD_HEADS = 16
SSD_HEAD_DIM = 64
SSD_GROUPS = 2
SSD_STATE = 128
SSD_CONV = 4
SSD_CHUNK = 128
S5_GROUPS = 32
S5_GROUP_CH = 16
S5_STATE = 64
N_EXPERTS = 16
EXPERTS_PER_GROUP = 4
N_EXPERT_GROUPS = 4

LANES = 128
SUBLANES = 8
MXU_TILE = 256
VMEM_LIMIT_BYTES = 56 * 1024 * 1024

ROW_TILE = 512
MOE_TILE = 256
EXPERT_TILE = 512
S5_STEPS = 64
S5_SCAN_LANES = 512
ROUTER_SUBTILES = 2
SSD_SAMPLE_SEQS = 4


def _cparams(sem):
    return pltpu.CompilerParams(dimension_semantics=sem, vmem_limit_bytes=VMEM_LIMIT_BYTES)


def _bdot(a, b):
    return jnp.dot(a.astype(BF16), b.astype(BF16), preferred_element_type=F32)


def _bdot_nt(a, b):
    return lax.dot_general(a.astype(BF16), b.astype(BF16), (((1,), (1,)), ((), ())),
                           preferred_element_type=F32)


def _split3(x):
    hi = x.astype(BF16)
    r1 = x - hi.astype(F32)
    mid = r1.astype(BF16)
    lo = (r1 - mid.astype(F32)).astype(BF16)
    return hi, mid, lo


def _dot3_lhs(x, m3):
    return jnp.dot(jnp.concatenate(_split3(x), axis=1), m3, preferred_element_type=F32)


def _dot3_rhs(m, x):
    return jnp.dot(jnp.concatenate([m] * 3, axis=1), jnp.concatenate(_split3(x), axis=0),
                   preferred_element_type=F32)


def _sigmoid(x):
    return jax.nn.sigmoid(x)


def _silu(x):
    return x * _sigmoid(x)


def _softplus(x):
    return jnp.maximum(x, 0.0) + jnp.log1p(jnp.exp(-jnp.abs(x)))


def _gelu_tanh(x):
    c = 0.7978845608028654
    return 0.5 * x * (1.0 + jnp.tanh(c * (x + 0.044715 * (x * x * x))))


def _rms(x):
    return x * lax.rsqrt(jnp.mean(x * x, axis=-1, keepdims=True) + EPS)


def _ada_kernel(c_ref, w_ref, b_ref, o_ref):
    o_ref[0] = _bdot(_silu(c_ref[...]), w_ref[0]) + b_ref[0]


def _ada_mod(c_all, w_ada, b_ada):
    depth, d, n = w_ada.shape
    nseq = c_all.shape[0]
    tn = 1536
    return pl.pallas_call(
        _ada_kernel,
        grid=(depth, n // tn),
        in_specs=[pl.BlockSpec((nseq, d), lambda l, j: (0, 0)),
                  pl.BlockSpec((1, d, tn), lambda l, j: (l, 0, j)),
                  pl.BlockSpec((1, 1, tn), lambda l, j: (l, 0, j))],
        out_specs=pl.BlockSpec((1, nseq, tn), lambda l, j: (l, 0, j)),
        out_shape=jax.ShapeDtypeStruct((depth, nseq, n), F32),
        compiler_params=_cparams(("arbitrary", "arbitrary")),
        name="ada_mod",
    )(c_all, w_ada, b_ada.reshape(depth, 1, n))


def _mod_rows(mod_ref, j, per_token, rows_per_seq=None):
    if per_token:
        return mod_ref[j]
    if rows_per_seq is not None:
        m = mod_ref[:, j, :]
        return jnp.broadcast_to(m[:, None, :], (m.shape[0], rows_per_seq, m.shape[1])).reshape(-1, m.shape[1])
    return mod_ref[j:j + 1, :]


def _inproj_kernel(x_ref, mod_ref, nw_ref, wz_ref, wx_ref, wdt_ref, wu_ref, wg_ref,
                   z_ref, xbc_ref, dt_ref, u_ref, g_ref, *, per_token):
    sh = _mod_rows(mod_ref, 0, per_token)
    sc = _mod_rows(mod_ref, 1, per_token)
    h = (_rms(x_ref[...]) * nw_ref[...] * (1.0 + sc) + sh).astype(BF16)
    d = functools.partial(jnp.dot, preferred_element_type=F32)
    z_ref[...] = d(h, wz_ref[...])
    xbc_ref[...] = d(h, wx_ref[...])
    dt_ref[...] = d(h, wdt_ref[...])
    u_ref[...] = d(h, wu_ref[...])
    g_ref[...] = d(h, wg_ref[...])


def _const_spec(shape):
    nd = len(shape)
    return pl.BlockSpec(shape, lambda *_: (0,) * nd)


def _inproj(x, mod, nw, ws, *, tm, per_token, mod_spec):
    t, d = x.shape
    wz, wx, wdt, wu, wg = ws
    row = lambda n: pl.BlockSpec((tm, n), lambda i: (i, 0))
    return pl.pallas_call(
        functools.partial(_inproj_kernel, per_token=per_token),
        grid=(t // tm,),
        in_specs=[row(d), mod_spec, _const_spec((1, d)),
                  _const_spec(wz.shape), _const_spec(wx.shape), _const_spec(wdt.shape),
                  _const_spec(wu.shape), _const_spec(wg.shape)],
        out_specs=[row(wz.shape[1]), row(wx.shape[1]), row(wdt.shape[1]), row(wu.shape[1]), row(wg.shape[1])],
        out_shape=[jax.ShapeDtypeStruct((t, wz.shape[1]), F32),
                   jax.ShapeDtypeStruct((t, wx.shape[1]), F32),
                   jax.ShapeDtypeStruct((t, wdt.shape[1]), F32),
                   jax.ShapeDtypeStruct((t, wu.shape[1]), F32),
                   jax.ShapeDtypeStruct((t, wg.shape[1]), F32)],
        compiler_params=_cparams(("arbitrary",)),
        name="inproj",
    )(x, mod, nw, wz, wx, wdt, wu, wg)


def _gated_group_norm(y, z, nw):
    y = y * _silu(z)
    half = y.shape[1] // SSD_GROUPS
    parts = [_rms(y[:, g * half:(g + 1) * half]) for g in range(SSD_GROUPS)]
    return jnp.concatenate(parts, axis=1) * nw


def _ssd_chunk(first, last, xbc_ref, dt_ref, z_ref, cw_ref, cb_ref, dtb_ref, alog_ref, dsk_ref, nw_ref,
               e_ref, yn_ref, conv_ref, ssm_ref, ext_ref, st_ref):
    q = SSD_CHUNK
    d_in = SSD_HEADS * SSD_HEAD_DIM
    gw = SSD_STATE
    hpg = SSD_HEADS // SSD_GROUPS
    grows = hpg * SSD_HEAD_DIM

    if first is not None:
        @pl.when(first)
        def _():
            ext_ref[...] = jnp.zeros(ext_ref.shape, F32)
            st_ref[...] = jnp.zeros(st_ref.shape, F32)

    xbc = xbc_ref[...]
    sub = lax.broadcasted_iota(jnp.int32, (SUBLANES, xbc.shape[1]), 0)
    taps = [cw_ref[k:k + 1, :] for k in range(SSD_CONV)]
    prev_rolled = [pltpu.roll(ext_ref[...], j, axis=0) for j in range(1, SSD_CONV)]
    acc_tiles = []
    for r0 in range(0, q, SUBLANES):
        cur = xbc[r0:r0 + SUBLANES, :]
        rolled = [pltpu.roll(cur, j, axis=0) for j in range(1, SSD_CONV)]
        acc = cb_ref[...] + taps[SSD_CONV - 1] * cur
        for j in range(1, SSD_CONV):
            acc = acc + taps[SSD_CONV - 1 - j] * jnp.where(sub < j, prev_rolled[j - 1], rolled[j - 1])
        acc_tiles.append(acc)
        prev_rolled = rolled
    ext_ref[...] = xbc[q - SUBLANES:q, :]
    v = _silu(jnp.concatenate(acc_tiles, axis=0))
    xs = v[:, :d_in]
    bm = v[:, d_in:d_in + SSD_GROUPS * gw]
    cm = v[:, d_in + SSD_GROUPS * gw:]

    dt = _softplus(dt_ref[...] + dtb_ref[...])
    a = -jnp.exp(alog_ref[...])
    da = dt * a
    rid = lax.broadcasted_iota(jnp.int32, (q, q), 0)
    cid = lax.broadcasted_iota(jnp.int32, (q, q), 1)
    causal = rid >= cid
    tri = jnp.where(causal, 1.0, 0.0).astype(BF16)
    a_cs = _dot3_rhs(tri, da)
    a_cs_t = a_cs.T
    a_last = a_cs[q - 1:q, :]
    per_head = jnp.concatenate([dt, jnp.exp(a_last - a_cs) * dt, jnp.exp(a_cs)], axis=0)
    per_chan = _dot3_lhs(per_head, e_ref[...])
    x_dt = xs * per_chan[0:q]
    x_end = xs * per_chan[q:2 * q]
    eacs_e = per_chan[2 * q:3 * q]
    x_end_t = x_end.T.astype(BF16)
    lane = lax.broadcasted_iota(jnp.int32, (q, 2 * SSD_HEAD_DIM), 1)
    first = lane < SSD_HEAD_DIM

    y_diag = []
    y_off = []
    for g in range(SSD_GROUPS):
        bg = bm[:, g * gw:(g + 1) * gw].astype(BF16)
        cg = cm[:, g * gw:(g + 1) * gw].astype(BF16)
        cb = _bdot_nt(cg, bg)
        sg = st_ref[g * grows:(g + 1) * grows, :]
        y_off.append(_bdot_nt(cg, sg))
        for pair in range(hpg // 2):
            ms = []
            for r in (2 * pair, 2 * pair + 1):
                h = g * hpg + r
                diff = a_cs[:, h:h + 1] - a_cs_t[h:h + 1, :]
                decay = jnp.where(causal, jnp.exp(jnp.where(causal, diff, 0.0)), 0.0)
                ms.append((cb * decay).astype(BF16))
            h0 = g * hpg + 2 * pair
            xp = x_dt[:, h0 * SSD_HEAD_DIM:(h0 + 2) * SSD_HEAD_DIM]
            rhs = jnp.concatenate([jnp.where(first, xp, 0.0), jnp.where(first, 0.0, xp)], axis=0)
            y_diag.append(jnp.dot(jnp.concatenate(ms, axis=1), rhs.astype(BF16), preferred_element_type=F32))
        contrib = jnp.dot(x_end_t[g * grows:(g + 1) * grows, :], bg, preferred_element_type=F32)
        for r in range(hpg):
            h = g * hpg + r
            rows = slice(h * SSD_HEAD_DIM, (h + 1) * SSD_HEAD_DIM)
            dec = jnp.exp(a_cs_t[h:h + 1, q - 1:q])
            st_ref[rows, :] = st_ref[rows, :] * dec + contrib[r * SSD_HEAD_DIM:(r + 1) * SSD_HEAD_DIM, :]

    y = (jnp.concatenate(y_diag, axis=1) + jnp.concatenate(y_off, axis=1) * eacs_e
         + dsk_ref[...] * xs)
    yn_ref[...] = _gated_group_norm(y, z_ref[...], nw_ref[...]).astype(BF16)

    if last is not None:
        @pl.when(last)
        def _():
            conv_ref[0] = ext_ref[pl.ds(SUBLANES - (SSD_CONV - 1), SSD_CONV - 1), :]
            ssm_ref[0] = st_ref[...].reshape(SSD_HEADS, SSD_HEAD_DIM, SSD_STATE)


def _inproj_ssd_kernel(x_ref, mod_ref, nw1_ref, wz_ref, wx_ref, wdt_ref, wu_ref, wg_ref,
                       cw_ref, cb_ref, dtb_ref, alog_ref, dsk_ref, nw_ref, e_ref,
                       u_ref, g_ref, yn_ref, conv_ref, ssm_ref,
                       z_s, xbc_s, dt_s, ext_ref, st_ref):
    j = pl.program_id(1)
    sh = mod_ref[0:1, :]
    sc = mod_ref[1:2, :]
    h = (_rms(x_ref[...]) * nw1_ref[...] * (1.0 + sc) + sh).astype(BF16)
    d = functools.partial(jnp.dot, preferred_element_type=F32)
    z_s[...] = d(h, wz_ref[...])
    xbc_s[...] = d(h, wx_ref[...])
    dt_s[...] = d(h, wdt_ref[...])
    u_ref[...] = d(h, wu_ref[...])
    g_ref[...] = d(h, wg_ref[...])
    n_chunks = x_ref.shape[0] // SSD_CHUNK
    for k in range(n_chunks):
        rows = pl.ds(k * SSD_CHUNK, SSD_CHUNK)
        first = (j == 0) if k == 0 else None
        last = (j == pl.num_programs(1) - 1) if k == n_chunks - 1 else None
        _ssd_chunk(first, last, xbc_s.at[rows], dt_s.at[rows], z_s.at[rows], cw_ref, cb_ref, dtb_ref, alog_ref,
                   dsk_ref, nw_ref, e_ref, yn_ref.at[rows], conv_ref, ssm_ref, ext_ref, st_ref)


def _inproj_ssd(x, mod, nw1, ws, p, nb, seq, tm):
    t, d = x.shape
    wz, wx, wdt, wu, wg = ws
    cdim = wx.shape[1]
    d_in = wz.shape[1]
    tiles = seq // tm
    row = lambda n: pl.BlockSpec((tm, n), lambda b, j: (b * tiles + j, 0))
    consts = [nw1, wz, wx, wdt, wu, wg, p["conv_w"], p["conv_b"], p["dt_bias"], p["a_log"], p["d_skip_e"],
              p["ssd_norm_w"], p["head_expand"]]
    return pl.pallas_call(
        _inproj_ssd_kernel,
        grid=(nb, tiles),
        in_specs=[row(d), pl.BlockSpec((None, 6, d), lambda b, j: (b, 0, 0))]
                 + [_resident_spec(a.shape) for a in consts],
        out_specs=[row(wu.shape[1]), row(wg.shape[1]), row(d_in),
                   pl.BlockSpec((1, SSD_CONV - 1, cdim), lambda b, j: (b, 0, 0)),
                   pl.BlockSpec((1, SSD_HEADS, SSD_HEAD_DIM, SSD_STATE), lambda b, j: (b, 0, 0, 0))],
        out_shape=[jax.ShapeDtypeStruct((t, wu.shape[1]), F32),
                   jax.ShapeDtypeStruct((t, wg.shape[1]), F32),
                   jax.ShapeDtypeStruct((t, d_in), BF16),
                   jax.ShapeDtypeStruct((nb, SSD_CONV - 1, cdim), F32),
                   jax.ShapeDtypeStruct((nb, SSD_HEADS, SSD_HEAD_DIM, SSD_STATE), F32)],
        scratch_shapes=[pltpu.VMEM((tm, d_in), F32), pltpu.VMEM((tm, cdim), F32), pltpu.VMEM((tm, LANES), F32),
                        pltpu.VMEM((SUBLANES, cdim), F32),
                        pltpu.VMEM((d_in, SSD_STATE), F32)],
        compiler_params=_cparams(("arbitrary", "arbitrary")),
        name="inproj_ssd",
    )(x, mod, *consts)


def _ssd_sample_a_kernel(xbc_ref, conv0_ref, dt_ref, cw_ref, cb_ref, dtb_ref, alog_ref, dsk_ref, e_ref,
                         ydiag_ref, eacs_ref, wt_ref, bs_ref, c8_ref, dec_ref, nconv_ref, *, steps, nseq):
    d_in = SSD_HEADS * SSD_HEAD_DIM
    gw = SSD_STATE
    hpg = SSD_HEADS // SSD_GROUPS
    k1 = SSD_CONV - 1
    slab = lambda ref, i: ref[i * nseq:(i + 1) * nseq, :]
    xp = [slab(conv0_ref, i) for i in range(k1)] + [slab(xbc_ref, i) for i in range(steps)]
    for i in range(k1):
        nconv_ref[i * nseq:(i + 1) * nseq, :] = xp[steps + i]
    e = e_ref[...]
    a = -jnp.exp(alog_ref[...])
    head = lax.broadcasted_iota(jnp.int32, (nseq, LANES), 1)
    xs, bm, cm, dt, a_cs = [], [], [], [], []
    run = jnp.zeros((nseq, LANES), F32)
    for t in range(steps):
        acc = cb_ref[...]
        for k in range(SSD_CONV):
            acc = acc + cw_ref[k:k + 1, :] * xp[t + k]
        v = _silu(acc)
        xs.append(v[:, :d_in])
        bm.append(v[:, d_in:d_in + SSD_GROUPS * gw])
        cm.append(v[:, d_in + SSD_GROUPS * gw:])
        dt.append(_softplus(slab(dt_ref, t) + dtb_ref[...]))
        run = run + dt[t] * a
        a_cs.append(run)
    dec_ref[...] = jnp.exp(a_cs[steps - 1])
    for t in range(steps):
        y = dsk_ref[...] * xs[t]
        for s in range(t + 1):
            cbs = [jnp.sum(cm[t][:, g * gw:(g + 1) * gw] * bm[s][:, g * gw:(g + 1) * gw],
                           axis=-1, keepdims=True) for g in range(SSD_GROUPS)]
            coef = jnp.exp(a_cs[t] - a_cs[s]) * dt[s] * jnp.where(head < hpg, cbs[0], cbs[1])
            y = y + _dot3_lhs(coef, e) * xs[s]
        ydiag_ref[t * nseq:(t + 1) * nseq, :] = y
        eacs_ref[t * nseq:(t + 1) * nseq, :] = _dot3_lhs(jnp.exp(a_cs[t]), e)
        w = xs[t] * _dot3_lhs(jnp.exp(a_cs[steps - 1] - a_cs[t]) * dt[t], e)
        wt_ref[:, t * nseq:(t + 1) * nseq] = w.T.astype(BF16)
        bs_ref[t * nseq:(t + 1) * nseq, :] = bm[t]
        for g in range(SSD_GROUPS):
            j = g * steps + t
            c8_ref[j * nseq:(j + 1) * nseq, :] = cm[t][:, g * gw:(g + 1) * gw]


def _ssd_sample_b_kernel(dec_ref, h0_ref, wt_ref, bs_ref, c8_ref, *rest, steps, nseq, layer):
    yoff_ref, ssm_ref = rest[-2:]
    if layer:
        ssm_ref[0:layer] = rest[0][...]
    hpg = SSD_HEADS // SSD_GROUPS
    grows = hpg * SSD_HEAD_DIM
    gw = SSD_STATE
    rid = lax.broadcasted_iota(jnp.int32, (steps * nseq, gw), 0)
    for s in range(h0_ref.shape[0]):
        b = pl.program_id(0) * h0_ref.shape[0] + s
        h0 = h0_ref[s].reshape(SSD_HEADS * SSD_HEAD_DIM, SSD_STATE)
        c8 = c8_ref[pl.ds(b, SSD_GROUPS * steps, stride=nseq), :]
        yoff_ref[s] = _bdot_nt(c8, h0)
        mine = rid == b
        for t in range(1, steps):
            mine = jnp.logical_or(mine, rid == b + t * nseq)
        for g in range(SSD_GROUPS):
            bsel = jnp.where(mine, bs_ref[:, g * gw:(g + 1) * gw], 0.0).astype(BF16)
            contrib = jnp.dot(wt_ref[g * grows:(g + 1) * grows, :], bsel, preferred_element_type=F32)
            for r in range(hpg):
                h = g * hpg + r
                rows = slice(h * SSD_HEAD_DIM, (h + 1) * SSD_HEAD_DIM)
                ssm_ref[layer, s, h] = (h0[rows, :] * dec_ref[b, h]
                                        + contrib[r * SSD_HEAD_DIM:(r + 1) * SSD_HEAD_DIM, :])


def _ssd_sample_c_kernel(ydiag_ref, yoff_ref, eacs_ref, z_ref, nw_ref, yn_ref, *, steps, nseq):
    d_in = SSD_HEADS * SSD_HEAD_DIM
    half = d_in // SSD_GROUPS
    for t in range(steps):
        lo = yoff_ref[:, t * d_in:t * d_in + half]
        hi = yoff_ref[:, (steps + t) * d_in + half:(steps + t + 1) * d_in]
        rows = slice(t * nseq, (t + 1) * nseq)
        y = ydiag_ref[rows, :] + jnp.concatenate([lo, hi], axis=1) * eacs_ref[rows, :]
        yn_ref[rows, :] = _gated_group_norm(y, z_ref[rows, :], nw_ref[...]).astype(BF16)


def _ssd_sample(xbc, dt_raw, z, conv0_tm, ssm_all, ssm_done, layer, p, nseq, steps):
    t, cdim = xbc.shape
    d_in = z.shape[1]
    k1 = SSD_CONV - 1
    nj = SSD_GROUPS * steps
    a_out = pl.pallas_call(
        functools.partial(_ssd_sample_a_kernel, steps=steps, nseq=nseq),
        out_shape=[jax.ShapeDtypeStruct((t, d_in), F32),
                   jax.ShapeDtypeStruct((t, d_in), F32),
                   jax.ShapeDtypeStruct((d_in, t), BF16),
                   jax.ShapeDtypeStruct((t, SSD_GROUPS * SSD_STATE), F32),
                   jax.ShapeDtypeStruct((nj * nseq, SSD_STATE), F32),
                   jax.ShapeDtypeStruct((nseq, LANES), F32),
                   jax.ShapeDtypeStruct((k1 * nseq, cdim), F32)],
        compiler_params=pltpu.CompilerParams(vmem_limit_bytes=VMEM_LIMIT_BYTES),
        name="ssd_sample_a",
    )(xbc, conv0_tm, dt_raw, p["conv_w"], p["conv_b"], p["dt_bias"], p["a_log"], p["d_skip_e"],
      p["head_expand"])
    ydiag, eacs_e, wt, bs, c8, dec, nconv = a_out
    state_blk = (SSD_HEADS, SSD_HEAD_DIM, SSD_STATE)
    operands = [dec[:, :SSD_HEADS], ssm_all, wt, bs, c8]
    sb = SSD_SAMPLE_SEQS
    in_specs = [pl.BlockSpec(memory_space=pltpu.SMEM),
                pl.BlockSpec((None, sb) + state_blk, lambda b: (layer, b, 0, 0, 0)),
                _const_spec(wt.shape), _const_spec(bs.shape), _const_spec(c8.shape)]
    if layer:
        operands.append(ssm_done)
        in_specs.append(pl.BlockSpec((layer, sb) + state_blk, lambda b: (0, b, 0, 0, 0)))
    yoff, nssm = pl.pallas_call(
        functools.partial(_ssd_sample_b_kernel, steps=steps, nseq=nseq, layer=layer),
        grid=(nseq // sb,),
        in_specs=in_specs,
        out_specs=[pl.BlockSpec((sb, nj, d_in), lambda b: (b, 0, 0)),
                   pl.BlockSpec((layer + 1, sb) + state_blk, lambda b: (0, b, 0, 0, 0))],
        out_shape=[jax.ShapeDtypeStruct((nseq, nj, d_in), F32),
                   jax.ShapeDtypeStruct((layer + 1, nseq) + state_blk, F32)],
        compiler_params=_cparams(("arbitrary",)),
        name="ssd_sample_b",
    )(*operands)
    yn = pl.pallas_call(
        functools.partial(_ssd_sample_c_kernel, steps=steps, nseq=nseq),
        out_shape=jax.ShapeDtypeStruct((t, d_in), BF16),
        compiler_params=pltpu.CompilerParams(vmem_limit_bytes=VMEM_LIMIT_BYTES),
        name="ssd_sample_c",
    )(ydiag, yoff.reshape(nseq, nj * d_in), eacs_e, z, p["ssd_norm_w"])
    return yn, nconv, nssm


def _s5_param_kernel(are_ref, aim_ref, ldt_ref, bre_ref, bim_ref, cre_ref, cim_ref,
                     ab_ref, wb_ref, cret_ref, cimt_ref):
    n = are_ref.shape[1]
    w = wb_ref.shape[0]
    a_re = are_ref[...]
    a_im = aim_ref[...]
    dt = jnp.exp(ldt_ref[...])
    mag = jnp.exp(dt * a_re)
    ab_re = mag * jnp.cos(dt * a_im)
    ab_im = mag * jnp.sin(dt * a_im)
    den = a_re * a_re + a_im * a_im
    f_re = ((ab_re - 1.0) * a_re + ab_im * a_im) / den
    f_im = (ab_im * a_re - (ab_re - 1.0) * a_im) / den
    ab_ref[0:1, :] = ab_re
    ab_ref[1:2, :] = ab_im
    b_re = bre_ref[...]
    b_im = bim_ref[...]
    row_g = lax.shift_right_logical(lax.broadcasted_iota(jnp.int32, (w, n), 0), S5_GROUP_CH.bit_length() - 1)
    col_g = lax.shift_right_logical(lax.broadcasted_iota(jnp.int32, (w, n), 1), S5_STATE.bit_length() - 1)
    same = row_g == col_g

    def spread(rows):
        return jnp.where(same, jnp.concatenate([rows] * S5_GROUPS, axis=0), 0.0).astype(BF16)

    wb_ref[:, 0:n] = spread(f_re * b_re - f_im * b_im)
    wb_ref[:, n:2 * n] = spread(f_re * b_im + f_im * b_re)
    cret_ref[...] = spread(cre_ref[...])
    cimt_ref[...] = spread(cim_ref[...])


def _s5_params(a_re, a_im, ldt, bt_re, bt_im, ct_re, ct_im):
    n = a_re.shape[1]
    w = S5_GROUPS * S5_GROUP_CH
    return pl.pallas_call(
        _s5_param_kernel,
        out_shape=[jax.ShapeDtypeStruct((2, n), F32),
                   jax.ShapeDtypeStruct((w, 2 * n), BF16),
                   jax.ShapeDtypeStruct((w, n), BF16),
                   jax.ShapeDtypeStruct((w, n), BF16)],
        compiler_params=pltpu.CompilerParams(vmem_limit_bytes=VMEM_LIMIT_BYTES),
        name="s5_params",
    )(a_re, a_im, ldt, bt_re, bt_im, ct_re, ct_im)


def _s5_kernel(u_ref, h0re_ref, h0im_ref, ab_ref, wb_ref, cret_ref, cimt_ref, d_ref, wglu_ref,
               yb_ref, nre_ref, nim_ref, xh_ref, st_ref, *, rows_per_step, steps, batch_major):
    i = pl.program_id(0)
    n = ab_ref.shape[1]
    w = wb_ref.shape[0]
    r = rows_per_step

    @pl.when(i == 0)
    def _():
        st_ref[0] = h0re_ref[...]
        st_ref[1] = h0im_ref[...]

    if batch_major:
        u = pltpu.einshape("btd->(tb)d", u_ref[...])
    else:
        u = u_ref[...]
    ub = u.astype(BF16)
    for j in range(2 * n // MXU_TILE):
        c0 = (j * MXU_TILE) % n
        k0 = (c0 // S5_STATE * S5_GROUP_CH) // LANES * LANES
        cols = slice(j * MXU_TILE, (j + 1) * MXU_TILE)
        xh_ref[:, cols] = jnp.dot(ub[:, k0:k0 + LANES], wb_ref[k0:k0 + LANES, cols], preferred_element_type=F32)
    for c0 in range(0, n, S5_SCAN_LANES):
        re = slice(c0, c0 + S5_SCAN_LANES)
        im = slice(n + c0, n + c0 + S5_SCAN_LANES)
        a_re = jnp.broadcast_to(ab_ref[0:1, re], (r, S5_SCAN_LANES))
        a_im = jnp.broadcast_to(ab_ref[1:2, re], (r, S5_SCAN_LANES))

        def step(t, carry, re=re, im=im, a_re=a_re, a_im=a_im):
            h_re, h_im = carry
            rows = pl.ds(pl.multiple_of(t * r, r), r)
            n_re = a_re * h_re - a_im * h_im + xh_ref[rows, re]
            n_im = a_re * h_im + a_im * h_re + xh_ref[rows, im]
            xh_ref[rows, re] = n_re
            xh_ref[rows, im] = n_im
            return n_re, n_im

        carry = (st_ref[0, :, re], st_ref[1, :, re])
        if steps <= 8:
            for t in range(steps):
                carry = step(t, carry)
        else:
            carry = lax.fori_loop(0, steps, step, carry, unroll=4)
        st_ref[0, :, re] = carry[0]
        st_ref[1, :, re] = carry[1]

    n_ct = w // MXU_TILE
    ys = []
    for j in range(n_ct):
        rows = slice(j * MXU_TILE, (j + 1) * MXU_TILE)
        ks = slice(j * (n // n_ct), (j + 1) * (n // n_ct))
        ks_im = slice(n + j * (n // n_ct), n + (j + 1) * (n // n_ct))
        ys.append(_bdot_nt(xh_ref[:, ks], cret_ref[rows, ks]) - _bdot_nt(xh_ref[:, ks_im], cimt_ref[rows, ks]))
    y = jnp.concatenate(ys, axis=1) + d_ref[...] * u
    pre = _bdot(_gelu_tanh(y), wglu_ref[...])
    half = pre.shape[1] // 2
    yb = pre[:, :half] * _sigmoid(pre[:, half:])
    if batch_major:
        yb_ref[...] = pltpu.einshape("(tb)d->btd", yb, b=r)
    else:
        yb_ref[...] = yb

    @pl.when(i == pl.num_programs(0) - 1)
    def _():
        nre_ref[...] = st_ref[0]
        nim_ref[...] = st_ref[1]


def _s5(u, h0_re, h0_im, sp, rows_per_step, steps, batch_major):
    n = sp["ab"].shape[1]
    w = u.shape[-1]
    tm = rows_per_step * steps
    d_model = sp["w_glu"].shape[1] // 2
    if batch_major:
        nb, seq, _ = u.shape
        grid = (seq // steps,)
        u_spec = pl.BlockSpec((nb, steps, w), lambda i: (0, i, 0))
        yb_spec = pl.BlockSpec((nb, steps, d_model), lambda i: (0, i, 0))
        yb_shape = (nb, seq, d_model)
    else:
        grid = (u.shape[0] // tm,)
        u_spec = pl.BlockSpec((tm, w), lambda i: (i, 0))
        yb_spec = pl.BlockSpec((tm, d_model), lambda i: (i, 0))
        yb_shape = (u.shape[0], d_model)
    return pl.pallas_call(
        functools.partial(_s5_kernel, rows_per_step=rows_per_step, steps=steps, batch_major=batch_major),
        grid=grid,
        in_specs=[u_spec,
                  _const_spec(h0_re.shape), _const_spec(h0_im.shape), _const_spec(sp["ab"].shape),
                  _const_spec(sp["wb"].shape), _const_spec(sp["c_re_t"].shape), _const_spec(sp["c_im_t"].shape),
                  _const_spec((1, w)), _const_spec(sp["w_glu"].shape)],
        out_specs=[yb_spec, _const_spec(h0_re.shape), _const_spec(h0_im.shape)],
        out_shape=[jax.ShapeDtypeStruct(yb_shape, F32),
                   jax.ShapeDtypeStruct(h0_re.shape, F32),
                   jax.ShapeDtypeStruct(h0_im.shape, F32)],
        scratch_shapes=[pltpu.VMEM((tm, 2 * n), F32),
                        pltpu.VMEM((2, rows_per_step, n), F32)],
        compiler_params=_cparams(("arbitrary",)),
        name="s5_scan",
    )(u, h0_re, h0_im, sp["ab"], sp["wb"], sp["c_re_t"], sp["c_im_t"], sp["d"], sp["w_glu"])


def _merge_router_steps(i, row_refs, mod_ref, weight_refs, cnt_ref, run_ref, *, per_token, rows_per_seq):
    @pl.when(i == 0)
    def _():
        run_ref[...] = jnp.zeros(run_ref.shape, F32)

    lead = row_refs[3].shape[0]
    rows_each = 1 if rows_per_seq is None else rows_per_seq
    sub = lead // ROUTER_SUBTILES if (lead * rows_each) % (ROUTER_SUBTILES * LANES) == 0 else lead
    for r0 in range(0, lead, sub):
        rows = pl.ds(r0, sub)
        if per_token:
            mod = mod_ref.at[:, rows]
        elif rows_per_seq is not None:
            mod = mod_ref.at[rows]
        else:
            mod = mod_ref
        _route_rows(*[ref.at[rows] for ref in row_refs], mod, *weight_refs, run_ref,
                    per_token=per_token, rows_per_seq=rows_per_seq)

    @pl.when(i == pl.num_programs(0) - 1)
    def _():
        cnt_ref[...] = run_ref[...]


def _merge_router_kernel(yn_ref, yb_ref, gate_ref, x_ref, mod_ref, woa_ref, wo_ref, nw_ref, wr_ref, br_ref,
                         x1_ref, h2_ref, meta_ref, cnt_ref, run_ref, *, per_token):
    _merge_router_steps(pl.program_id(0), (yn_ref, yb_ref, gate_ref, x_ref, x1_ref, h2_ref, meta_ref), mod_ref,
                        (woa_ref, wo_ref, nw_ref, wr_ref, br_ref), cnt_ref, run_ref,
                        per_token=per_token, rows_per_seq=None)


def _rows2d(ref):
    v = ref[...]
    return v.reshape(-1, v.shape[-1]) if v.ndim == 3 else v


def _route_rows(yn_ref, yb_ref, gate_ref, x_ref, x1_ref, h2_ref, meta_ref, mod_ref,
                woa_ref, wo_ref, nw_ref, wr_ref, br_ref, run_ref, *, per_token, rows_per_seq):
    x = _rows2d(x_ref)
    tm, d = x.shape
    ya = jnp.dot(_rows2d(yn_ref), woa_ref[...], preferred_element_type=F32)
    gate = _sigmoid(_rows2d(gate_ref))
    merged = gate[:, :d] * ya + gate[:, d:] * _rows2d(yb_ref)
    mix = _bdot(merged, wo_ref[...])
    g1 = _mod_rows(mod_ref, 2, per_token, rows_per_seq)
    sh2 = _mod_rows(mod_ref, 3, per_token, rows_per_seq)
    sc2 = _mod_rows(mod_ref, 4, per_token, rows_per_seq)
    x1 = x + g1 * mix
    x1_ref[...] = x1.reshape(x1_ref.shape)
    h2 = _rms(x1) * nw_ref[...] * (1.0 + sc2) + sh2
    h2_ref[...] = h2.reshape(tm, d // LANES, LANES).reshape(h2_ref.shape)

    wr = wr_ref[...]
    wr_hi = wr.astype(BF16)
    wr_lo = (wr - wr_hi.astype(F32)).astype(BF16)
    h2_hi = h2.astype(BF16)
    h2_lo = (h2 - h2_hi.astype(F32)).astype(BF16)
    d = functools.partial(jnp.dot, preferred_element_type=F32)
    logits = d(h2_hi, wr_hi) + d(h2_lo, wr_hi) + d(h2_hi, wr_lo)
    lt = logits.T
    erow = lax.broadcasted_iota(jnp.int32, lt.shape, 0)
    lt = jnp.where(erow < N_EXPERTS, lt, -jnp.inf)
    ex = jnp.exp(lt - jnp.max(lt, axis=0, keepdims=True))
    scores = ex / jnp.sum(ex, axis=0, keepdims=True)
    sel = scores + br_ref[...]
    s = [sel[e:e + 1, :] for e in range(N_EXPERTS)]
    p = [scores[e:e + 1, :] for e in range(N_EXPERTS)]

    def group_top2_sum(vals):
        best = None
        for a in range(len(vals)):
            for b in range(a + 1, len(vals)):
                pair = vals[a] + vals[b]
                best = pair if best is None else jnp.maximum(best, pair)
        return best

    gs = [group_top2_sum(s[EXPERTS_PER_GROUP * g:EXPERTS_PER_GROUP * (g + 1)]) for g in range(N_EXPERT_GROUPS)]
    best = gs[0]
    bg = jnp.zeros(best.shape, jnp.int32)
    for g in range(1, N_EXPERT_GROUPS):
        better = gs[g] > best
        bg = jnp.where(better, g, bg)
        best = jnp.where(better, gs[g], best)

    def pick(rows, j):
        out = rows[j]
        for g in range(1, N_EXPERT_GROUPS):
            out = jnp.where(bg == g, rows[EXPERTS_PER_GROUP * g + j], out)
        return out

    cs = [pick(s, j) for j in range(EXPERTS_PER_GROUP)]
    cp = [pick(p, j) for j in range(EXPERTS_PER_GROUP)]

    def argmax_first(vals, skip):
        bv = None
        bi = None
        bw = None
        for j in range(len(vals)):
            v = vals[j] if skip is None else jnp.where(skip == j, -jnp.inf, vals[j])
            if bv is None:
                bv, bi, bw = v, jnp.zeros(v.shape, jnp.int32), cp[j]
            else:
                better = v > bv
                bi = jnp.where(better, j, bi)
                bw = jnp.where(better, cp[j], bw)
                bv = jnp.where(better, v, bv)
        return bi, bw

    i1, w1 = argmax_first(cs, None)
    i2, w2 = argmax_first(cs, i1)
    wsum = w1 + w2
    e1 = bg * EXPERTS_PER_GROUP + i1
    e2 = bg * EXPERTS_PER_GROUP + i2
    oh1 = jnp.where(erow == e1, 1.0, 0.0)
    oh2 = jnp.where(erow == e2, 1.0, 0.0)
    both = oh1 + oh2
    ta = lax.broadcasted_iota(jnp.int32, (tm, tm), 0)
    tb = lax.broadcasted_iota(jnp.int32, (tm, tm), 1)
    earlier = jnp.where(ta < tb, 1.0, 0.0).astype(BF16)
    before = jnp.dot(both.astype(BF16), earlier, preferred_element_type=F32) + run_ref[...]
    r1 = jnp.sum(oh1 * before, axis=0, keepdims=True)
    r2 = jnp.sum(oh2 * before, axis=0, keepdims=True)
    run_ref[...] = run_ref[...] + jnp.sum(both, axis=1, keepdims=True)

    record = [w1 / wsum, w2 / wsum, e1.astype(F32), e2.astype(F32), r1, r2]
    wrow = lax.broadcasted_iota(jnp.int32, lt.shape, 0)
    wmat = jnp.zeros(lt.shape, F32)
    for k, v in enumerate(record):
        wmat = jnp.where(wrow == k, v, wmat)
    meta_ref[...] = wmat.T.reshape(meta_ref.shape)


def _merge_router(yn, yb, gate, x, mod, lw, rw, *, tm, per_token, mod_spec):
    t, d = x.shape
    row = lambda n: pl.BlockSpec((tm, n), lambda i: (i, 0))
    return pl.pallas_call(
        functools.partial(_merge_router_kernel, per_token=per_token),
        grid=(t // tm,),
        in_specs=[row(d), row(d), row(2 * d), row(d), mod_spec,
                  _const_spec((d, d)), _const_spec((d, d)), _const_spec((1, d)),
                  _const_spec((d, LANES)), _const_spec((LANES, 1))],
        out_specs=[row(d), _row_tiles(tm, d), row(LANES), _const_spec((LANES, 1))],
        out_shape=[jax.ShapeDtypeStruct((t, d), F32),
                   jax.ShapeDtypeStruct((t, d // LANES, LANES), F32),
                   jax.ShapeDtypeStruct((t, LANES), F32),
                   jax.ShapeDtypeStruct((LANES, 1), F32)],
        scratch_shapes=[pltpu.VMEM((LANES, 1), F32)],
        compiler_params=_cparams(("arbitrary",)),
        name="merge_router",
    )(yn, yb, gate, x, mod, lw["w_out_a"], lw["w_out"], lw["norm2_w"], rw["w_router"], rw["b_router"])


def _s5_merge_kernel(u_ref, h0re_ref, h0im_ref, ab_ref, wb_ref, cret_ref, cimt_ref, d_ref, wglu_ref,
                     yn_ref, gate_ref, x_ref, mod_ref, woa_ref, wo_ref, nw_ref, wr_ref, br_ref,
                     nre_ref, nim_ref, x1_ref, h2_ref, meta_ref, cnt_ref,
                     xh_ref, st_ref, yb_ref, run_ref, *, steps):
    _s5_kernel(u_ref, h0re_ref, h0im_ref, ab_ref, wb_ref, cret_ref, cimt_ref, d_ref, wglu_ref,
               yb_ref, nre_ref, nim_ref, xh_ref, st_ref, rows_per_step=u_ref.shape[0], steps=steps, batch_major=True)
    _merge_router_steps(pl.program_id(0), (yn_ref, yb_ref, gate_ref, x_ref, x1_ref, h2_ref, meta_ref), mod_ref,
                        (woa_ref, wo_ref, nw_ref, wr_ref, br_ref), cnt_ref, run_ref,
                        per_token=False, rows_per_seq=steps)


def _resident_spec(shape):
    nd = len(shape)
    return pl.BlockSpec(shape, lambda *_: (0,) * nd, pipeline_mode=pl.Buffered(1))


def _s5_merge(u, yn, gate, x, mod, zero_state, sp, lw, rw, steps):
    nb, seq, w = u.shape
    d = x.shape[-1]
    n = sp["ab"].shape[1]
    blk = lambda *tail: pl.BlockSpec((nb, steps) + tail, lambda i: (0, i) + (0,) * len(tail))
    weights = [sp["ab"], sp["wb"], sp["c_re_t"], sp["c_im_t"], sp["d"], sp["w_glu"]]
    merge_w = [lw["w_out_a"], lw["w_out"], lw["norm2_w"], rw["w_router"], rw["b_router"]]
    return pl.pallas_call(
        functools.partial(_s5_merge_kernel, steps=steps),
        grid=(seq // steps,),
        in_specs=[blk(w), _resident_spec(zero_state.shape), _resident_spec(zero_state.shape)]
                 + [_resident_spec(a.shape) for a in weights]
                 + [blk(d), blk(2 * d), blk(d), _resident_spec(mod.shape)]
                 + [_resident_spec(a.shape) for a in merge_w],
        out_specs=[_const_spec(zero_state.shape), _const_spec(zero_state.shape),
                   blk(d), blk(d // LANES, LANES), blk(LANES), _const_spec((LANES, 1))],
        out_shape=[jax.ShapeDtypeStruct(zero_state.shape, F32),
                   jax.ShapeDtypeStruct(zero_state.shape, F32),
                   jax.ShapeDtypeStruct((nb, seq, d), F32),
                   jax.ShapeDtypeStruct((nb, seq, d // LANES, LANES), F32),
                   jax.ShapeDtypeStruct((nb, seq, LANES), F32),
                   jax.ShapeDtypeStruct((LANES, 1), F32)],
        scratch_shapes=[pltpu.VMEM((nb * steps, 2 * n), F32), pltpu.VMEM((2, nb, n), F32),
                        pltpu.VMEM((nb, steps, d), F32), pltpu.VMEM((LANES, 1), F32)],
        compiler_params=_cparams(("arbitrary",)),
        name="s5_merge_router",
    )(u, zero_state, zero_state, *weights, yn, gate, x, mod, *merge_w)


def _meta_rows_kernel(meta_ref, idx_ref, rank_ref):
    m = meta_ref[...].T
    idx_ref[...] = m[2:4, :].astype(jnp.int32)
    rank_ref[...] = m[4:6, :].astype(jnp.int32)


def _meta_rows(meta):
    t = meta.shape[0]
    tm = min(t, ROW_TILE)
    pair = pl.BlockSpec((2, tm), lambda i: (0, i))
    return pl.pallas_call(
        _meta_rows_kernel,
        grid=(t // tm,),
        in_specs=[pl.BlockSpec((tm, LANES), lambda i: (i, 0))],
        out_specs=[pair, pair],
        out_shape=[jax.ShapeDtypeStruct((2, t), jnp.int32), jax.ShapeDtypeStruct((2, t), jnp.int32)],
        compiler_params=_cparams(("arbitrary",)),
        name="moe_meta_rows",
    )(meta)


def _plan_kernel(cnt_ref, idx_ref, rank_ref, dest_ref, te_ref, ends_ref, nv_ref, *, tile, n_tiles):
    shift = tile.bit_length() - 1
    run = jnp.int32(0)
    starts = []
    for e in range(N_EXPERTS):
        starts.append(run)
        run = run + (((cnt_ref[e] + (tile - 1)) >> shift) << shift)
        ends_ref[e] = run
    n_valid = run >> shift
    nv_ref[0] = n_valid
    idx = idx_ref[...]
    dest = rank_ref[...]
    for e in range(N_EXPERTS):
        dest = dest + jnp.where(idx == e, starts[e], 0)
    dest_ref[...] = dest

    def tile_owner(j, carry):
        pos = jnp.minimum(j, n_valid - 1) * tile
        owner = jnp.int32(0)
        for e in range(N_EXPERTS - 1):
            owner = owner + jnp.where(ends_ref[e] <= pos, 1, 0)
        te_ref[j] = owner
        return carry

    lax.fori_loop(0, n_tiles, tile_owner, 0)


def _plan(cnt, idx, rank, tile, n_tiles):
    smem = pl.BlockSpec(memory_space=pltpu.SMEM)
    vmem = pl.BlockSpec(memory_space=pltpu.VMEM)
    return pl.pallas_call(
        functools.partial(_plan_kernel, tile=tile, n_tiles=n_tiles),
        in_specs=[smem, vmem, vmem],
        out_specs=[vmem, smem, smem, smem],
        out_shape=[jax.ShapeDtypeStruct(idx.shape, jnp.int32),
                   jax.ShapeDtypeStruct((n_tiles,), jnp.int32),
                   jax.ShapeDtypeStruct((N_EXPERTS,), jnp.int32),
                   jax.ShapeDtypeStruct((1,), jnp.int32)],
        name="moe_plan",
    )(cnt, idx, rank)


def _row_tiles(tm, d):
    return pl.BlockSpec((tm, d // LANES, LANES), lambda i: (i, 0, 0))


def _row_copy(src_ref, src_row, dst_ref, dst_row, sem):
    return pltpu.make_async_copy(src_ref.at[src_row], dst_ref.at[dst_row], sem)


def _each_row(tm, fn):
    def body(t, carry):
        for k in range(2):
            fn(t, k)
        return carry

    lax.fori_loop(0, tm, body, 0, unroll=8)


def _dispatch_kernel(ends_ref, dest_ref, prev_dest_ref, h_ref, out_ref, zero_ref, stage_ref, sem, zsem, *, tile):
    i = pl.program_id(0)
    tm = h_ref.shape[0]
    slot = i % 2

    @pl.when(i == 0)
    def _():
        zero_ref[...] = jnp.zeros(zero_ref.shape, F32)

        def clear(start):
            return pltpu.make_async_copy(zero_ref, out_ref.at[pl.ds(pl.multiple_of(start, tile), tile)], zsem)

        def used(e):
            return ends_ref[e] > (ends_ref[e - 1] if e else 0)

        def unused_tiles(fn):
            n_valid = ends_ref[N_EXPERTS - 1] >> (tile.bit_length() - 1)
            lax.fori_loop(n_valid, out_ref.shape[0] // tile, lambda j, c: (fn(clear(j * tile)), c)[1], 0)

        for e in range(N_EXPERTS):
            pl.when(used(e))(lambda e=e: clear(ends_ref[e] - tile).start())
        unused_tiles(lambda cp: cp.start())
        for e in range(N_EXPERTS):
            pl.when(used(e))(lambda e=e: clear(ends_ref[e] - tile).wait())
        unused_tiles(lambda cp: cp.wait())

    stage_ref[slot] = h_ref[...]
    _each_row(tm, lambda t, k: _row_copy(stage_ref.at[slot], t, out_ref, dest_ref[k, t], sem.at[slot]).start(priority=k))

    @pl.when(i > 0)
    def _():
        _each_row(tm, lambda t, k: _row_copy(stage_ref.at[1 - slot], t, out_ref, prev_dest_ref[k, t],
                                             sem.at[1 - slot]).wait())

    @pl.when(i == pl.num_programs(0) - 1)
    def _():
        _each_row(tm, lambda t, k: _row_copy(stage_ref.at[slot], t, out_ref, dest_ref[k, t], sem.at[slot]).wait())


def _dispatch(ends, dest, h2, n_rows, tm, tile):
    t, nt, lanes = h2.shape
    idx_spec = lambda fn: pl.BlockSpec((2, tm), fn, memory_space=pltpu.SMEM)
    return pl.pallas_call(
        functools.partial(_dispatch_kernel, tile=tile),
        grid=(t // tm,),
        in_specs=[pl.BlockSpec(memory_space=pltpu.SMEM),
                  idx_spec(lambda i: (0, i)), idx_spec(lambda i: (0, jnp.maximum(i - 1, 0))),
                  _row_tiles(tm, nt * lanes)],
        out_specs=pl.BlockSpec(memory_space=pl.ANY),
        out_shape=jax.ShapeDtypeStruct((n_rows, nt, lanes), F32),
        scratch_shapes=[pltpu.VMEM((tile, nt, lanes), F32), pltpu.VMEM((2, tm, nt, lanes), F32),
                        pltpu.SemaphoreType.DMA((2,)), pltpu.SemaphoreType.DMA(())],
        compiler_params=_cparams(("arbitrary",)),
        name="moe_dispatch",
    )(ends, dest, dest, h2)


def _expert_kernel(te_ref, nv_ref, x_ref, w1_ref, w3_ref, w2_ref, y_ref, w1b_ref, w3b_ref, w2b_ref):
    j = pl.program_id(0)
    fresh = jnp.logical_or(j == 0, te_ref[j] != te_ref[jnp.maximum(j - 1, 0)])

    @pl.when(fresh)
    def _():
        w1b_ref[...] = w1_ref[...].astype(BF16)
        w3b_ref[...] = w3_ref[...].astype(BF16)
        w2b_ref[...] = w2_ref[...].astype(BF16)

    @pl.when(j < nv_ref[0])
    def _():
        tm, nt, lanes = x_ref.shape
        xb = x_ref[...].reshape(tm, nt * lanes).astype(BF16)
        a = jnp.dot(xb, w1b_ref[...], preferred_element_type=F32)
        b = jnp.dot(xb, w3b_ref[...], preferred_element_type=F32)
        y_ref[...] = _bdot(_silu(a) * b, w2b_ref[...]).reshape(tm, nt, lanes)

    @pl.when(j >= nv_ref[0])
    def _():
        y_ref[...] = jnp.zeros(y_ref.shape, F32)


def _experts(tile_expert, n_valid, xs, w1, w3, w2, layer, tm):
    n_rows, nt, lanes = xs.shape
    d = nt * lanes
    de = w1.shape[3]
    wspec = lambda r, c: pl.BlockSpec((None, None, r, c), lambda j, te, nv: (layer, te[j], 0, 0))
    grid_spec = pltpu.PrefetchScalarGridSpec(
        num_scalar_prefetch=2,
        grid=(n_rows // tm,),
        in_specs=[pl.BlockSpec((tm, nt, lanes), lambda j, te, nv: (jnp.minimum(j, nv[0] - 1), 0, 0)),
                  wspec(d, de), wspec(d, de), wspec(de, d)],
        out_specs=pl.BlockSpec((tm, nt, lanes), lambda j, te, nv: (j, 0, 0)),
        scratch_shapes=[pltpu.VMEM((d, de), BF16), pltpu.VMEM((d, de), BF16), pltpu.VMEM((de, d), BF16)],
    )
    return pl.pallas_call(
        _expert_kernel,
        grid_spec=grid_spec,
        out_shape=jax.ShapeDtypeStruct((n_rows, nt, lanes), F32),
        compiler_params=_cparams(("arbitrary",)),
        name="moe_experts",
    )(tile_expert, n_valid, xs, w1, w3, w2)


def _combine_kernel(dest_ref, next_dest_ref, ys_ref, x1_ref, wts_ref, mod_ref, fw_ref, out_ref, buf_ref, sem,
                    *, per_token, final_norm):
    i = pl.program_id(0)
    tm, d = x1_ref.shape
    slot = i % 2

    def gather(idx_ref, s):
        return lambda t, k: _row_copy(ys_ref, idx_ref[k, t], buf_ref.at[s, k], t, sem.at[s])

    @pl.when(i == 0)
    def _():
        _each_row(tm, lambda t, k: gather(dest_ref, slot)(t, k).start(priority=k))

    @pl.when(i + 1 < pl.num_programs(0))
    def _():
        _each_row(tm, lambda t, k: gather(next_dest_ref, 1 - slot)(t, k).start(priority=k))

    _each_row(tm, lambda t, k: gather(dest_ref, slot)(t, k).wait())
    w = wts_ref[...]
    rows = lambda k: buf_ref[slot, k].reshape(tm, d)
    moe = w[:, 0:1] * rows(0) + w[:, 1:2] * rows(1)
    x2 = x1_ref[...] + _mod_rows(mod_ref, 5, per_token) * moe
    if final_norm:
        x2 = _rms(x2) * fw_ref[...]
    out_ref[...] = x2


def _combine(dest, ys, x1, wts, mod, fw, *, tm, per_token, mod_spec, final_norm):
    t, d = x1.shape
    row = lambda n: pl.BlockSpec((tm, n), lambda i: (i, 0))
    n_steps = t // tm
    idx_spec = lambda fn: pl.BlockSpec((2, tm), fn, memory_space=pltpu.SMEM)
    return pl.pallas_call(
        functools.partial(_combine_kernel, per_token=per_token, final_norm=final_norm),
        grid=(n_steps,),
        in_specs=[idx_spec(lambda i: (0, i)), idx_spec(lambda i: (0, jnp.minimum(i + 1, n_steps - 1))),
                  pl.BlockSpec(memory_space=pl.ANY),
                  row(d), row(LANES), mod_spec, _const_spec((1, d))],
        out_specs=row(d),
        out_shape=jax.ShapeDtypeStruct((t, d), F32),
        scratch_shapes=[pltpu.VMEM((2, 2, tm) + ys.shape[1:], F32), pltpu.SemaphoreType.DMA((2,))],
        compiler_params=_cparams(("arbitrary",)),
        name="moe_combine",
    )(dest, dest, ys, x1, wts, mod, fw)


def _moe(h2, idx, rank, counts, x1, wts, mod, fw, ew, layer, *, tok_tile, expert_tile, per_token, mod_spec_fn,
         final_norm):
    t = h2.shape[0]
    tm = expert_tile
    n_rows = ((2 * t + N_EXPERTS * (tm - 1)) // tm + 1) * tm
    cnt = counts[:N_EXPERTS, 0].astype(jnp.int32)
    dest, tile_expert, ends, n_valid = _plan(cnt, idx, rank, tm, n_rows // tm)
    xs = _dispatch(ends, dest, h2, n_rows, tok_tile, tm)
    ys = _experts(tile_expert, n_valid, xs, ew["w1"], ew["w3"], ew["w2"], layer, tm)
    return _combine(dest, ys, x1, wts, mod, fw, tm=tok_tile, per_token=per_token,
                    mod_spec=mod_spec_fn(tok_tile), final_norm=final_norm)


def _pad_lanes(a, n=LANES):
    return jnp.pad(a, [(0, 0)] * (a.ndim - 1) + [(0, n - a.shape[-1])])


def kernel(x_prompt, x_sample, state_ssm, state_conv, state_s5_re, state_s5_im, c_prompt, c_sample, w_in, conv_w, conv_b, dt_bias, a_log, d_ssd, ssd_norm_w, w_out_a, s5_a_re, s5_a_im, s5_log_dt, s5_b_re, s5_b_im, s5_c_re, s5_c_im, s5_d, w_glu, w_out, norm1_w, norm2_w, w_ada, b_ada, w_router, b_router, w1, w3, w2, final_norm_w):
    nb, seq, d = x_prompt.shape
    ns, steps, _ = x_sample.shape
    depth = w_in.shape[0]
    d_in = SSD_HEADS * SSD_HEAD_DIM
    cdim = conv_w.shape[2]
    s5w = S5_GROUPS * S5_GROUP_CH
    s5n = S5_GROUPS * S5_STATE
    k1 = SSD_CONV - 1
    tp = nb * seq
    ts = ns * steps

    mod = _ada_mod(jnp.concatenate([c_prompt, c_sample], axis=0), w_ada, b_ada)
    head_expand = jnp.tile(jnp.repeat(jnp.eye(LANES, SSD_HEADS, dtype=BF16), SSD_HEAD_DIM, axis=1), (3, 1))
    rw = {"w_router": _pad_lanes(w_router), "b_router": _pad_lanes(b_router[None, :]).reshape(LANES, 1)}

    xp = x_prompt.reshape(tp, d)
    xsm = jnp.transpose(x_sample, (1, 0, 2)).reshape(ts, d)
    outs = {k: [] for k in ("ssm_p", "conv_p", "conv_s", "re_p", "re_s", "im_p", "im_s")}
    ssm_s = None
    zero_state = jnp.zeros((nb, s5n), F32)

    for l in range(depth):
        o0 = 0
        o1 = d_in
        o2 = o1 + cdim
        o3 = o2 + SSD_HEADS
        o4 = o3 + s5w
        wl = w_in[l].astype(BF16)
        ws = (wl[:, o0:o1], wl[:, o1:o2], _pad_lanes(wl[:, o2:o3]), wl[:, o3:o4], wl[:, o4:])
        p = {"conv_w": conv_w[l], "conv_b": conv_b[l][None, :], "dt_bias": _pad_lanes(dt_bias[l][None, :]),
             "a_log": _pad_lanes(a_log[l][None, :]),
             "d_skip_e": jnp.repeat(d_ssd[l], SSD_HEAD_DIM)[None, :],
             "ssd_norm_w": ssd_norm_w[l][None, :], "head_expand": head_expand}
        chan_rows = lambda a, perm: jnp.transpose(a, perm).reshape(S5_GROUP_CH, s5n)
        ab, wb, c_re_t, c_im_t = _s5_params(
            s5_a_re[l].reshape(1, s5n), s5_a_im[l].reshape(1, s5n), jnp.repeat(s5_log_dt[l], S5_STATE)[None, :],
            chan_rows(s5_b_re[l], (2, 0, 1)), chan_rows(s5_b_im[l], (2, 0, 1)),
            chan_rows(s5_c_re[l], (1, 0, 2)), chan_rows(s5_c_im[l], (1, 0, 2)))
        sp = {"ab": ab, "wb": wb, "c_re_t": c_re_t, "c_im_t": c_im_t,
              "d": s5_d[l][None, :], "w_glu": w_glu[l].astype(BF16)}
        lw = {"w_out_a": w_out_a[l].astype(BF16), "w_out": w_out[l].astype(BF16), "norm2_w": norm2_w[l][None, :]}
        ew = {"w1": w1, "w3": w3, "w2": w2}
        final = l == depth - 1
        fw = final_norm_w[None, :]

        mod_p = mod[l, :nb].reshape(nb, 6, d)
        mod_p_spec = lambda tm: pl.BlockSpec((None, 6, d), lambda i: (i // (seq // tm), 0, 0))
        u, gate, yn, nconv, nssm = _inproj_ssd(xp, mod_p, norm1_w[l][None, :], ws, p, nb, seq, ROW_TILE)
        seq3 = lambda a: a.reshape((nb, seq) + a.shape[1:])
        nre, nim, x1, h2, meta, counts = _s5_merge(seq3(u), seq3(yn), seq3(gate), seq3(xp), mod_p, zero_state,
                                                   sp, lw, rw, S5_STEPS)
        meta = meta.reshape(tp, LANES)
        idx, rank = _meta_rows(meta)
        xp = _moe(h2.reshape(tp, d // LANES, LANES), idx, rank, counts, x1.reshape(tp, d), meta, mod_p, fw, ew, l,
                  tok_tile=MOE_TILE, expert_tile=EXPERT_TILE, per_token=False, mod_spec_fn=mod_p_spec,
                  final_norm=final)
        outs["ssm_p"].append(nssm)
        outs["conv_p"].append(nconv)
        outs["re_p"].append(nre.reshape(nb, S5_GROUPS, S5_STATE))
        outs["im_p"].append(nim.reshape(nb, S5_GROUPS, S5_STATE))

        mod_s = jnp.transpose(mod[l, nb:].reshape(ns, 6, d), (1, 0, 2))
        mod_s_spec = lambda tm: pl.BlockSpec((6, tm, d), lambda i: (0, i % (ns // tm), 0))
        z, xbc, dtr, u, gate = _inproj(
            xsm, mod_s, norm1_w[l][None, :], ws, tm=ns, per_token=True, mod_spec=mod_s_spec(ns))
        conv0_tm = jnp.transpose(state_conv[l], (1, 0, 2)).reshape(k1 * ns, cdim)
        yn, nconv_tm, ssm_s = _ssd_sample(xbc, dtr, z, conv0_tm, state_ssm, ssm_s, l, p, ns, steps)
        yb, nre, nim = _s5(u, state_s5_re[l].reshape(ns, s5n), state_s5_im[l].reshape(ns, s5n), sp, ns, steps,
                           False)
        x1, h2, meta, counts = _merge_router(
            yn, yb, gate, xsm, mod_s, lw, rw, tm=ns, per_token=True, mod_spec=mod_s_spec(ns))
        idx, rank = _meta_rows(meta)
        xsm = _moe(h2, idx, rank, counts, x1, meta, mod_s, fw, ew, l, tok_tile=ns, expert_tile=ns, per_token=True,
                   mod_spec_fn=mod_s_spec, final_norm=final)
        outs["conv_s"].append(jnp.transpose(nconv_tm.reshape(k1, ns, cdim), (1, 0, 2)))
        outs["re_s"].append(nre.reshape(ns, S5_GROUPS, S5_STATE))
        outs["im_s"].append(nim.reshape(ns, S5_GROUPS, S5_STATE))

    y_prompt = xp.reshape(nb, seq, d)
    y_sample = jnp.transpose(xsm.reshape(steps, ns, d), (1, 0, 2))
    st = lambda k: jnp.stack(outs[k])
    return (y_prompt, y_sample, st("ssm_p"), ssm_s, st("conv_p"), st("conv_s"),
            st("re_p"), st("re_s"), st("im_p"), st("im_s"))
```

```python
import functools

import jax
import jax.numpy as jnp
from jax import lax
from jax.experimental import pallas as pl
from jax.experimental.pallas import tpu as pltpu

F32 = jnp.float32
BF16 = jnp.bfloat16
EPS = 1e-6

SSD_HEADS = 16
SSD_HEAD_DIM = 64
SSD_GROUPS = 2
SSD_STATE = 128
SSD_CONV = 4
SSD_CHUNK = 128
S5_GROUPS = 32
S5_GROUP_CH = 16
S5_STATE = 64
N_EXPERTS = 16
EXPERTS_PER_GROUP = 4
N_EXPERT_GROUPS = 4

LANES = 128
SUBLANES = 8
MXU_TILE = 256
VMEM_LIMIT_BYTES = 56 * 1024 * 1024

ROW_TILE = 512
MOE_TILE = 256
EXPERT_TILE = 512
S5_STEPS = 64
S5_SCAN_LANES = 512
ROUTER_SUBTILES = 2
SSD_SAMPLE_SEQS = 4


def _cparams(sem):
    return pltpu.CompilerParams(dimension_semantics=sem, vmem_limit_bytes=VMEM_LIMIT_BYTES)


def _bdot(a, b):
    return jnp.dot(a.astype(BF16), b.astype(BF16), preferred_element_type=F32)


def _bdot_nt(a, b):
    return lax.dot_general(a.astype(BF16), b.astype(BF16), (((1,), (1,)), ((), ())),
                           preferred_element_type=F32)


def _split3(x):
    hi = x.astype(BF16)
    r1 = x - hi.astype(F32)
    mid = r1.astype(BF16)
    lo = (r1 - mid.astype(F32)).astype(BF16)
    return hi, mid, lo


def _dot3_lhs(x, m3):
    return jnp.dot(jnp.concatenate(_split3(x), axis=1), m3, preferred_element_type=F32)


def _dot3_rhs(m, x):
    return jnp.dot(jnp.concatenate([m] * 3, axis=1), jnp.concatenate(_split3(x), axis=0),
                   preferred_element_type=F32)


def _sigmoid(x):
    return jax.nn.sigmoid(x)


def _silu(x):
    return x * _sigmoid(x)


def _softplus(x):
    return jnp.maximum(x, 0.0) + jnp.log1p(jnp.exp(-jnp.abs(x)))


def _gelu_tanh(x):
    c = 0.7978845608028654
    return 0.5 * x * (1.0 + jnp.tanh(c * (x + 0.044715 * (x * x * x))))


def _rms(x):
    return x * lax.rsqrt(jnp.mean(x * x, axis=-1, keepdims=True) + EPS)


def _ada_kernel(c_ref, w_ref, b_ref, o_ref):
    o_ref[0] = _bdot(_silu(c_ref[...]), w_ref[0]) + b_ref[0]


def _ada_mod(c_all, w_ada, b_ada):
    depth, d, n = w_ada.shape
    nseq = c_all.shape[0]
    tn = 1536
    return pl.pallas_call(
        _ada_kernel,
        grid=(depth, n // tn),
        in_specs=[pl.BlockSpec((nseq, d), lambda l, j: (0, 0)),
                  pl.BlockSpec((1, d, tn), lambda l, j: (l, 0, j)),
                  pl.BlockSpec((1, 1, tn), lambda l, j: (l, 0, j))],
        out_specs=pl.BlockSpec((1, nseq, tn), lambda l, j: (l, 0, j)),
        out_shape=jax.ShapeDtypeStruct((depth, nseq, n), F32),
        compiler_params=_cparams(("arbitrary", "arbitrary")),
        name="ada_mod",
    )(c_all, w_ada, b_ada.reshape(depth, 1, n))


def _mod_rows(mod_ref, j, per_token, rows_per_seq=None):
    if per_token:
        return mod_ref[j]
    if rows_per_seq is not None:
        m = mod_ref[:, j, :]
        return jnp.broadcast_to(m[:, None, :], (m.shape[0], rows_per_seq, m.shape[1])).reshape(-1, m.shape[1])
    return mod_ref[j:j + 1, :]


def _inproj_kernel(x_ref, mod_ref, nw_ref, wz_ref, wx_ref, wdt_ref, wu_ref, wg_ref,
                   z_ref, xbc_ref, dt_ref, u_ref, g_ref, *, per_token):
    sh = _mod_rows(mod_ref, 0, per_token)
    sc = _mod_rows(mod_ref, 1, per_token)
    h = (_rms(x_ref[...]) * nw_ref[...] * (1.0 + sc) + sh).astype(BF16)
    d = functools.partial(jnp.dot, preferred_element_type=F32)
    z_ref[...] = d(h, wz_ref[...])
    xbc_ref[...] = d(h, wx_ref[...])
    dt_ref[...] = d(h, wdt_ref[...])
    u_ref[...] = d(h, wu_ref[...])
    g_ref[...] = d(h, wg_ref[...])


def _const_spec(shape):
    nd = len(shape)
    return pl.BlockSpec(shape, lambda *_: (0,) * nd)


def _inproj(x, mod, nw, ws, *, tm, per_token, mod_spec):
    t, d = x.shape
    wz, wx, wdt, wu, wg = ws
    row = lambda n: pl.BlockSpec((tm, n), lambda i: (i, 0))
    return pl.pallas_call(
        functools.partial(_inproj_kernel, per_token=per_token),
        grid=(t // tm,),
        in_specs=[row(d), mod_spec, _const_spec((1, d)),
                  _const_spec(wz.shape), _const_spec(wx.shape), _const_spec(wdt.shape),
                  _const_spec(wu.shape), _const_spec(wg.shape)],
        out_specs=[row(wz.shape[1]), row(wx.shape[1]), row(wdt.shape[1]), row(wu.shape[1]), row(wg.shape[1])],
        out_shape=[jax.ShapeDtypeStruct((t, wz.shape[1]), F32),
                   jax.ShapeDtypeStruct((t, wx.shape[1]), F32),
                   jax.ShapeDtypeStruct((t, wdt.shape[1]), F32),
                   jax.ShapeDtypeStruct((t, wu.shape[1]), F32),
                   jax.ShapeDtypeStruct((t, wg.shape[1]), F32)],
        compiler_params=_cparams(("arbitrary",)),
        name="inproj",
    )(x, mod, nw, wz, wx, wdt, wu, wg)


def _gated_group_norm(y, z, nw):
    y = y * _silu(z)
    half = y.shape[1] // SSD_GROUPS
    parts = [_rms(y[:, g * half:(g + 1) * half]) for g in range(SSD_GROUPS)]
    return jnp.concatenate(parts, axis=1) * nw


def _ssd_chunk(first, last, xbc_ref, dt_ref, z_ref, cw_ref, cb_ref, dtb_ref, alog_ref, dsk_ref, nw_ref,
               e_ref, yn_ref, conv_ref, ssm_ref, ext_ref, st_ref):
    q = SSD_CHUNK
    d_in = SSD_HEADS * SSD_HEAD_DIM
    gw = SSD_STATE
    hpg = SSD_HEADS // SSD_GROUPS
    grows = hpg * SSD_HEAD_DIM

    if first is not None:
        @pl.when(first)
        def _():
            ext_ref[...] = jnp.zeros(ext_ref.shape, F32)
            st_ref[...] = jnp.zeros(st_ref.shape, F32)

    xbc = xbc_ref[...]
    sub = lax.broadcasted_iota(jnp.int32, (SUBLANES, xbc.shape[1]), 0)
    taps = [cw_ref[k:k + 1, :] for k in range(SSD_CONV)]
    prev_rolled = [pltpu.roll(ext_ref[...], j, axis=0) for j in range(1, SSD_CONV)]
    acc_tiles = []
    for r0 in range(0, q, SUBLANES):
        cur = xbc[r0:r0 + SUBLANES, :]
        rolled = [pltpu.roll(cur, j, axis=0) for j in range(1, SSD_CONV)]
        acc = cb_ref[...] + taps[SSD_CONV - 1] * cur
        for j in range(1, SSD_CONV):
            acc = acc + taps[SSD_CONV - 1 - j] * jnp.where(sub < j, prev_rolled[j - 1], rolled[j - 1])
        acc_tiles.append(acc)
        prev_rolled = rolled
    ext_ref[...] = xbc[q - SUBLANES:q, :]
    v = _silu(jnp.concatenate(acc_tiles, axis=0))
    xs = v[:, :d_in]
    bm = v[:, d_in:d_in + SSD_GROUPS * gw]
    cm = v[:, d_in + SSD_GROUPS * gw:]

    dt = _softplus(dt_ref[...] + dtb_ref[...])
    a = -jnp.exp(alog_ref[...])
    da = dt * a
    rid = lax.broadcasted_iota(jnp.int32, (q, q), 0)
    cid = lax.broadcasted_iota(jnp.int32, (q, q), 1)
    causal = rid >= cid
    tri = jnp.where(causal, 1.0, 0.0).astype(BF16)
    a_cs = _dot3_rhs(tri, da)
    a_cs_t = a_cs.T
    a_last = a_cs[q - 1:q, :]
    per_head = jnp.concatenate([dt, jnp.exp(a_last - a_cs) * dt, jnp.exp(a_cs)], axis=0)
    per_chan = _dot3_lhs(per_head, e_ref[...])
    x_dt = xs * per_chan[0:q]
    x_end = xs * per_chan[q:2 * q]
    eacs_e = per_chan[2 * q:3 * q]
    x_end_t = x_end.T.astype(BF16)
    lane = lax.broadcasted_iota(jnp.int32, (q, 2 * SSD_HEAD_DIM), 1)
    first = lane < SSD_HEAD_DIM

    y_diag = []
    y_off = []
    for g in range(SSD_GROUPS):
        bg = bm[:, g * gw:(g + 1) * gw].astype(BF16)
        cg = cm[:, g * gw:(g + 1) * gw].astype(BF16)
        cb = _bdot_nt(cg, bg)
        sg = st_ref[g * grows:(g + 1) * grows, :]
        y_off.append(_bdot_nt(cg, sg))
        for pair in range(hpg // 2):
            ms = []
            for r in (2 * pair, 2 * pair + 1):
                h = g * hpg + r
                diff = a_cs[:, h:h + 1] - a_cs_t[h:h + 1, :]
                decay = jnp.where(causal, jnp.exp(jnp.where(causal, diff, 0.0)), 0.0)
                ms.append((cb * decay).astype(BF16))
            h0 = g * hpg + 2 * pair
            xp = x_dt[:, h0 * SSD_HEAD_DIM:(h0 + 2) * SSD_HEAD_DIM]
            rhs = jnp.concatenate([jnp.where(first, xp, 0.0), jnp.where(first, 0.0, xp)], axis=0)
            y_diag.append(jnp.dot(jnp.concatenate(ms, axis=1), rhs.astype(BF16), preferred_element_type=F32))
        contrib = jnp.dot(x_end_t[g * grows:(g + 1) * grows, :], bg, preferred_element_type=F32)
        for r in range(hpg):
            h = g * hpg + r
            rows = slice(h * SSD_HEAD_DIM, (h + 1) * SSD_HEAD_DIM)
            dec = jnp.exp(a_cs_t[h:h + 1, q - 1:q])
            st_ref[rows, :] = st_ref[rows, :] * dec + contrib[r * SSD_HEAD_DIM:(r + 1) * SSD_HEAD_DIM, :]

    y = (jnp.concatenate(y_diag, axis=1) + jnp.concatenate(y_off, axis=1) * eacs_e
         + dsk_ref[...] * xs)
    yn_ref[...] = _gated_group_norm(y, z_ref[...], nw_ref[...]).astype(BF16)

    if last is not None:
        @pl.when(last)
        def _():
            conv_ref[0] = ext_ref[pl.ds(SUBLANES - (SSD_CONV - 1), SSD_CONV - 1), :]
            ssm_ref[0] = st_ref[...].reshape(SSD_HEADS, SSD_HEAD_DIM, SSD_STATE)


def _inproj_ssd_kernel(x_ref, mod_ref, nw1_ref, wz_ref, wx_ref, wdt_ref, wu_ref, wg_ref,
                       cw_ref, cb_ref, dtb_ref, alog_ref, dsk_ref, nw_ref, e_ref,
                       u_ref, g_ref, yn_ref, conv_ref, ssm_ref,
                       z_s, xbc_s, dt_s, ext_ref, st_ref):
    j = pl.program_id(1)
    sh = mod_ref[0:1, :]
    sc = mod_ref[1:2, :]
    h = (_rms(x_ref[...]) * nw1_ref[...] * (1.0 + sc) + sh).astype(BF16)
    d = functools.partial(jnp.dot, preferred_element_type=F32)
    z_s[...] = d(h, wz_ref[...])
    xbc_s[...] = d(h, wx_ref[...])
    dt_s[...] = d(h, wdt_ref[...])
    u_ref[...] = d(h, wu_ref[...])
    g_ref[...] = d(h, wg_ref[...])
    n_chunks = x_ref.shape[0] // SSD_CHUNK
    for k in range(n_chunks):
        rows = pl.ds(k * SSD_CHUNK, SSD_CHUNK)
        first = (j == 0) if k == 0 else None
        last = (j == pl.num_programs(1) - 1) if k == n_chunks - 1 else None
        _ssd_chunk(first, last, xbc_s.at[rows], dt_s.at[rows], z_s.at[rows], cw_ref, cb_ref, dtb_ref, alog_ref,
                   dsk_ref, nw_ref, e_ref, yn_ref.at[rows], conv_ref, ssm_ref, ext_ref, st_ref)


def _inproj_ssd(x, mod, nw1, ws, p, nb, seq, tm):
    t, d = x.shape
    wz, wx, wdt, wu, wg = ws
    cdim = wx.shape[1]
    d_in = wz.shape[1]
    tiles = seq // tm
    row = lambda n: pl.BlockSpec((tm, n), lambda b, j: (b * tiles + j, 0))
    consts = [nw1, wz, wx, wdt, wu, wg, p["conv_w"], p["conv_b"], p["dt_bias"], p["a_log"], p["d_skip_e"],
              p["ssd_norm_w"], p["head_expand"]]
    return pl.pallas_call(
        _inproj_ssd_kernel,
        grid=(nb, tiles),
        in_specs=[row(d), pl.BlockSpec((None, 6, d), lambda b, j: (b, 0, 0))]
                 + [_resident_spec(a.shape) for a in consts],
        out_specs=[row(wu.shape[1]), row(wg.shape[1]), row(d_in),
                   pl.BlockSpec((1, SSD_CONV - 1, cdim), lambda b, j: (b, 0, 0)),
                   pl.BlockSpec((1, SSD_HEADS, SSD_HEAD_DIM, SSD_STATE), lambda b, j: (b, 0, 0, 0))],
        out_shape=[jax.ShapeDtypeStruct((t, wu.shape[1]), F32),
                   jax.ShapeDtypeStruct((t, wg.shape[1]), F32),
                   jax.ShapeDtypeStruct((t, d_in), BF16),
                   jax.ShapeDtypeStruct((nb, SSD_CONV - 1, cdim), F32),
                   jax.ShapeDtypeStruct((nb, SSD_HEADS, SSD_HEAD_DIM, SSD_STATE), F32)],
        scratch_shapes=[pltpu.VMEM((tm, d_in), F32), pltpu.VMEM((tm, cdim), F32), pltpu.VMEM((tm, LANES), F32),
                        pltpu.VMEM((SUBLANES, cdim), F32),
                        pltpu.VMEM((d_in, SSD_STATE), F32)],
        compiler_params=_cparams(("arbitrary", "arbitrary")),
        name="inproj_ssd",
    )(x, mod, *consts)


def _ssd_sample_a_kernel(xbc_ref, conv0_ref, dt_ref, cw_ref, cb_ref, dtb_ref, alog_ref, dsk_ref, e_ref,
                         ydiag_ref, eacs_ref, wt_ref, bs_ref, c8_ref, dec_ref, nconv_ref, *, steps, nseq):
    d_in = SSD_HEADS * SSD_HEAD_DIM
    gw = SSD_STATE
    hpg = SSD_HEADS // SSD_GROUPS
    k1 = SSD_CONV - 1
    slab = lambda ref, i: ref[i * nseq:(i + 1) * nseq, :]
    xp = [slab(conv0_ref, i) for i in range(k1)] + [slab(xbc_ref, i) for i in range(steps)]
    for i in range(k1):
        nconv_ref[i * nseq:(i + 1) * nseq, :] = xp[steps + i]
    e = e_ref[...]
    a = -jnp.exp(alog_ref[...])
    head = lax.broadcasted_iota(jnp.int32, (nseq, LANES), 1)
    xs, bm, cm, dt, a_cs = [], [], [], [], []
    run = jnp.zeros((nseq, LANES), F32)
    for t in range(steps):
        acc = cb_ref[...]
        for k in range(SSD_CONV):
            acc = acc + cw_ref[k:k + 1, :] * xp[t + k]
        v = _silu(acc)
        xs.append(v[:, :d_in])
        bm.append(v[:, d_in:d_in + SSD_GROUPS * gw])
        cm.append(v[:, d_in + SSD_GROUPS * gw:])
        dt.append(_softplus(slab(dt_ref, t) + dtb_ref[...]))
        run = run + dt[t] * a
        a_cs.append(run)
    dec_ref[...] = jnp.exp(a_cs[steps - 1])
    for t in range(steps):
        y = dsk_ref[...] * xs[t]
        for s in range(t + 1):
            cbs = [jnp.sum(cm[t][:, g * gw:(g + 1) * gw] * bm[s][:, g * gw:(g + 1) * gw],
                           axis=-1, keepdims=True) for g in range(SSD_GROUPS)]
            coef = jnp.exp(a_cs[t] - a_cs[s]) * dt[s] * jnp.where(head < hpg, cbs[0], cbs[1])
            y = y + _dot3_lhs(coef, e) * xs[s]
        ydiag_ref[t * nseq:(t + 1) * nseq, :] = y
        eacs_ref[t * nseq:(t + 1) * nseq, :] = _dot3_lhs(jnp.exp(a_cs[t]), e)
        w = xs[t] * _dot3_lhs(jnp.exp(a_cs[steps - 1] - a_cs[t]) * dt[t], e)
        wt_ref[:, t * nseq:(t + 1) * nseq] = w.T.astype(BF16)
        bs_ref[t * nseq:(t + 1) * nseq, :] = bm[t]
        for g in range(SSD_GROUPS):
            j = g * steps + t
            c8_ref[j * nseq:(j + 1) * nseq, :] = cm[t][:, g * gw:(g + 1) * gw]


def _ssd_sample_b_kernel(dec_ref, h0_ref, wt_ref, bs_ref, c8_ref, *rest, steps, nseq, layer):
    yoff_ref, ssm_ref = rest[-2:]
    if layer:
        ssm_ref[0:layer] = rest[0][...]
    hpg = SSD_HEADS // SSD_GROUPS
    grows = hpg * SSD_HEAD_DIM
    gw = SSD_STATE
    rid = lax.broadcasted_iota(jnp.int32, (steps * nseq, gw), 0)
    for s in range(h0_ref.shape[0]):
        b = pl.program_id(0) * h0_ref.shape[0] + s
        h0 = h0_ref[s].reshape(SSD_HEADS * SSD_HEAD_DIM, SSD_STATE)
        c8 = c8_ref[pl.ds(b, SSD_GROUPS * steps, stride=nseq), :]
        yoff_ref[s] = _bdot_nt(c8, h0)
        mine = rid == b
        for t in range(1, steps):
            mine = jnp.logical_or(mine, rid == b + t * nseq)
        for g in range(SSD_GROUPS):
            bsel = jnp.where(mine, bs_ref[:, g * gw:(g + 1) * gw], 0.0).astype(BF16)
            contrib = jnp.dot(wt_ref[g * grows:(g + 1) * grows, :], bsel, preferred_element_type=F32)
            for r in range(hpg):
                h = g * hpg + r
                rows = slice(h * SSD_HEAD_DIM, (h + 1) * SSD_HEAD_DIM)
                ssm_ref[layer, s, h] = (h0[rows, :] * dec_ref[b, h]
                                        + contrib[r * SSD_HEAD_DIM:(r + 1) * SSD_HEAD_DIM, :])


def _ssd_sample_c_kernel(ydiag_ref, yoff_ref, eacs_ref, z_ref, nw_ref, yn_ref, *, steps, nseq):
    d_in = SSD_HEADS * SSD_HEAD_DIM
    half = d_in // SSD_GROUPS
    for t in range(steps):
        lo = yoff_ref[:, t * d_in:t * d_in + half]
        hi = yoff_ref[:, (steps + t) * d_in + half:(steps + t + 1) * d_in]
        rows = slice(t * nseq, (t + 1) * nseq)
        y = ydiag_ref[rows, :] + jnp.concatenate([lo, hi], axis=1) * eacs_ref[rows, :]
        yn_ref[rows, :] = _gated_group_norm(y, z_ref[rows, :], nw_ref[...]).astype(BF16)


def _ssd_sample(xbc, dt_raw, z, conv0_tm, ssm_all, ssm_done, layer, p, nseq, steps):
    t, cdim = xbc.shape
    d_in = z.shape[1]
    k1 = SSD_CONV - 1
    nj = SSD_GROUPS * steps
    a_out = pl.pallas_call(
        functools.partial(_ssd_sample_a_kernel, steps=steps, nseq=nseq),
        out_shape=[jax.ShapeDtypeStruct((t, d_in), F32),
                   jax.ShapeDtypeStruct((t, d_in), F32),
                   jax.ShapeDtypeStruct((d_in, t), BF16),
                   jax.ShapeDtypeStruct((t, SSD_GROUPS * SSD_STATE), F32),
                   jax.ShapeDtypeStruct((nj * nseq, SSD_STATE), F32),
                   jax.ShapeDtypeStruct((nseq, LANES), F32),
                   jax.ShapeDtypeStruct((k1 * nseq, cdim), F32)],
        compiler_params=pltpu.CompilerParams(vmem_limit_bytes=VMEM_LIMIT_BYTES),
        name="ssd_sample_a",
    )(xbc, conv0_tm, dt_raw, p["conv_w"], p["conv_b"], p["dt_bias"], p["a_log"], p["d_skip_e"],
      p["head_expand"])
    ydiag, eacs_e, wt, bs, c8, dec, nconv = a_out
    state_blk = (SSD_HEADS, SSD_HEAD_DIM, SSD_STATE)
    operands = [dec[:, :SSD_HEADS], ssm_all, wt, bs, c8]
    sb = SSD_SAMPLE_SEQS
    in_specs = [pl.BlockSpec(memory_space=pltpu.SMEM),
                pl.BlockSpec((None, sb) + state_blk, lambda b: (layer, b, 0, 0, 0)),
                _const_spec(wt.shape), _const_spec(bs.shape), _const_spec(c8.shape)]
    if layer:
        operands.append(ssm_done)
        in_specs.append(pl.BlockSpec((layer, sb) + state_blk, lambda b: (0, b, 0, 0, 0)))
    yoff, nssm = pl.pallas_call(
        functools.partial(_ssd_sample_b_kernel, steps=steps, nseq=nseq, layer=layer),
        grid=(nseq // sb,),
        in_specs=in_specs,
        out_specs=[pl.BlockSpec((sb, nj, d_in), lambda b: (b, 0, 0)),
                   pl.BlockSpec((layer + 1, sb) + state_blk, lambda b: (0, b, 0, 0, 0))],
        out_shape=[jax.ShapeDtypeStruct((nseq, nj, d_in), F32),
                   jax.ShapeDtypeStruct((layer + 1, nseq) + state_blk, F32)],
        compiler_params=_cparams(("arbitrary",)),
        name="ssd_sample_b",
    )(*operands)
    yn = pl.pallas_call(
        functools.partial(_ssd_sample_c_kernel, steps=steps, nseq=nseq),
        out_shape=jax.ShapeDtypeStruct((t, d_in), BF16),
        compiler_params=pltpu.CompilerParams(vmem_limit_bytes=VMEM_LIMIT_BYTES),
        name="ssd_sample_c",
    )(ydiag, yoff.reshape(nseq, nj * d_in), eacs_e, z, p["ssd_norm_w"])
    return yn, nconv, nssm


def _s5_param_kernel(are_ref, aim_ref, ldt_ref, bre_ref, bim_ref, cre_ref, cim_ref,
                     ab_ref, wb_ref, cret_ref, cimt_ref):
    n = are_ref.shape[1]
    w = wb_ref.shape[0]
    a_re = are_ref[...]
    a_im = aim_ref[...]
    dt = jnp.exp(ldt_ref[...])
    mag = jnp.exp(dt * a_re)
    ab_re = mag * jnp.cos(dt * a_im)
    ab_im = mag * jnp.sin(dt * a_im)
    den = a_re * a_re + a_im * a_im
    f_re = ((ab_re - 1.0) * a_re + ab_im * a_im) / den
    f_im = (ab_im * a_re - (ab_re - 1.0) * a_im) / den
    ab_ref[0:1, :] = ab_re
    ab_ref[1:2, :] = ab_im
    b_re = bre_ref[...]
    b_im = bim_ref[...]
    row_g = lax.shift_right_logical(lax.broadcasted_iota(jnp.int32, (w, n), 0), S5_GROUP_CH.bit_length() - 1)
    col_g = lax.shift_right_logical(lax.broadcasted_iota(jnp.int32, (w, n), 1), S5_STATE.bit_length() - 1)
    same = row_g == col_g

    def spread(rows):
        return jnp.where(same, jnp.concatenate([rows] * S5_GROUPS, axis=0), 0.0).astype(BF16)

    wb_ref[:, 0:n] = spread(f_re * b_re - f_im * b_im)
    wb_ref[:, n:2 * n] = spread(f_re * b_im + f_im * b_re)
    cret_ref[...] = spread(cre_ref[...])
    cimt_ref[...] = spread(cim_ref[...])


def _s5_params(a_re, a_im, ldt, bt_re, bt_im, ct_re, ct_im):
    n = a_re.shape[1]
    w = S5_GROUPS * S5_GROUP_CH
    return pl.pallas_call(
        _s5_param_kernel,
        out_shape=[jax.ShapeDtypeStruct((2, n), F32),
                   jax.ShapeDtypeStruct((w, 2 * n), BF16),
                   jax.ShapeDtypeStruct((w, n), BF16),
                   jax.ShapeDtypeStruct((w, n), BF16)],
        compiler_params=pltpu.CompilerParams(vmem_limit_bytes=VMEM_LIMIT_BYTES),
        name="s5_params",
    )(a_re, a_im, ldt, bt_re, bt_im, ct_re, ct_im)


def _s5_kernel(u_ref, h0re_ref, h0im_ref, ab_ref, wb_ref, cret_ref, cimt_ref, d_ref, wglu_ref,
               yb_ref, nre_ref, nim_ref, xh_ref, st_ref, *, rows_per_step, steps, batch_major):
    i = pl.program_id(0)
    n = ab_ref.shape[1]
    w = wb_ref.shape[0]
    r = rows_per_step

    @pl.when(i == 0)
    def _():
        st_ref[0] = h0re_ref[...]
        st_ref[1] = h0im_ref[...]

    if batch_major:
        u = pltpu.einshape("btd->(tb)d", u_ref[...])
    else:
        u = u_ref[...]
    ub = u.astype(BF16)
    for j in range(2 * n // MXU_TILE):
        c0 = (j * MXU_TILE) % n
        k0 = (c0 // S5_STATE * S5_GROUP_CH) // LANES * LANES
        cols = slice(j * MXU_TILE, (j + 1) * MXU_TILE)
        xh_ref[:, cols] = jnp.dot(ub[:, k0:k0 + LANES], wb_ref[k0:k0 + LANES, cols], preferred_element_type=F32)
    for c0 in range(0, n, S5_SCAN_LANES):
        re = slice(c0, c0 + S5_SCAN_LANES)
        im = slice(n + c0, n + c0 + S5_SCAN_LANES)
        a_re = jnp.broadcast_to(ab_ref[0:1, re], (r, S5_SCAN_LANES))
        a_im = jnp.broadcast_to(ab_ref[1:2, re], (r, S5_SCAN_LANES))

        def step(t, carry, re=re, im=im, a_re=a_re, a_im=a_im):
            h_re, h_im = carry
            rows = pl.ds(pl.multiple_of(t * r, r), r)
            n_re = a_re * h_re - a_im * h_im + xh_ref[rows, re]
            n_im = a_re * h_im + a_im * h_re + xh_ref[rows, im]
            xh_ref[rows, re] = n_re
            xh_ref[rows, im] = n_im
            return n_re, n_im

        carry = (st_ref[0, :, re], st_ref[1, :, re])
        if steps <= 8:
            for t in range(steps):
                carry = step(t, carry)
        else:
            carry = lax.fori_loop(0, steps, step, carry, unroll=4)
        st_ref[0, :, re] = carry[0]
        st_ref[1, :, re] = carry[1]

    n_ct = w // MXU_TILE
    ys = []
    for j in range(n_ct):
        rows = slice(j * MXU_TILE, (j + 1) * MXU_TILE)
        ks = slice(j * (n // n_ct), (j + 1) * (n // n_ct))
        ks_im = slice(n + j * (n // n_ct), n + (j + 1) * (n // n_ct))
        ys.append(_bdot_nt(xh_ref[:, ks], cret_ref[rows, ks]) - _bdot_nt(xh_ref[:, ks_im], cimt_ref[rows, ks]))
    y = jnp.concatenate(ys, axis=1) + d_ref[...] * u
    pre = _bdot(_gelu_tanh(y), wglu_ref[...])
    half = pre.shape[1] // 2
    yb = pre[:, :half] * _sigmoid(pre[:, half:])
    if batch_major:
        yb_ref[...] = pltpu.einshape("(tb)d->btd", yb, b=r)
    else:
        yb_ref[...] = yb

    @pl.when(i == pl.num_programs(0) - 1)
    def _():
        nre_ref[...] = st_ref[0]
        nim_ref[...] = st_ref[1]


def _s5(u, h0_re, h0_im, sp, rows_per_step, steps, batch_major):
    n = sp["ab"].shape[1]
    w = u.shape[-1]
    tm = rows_per_step * steps
    d_model = sp["w_glu"].shape[1] // 2
    if batch_major:
        nb, seq, _ = u.shape
        grid = (seq // steps,)
        u_spec = pl.BlockSpec((nb, steps, w), lambda i: (0, i, 0))
        yb_spec = pl.BlockSpec((nb, steps, d_model), lambda i: (0, i, 0))
        yb_shape = (nb, seq, d_model)
    else:
        grid = (u.shape[0] // tm,)
        u_spec = pl.BlockSpec((tm, w), lambda i: (i, 0))
        yb_spec = pl.BlockSpec((tm, d_model), lambda i: (i, 0))
        yb_shape = (u.shape[0], d_model)
    return pl.pallas_call(
        functools.partial(_s5_kernel, rows_per_step=rows_per_step, steps=steps, batch_major=batch_major),
        grid=grid,
        in_specs=[u_spec,
                  _const_spec(h0_re.shape), _const_spec(h0_im.shape), _const_spec(sp["ab"].shape),
                  _const_spec(sp["wb"].shape), _const_spec(sp["c_re_t"].shape), _const_spec(sp["c_im_t"].shape),
                  _const_spec((1, w)), _const_spec(sp["w_glu"].shape)],
        out_specs=[yb_spec, _const_spec(h0_re.shape), _const_spec(h0_im.shape)],
        out_shape=[jax.ShapeDtypeStruct(yb_shape, F32),
                   jax.ShapeDtypeStruct(h0_re.shape, F32),
                   jax.ShapeDtypeStruct(h0_im.shape, F32)],
        scratch_shapes=[pltpu.VMEM((tm, 2 * n), F32),
                        pltpu.VMEM((2, rows_per_step, n), F32)],
        compiler_params=_cparams(("arbitrary",)),
        name="s5_scan",
    )(u, h0_re, h0_im, sp["ab"], sp["wb"], sp["c_re_t"], sp["c_im_t"], sp["d"], sp["w_glu"])


def _merge_router_steps(i, row_refs, mod_ref, weight_refs, cnt_ref, run_ref, *, per_token, rows_per_seq):
    @pl.when(i == 0)
    def _():
        run_ref[...] = jnp.zeros(run_ref.shape, F32)

    lead = row_refs[3].shape[0]
    rows_each = 1 if rows_per_seq is None else rows_per_seq
    sub = lead // ROUTER_SUBTILES if (lead * rows_each) % (ROUTER_SUBTILES * LANES) == 0 else lead
    for r0 in range(0, lead, sub):
        rows = pl.ds(r0, sub)
        if per_token:
            mod = mod_ref.at[:, rows]
        elif rows_per_seq is not None:
            mod = mod_ref.at[rows]
        else:
            mod = mod_ref
        _route_rows(*[ref.at[rows] for ref in row_refs], mod, *weight_refs, run_ref,
                    per_token=per_token, rows_per_seq=rows_per_seq)

    @pl.when(i == pl.num_programs(0) - 1)
    def _():
        cnt_ref[...] = run_ref[...]


def _merge_router_kernel(yn_ref, yb_ref, gate_ref, x_ref, mod_ref, woa_ref, wo_ref, nw_ref, wr_ref, br_ref,
                         x1_ref, h2_ref, meta_ref, cnt_ref, run_ref, *, per_token):
    _merge_router_steps(pl.program_id(0), (yn_ref, yb_ref, gate_ref, x_ref, x1_ref, h2_ref, meta_ref), mod_ref,
                        (woa_ref, wo_ref, nw_ref, wr_ref, br_ref), cnt_ref, run_ref,
                        per_token=per_token, rows_per_seq=None)


def _rows2d(ref):
    v = ref[...]
    return v.reshape(-1, v.shape[-1]) if v.ndim == 3 else v


def _route_rows(yn_ref, yb_ref, gate_ref, x_ref, x1_ref, h2_ref, meta_ref, mod_ref,
                woa_ref, wo_ref, nw_ref, wr_ref, br_ref, run_ref, *, per_token, rows_per_seq):
    x = _rows2d(x_ref)
    tm, d = x.shape
    ya = jnp.dot(_rows2d(yn_ref), woa_ref[...], preferred_element_type=F32)
    gate = _sigmoid(_rows2d(gate_ref))
    merged = gate[:, :d] * ya + gate[:, d:] * _rows2d(yb_ref)
    mix = _bdot(merged, wo_ref[...])
    g1 = _mod_rows(mod_ref, 2, per_token, rows_per_seq)
    sh2 = _mod_rows(mod_ref, 3, per_token, rows_per_seq)
    sc2 = _mod_rows(mod_ref, 4, per_token, rows_per_seq)
    x1 = x + g1 * mix
    x1_ref[...] = x1.reshape(x1_ref.shape)
    h2 = _rms(x1) * nw_ref[...] * (1.0 + sc2) + sh2
    h2_ref[...] = h2.reshape(tm, d // LANES, LANES).reshape(h2_ref.shape)

    wr = wr_ref[...]
    wr_hi = wr.astype(BF16)
    wr_lo = (wr - wr_hi.astype(F32)).astype(BF16)
    h2_hi = h2.astype(BF16)
    h2_lo = (h2 - h2_hi.astype(F32)).astype(BF16)
    d = functools.partial(jnp.dot, preferred_element_type=F32)
    logits = d(h2_hi, wr_hi) + d(h2_lo, wr_hi) + d(h2_hi, wr_lo)
    lt = logits.T
    erow = lax.broadcasted_iota(jnp.int32, lt.shape, 0)
    lt = jnp.where(erow < N_EXPERTS, lt, -jnp.inf)
    ex = jnp.exp(lt - jnp.max(lt, axis=0, keepdims=True))
    scores = ex / jnp.sum(ex, axis=0, keepdims=True)
    sel = scores + br_ref[...]
    s = [sel[e:e + 1, :] for e in range(N_EXPERTS)]
    p = [scores[e:e + 1, :] for e in range(N_EXPERTS)]

    def group_top2_sum(vals):
        best = None
        for a in range(len(vals)):
            for b in range(a + 1, len(vals)):
                pair = vals[a] + vals[b]
                best = pair if best is None else jnp.maximum(best, pair)
        return best

    gs = [group_top2_sum(s[EXPERTS_PER_GROUP * g:EXPERTS_PER_GROUP * (g + 1)]) for g in range(N_EXPERT_GROUPS)]
    best = gs[0]
    bg = jnp.zeros(best.shape, jnp.int32)
    for g in range(1, N_EXPERT_GROUPS):
        better = gs[g] > best
        bg = jnp.where(better, g, bg)
        best = jnp.where(better, gs[g], best)

    def pick(rows, j):
        out = rows[j]
        for g in range(1, N_EXPERT_GROUPS):
            out = jnp.where(bg == g, rows[EXPERTS_PER_GROUP * g + j], out)
        return out

    cs = [pick(s, j) for j in range(EXPERTS_PER_GROUP)]
    cp = [pick(p, j) for j in range(EXPERTS_PER_GROUP)]

    def argmax_first(vals, skip):
        bv = None
        bi = None
        bw = None
        for j in range(len(vals)):
            v = vals[j] if skip is None else jnp.where(skip == j, -jnp.inf, vals[j])
            if bv is None:
                bv, bi, bw = v, jnp.zeros(v.shape, jnp.int32), cp[j]
            else:
                better = v > bv
                bi = jnp.where(better, j, bi)
                bw = jnp.where(better, cp[j], bw)
                bv = jnp.where(better, v, bv)
        return bi, bw

    i1, w1 = argmax_first(cs, None)
    i2, w2 = argmax_first(cs, i1)
    wsum = w1 + w2
    e1 = bg * EXPERTS_PER_GROUP + i1
    e2 = bg * EXPERTS_PER_GROUP + i2
    oh1 = jnp.where(erow == e1, 1.0, 0.0)
    oh2 = jnp.where(erow == e2, 1.0, 0.0)
    both = oh1 + oh2
    ta = lax.broadcasted_iota(jnp.int32, (tm, tm), 0)
    tb = lax.broadcasted_iota(jnp.int32, (tm, tm), 1)
    earlier = jnp.where(ta < tb, 1.0, 0.0).astype(BF16)
    before = jnp.dot(both.astype(BF16), earlier, preferred_element_type=F32) + run_ref[...]
    r1 = jnp.sum(oh1 * before, axis=0, keepdims=True)
    r2 = jnp.sum(oh2 * before, axis=0, keepdims=True)
    run_ref[...] = run_ref[...] + jnp.sum(both, axis=1, keepdims=True)

    record = [w1 / wsum, w2 / wsum, e1.astype(F32), e2.astype(F32), r1, r2]
    wrow = lax.broadcasted_iota(jnp.int32, lt.shape, 0)
    wmat = jnp.zeros(lt.shape, F32)
    for k, v in enumerate(record):
        wmat = jnp.where(wrow == k, v, wmat)
    meta_ref[...] = wmat.T.reshape(meta_ref.shape)


def _merge_router(yn, yb, gate, x, mod, lw, rw, *, tm, per_token, mod_spec):
    t, d = x.shape
    row = lambda n: pl.BlockSpec((tm, n), lambda i: (i, 0))
    return pl.pallas_call(
        functools.partial(_merge_router_kernel, per_token=per_token),
        grid=(t // tm,),
        in_specs=[row(d), row(d), row(2 * d), row(d), mod_spec,
                  _const_spec((d, d)), _const_spec((d, d)), _const_spec((1, d)),
                  _const_spec((d, LANES)), _const_spec((LANES, 1))],
        out_specs=[row(d), _row_tiles(tm, d), row(LANES), _const_spec((LANES, 1))],
        out_shape=[jax.ShapeDtypeStruct((t, d), F32),
                   jax.ShapeDtypeStruct((t, d // LANES, LANES), F32),
                   jax.ShapeDtypeStruct((t, LANES), F32),
                   jax.ShapeDtypeStruct((LANES, 1), F32)],
        scratch_shapes=[pltpu.VMEM((LANES, 1), F32)],
        compiler_params=_cparams(("arbitrary",)),
        name="merge_router",
    )(yn, yb, gate, x, mod, lw["w_out_a"], lw["w_out"], lw["norm2_w"], rw["w_router"], rw["b_router"])


def _s5_merge_kernel(u_ref, h0re_ref, h0im_ref, ab_ref, wb_ref, cret_ref, cimt_ref, d_ref, wglu_ref,
                     yn_ref, gate_ref, x_ref, mod_ref, woa_ref, wo_ref, nw_ref, wr_ref, br_ref,
                     nre_ref, nim_ref, x1_ref, h2_ref, meta_ref, cnt_ref,
                     xh_ref, st_ref, yb_ref, run_ref, *, steps):
    _s5_kernel(u_ref, h0re_ref, h0im_ref, ab_ref, wb_ref, cret_ref, cimt_ref, d_ref, wglu_ref,
               yb_ref, nre_ref, nim_ref, xh_ref, st_ref, rows_per_step=u_ref.shape[0], steps=steps, batch_major=True)
    _merge_router_steps(pl.program_id(0), (yn_ref, yb_ref, gate_ref, x_ref, x1_ref, h2_ref, meta_ref), mod_ref,
                        (woa_ref, wo_ref, nw_ref, wr_ref, br_ref), cnt_ref, run_ref,
                        per_token=False, rows_per_seq=steps)


def _resident_spec(shape):
    nd = len(shape)
    return pl.BlockSpec(shape, lambda *_: (0,) * nd, pipeline_mode=pl.Buffered(1))


def _s5_merge(u, yn, gate, x, mod, zero_state, sp, lw, rw, steps):
    nb, seq, w = u.shape
    d = x.shape[-1]
    n = sp["ab"].shape[1]
    blk = lambda *tail: pl.BlockSpec((nb, steps) + tail, lambda i: (0, i) + (0,) * len(tail))
    weights = [sp["ab"], sp["wb"], sp["c_re_t"], sp["c_im_t"], sp["d"], sp["w_glu"]]
    merge_w = [lw["w_out_a"], lw["w_out"], lw["norm2_w"], rw["w_router"], rw["b_router"]]
    return pl.pallas_call(
        functools.partial(_s5_merge_kernel, steps=steps),
        grid=(seq // steps,),
        in_specs=[blk(w), _resident_spec(zero_state.shape), _resident_spec(zero_state.shape)]
                 + [_resident_spec(a.shape) for a in weights]
                 + [blk(d), blk(2 * d), blk(d), _resident_spec(mod.shape)]
                 + [_resident_spec(a.shape) for a in merge_w],
        out_specs=[_const_spec(zero_state.shape), _const_spec(zero_state.shape),
                   blk(d), blk(d // LANES, LANES), blk(LANES), _const_spec((LANES, 1))],
        out_shape=[jax.ShapeDtypeStruct(zero_state.shape, F32),
                   jax.ShapeDtypeStruct(zero_state.shape, F32),
                   jax.ShapeDtypeStruct((nb, seq, d), F32),
                   jax.ShapeDtypeStruct((nb, seq, d // LANES, LANES), F32),
                   jax.ShapeDtypeStruct((nb, seq, LANES), F32),
                   jax.ShapeDtypeStruct((LANES, 1), F32)],
        scratch_shapes=[pltpu.VMEM((nb * steps, 2 * n), F32), pltpu.VMEM((2, nb, n), F32),
                        pltpu.VMEM((nb, steps, d), F32), pltpu.VMEM((LANES, 1), F32)],
        compiler_params=_cparams(("arbitrary",)),
        name="s5_merge_router",
    )(u, zero_state, zero_state, *weights, yn, gate, x, mod, *merge_w)


def _meta_rows_kernel(meta_ref, idx_ref, rank_ref):
    m = meta_ref[...].T
    idx_ref[...] = m[2:4, :].astype(jnp.int32)
    rank_ref[...] = m[4:6, :].astype(jnp.int32)


def _meta_rows(meta):
    t = meta.shape[0]
    tm = min(t, ROW_TILE)
    pair = pl.BlockSpec((2, tm), lambda i: (0, i))
    return pl.pallas_call(
        _meta_rows_kernel,
        grid=(t // tm,),
        in_specs=[pl.BlockSpec((tm, LANES), lambda i: (i, 0))],
        out_specs=[pair, pair],
        out_shape=[jax.ShapeDtypeStruct((2, t), jnp.int32), jax.ShapeDtypeStruct((2, t), jnp.int32)],
        compiler_params=_cparams(("arbitrary",)),
        name="moe_meta_rows",
    )(meta)


def _plan_kernel(cnt_ref, idx_ref, rank_ref, dest_ref, te_ref, ends_ref, nv_ref, *, tile, n_tiles):
    shift = tile.bit_length() - 1
    run = jnp.int32(0)
    starts = []
    for e in range(N_EXPERTS):
        starts.append(run)
        run = run + (((cnt_ref[e] + (tile - 1)) >> shift) << shift)
        ends_ref[e] = run
    n_valid = run >> shift
    nv_ref[0] = n_valid
    idx = idx_ref[...]
    dest = rank_ref[...]
    for e in range(N_EXPERTS):
        dest = dest + jnp.where(idx == e, starts[e], 0)
    dest_ref[...] = dest

    def tile_owner(j, carry):
        pos = jnp.minimum(j, n_valid - 1) * tile
        owner = jnp.int32(0)
        for e in range(N_EXPERTS - 1):
            owner = owner + jnp.where(ends_ref[e] <= pos, 1, 0)
        te_ref[j] = owner
        return carry

    lax.fori_loop(0, n_tiles, tile_owner, 0)


def _plan(cnt, idx, rank, tile, n_tiles):
    smem = pl.BlockSpec(memory_space=pltpu.SMEM)
    vmem = pl.BlockSpec(memory_space=pltpu.VMEM)
    return pl.pallas_call(
        functools.partial(_plan_kernel, tile=tile, n_tiles=n_tiles),
        in_specs=[smem, vmem, vmem],
        out_specs=[vmem, smem, smem, smem],
        out_shape=[jax.ShapeDtypeStruct(idx.shape, jnp.int32),
                   jax.ShapeDtypeStruct((n_tiles,), jnp.int32),
                   jax.ShapeDtypeStruct((N_EXPERTS,), jnp.int32),
                   jax.ShapeDtypeStruct((1,), jnp.int32)],
        name="moe_plan",
    )(cnt, idx, rank)


def _row_tiles(tm, d):
    return pl.BlockSpec((tm, d // LANES, LANES), lambda i: (i, 0, 0))


def _row_copy(src_ref, src_row, dst_ref, dst_row, sem):
    return pltpu.make_async_copy(src_ref.at[src_row], dst_ref.at[dst_row], sem)


def _each_row(tm, fn):
    def body(t, carry):
        for k in range(2):
            fn(t, k)
        return carry

    lax.fori_loop(0, tm, body, 0, unroll=8)


def _dispatch_kernel(ends_ref, dest_ref, prev_dest_ref, h_ref, out_ref, zero_ref, stage_ref, sem, zsem, *, tile):
    i = pl.program_id(0)
    tm = h_ref.shape[0]
    slot = i % 2

    @pl.when(i == 0)
    def _():
        zero_ref[...] = jnp.zeros(zero_ref.shape, F32)

        def clear(start):
            return pltpu.make_async_copy(zero_ref, out_ref.at[pl.ds(pl.multiple_of(start, tile), tile)], zsem)

        def used(e):
            return ends_ref[e] > (ends_ref[e - 1] if e else 0)

        def unused_tiles(fn):
            n_valid = ends_ref[N_EXPERTS - 1] >> (tile.bit_length() - 1)
            lax.fori_loop(n_valid, out_ref.shape[0] // tile, lambda j, c: (fn(clear(j * tile)), c)[1], 0)

        for e in range(N_EXPERTS):
            pl.when(used(e))(lambda e=e: clear(ends_ref[e] - tile).start())
        unused_tiles(lambda cp: cp.start())
        for e in range(N_EXPERTS):
            pl.when(used(e))(lambda e=e: clear(ends_ref[e] - tile).wait())
        unused_tiles(lambda cp: cp.wait())

    stage_ref[slot] = h_ref[...]
    _each_row(tm, lambda t, k: _row_copy(stage_ref.at[slot], t, out_ref, dest_ref[k * tm + t], sem.at[slot]).start(priority=k))

    @pl.when(i > 0)
    def _():
        _each_row(tm, lambda t, k: _row_copy(stage_ref.at[1 - slot], t, out_ref, prev_dest_ref[k * tm + t],
                                             sem.at[1 - slot]).wait())

    @pl.when(i == pl.num_programs(0) - 1)
    def _():
        _each_row(tm, lambda t, k: _row_copy(stage_ref.at[slot], t, out_ref, dest_ref[k * tm + t], sem.at[slot]).wait())


def _dispatch(ends, dest, h2, n_rows, tm, tile):
    t, nt, lanes = h2.shape
    idx_spec = lambda fn: pl.BlockSpec((2 * tm,), fn, memory_space=pltpu.SMEM)
    return pl.pallas_call(
        functools.partial(_dispatch_kernel, tile=tile),
        grid=(t // tm,),
        in_specs=[pl.BlockSpec(memory_space=pltpu.SMEM),
                  idx_spec(lambda i: (i,)), idx_spec(lambda i: (jnp.maximum(i - 1, 0),)),
                  _row_tiles(tm, nt * lanes)],
        out_specs=pl.BlockSpec(memory_space=pl.ANY),
        out_shape=jax.ShapeDtypeStruct((n_rows, nt, lanes), F32),
        scratch_shapes=[pltpu.VMEM((tile, nt, lanes), F32), pltpu.VMEM((2, tm, nt, lanes), F32),
                        pltpu.SemaphoreType.DMA((2,)), pltpu.SemaphoreType.DMA(())],
        compiler_params=_cparams(("arbitrary",)),
        name="moe_dispatch",
    )(ends, dest, dest, h2)


def _expert_kernel(te_ref, nv_ref, x_ref, w1_ref, w3_ref, w2_ref, y_ref, w1b_ref, w3b_ref, w2b_ref):
    j = pl.program_id(0)
    fresh = jnp.logical_or(j == 0, te_ref[j] != te_ref[jnp.maximum(j - 1, 0)])

    @pl.when(fresh)
    def _():
        w1b_ref[...] = w1_ref[...].astype(BF16)
        w3b_ref[...] = w3_ref[...].astype(BF16)
        w2b_ref[...] = w2_ref[...].astype(BF16)

    @pl.when(j < nv_ref[0])
    def _():
        tm, nt, lanes = x_ref.shape
        xb = x_ref[...].reshape(tm, nt * lanes).astype(BF16)
        a = jnp.dot(xb, w1b_ref[...], preferred_element_type=F32)
        b = jnp.dot(xb, w3b_ref[...], preferred_element_type=F32)
        y_ref[...] = _bdot(_silu(a) * b, w2b_ref[...]).reshape(tm, nt, lanes)

    @pl.when(j >= nv_ref[0])
    def _():
        y_ref[...] = jnp.zeros(y_ref.shape, F32)


def _experts(tile_expert, n_valid, xs, w1, w3, w2, layer, tm):
    n_rows, nt, lanes = xs.shape
    d = nt * lanes
    de = w1.shape[3]
    wspec = lambda r, c: pl.BlockSpec((None, None, r, c), lambda j, te, nv: (layer, te[j], 0, 0))
    grid_spec = pltpu.PrefetchScalarGridSpec(
        num_scalar_prefetch=2,
        grid=(n_rows // tm,),
        in_specs=[pl.BlockSpec((tm, nt, lanes), lambda j, te, nv: (jnp.minimum(j, nv[0] - 1), 0, 0)),
                  wspec(d, de), wspec(d, de), wspec(de, d)],
        out_specs=pl.BlockSpec((tm, nt, lanes), lambda j, te, nv: (j, 0, 0)),
        scratch_shapes=[pltpu.VMEM((d, de), BF16), pltpu.VMEM((d, de), BF16), pltpu.VMEM((de, d), BF16)],
    )
    return pl.pallas_call(
        _expert_kernel,
        grid_spec=grid_spec,
        out_shape=jax.ShapeDtypeStruct((n_rows, nt, lanes), F32),
        compiler_params=_cparams(("arbitrary",)),
        name="moe_experts",
    )(tile_expert, n_valid, xs, w1, w3, w2)


def _combine_kernel(dest_ref, next_dest_ref, ys_ref, x1_ref, wts_ref, mod_ref, fw_ref, out_ref, buf_ref, sem,
                    *, per_token, final_norm):
    i = pl.program_id(0)
    tm, d = x1_ref.shape
    slot = i % 2

    def gather(idx_ref, s):
        return lambda t, k: _row_copy(ys_ref, idx_ref[k * tm + t], buf_ref.at[s, k], t, sem.at[s])

    @pl.when(i == 0)
    def _():
        _each_row(tm, lambda t, k: gather(dest_ref, slot)(t, k).start(priority=k))

    @pl.when(i + 1 < pl.num_programs(0))
    def _():
        _each_row(tm, lambda t, k: gather(next_dest_ref, 1 - slot)(t, k).start(priority=k))

    _each_row(tm, lambda t, k: gather(dest_ref, slot)(t, k).wait())
    w = wts_ref[...]
    rows = lambda k: buf_ref[slot, k].reshape(tm, d)
    moe = w[:, 0:1] * rows(0) + w[:, 1:2] * rows(1)
    x2 = x1_ref[...] + _mod_rows(mod_ref, 5, per_token) * moe
    if final_norm:
        x2 = _rms(x2) * fw_ref[...]
    out_ref[...] = x2


def _combine(dest, ys, x1, wts, mod, fw, *, tm, per_token, mod_spec, final_norm):
    t, d = x1.shape
    row = lambda n: pl.BlockSpec((tm, n), lambda i: (i, 0))
    n_steps = t // tm
    idx_spec = lambda fn: pl.BlockSpec((2 * tm,), fn, memory_space=pltpu.SMEM)
    return pl.pallas_call(
        functools.partial(_combine_kernel, per_token=per_token, final_norm=final_norm),
        grid=(n_steps,),
        in_specs=[idx_spec(lambda i: (i,)), idx_spec(lambda i: (jnp.minimum(i + 1, n_steps - 1),)),
                  pl.BlockSpec(memory_space=pl.ANY),
                  row(d), row(LANES), mod_spec, _const_spec((1, d))],
        out_specs=row(d),
        out_shape=jax.ShapeDtypeStruct((t, d), F32),
        scratch_shapes=[pltpu.VMEM((2, 2, tm) + ys.shape[1:], F32), pltpu.SemaphoreType.DMA((2,))],
        compiler_params=_cparams(("arbitrary",)),
        name="moe_combine",
    )(dest, dest, ys, x1, wts, mod, fw)


def _moe(h2, idx, rank, counts, x1, wts, mod, fw, ew, layer, *, tok_tile, expert_tile, per_token, mod_spec_fn,
         final_norm):
    t = h2.shape[0]
    tm = expert_tile
    n_rows = ((2 * t + N_EXPERTS * (tm - 1)) // tm + 1) * tm
    cnt = counts[:N_EXPERTS, 0].astype(jnp.int32)
    dest, tile_expert, ends, n_valid = _plan(cnt, idx, rank, tm, n_rows // tm)
    dest = jnp.transpose(dest.reshape(2, t // tok_tile, tok_tile), (1, 0, 2)).reshape(-1)
    xs = _dispatch(ends, dest, h2, n_rows, tok_tile, tm)
    ys = _experts(tile_expert, n_valid, xs, ew["w1"], ew["w3"], ew["w2"], layer, tm)
    return _combine(dest, ys, x1, wts, mod, fw, tm=tok_tile, per_token=per_token,
                    mod_spec=mod_spec_fn(tok_tile), final_norm=final_norm)


def _pad_lanes(a, n=LANES):
    return jnp.pad(a, [(0, 0)] * (a.ndim - 1) + [(0, n - a.shape[-1])])


def kernel(x_prompt, x_sample, state_ssm, state_conv, state_s5_re, state_s5_im, c_prompt, c_sample, w_in, conv_w, conv_b, dt_bias, a_log, d_ssd, ssd_norm_w, w_out_a, s5_a_re, s5_a_im, s5_log_dt, s5_b_re, s5_b_im, s5_c_re, s5_c_im, s5_d, w_glu, w_out, norm1_w, norm2_w, w_ada, b_ada, w_router, b_router, w1, w3, w2, final_norm_w):
    nb, seq, d = x_prompt.shape
    ns, steps, _ = x_sample.shape
    depth = w_in.shape[0]
    d_in = SSD_HEADS * SSD_HEAD_DIM
    cdim = conv_w.shape[2]
    s5w = S5_GROUPS * S5_GROUP_CH
    s5n = S5_GROUPS * S5_STATE
    k1 = SSD_CONV - 1
    tp = nb * seq
    ts = ns * steps

    mod = _ada_mod(jnp.concatenate([c_prompt, c_sample], axis=0), w_ada, b_ada)
    head_expand = jnp.tile(jnp.repeat(jnp.eye(LANES, SSD_HEADS, dtype=BF16), SSD_HEAD_DIM, axis=1), (3, 1))
    rw = {"w_router": _pad_lanes(w_router), "b_router": _pad_lanes(b_router[None, :]).reshape(LANES, 1)}

    xp = x_prompt.reshape(tp, d)
    xsm = jnp.transpose(x_sample, (1, 0, 2)).reshape(ts, d)
    outs = {k: [] for k in ("ssm_p", "conv_p", "conv_s", "re_p", "re_s", "im_p", "im_s")}
    ssm_s = None
    zero_state = jnp.zeros((nb, s5n), F32)

    for l in range(depth):
        o0 = 0
        o1 = d_in
        o2 = o1 + cdim
        o3 = o2 + SSD_HEADS
        o4 = o3 + s5w
        wl = w_in[l].astype(BF16)
        ws = (wl[:, o0:o1], wl[:, o1:o2], _pad_lanes(wl[:, o2:o3]), wl[:, o3:o4], wl[:, o4:])
        p = {"conv_w": conv_w[l], "conv_b": conv_b[l][None, :], "dt_bias": _pad_lanes(dt_bias[l][None, :]),
             "a_log": _pad_lanes(a_log[l][None, :]),
             "d_skip_e": jnp.repeat(d_ssd[l], SSD_HEAD_DIM)[None, :],
             "ssd_norm_w": ssd_norm_w[l][None, :], "head_expand": head_expand}
        chan_rows = lambda a, perm: jnp.transpose(a, perm).reshape(S5_GROUP_CH, s5n)
        ab, wb, c_re_t, c_im_t = _s5_params(
            s5_a_re[l].reshape(1, s5n), s5_a_im[l].reshape(1, s5n), jnp.repeat(s5_log_dt[l], S5_STATE)[None, :],
            chan_rows(s5_b_re[l], (2, 0, 1)), chan_rows(s5_b_im[l], (2, 0, 1)),
            chan_rows(s5_c_re[l], (1, 0, 2)), chan_rows(s5_c_im[l], (1, 0, 2)))
        sp = {"ab": ab, "wb": wb, "c_re_t": c_re_t, "c_im_t": c_im_t,
              "d": s5_d[l][None, :], "w_glu": w_glu[l].astype(BF16)}
        lw = {"w_out_a": w_out_a[l].astype(BF16), "w_out": w_out[l].astype(BF16), "norm2_w": norm2_w[l][None, :]}
        ew = {"w1": w1, "w3": w3, "w2": w2}
        final = l == depth - 1
        fw = final_norm_w[None, :]

        mod_p = mod[l, :nb].reshape(nb, 6, d)
        mod_p_spec = lambda tm: pl.BlockSpec((None, 6, d), lambda i: (i // (seq // tm), 0, 0))
        u, gate, yn, nconv, nssm = _inproj_ssd(xp, mod_p, norm1_w[l][None, :], ws, p, nb, seq, ROW_TILE)
        seq3 = lambda a: a.reshape((nb, seq) + a.shape[1:])
        nre, nim, x1, h2, meta, counts = _s5_merge(seq3(u), seq3(yn), seq3(gate), seq3(xp), mod_p, zero_state,
                                                   sp, lw, rw, S5_STEPS)
        meta = meta.reshape(tp, LANES)
        idx, rank = _meta_rows(meta)
        xp = _moe(h2.reshape(tp, d // LANES, LANES), idx, rank, counts, x1.reshape(tp, d), meta, mod_p, fw, ew, l,
                  tok_tile=MOE_TILE, expert_tile=EXPERT_TILE, per_token=False, mod_spec_fn=mod_p_spec,
                  final_norm=final)
        outs["ssm_p"].append(nssm)
        outs["conv_p"].append(nconv)
        outs["re_p"].append(nre.reshape(nb, S5_GROUPS, S5_STATE))
        outs["im_p"].append(nim.reshape(nb, S5_GROUPS, S5_STATE))

        mod_s = jnp.transpose(mod[l, nb:].reshape(ns, 6, d), (1, 0, 2))
        mod_s_spec = lambda tm: pl.BlockSpec((6, tm, d), lambda i: (0, i % (ns // tm), 0))
        z, xbc, dtr, u, gate = _inproj(
            xsm, mod_s, norm1_w[l][None, :], ws, tm=ns, per_token=True, mod_spec=mod_s_spec(ns))
        conv0_tm = jnp.transpose(state_conv[l], (1, 0, 2)).reshape(k1 * ns, cdim)
        yn, nconv_tm, ssm_s = _ssd_sample(xbc, dtr, z, conv0_tm, state_ssm, ssm_s, l, p, ns, steps)
        yb, nre, nim = _s5(u, state_s5_re[l].reshape(ns, s5n), state_s5_im[l].reshape(ns, s5n), sp, ns, steps,
                           False)
        x1, h2, meta, counts = _merge_router(
            yn, yb, gate, xsm, mod_s, lw, rw, tm=ns, per_token=True, mod_spec=mod_s_spec(ns))
        idx, rank = _meta_rows(meta)
        xsm = _moe(h2, idx, rank, counts, x1, meta, mod_s, fw, ew, l, tok_tile=ns, expert_tile=ns, per_token=True,
                   mod_spec_fn=mod_s_spec, final_norm=final)
        outs["conv_s"].append(jnp.transpose(nconv_tm.reshape(k1, ns, cdim), (1, 0, 2)))
        outs["re_s"].append(nre.reshape(ns, S5_GROUPS, S5_STATE))
        outs["im_s"].append(nim.reshape(ns, S5_GROUPS, S5_STATE))

    y_prompt = xp.reshape(nb, seq, d)
    y_sample = jnp.transpose(xsm.reshape(steps, ns, d), (1, 0, 2))
    st = lambda k: jnp.stack(outs[k])
    return (y_prompt, y_sample, st("ssm_p"), ssm_s, st("conv_p"), st("conv_s"),
            st("re_p"), st("re_s"), st("im_p"), st("im_s"))
```

```python
import functools

import jax
import jax.numpy as jnp
from jax import lax
from jax.experimental import pallas as pl
from jax.experimental.pallas import tpu as pltpu

F32 = jnp.float32
BF16 = jnp.bfloat16
EPS = 1e-6

SSD_HEADS = 16
SSD_HEAD_DIM = 64
SSD_GROUPS = 2
SSD_STATE = 128
SSD_CONV = 4
SSD_CHUNK = 128
S5_GROUPS = 32
S5_GROUP_CH = 16
S5_STATE = 64
N_EXPERTS = 16
EXPERTS_PER_GROUP = 4
N_EXPERT_GROUPS = 4

LANES = 128
SUBLANES = 8
MXU_TILE = 256
VMEM_LIMIT_BYTES = 56 * 1024 * 1024

ROW_TILE = 512
MOE_TILE = 256
EXPERT_TILE = 512
S5_STEPS = 64
S5_SCAN_LANES = 512
ROUTER_SUBTILES = 2
SSD_SAMPLE_SEQS = 4


def _cparams(sem):
    return pltpu.CompilerParams(dimension_semantics=sem, vmem_limit_bytes=VMEM_LIMIT_BYTES)


def _bdot(a, b):
    return jnp.dot(a.astype(BF16), b.astype(BF16), preferred_element_type=F32)


def _bdot_nt(a, b):
    return lax.dot_general(a.astype(BF16), b.astype(BF16), (((1,), (1,)), ((), ())),
                           preferred_element_type=F32)


def _split3(x):
    hi = x.astype(BF16)
    r1 = x - hi.astype(F32)
    mid = r1.astype(BF16)
    lo = (r1 - mid.astype(F32)).astype(BF16)
    return hi, mid, lo


def _dot3_lhs(x, m3):
    return jnp.dot(jnp.concatenate(_split3(x), axis=1), m3, preferred_element_type=F32)


def _dot3_rhs(m, x):
    return jnp.dot(jnp.concatenate([m] * 3, axis=1), jnp.concatenate(_split3(x), axis=0),
                   preferred_element_type=F32)


def _sigmoid(x):
    return jax.nn.sigmoid(x)


def _silu(x):
    return x * _sigmoid(x)


def _softplus(x):
    return jnp.maximum(x, 0.0) + jnp.log1p(jnp.exp(-jnp.abs(x)))


def _gelu_tanh(x):
    c = 0.7978845608028654
    return 0.5 * x * (1.0 + jnp.tanh(c * (x + 0.044715 * (x * x * x))))


def _rms(x):
    return x * lax.rsqrt(jnp.mean(x * x, axis=-1, keepdims=True) + EPS)


def _ada_kernel(c_ref, w_ref, b_ref, o_ref):
    o_ref[0] = _bdot(_silu(c_ref[...]), w_ref[0]) + b_ref[0]


def _ada_mod(c_all, w_ada, b_ada):
    depth, d, n = w_ada.shape
    nseq = c_all.shape[0]
    tn = 1536
    return pl.pallas_call(
        _ada_kernel,
        grid=(depth, n // tn),
        in_specs=[pl.BlockSpec((nseq, d), lambda l, j: (0, 0)),
                  pl.BlockSpec((1, d, tn), lambda l, j: (l, 0, j)),
                  pl.BlockSpec((1, 1, tn), lambda l, j: (l, 0, j))],
        out_specs=pl.BlockSpec((1, nseq, tn), lambda l, j: (l, 0, j)),
        out_shape=jax.ShapeDtypeStruct((depth, nseq, n), F32),
        compiler_params=_cparams(("arbitrary", "arbitrary")),
        name="ada_mod",
    )(c_all, w_ada, b_ada.reshape(depth, 1, n))


def _mod_rows(mod_ref, j, per_token, rows_per_seq=None):
    if per_token:
        return mod_ref[j]
    if rows_per_seq is not None:
        m = mod_ref[:, j, :]
        return jnp.broadcast_to(m[:, None, :], (m.shape[0], rows_per_seq, m.shape[1])).reshape(-1, m.shape[1])
    return mod_ref[j:j + 1, :]


def _inproj_kernel(x_ref, mod_ref, nw_ref, wz_ref, wx_ref, wdt_ref, wu_ref, wg_ref,
                   z_ref, xbc_ref, dt_ref, u_ref, g_ref, *, per_token):
    sh = _mod_rows(mod_ref, 0, per_token)
    sc = _mod_rows(mod_ref, 1, per_token)
    h = (_rms(x_ref[...]) * nw_ref[...] * (1.0 + sc) + sh).astype(BF16)
    d = functools.partial(jnp.dot, preferred_element_type=F32)
    z_ref[...] = d(h, wz_ref[...])
    xbc_ref[...] = d(h, wx_ref[...])
    dt_ref[...] = d(h, wdt_ref[...])
    u_ref[...] = d(h, wu_ref[...])
    g_ref[...] = d(h, wg_ref[...])


def _const_spec(shape):
    nd = len(shape)
    return pl.BlockSpec(shape, lambda *_: (0,) * nd)


def _inproj(x, mod, nw, ws, *, tm, per_token, mod_spec):
    t, d = x.shape
    wz, wx, wdt, wu, wg = ws
    row = lambda n: pl.BlockSpec((tm, n), lambda i: (i, 0))
    return pl.pallas_call(
        functools.partial(_inproj_kernel, per_token=per_token),
        grid=(t // tm,),
        in_specs=[row(d), mod_spec, _const_spec((1, d)),
                  _const_spec(wz.shape), _const_spec(wx.shape), _const_spec(wdt.shape),
                  _const_spec(wu.shape), _const_spec(wg.shape)],
        out_specs=[row(wz.shape[1]), row(wx.shape[1]), row(wdt.shape[1]), row(wu.shape[1]), row(wg.shape[1])],
        out_shape=[jax.ShapeDtypeStruct((t, wz.shape[1]), F32),
                   jax.ShapeDtypeStruct((t, wx.shape[1]), F32),
                   jax.ShapeDtypeStruct((t, wdt.shape[1]), F32),
                   jax.ShapeDtypeStruct((t, wu.shape[1]), F32),
                   jax.ShapeDtypeStruct((t, wg.shape[1]), F32)],
        compiler_params=_cparams(("arbitrary",)),
        name="inproj",
    )(x, mod, nw, wz, wx, wdt, wu, wg)


def _gated_group_norm(y, z, nw):
    y = y * _silu(z)
    half = y.shape[1] // SSD_GROUPS
    parts = [_rms(y[:, g * half:(g + 1) * half]) for g in range(SSD_GROUPS)]
    return jnp.concatenate(parts, axis=1) * nw


def _ssd_chunk(first, last, xbc_ref, dt_ref, z_ref, cw_ref, cb_ref, dtb_ref, alog_ref, dsk_ref, nw_ref,
               e_ref, yn_ref, conv_ref, ssm_ref, ext_ref, st_ref):
    q = SSD_CHUNK
    d_in = SSD_HEADS * SSD_HEAD_DIM
    gw = SSD_STATE
    hpg = SSD_HEADS // SSD_GROUPS
    grows = hpg * SSD_HEAD_DIM

    if first is not None:
        @pl.when(first)
        def _():
            ext_ref[...] = jnp.zeros(ext_ref.shape, F32)
            st_ref[...] = jnp.zeros(st_ref.shape, F32)

    xbc = xbc_ref[...]
    sub = lax.broadcasted_iota(jnp.int32, (SUBLANES, xbc.shape[1]), 0)
    taps = [cw_ref[k:k + 1, :] for k in range(SSD_CONV)]
    prev_rolled = [pltpu.roll(ext_ref[...], j, axis=0) for j in range(1, SSD_CONV)]
    acc_tiles = []
    for r0 in range(0, q, SUBLANES):
        cur = xbc[r0:r0 + SUBLANES, :]
        rolled = [pltpu.roll(cur, j, axis=0) for j in range(1, SSD_CONV)]
        acc = cb_ref[...] + taps[SSD_CONV - 1] * cur
        for j in range(1, SSD_CONV):
            acc = acc + taps[SSD_CONV - 1 - j] * jnp.where(sub < j, prev_rolled[j - 1], rolled[j - 1])
        acc_tiles.append(acc)
        prev_rolled = rolled
    ext_ref[...] = xbc[q - SUBLANES:q, :]
    v = _silu(jnp.concatenate(acc_tiles, axis=0))
    xs = v[:, :d_in]
    bm = v[:, d_in:d_in + SSD_GROUPS * gw]
    cm = v[:, d_in + SSD_GROUPS * gw:]

    dt = _softplus(dt_ref[...] + dtb_ref[...])
    a = -jnp.exp(alog_ref[...])
    da = dt * a
    rid = lax.broadcasted_iota(jnp.int32, (q, q), 0)
    cid = lax.broadcasted_iota(jnp.int32, (q, q), 1)
    causal = rid >= cid
    tri = jnp.where(causal, 1.0, 0.0).astype(BF16)
    a_cs = _dot3_rhs(tri, da)
    a_cs_t = a_cs.T
    a_last = a_cs[q - 1:q, :]
    per_head = jnp.concatenate([dt, jnp.exp(a_last - a_cs) * dt, jnp.exp(a_cs)], axis=0)
    per_chan = _dot3_lhs(per_head, e_ref[...])
    x_dt = xs * per_chan[0:q]
    x_end = xs * per_chan[q:2 * q]
    eacs_e = per_chan[2 * q:3 * q]
    x_end_t = x_end.T.astype(BF16)
    lane = lax.broadcasted_iota(jnp.int32, (q, 2 * SSD_HEAD_DIM), 1)
    first = lane < SSD_HEAD_DIM

    y_diag = []
    y_off = []
    for g in range(SSD_GROUPS):
        bg = bm[:, g * gw:(g + 1) * gw].astype(BF16)
        cg = cm[:, g * gw:(g + 1) * gw].astype(BF16)
        cb = _bdot_nt(cg, bg)
        sg = st_ref[g * grows:(g + 1) * grows, :]
        y_off.append(_bdot_nt(cg, sg))
        for pair in range(hpg // 2):
            ms = []
            for r in (2 * pair, 2 * pair + 1):
                h = g * hpg + r
                diff = a_cs[:, h:h + 1] - a_cs_t[h:h + 1, :]
                decay = jnp.where(causal, jnp.exp(jnp.where(causal, diff, 0.0)), 0.0)
                ms.append((cb * decay).astype(BF16))
            h0 = g * hpg + 2 * pair
            xp = x_dt[:, h0 * SSD_HEAD_DIM:(h0 + 2) * SSD_HEAD_DIM]
            rhs = jnp.concatenate([jnp.where(first, xp, 0.0), jnp.where(first, 0.0, xp)], axis=0)
            y_diag.append(jnp.dot(jnp.concatenate(ms, axis=1), rhs.astype(BF16), preferred_element_type=F32))
        contrib = jnp.dot(x_end_t[g * grows:(g + 1) * grows, :], bg, preferred_element_type=F32)
        for r in range(hpg):
            h = g * hpg + r
            rows = slice(h * SSD_HEAD_DIM, (h + 1) * SSD_HEAD_DIM)
            dec = jnp.exp(a_cs_t[h:h + 1, q - 1:q])
            st_ref[rows, :] = st_ref[rows, :] * dec + contrib[r * SSD_HEAD_DIM:(r + 1) * SSD_HEAD_DIM, :]

    y = (jnp.concatenate(y_diag, axis=1) + jnp.concatenate(y_off, axis=1) * eacs_e
         + dsk_ref[...] * xs)
    yn_ref[...] = _gated_group_norm(y, z_ref[...], nw_ref[...]).astype(BF16)

    if last is not None:
        @pl.when(last)
        def _():
            conv_ref[0] = ext_ref[pl.ds(SUBLANES - (SSD_CONV - 1), SSD_CONV - 1), :]
            ssm_ref[0] = st_ref[...].reshape(SSD_HEADS, SSD_HEAD_DIM, SSD_STATE)


def _inproj_ssd_kernel(x_ref, mod_ref, nw1_ref, wz_ref, wx_ref, wdt_ref, wu_ref, wg_ref,
                       cw_ref, cb_ref, dtb_ref, alog_ref, dsk_ref, nw_ref, e_ref,
                       u_ref, g_ref, yn_ref, conv_ref, ssm_ref,
                       z_s, xbc_s, dt_s, ext_ref, st_ref):
    j = pl.program_id(1)
    sh = mod_ref[0:1, :]
    sc = mod_ref[1:2, :]
    h = (_rms(x_ref[...]) * nw1_ref[...] * (1.0 + sc) + sh).astype(BF16)
    d = functools.partial(jnp.dot, preferred_element_type=F32)
    z_s[...] = d(h, wz_ref[...])
    xbc_s[...] = d(h, wx_ref[...])
    dt_s[...] = d(h, wdt_ref[...])
    u_ref[...] = d(h, wu_ref[...])
    g_ref[...] = d(h, wg_ref[...])
    n_chunks = x_ref.shape[0] // SSD_CHUNK
    for k in range(n_chunks):
        rows = pl.ds(k * SSD_CHUNK, SSD_CHUNK)
        first = (j == 0) if k == 0 else None
        last = (j == pl.num_programs(1) - 1) if k == n_chunks - 1 else None
        _ssd_chunk(first, last, xbc_s.at[rows], dt_s.at[rows], z_s.at[rows], cw_ref, cb_ref, dtb_ref, alog_ref,
                   dsk_ref, nw_ref, e_ref, yn_ref.at[rows], conv_ref, ssm_ref, ext_ref, st_ref)


def _inproj_ssd(x, mod, nw1, ws, p, nb, seq, tm):
    t, d = x.shape
    wz, wx, wdt, wu, wg = ws
    cdim = wx.shape[1]
    d_in = wz.shape[1]
    tiles = seq // tm
    row = lambda n: pl.BlockSpec((tm, n), lambda b, j: (b * tiles + j, 0))
    consts = [nw1, wz, wx, wdt, wu, wg, p["conv_w"], p["conv_b"], p["dt_bias"], p["a_log"], p["d_skip_e"],
              p["ssd_norm_w"], p["head_expand"]]
    return pl.pallas_call(
        _inproj_ssd_kernel,
        grid=(nb, tiles),
        in_specs=[row(d), pl.BlockSpec((None, 6, d), lambda b, j: (b, 0, 0))]
                 + [_resident_spec(a.shape) for a in consts],
        out_specs=[row(wu.shape[1]), row(wg.shape[1]), row(d_in),
                   pl.BlockSpec((1, SSD_CONV - 1, cdim), lambda b, j: (b, 0, 0)),
                   pl.BlockSpec((1, SSD_HEADS, SSD_HEAD_DIM, SSD_STATE), lambda b, j: (b, 0, 0, 0))],
        out_shape=[jax.ShapeDtypeStruct((t, wu.shape[1]), F32),
                   jax.ShapeDtypeStruct((t, wg.shape[1]), F32),
                   jax.ShapeDtypeStruct((t, d_in), BF16),
                   jax.ShapeDtypeStruct((nb, SSD_CONV - 1, cdim), F32),
                   jax.ShapeDtypeStruct((nb, SSD_HEADS, SSD_HEAD_DIM, SSD_STATE), F32)],
        scratch_shapes=[pltpu.VMEM((tm, d_in), F32), pltpu.VMEM((tm, cdim), F32), pltpu.VMEM((tm, LANES), F32),
                        pltpu.VMEM((SUBLANES, cdim), F32),
                        pltpu.VMEM((d_in, SSD_STATE), F32)],
        compiler_params=_cparams(("arbitrary", "arbitrary")),
        name="inproj_ssd",
    )(x, mod, *consts)


def _ssd_sample_a_kernel(xbc_ref, conv0_ref, dt_ref, cw_ref, cb_ref, dtb_ref, alog_ref, dsk_ref, e_ref,
                         ydiag_ref, eacs_ref, wt_ref, bs_ref, c8_ref, dec_ref, nconv_ref, *, steps, nseq):
    d_in = SSD_HEADS * SSD_HEAD_DIM
    gw = SSD_STATE
    hpg = SSD_HEADS // SSD_GROUPS
    k1 = SSD_CONV - 1
    slab = lambda ref, i: ref[i * nseq:(i + 1) * nseq, :]
    xp = [slab(conv0_ref, i) for i in range(k1)] + [slab(xbc_ref, i) for i in range(steps)]
    for i in range(k1):
        nconv_ref[i * nseq:(i + 1) * nseq, :] = xp[steps + i]
    e = e_ref[...]
    a = -jnp.exp(alog_ref[...])
    head = lax.broadcasted_iota(jnp.int32, (nseq, LANES), 1)
    xs, bm, cm, dt, a_cs = [], [], [], [], []
    run = jnp.zeros((nseq, LANES), F32)
    for t in range(steps):
        acc = cb_ref[...]
        for k in range(SSD_CONV):
            acc = acc + cw_ref[k:k + 1, :] * xp[t + k]
        v = _silu(acc)
        xs.append(v[:, :d_in])
        bm.append(v[:, d_in:d_in + SSD_GROUPS * gw])
        cm.append(v[:, d_in + SSD_GROUPS * gw:])
        dt.append(_softplus(slab(dt_ref, t) + dtb_ref[...]))
        run = run + dt[t] * a
        a_cs.append(run)
    dec_ref[...] = jnp.exp(a_cs[steps - 1])
    for t in range(steps):
        y = dsk_ref[...] * xs[t]
        for s in range(t + 1):
            cbs = [jnp.sum(cm[t][:, g * gw:(g + 1) * gw] * bm[s][:, g * gw:(g + 1) * gw],
                           axis=-1, keepdims=True) for g in range(SSD_GROUPS)]
            coef = jnp.exp(a_cs[t] - a_cs[s]) * dt[s] * jnp.where(head < hpg, cbs[0], cbs[1])
            y = y + _dot3_lhs(coef, e) * xs[s]
        ydiag_ref[t * nseq:(t + 1) * nseq, :] = y
        eacs_ref[t * nseq:(t + 1) * nseq, :] = _dot3_lhs(jnp.exp(a_cs[t]), e)
        w = xs[t] * _dot3_lhs(jnp.exp(a_cs[steps - 1] - a_cs[t]) * dt[t], e)
        wt_ref[:, t * nseq:(t + 1) * nseq] = w.T.astype(BF16)
        bs_ref[t * nseq:(t + 1) * nseq, :] = bm[t]
        for g in range(SSD_GROUPS):
            j = g * steps + t
            c8_ref[j * nseq:(j + 1) * nseq, :] = cm[t][:, g * gw:(g + 1) * gw]


def _ssd_sample_b_kernel(dec_ref, h0_ref, wt_ref, bs_ref, c8_ref, *rest, steps, nseq, layer):
    yoff_ref, ssm_ref = rest[-2:]
    if layer:
        ssm_ref[0:layer] = rest[0][...]
    hpg = SSD_HEADS // SSD_GROUPS
    grows = hpg * SSD_HEAD_DIM
    gw = SSD_STATE
    rid = lax.broadcasted_iota(jnp.int32, (steps * nseq, gw), 0)
    for s in range(h0_ref.shape[0]):
        b = pl.program_id(0) * h0_ref.shape[0] + s
        h0 = h0_ref[s].reshape(SSD_HEADS * SSD_HEAD_DIM, SSD_STATE)
        c8 = c8_ref[pl.ds(b, SSD_GROUPS * steps, stride=nseq), :]
        yoff_ref[s] = _bdot_nt(c8, h0)
        mine = rid == b
        for t in range(1, steps):
            mine = jnp.logical_or(mine, rid == b + t * nseq)
        for g in range(SSD_GROUPS):
            bsel = jnp.where(mine, bs_ref[:, g * gw:(g + 1) * gw], 0.0).astype(BF16)
            contrib = jnp.dot(wt_ref[g * grows:(g + 1) * grows, :], bsel, preferred_element_type=F32)
            for r in range(hpg):
                h = g * hpg + r
                rows = slice(h * SSD_HEAD_DIM, (h + 1) * SSD_HEAD_DIM)
                ssm_ref[layer, s, h] = (h0[rows, :] * dec_ref[b, h]
                                        + contrib[r * SSD_HEAD_DIM:(r + 1) * SSD_HEAD_DIM, :])


def _ssd_sample_c_kernel(ydiag_ref, yoff_ref, eacs_ref, z_ref, nw_ref, yn_ref, *, steps, nseq):
    d_in = SSD_HEADS * SSD_HEAD_DIM
    half = d_in // SSD_GROUPS
    for t in range(steps):
        lo = yoff_ref[:, t * d_in:t * d_in + half]
        hi = yoff_ref[:, (steps + t) * d_in + half:(steps + t + 1) * d_in]
        rows = slice(t * nseq, (t + 1) * nseq)
        y = ydiag_ref[rows, :] + jnp.concatenate([lo, hi], axis=1) * eacs_ref[rows, :]
        yn_ref[rows, :] = _gated_group_norm(y, z_ref[rows, :], nw_ref[...]).astype(BF16)


def _ssd_sample(xbc, dt_raw, z, conv0_tm, ssm_all, ssm_done, layer, p, nseq, steps):
    t, cdim = xbc.shape
    d_in = z.shape[1]
    k1 = SSD_CONV - 1
    nj = SSD_GROUPS * steps
    a_out = pl.pallas_call(
        functools.partial(_ssd_sample_a_kernel, steps=steps, nseq=nseq),
        out_shape=[jax.ShapeDtypeStruct((t, d_in), F32),
                   jax.ShapeDtypeStruct((t, d_in), F32),
                   jax.ShapeDtypeStruct((d_in, t), BF16),
                   jax.ShapeDtypeStruct((t, SSD_GROUPS * SSD_STATE), F32),
                   jax.ShapeDtypeStruct((nj * nseq, SSD_STATE), F32),
                   jax.ShapeDtypeStruct((nseq, LANES), F32),
                   jax.ShapeDtypeStruct((k1 * nseq, cdim), F32)],
        compiler_params=pltpu.CompilerParams(vmem_limit_bytes=VMEM_LIMIT_BYTES),
        name="ssd_sample_a",
    )(xbc, conv0_tm, dt_raw, p["conv_w"], p["conv_b"], p["dt_bias"], p["a_log"], p["d_skip_e"],
      p["head_expand"])
    ydiag, eacs_e, wt, bs, c8, dec, nconv = a_out
    state_blk = (SSD_HEADS, SSD_HEAD_DIM, SSD_STATE)
    operands = [dec[:, :SSD_HEADS], ssm_all, wt, bs, c8]
    sb = SSD_SAMPLE_SEQS
    in_specs = [pl.BlockSpec(memory_space=pltpu.SMEM),
                pl.BlockSpec((None, sb) + state_blk, lambda b: (layer, b, 0, 0, 0)),
                _const_spec(wt.shape), _const_spec(bs.shape), _const_spec(c8.shape)]
    if layer:
        operands.append(ssm_done)
        in_specs.append(pl.BlockSpec((layer, sb) + state_blk, lambda b: (0, b, 0, 0, 0)))
    yoff, nssm = pl.pallas_call(
        functools.partial(_ssd_sample_b_kernel, steps=steps, nseq=nseq, layer=layer),
        grid=(nseq // sb,),
        in_specs=in_specs,
        out_specs=[pl.BlockSpec((sb, nj, d_in), lambda b: (b, 0, 0)),
                   pl.BlockSpec((layer + 1, sb) + state_blk, lambda b: (0, b, 0, 0, 0))],
        out_shape=[jax.ShapeDtypeStruct((nseq, nj, d_in), F32),
                   jax.ShapeDtypeStruct((layer + 1, nseq) + state_blk, F32)],
        compiler_params=_cparams(("arbitrary",)),
        name="ssd_sample_b",
    )(*operands)
    yn = pl.pallas_call(
        functools.partial(_ssd_sample_c_kernel, steps=steps, nseq=nseq),
        out_shape=jax.ShapeDtypeStruct((t, d_in), BF16),
        compiler_params=pltpu.CompilerParams(vmem_limit_bytes=VMEM_LIMIT_BYTES),
        name="ssd_sample_c",
    )(ydiag, yoff.reshape(nseq, nj * d_in), eacs_e, z, p["ssd_norm_w"])
    return yn, nconv, nssm


def _s5_param_kernel(are_ref, aim_ref, ldt_ref, bre_ref, bim_ref, cre_ref, cim_ref,
                     ab_ref, wb_ref, cret_ref, cimt_ref):
    n = are_ref.shape[1]
    w = wb_ref.shape[0]
    a_re = are_ref[...]
    a_im = aim_ref[...]
    dt = jnp.exp(ldt_ref[...])
    mag = jnp.exp(dt * a_re)
    ab_re = mag * jnp.cos(dt * a_im)
    ab_im = mag * jnp.sin(dt * a_im)
    den = a_re * a_re + a_im * a_im
    f_re = ((ab_re - 1.0) * a_re + ab_im * a_im) / den
    f_im = (ab_im * a_re - (ab_re - 1.0) * a_im) / den
    ab_ref[0:1, :] = ab_re
    ab_ref[1:2, :] = ab_im
    b_re = bre_ref[...]
    b_im = bim_ref[...]
    row_g = lax.shift_right_logical(lax.broadcasted_iota(jnp.int32, (w, n), 0), S5_GROUP_CH.bit_length() - 1)
    col_g = lax.shift_right_logical(lax.broadcasted_iota(jnp.int32, (w, n), 1), S5_STATE.bit_length() - 1)
    same = row_g == col_g

    def spread(rows):
        return jnp.where(same, jnp.concatenate([rows] * S5_GROUPS, axis=0), 0.0).astype(BF16)

    wb_ref[:, 0:n] = spread(f_re * b_re - f_im * b_im)
    wb_ref[:, n:2 * n] = spread(f_re * b_im + f_im * b_re)
    cret_ref[...] = spread(cre_ref[...])
    cimt_ref[...] = spread(cim_ref[...])


def _s5_params(a_re, a_im, ldt, bt_re, bt_im, ct_re, ct_im):
    n = a_re.shape[1]
    w = S5_GROUPS * S5_GROUP_CH
    return pl.pallas_call(
        _s5_param_kernel,
        out_shape=[jax.ShapeDtypeStruct((2, n), F32),
                   jax.ShapeDtypeStruct((w, 2 * n), BF16),
                   jax.ShapeDtypeStruct((w, n), BF16),
                   jax.ShapeDtypeStruct((w, n), BF16)],
        compiler_params=pltpu.CompilerParams(vmem_limit_bytes=VMEM_LIMIT_BYTES),
        name="s5_params",
    )(a_re, a_im, ldt, bt_re, bt_im, ct_re, ct_im)


def _s5_kernel(u_ref, h0re_ref, h0im_ref, ab_ref, wb_ref, cret_ref, cimt_ref, d_ref, wglu_ref,
               yb_ref, nre_ref, nim_ref, xh_ref, st_ref, *, rows_per_step, steps, batch_major):
    i = pl.program_id(0)
    n = ab_ref.shape[1]
    w = wb_ref.shape[0]
    r = rows_per_step

    @pl.when(i == 0)
    def _():
        st_ref[0] = h0re_ref[...]
        st_ref[1] = h0im_ref[...]

    if batch_major:
        u = pltpu.einshape("btd->(tb)d", u_ref[...])
    else:
        u = u_ref[...]
    ub = u.astype(BF16)
    for j in range(2 * n // MXU_TILE):
        c0 = (j * MXU_TILE) % n
        k0 = (c0 // S5_STATE * S5_GROUP_CH) // LANES * LANES
        cols = slice(j * MXU_TILE, (j + 1) * MXU_TILE)
        xh_ref[:, cols] = jnp.dot(ub[:, k0:k0 + LANES], wb_ref[k0:k0 + LANES, cols], preferred_element_type=F32)
    for c0 in range(0, n, S5_SCAN_LANES):
        re = slice(c0, c0 + S5_SCAN_LANES)
        im = slice(n + c0, n + c0 + S5_SCAN_LANES)
        a_re = jnp.broadcast_to(ab_ref[0:1, re], (r, S5_SCAN_LANES))
        a_im = jnp.broadcast_to(ab_ref[1:2, re], (r, S5_SCAN_LANES))

        def step(t, carry, re=re, im=im, a_re=a_re, a_im=a_im):
            h_re, h_im = carry
            rows = pl.ds(pl.multiple_of(t * r, r), r)
            n_re = a_re * h_re - a_im * h_im + xh_ref[rows, re]
            n_im = a_re * h_im + a_im * h_re + xh_ref[rows, im]
            xh_ref[rows, re] = n_re
            xh_ref[rows, im] = n_im
            return n_re, n_im

        carry = (st_ref[0, :, re], st_ref[1, :, re])
        if steps <= 8:
            for t in range(steps):
                carry = step(t, carry)
        else:
            carry = lax.fori_loop(0, steps, step, carry, unroll=4)
        st_ref[0, :, re] = carry[0]
        st_ref[1, :, re] = carry[1]

    n_ct = w // MXU_TILE
    ys = []
    for j in range(n_ct):
        rows = slice(j * MXU_TILE, (j + 1) * MXU_TILE)
        ks = slice(j * (n // n_ct), (j + 1) * (n // n_ct))
        ks_im = slice(n + j * (n // n_ct), n + (j + 1) * (n // n_ct))
        ys.append(_bdot_nt(xh_ref[:, ks], cret_ref[rows, ks]) - _bdot_nt(xh_ref[:, ks_im], cimt_ref[rows, ks]))
    y = jnp.concatenate(ys, axis=1) + d_ref[...] * u
    pre = _bdot(_gelu_tanh(y), wglu_ref[...])
    half = pre.shape[1] // 2
    yb = pre[:, :half] * _sigmoid(pre[:, half:])
    if batch_major:
        yb_ref[...] = pltpu.einshape("(tb)d->btd", yb, b=r)
    else:
        yb_ref[...] = yb

    @pl.when(i == pl.num_programs(0) - 1)
    def _():
        nre_ref[...] = st_ref[0]
        nim_ref[...] = st_ref[1]


def _s5(u, h0_re, h0_im, sp, rows_per_step, steps, batch_major):
    n = sp["ab"].shape[1]
    w = u.shape[-1]
    tm = rows_per_step * steps
    d_model = sp["w_glu"].shape[1] // 2
    if batch_major:
        nb, seq, _ = u.shape
        grid = (seq // steps,)
        u_spec = pl.BlockSpec((nb, steps, w), lambda i: (0, i, 0))
        yb_spec = pl.BlockSpec((nb, steps, d_model), lambda i: (0, i, 0))
        yb_shape = (nb, seq, d_model)
    else:
        grid = (u.shape[0] // tm,)
        u_spec = pl.BlockSpec((tm, w), lambda i: (i, 0))
        yb_spec = pl.BlockSpec((tm, d_model), lambda i: (i, 0))
        yb_shape = (u.shape[0], d_model)
    return pl.pallas_call(
        functools.partial(_s5_kernel, rows_per_step=rows_per_step, steps=steps, batch_major=batch_major),
        grid=grid,
        in_specs=[u_spec,
                  _const_spec(h0_re.shape), _const_spec(h0_im.shape), _const_spec(sp["ab"].shape),
                  _const_spec(sp["wb"].shape), _const_spec(sp["c_re_t"].shape), _const_spec(sp["c_im_t"].shape),
                  _const_spec((1, w)), _const_spec(sp["w_glu"].shape)],
        out_specs=[yb_spec, _const_spec(h0_re.shape), _const_spec(h0_im.shape)],
        out_shape=[jax.ShapeDtypeStruct(yb_shape, F32),
                   jax.ShapeDtypeStruct(h0_re.shape, F32),
                   jax.ShapeDtypeStruct(h0_im.shape, F32)],
        scratch_shapes=[pltpu.VMEM((tm, 2 * n), F32),
                        pltpu.VMEM((2, rows_per_step, n), F32)],
        compiler_params=_cparams(("arbitrary",)),
        name="s5_scan",
    )(u, h0_re, h0_im, sp["ab"], sp["wb"], sp["c_re_t"], sp["c_im_t"], sp["d"], sp["w_glu"])


def _merge_router_steps(i, row_refs, mod_ref, weight_refs, cnt_ref, run_ref, *, per_token, rows_per_seq):
    @pl.when(i == 0)
    def _():
        run_ref[...] = jnp.zeros(run_ref.shape, F32)

    lead = row_refs[3].shape[0]
    rows_each = 1 if rows_per_seq is None else rows_per_seq
    sub = lead // ROUTER_SUBTILES if (lead * rows_each) % (ROUTER_SUBTILES * LANES) == 0 else lead
    for r0 in range(0, lead, sub):
        rows = pl.ds(r0, sub)
        if per_token:
            mod = mod_ref.at[:, rows]
        elif rows_per_seq is not None:
            mod = mod_ref.at[rows]
        else:
            mod = mod_ref
        _route_rows(*[ref.at[rows] for ref in row_refs], mod, *weight_refs, run_ref,
                    per_token=per_token, rows_per_seq=rows_per_seq)

    @pl.when(i == pl.num_programs(0) - 1)
    def _():
        cnt_ref[...] = run_ref[...]


def _merge_router_kernel(yn_ref, yb_ref, gate_ref, x_ref, mod_ref, woa_ref, wo_ref, nw_ref, wr_ref, br_ref,
                         x1_ref, h2_ref, meta_ref, cnt_ref, run_ref, *, per_token):
    _merge_router_steps(pl.program_id(0), (yn_ref, yb_ref, gate_ref, x_ref, x1_ref, h2_ref, meta_ref), mod_ref,
                        (woa_ref, wo_ref, nw_ref, wr_ref, br_ref), cnt_ref, run_ref,
                        per_token=per_token, rows_per_seq=None)


def _rows2d(ref):
    v = ref[...]
    return v.reshape(-1, v.shape[-1]) if v.ndim == 3 else v


def _route_rows(yn_ref, yb_ref, gate_ref, x_ref, x1_ref, h2_ref, meta_ref, mod_ref,
                woa_ref, wo_ref, nw_ref, wr_ref, br_ref, run_ref, *, per_token, rows_per_seq):
    x = _rows2d(x_ref)
    tm, d = x.shape
    ya = jnp.dot(_rows2d(yn_ref), woa_ref[...], preferred_element_type=F32)
    gate = _sigmoid(_rows2d(gate_ref))
    merged = gate[:, :d] * ya + gate[:, d:] * _rows2d(yb_ref)
    mix = _bdot(merged, wo_ref[...])
    g1 = _mod_rows(mod_ref, 2, per_token, rows_per_seq)
    sh2 = _mod_rows(mod_ref, 3, per_token, rows_per_seq)
    sc2 = _mod_rows(mod_ref, 4, per_token, rows_per_seq)
    x1 = x + g1 * mix
    x1_ref[...] = x1.reshape(x1_ref.shape)
    h2 = _rms(x1) * nw_ref[...] * (1.0 + sc2) + sh2
    h2_ref[...] = h2.reshape(tm, d // LANES, LANES).reshape(h2_ref.shape)

    wr = wr_ref[...]
    wr_hi = wr.astype(BF16)
    wr_lo = (wr - wr_hi.astype(F32)).astype(BF16)
    h2_hi = h2.astype(BF16)
    h2_lo = (h2 - h2_hi.astype(F32)).astype(BF16)
    d = functools.partial(jnp.dot, preferred_element_type=F32)
    logits = d(h2_hi, wr_hi) + d(h2_lo, wr_hi) + d(h2_hi, wr_lo)
    lt = logits.T
    erow = lax.broadcasted_iota(jnp.int32, lt.shape, 0)
    lt = jnp.where(erow < N_EXPERTS, lt, -jnp.inf)
    ex = jnp.exp(lt - jnp.max(lt, axis=0, keepdims=True))
    scores = ex / jnp.sum(ex, axis=0, keepdims=True)
    sel = scores + br_ref[...]
    s = [sel[e:e + 1, :] for e in range(N_EXPERTS)]
    p = [scores[e:e + 1, :] for e in range(N_EXPERTS)]

    def group_top2_sum(vals):
        best = None
        for a in range(len(vals)):
            for b in range(a + 1, len(vals)):
                pair = vals[a] + vals[b]
                best = pair if best is None else jnp.maximum(best, pair)
        return best

    gs = [group_top2_sum(s[EXPERTS_PER_GROUP * g:EXPERTS_PER_GROUP * (g + 1)]) for g in range(N_EXPERT_GROUPS)]
    best = gs[0]
    bg = jnp.zeros(best.shape, jnp.int32)
    for g in range(1, N_EXPERT_GROUPS):
        better = gs[g] > best
        bg = jnp.where(better, g, bg)
        best = jnp.where(better, gs[g], best)

    def pick(rows, j):
        out = rows[j]
        for g in range(1, N_EXPERT_GROUPS):
            out = jnp.where(bg == g, rows[EXPERTS_PER_GROUP * g + j], out)
        return out

    cs = [pick(s, j) for j in range(EXPERTS_PER_GROUP)]
    cp = [pick(p, j) for j in range(EXPERTS_PER_GROUP)]

    def argmax_first(vals, skip):
        bv = None
        bi = None
        bw = None
        for j in range(len(vals)):
            v = vals[j] if skip is None else jnp.where(skip == j, -jnp.inf, vals[j])
            if bv is None:
                bv, bi, bw = v, jnp.zeros(v.shape, jnp.int32), cp[j]
            else:
                better = v > bv
                bi = jnp.where(better, j, bi)
                bw = jnp.where(better, cp[j], bw)
                bv = jnp.where(better, v, bv)
        return bi, bw

    i1, w1 = argmax_first(cs, None)
    i2, w2 = argmax_first(cs, i1)
    wsum = w1 + w2
    e1 = bg * EXPERTS_PER_GROUP + i1
    e2 = bg * EXPERTS_PER_GROUP + i2
    oh1 = jnp.where(erow == e1, 1.0, 0.0)
    oh2 = jnp.where(erow == e2, 1.0, 0.0)
    both = oh1 + oh2
    ta = lax.broadcasted_iota(jnp.int32, (tm, tm), 0)
    tb = lax.broadcasted_iota(jnp.int32, (tm, tm), 1)
    earlier = jnp.where(ta < tb, 1.0, 0.0).astype(BF16)
    before = jnp.dot(both.astype(BF16), earlier, preferred_element_type=F32) + run_ref[...]
    r1 = jnp.sum(oh1 * before, axis=0, keepdims=True)
    r2 = jnp.sum(oh2 * before, axis=0, keepdims=True)
    run_ref[...] = run_ref[...] + jnp.sum(both, axis=1, keepdims=True)

    record = [w1 / wsum, w2 / wsum, e1.astype(F32), e2.astype(F32), r1, r2]
    wrow = lax.broadcasted_iota(jnp.int32, lt.shape, 0)
    wmat = jnp.zeros(lt.shape, F32)
    for k, v in enumerate(record):
        wmat = jnp.where(wrow == k, v, wmat)
    meta_ref[...] = wmat.T.reshape(meta_ref.shape)


def _merge_router(yn, yb, gate, x, mod, lw, rw, *, tm, per_token, mod_spec):
    t, d = x.shape
    row = lambda n: pl.BlockSpec((tm, n), lambda i: (i, 0))
    return pl.pallas_call(
        functools.partial(_merge_router_kernel, per_token=per_token),
        grid=(t // tm,),
        in_specs=[row(d), row(d), row(2 * d), row(d), mod_spec,
                  _const_spec((d, d)), _const_spec((d, d)), _const_spec((1, d)),
                  _const_spec((d, LANES)), _const_spec((LANES, 1))],
        out_specs=[row(d), _row_tiles(tm, d), row(LANES), _const_spec((LANES, 1))],
        out_shape=[jax.ShapeDtypeStruct((t, d), F32),
                   jax.ShapeDtypeStruct((t, d // LANES, LANES), F32),
                   jax.ShapeDtypeStruct((t, LANES), F32),
                   jax.ShapeDtypeStruct((LANES, 1), F32)],
        scratch_shapes=[pltpu.VMEM((LANES, 1), F32)],
        compiler_params=_cparams(("arbitrary",)),
        name="merge_router",
    )(yn, yb, gate, x, mod, lw["w_out_a"], lw["w_out"], lw["norm2_w"], rw["w_router"], rw["b_router"])


def _s5_merge_kernel(u_ref, h0re_ref, h0im_ref, ab_ref, wb_ref, cret_ref, cimt_ref, d_ref, wglu_ref,
                     yn_ref, gate_ref, x_ref, mod_ref, woa_ref, wo_ref, nw_ref, wr_ref, br_ref,
                     nre_ref, nim_ref, x1_ref, h2_ref, meta_ref, cnt_ref,
                     xh_ref, st_ref, yb_ref, run_ref, *, steps):
    _s5_kernel(u_ref, h0re_ref, h0im_ref, ab_ref, wb_ref, cret_ref, cimt_ref, d_ref, wglu_ref,
               yb_ref, nre_ref, nim_ref, xh_ref, st_ref, rows_per_step=u_ref.shape[0], steps=steps, batch_major=True)
    _merge_router_steps(pl.program_id(0), (yn_ref, yb_ref, gate_ref, x_ref, x1_ref, h2_ref, meta_ref), mod_ref,
                        (woa_ref, wo_ref, nw_ref, wr_ref, br_ref), cnt_ref, run_ref,
                        per_token=False, rows_per_seq=steps)


def _resident_spec(shape):
    nd = len(shape)
    return pl.BlockSpec(shape, lambda *_: (0,) * nd, pipeline_mode=pl.Buffered(1))


def _s5_merge(u, yn, gate, x, mod, zero_state, sp, lw, rw, steps):
    nb, seq, w = u.shape
    d = x.shape[-1]
    n = sp["ab"].shape[1]
    blk = lambda *tail: pl.BlockSpec((nb, steps) + tail, lambda i: (0, i) + (0,) * len(tail))
    weights = [sp["ab"], sp["wb"], sp["c_re_t"], sp["c_im_t"], sp["d"], sp["w_glu"]]
    merge_w = [lw["w_out_a"], lw["w_out"], lw["norm2_w"], rw["w_router"], rw["b_router"]]
    return pl.pallas_call(
        functools.partial(_s5_merge_kernel, steps=steps),
        grid=(seq // steps,),
        in_specs=[blk(w), _resident_spec(zero_state.shape), _resident_spec(zero_state.shape)]
                 + [_resident_spec(a.shape) for a in weights]
                 + [blk(d), blk(2 * d), blk(d), _resident_spec(mod.shape)]
                 + [_resident_spec(a.shape) for a in merge_w],
        out_specs=[_const_spec(zero_state.shape), _const_spec(zero_state.shape),
                   blk(d), blk(d // LANES, LANES), blk(LANES), _const_spec((LANES, 1))],
        out_shape=[jax.ShapeDtypeStruct(zero_state.shape, F32),
                   jax.ShapeDtypeStruct(zero_state.shape, F32),
                   jax.ShapeDtypeStruct((nb, seq, d), F32),
                   jax.ShapeDtypeStruct((nb, seq, d // LANES, LANES), F32),
                   jax.ShapeDtypeStruct((nb, seq, LANES), F32),
                   jax.ShapeDtypeStruct((LANES, 1), F32)],
        scratch_shapes=[pltpu.VMEM((nb * steps, 2 * n), F32), pltpu.VMEM((2, nb, n), F32),
                        pltpu.VMEM((nb, steps, d), F32), pltpu.VMEM((LANES, 1), F32)],
        compiler_params=_cparams(("arbitrary",)),
        name="s5_merge_router",
    )(u, zero_state, zero_state, *weights, yn, gate, x, mod, *merge_w)


def _meta_rows_kernel(meta_ref, idx_ref, rank_ref):
    m = meta_ref[...].T
    idx_ref[...] = m[2:4, :].astype(jnp.int32)
    rank_ref[...] = m[4:6, :].astype(jnp.int32)


def _meta_rows(meta):
    t = meta.shape[0]
    tm = min(t, ROW_TILE)
    pair = pl.BlockSpec((2, tm), lambda i: (0, i))
    return pl.pallas_call(
        _meta_rows_kernel,
        grid=(t // tm,),
        in_specs=[pl.BlockSpec((tm, LANES), lambda i: (i, 0))],
        out_specs=[pair, pair],
        out_shape=[jax.ShapeDtypeStruct((2, t), jnp.int32), jax.ShapeDtypeStruct((2, t), jnp.int32)],
        compiler_params=_cparams(("arbitrary",)),
        name="moe_meta_rows",
    )(meta)


def _plan_kernel(cnta_ref, cntb_ref, idxa_ref, ranka_ref, idxb_ref, rankb_ref,
                 desta_ref, destb_ref, te_ref, ends_ref, nv_ref, *, tile, n_tiles):
    shift = tile.bit_length() - 1
    run = jnp.int32(0)
    starts = []
    for e in range(N_EXPERTS):
        starts.append(run)
        run = run + (((cnta_ref[e] + cntb_ref[e] + (tile - 1)) >> shift) << shift)
        ends_ref[e] = run
    n_valid = run >> shift
    nv_ref[0] = n_valid
    idx_a = idxa_ref[...]
    idx_b = idxb_ref[...]
    dest_a = ranka_ref[...]
    dest_b = rankb_ref[...]
    for e in range(N_EXPERTS):
        dest_a = dest_a + jnp.where(idx_a == e, starts[e], 0)
        dest_b = dest_b + jnp.where(idx_b == e, starts[e] + cnta_ref[e], 0)
    desta_ref[...] = dest_a
    destb_ref[...] = dest_b

    def tile_owner(j, carry):
        pos = jnp.minimum(j, n_valid - 1) * tile
        owner = jnp.int32(0)
        for e in range(N_EXPERTS - 1):
            owner = owner + jnp.where(ends_ref[e] <= pos, 1, 0)
        te_ref[j] = owner
        return carry

    lax.fori_loop(0, n_tiles, tile_owner, 0)


def _plan(cnt_a, cnt_b, idx_a, rank_a, idx_b, rank_b, tile, n_tiles):
    smem = pl.BlockSpec(memory_space=pltpu.SMEM)
    vmem = pl.BlockSpec(memory_space=pltpu.VMEM)
    return pl.pallas_call(
        functools.partial(_plan_kernel, tile=tile, n_tiles=n_tiles),
        in_specs=[smem, smem, vmem, vmem, vmem, vmem],
        out_specs=[vmem, vmem, smem, smem, smem],
        out_shape=[jax.ShapeDtypeStruct(idx_a.shape, jnp.int32),
                   jax.ShapeDtypeStruct(idx_b.shape, jnp.int32),
                   jax.ShapeDtypeStruct((n_tiles,), jnp.int32),
                   jax.ShapeDtypeStruct((N_EXPERTS,), jnp.int32),
                   jax.ShapeDtypeStruct((1,), jnp.int32)],
        name="moe_plan",
    )(cnt_a, cnt_b, idx_a, rank_a, idx_b, rank_b)


def _row_tiles(tm, d):
    return pl.BlockSpec((tm, d // LANES, LANES), lambda i: (i, 0, 0))


def _row_copy(src_ref, src_row, dst_ref, dst_row, sem):
    return pltpu.make_async_copy(src_ref.at[src_row], dst_ref.at[dst_row], sem)


def _each_row(tm, fn):
    def body(t, carry):
        for k in range(2):
            fn(t, k)
        return carry

    lax.fori_loop(0, tm, body, 0, unroll=8)


def _dispatch_kernel(ends_ref, dest_ref, prev_dest_ref, ha_ref, hb_ref, out_ref, zero_ref, stage_ref, sem, zsem,
                     *, tile, steps_a):
    i = pl.program_id(0)
    tm = ha_ref.shape[0]
    slot = i % 2

    @pl.when(i == 0)
    def _():
        zero_ref[...] = jnp.zeros(zero_ref.shape, F32)

        def clear(start):
            return pltpu.make_async_copy(zero_ref, out_ref.at[pl.ds(pl.multiple_of(start, tile), tile)], zsem)

        def used(e):
            return ends_ref[e] > (ends_ref[e - 1] if e else 0)

        def unused_tiles(fn):
            n_valid = ends_ref[N_EXPERTS - 1] >> (tile.bit_length() - 1)
            lax.fori_loop(n_valid, out_ref.shape[0] // tile, lambda j, c: (fn(clear(j * tile)), c)[1], 0)

        for e in range(N_EXPERTS):
            pl.when(used(e))(lambda e=e: clear(ends_ref[e] - tile).start())
        unused_tiles(lambda cp: cp.start())
        for e in range(N_EXPERTS):
            pl.when(used(e))(lambda e=e: clear(ends_ref[e] - tile).wait())
        unused_tiles(lambda cp: cp.wait())

    @pl.when(i < steps_a)
    def _():
        stage_ref[slot] = ha_ref[...]

    @pl.when(i >= steps_a)
    def _():
        stage_ref[slot] = hb_ref[...]

    _each_row(tm, lambda t, k: _row_copy(stage_ref.at[slot], t, out_ref, dest_ref[k * tm + t], sem.at[slot]).start(priority=k))

    @pl.when(i > 0)
    def _():
        _each_row(tm, lambda t, k: _row_copy(stage_ref.at[1 - slot], t, out_ref, prev_dest_ref[k * tm + t],
                                             sem.at[1 - slot]).wait())

    @pl.when(i == pl.num_programs(0) - 1)
    def _():
        _each_row(tm, lambda t, k: _row_copy(stage_ref.at[slot], t, out_ref, dest_ref[k * tm + t], sem.at[slot]).wait())


def _dispatch(ends, dest, h2_a, h2_b, n_rows, tm, tile):
    _, nt, lanes = h2_a.shape
    steps_a = h2_a.shape[0] // tm
    steps_b = h2_b.shape[0] // tm
    idx_spec = lambda fn: pl.BlockSpec((2 * tm,), fn, memory_space=pltpu.SMEM)
    rows = lambda fn: pl.BlockSpec((tm, nt, lanes), fn)
    return pl.pallas_call(
        functools.partial(_dispatch_kernel, tile=tile, steps_a=steps_a),
        grid=(steps_a + steps_b,),
        in_specs=[pl.BlockSpec(memory_space=pltpu.SMEM),
                  idx_spec(lambda i: (i,)), idx_spec(lambda i: (jnp.maximum(i - 1, 0),)),
                  rows(lambda i: (jnp.minimum(i, steps_a - 1), 0, 0)),
                  rows(lambda i: (jnp.maximum(i - steps_a, 0), 0, 0))],
        out_specs=pl.BlockSpec(memory_space=pl.ANY),
        out_shape=jax.ShapeDtypeStruct((n_rows, nt, lanes), F32),
        scratch_shapes=[pltpu.VMEM((tile, nt, lanes), F32), pltpu.VMEM((2, tm, nt, lanes), F32),
                        pltpu.SemaphoreType.DMA((2,)), pltpu.SemaphoreType.DMA(())],
        compiler_params=_cparams(("arbitrary",)),
        name="moe_dispatch",
    )(ends, dest, dest, h2_a, h2_b)


def _expert_kernel(te_ref, nv_ref, x_ref, w1_ref, w3_ref, w2_ref, y_ref, w1b_ref, w3b_ref, w2b_ref):
    j = pl.program_id(0)
    fresh = jnp.logical_or(j == 0, te_ref[j] != te_ref[jnp.maximum(j - 1, 0)])

    @pl.when(fresh)
    def _():
        w1b_ref[...] = w1_ref[...].astype(BF16)
        w3b_ref[...] = w3_ref[...].astype(BF16)
        w2b_ref[...] = w2_ref[...].astype(BF16)

    @pl.when(j < nv_ref[0])
    def _():
        tm, nt, lanes = x_ref.shape
        xb = x_ref[...].reshape(tm, nt * lanes).astype(BF16)
        a = jnp.dot(xb, w1b_ref[...], preferred_element_type=F32)
        b = jnp.dot(xb, w3b_ref[...], preferred_element_type=F32)
        y_ref[...] = _bdot(_silu(a) * b, w2b_ref[...]).reshape(tm, nt, lanes)

    @pl.when(j >= nv_ref[0])
    def _():
        y_ref[...] = jnp.zeros(y_ref.shape, F32)


def _experts(tile_expert, n_valid, xs, w1, w3, w2, layer, tm):
    n_rows, nt, lanes = xs.shape
    d = nt * lanes
    de = w1.shape[3]
    wspec = lambda r, c: pl.BlockSpec((None, None, r, c), lambda j, te, nv: (layer, te[j], 0, 0))
    grid_spec = pltpu.PrefetchScalarGridSpec(
        num_scalar_prefetch=2,
        grid=(n_rows // tm,),
        in_specs=[pl.BlockSpec((tm, nt, lanes), lambda j, te, nv: (jnp.minimum(j, nv[0] - 1), 0, 0)),
                  wspec(d, de), wspec(d, de), wspec(de, d)],
        out_specs=pl.BlockSpec((tm, nt, lanes), lambda j, te, nv: (j, 0, 0)),
        scratch_shapes=[pltpu.VMEM((d, de), BF16), pltpu.VMEM((d, de), BF16), pltpu.VMEM((de, d), BF16)],
    )
    return pl.pallas_call(
        _expert_kernel,
        grid_spec=grid_spec,
        out_shape=jax.ShapeDtypeStruct((n_rows, nt, lanes), F32),
        compiler_params=_cparams(("arbitrary",)),
        name="moe_experts",
    )(tile_expert, n_valid, xs, w1, w3, w2)


def _combine_kernel(dest_ref, next_dest_ref, ys_ref, x1_ref, wts_ref, mod_ref, fw_ref, out_ref, buf_ref, sem,
                    *, per_token, final_norm):
    i = pl.program_id(0)
    tm, d = x1_ref.shape
    slot = i % 2

    def gather(idx_ref, s):
        return lambda t, k: _row_copy(ys_ref, idx_ref[k * tm + t], buf_ref.at[s, k], t, sem.at[s])

    @pl.when(i == 0)
    def _():
        _each_row(tm, lambda t, k: gather(dest_ref, slot)(t, k).start(priority=k))

    @pl.when(i + 1 < pl.num_programs(0))
    def _():
        _each_row(tm, lambda t, k: gather(next_dest_ref, 1 - slot)(t, k).start(priority=k))

    _each_row(tm, lambda t, k: gather(dest_ref, slot)(t, k).wait())
    w = wts_ref[...]
    rows = lambda k: buf_ref[slot, k].reshape(tm, d)
    moe = w[:, 0:1] * rows(0) + w[:, 1:2] * rows(1)
    x2 = x1_ref[...] + _mod_rows(mod_ref, 5, per_token) * moe
    if final_norm:
        x2 = _rms(x2) * fw_ref[...]
    out_ref[...] = x2


def _combine(dest, ys, x1, wts, mod, fw, *, tm, per_token, mod_spec, final_norm):
    t, d = x1.shape
    row = lambda n: pl.BlockSpec((tm, n), lambda i: (i, 0))
    n_steps = t // tm
    idx_spec = lambda fn: pl.BlockSpec((2 * tm,), fn, memory_space=pltpu.SMEM)
    return pl.pallas_call(
        functools.partial(_combine_kernel, per_token=per_token, final_norm=final_norm),
        grid=(n_steps,),
        in_specs=[idx_spec(lambda i: (i,)), idx_spec(lambda i: (jnp.minimum(i + 1, n_steps - 1),)),
                  pl.BlockSpec(memory_space=pl.ANY),
                  row(d), row(LANES), mod_spec, _const_spec((1, d))],
        out_specs=row(d),
        out_shape=jax.ShapeDtypeStruct((t, d), F32),
        scratch_shapes=[pltpu.VMEM((2, 2, tm) + ys.shape[1:], F32), pltpu.SemaphoreType.DMA((2,))],
        compiler_params=_cparams(("arbitrary",)),
        name="moe_combine",
    )(dest, dest, ys, x1, wts, mod, fw)


def _step_table(dest, tm):
    return jnp.transpose(dest.reshape(2, dest.shape[1] // tm, tm), (1, 0, 2)).reshape(-1)


def _moe(ga, gb, fw, ew, layer, final_norm):
    tm = EXPERT_TILE
    t_all = ga["h2"].shape[0] + gb["h2"].shape[0]
    n_rows = ((2 * t_all + N_EXPERTS * (tm - 1)) // tm + 1) * tm
    cnt = lambda g: g["counts"][:N_EXPERTS, 0].astype(jnp.int32)
    dest_a, dest_b, tile_expert, ends, n_valid = _plan(cnt(ga), cnt(gb), ga["idx"], ga["rank"], gb["idx"], gb["rank"],
                                                       tm, n_rows // tm)
    dispatch_table = jnp.concatenate([_step_table(dest_a, MOE_TILE), _step_table(dest_b, MOE_TILE)])
    xs = _dispatch(ends, dispatch_table, ga["h2"], gb["h2"], n_rows, MOE_TILE, tm)
    ys = _experts(tile_expert, n_valid, xs, ew["w1"], ew["w3"], ew["w2"], layer, tm)
    outs = []
    for g, dest in ((ga, dest_a), (gb, dest_b)):
        tok = g["tok_tile"]
        outs.append(_combine(_step_table(dest, tok), ys, g["x1"], g["meta"], g["mod"], fw, tm=tok,
                             per_token=g["per_token"], mod_spec=g["mod_spec_fn"](tok), final_norm=final_norm))
    return outs


def _pad_lanes(a, n=LANES):
    return jnp.pad(a, [(0, 0)] * (a.ndim - 1) + [(0, n - a.shape[-1])])


def kernel(x_prompt, x_sample, state_ssm, state_conv, state_s5_re, state_s5_im, c_prompt, c_sample, w_in, conv_w, conv_b, dt_bias, a_log, d_ssd, ssd_norm_w, w_out_a, s5_a_re, s5_a_im, s5_log_dt, s5_b_re, s5_b_im, s5_c_re, s5_c_im, s5_d, w_glu, w_out, norm1_w, norm2_w, w_ada, b_ada, w_router, b_router, w1, w3, w2, final_norm_w):
    nb, seq, d = x_prompt.shape
    ns, steps, _ = x_sample.shape
    depth = w_in.shape[0]
    d_in = SSD_HEADS * SSD_HEAD_DIM
    cdim = conv_w.shape[2]
    s5w = S5_GROUPS * S5_GROUP_CH
    s5n = S5_GROUPS * S5_STATE
    k1 = SSD_CONV - 1
    tp = nb * seq
    ts = ns * steps

    mod = _ada_mod(jnp.concatenate([c_prompt, c_sample], axis=0), w_ada, b_ada)
    head_expand = jnp.tile(jnp.repeat(jnp.eye(LANES, SSD_HEADS, dtype=BF16), SSD_HEAD_DIM, axis=1), (3, 1))
    rw = {"w_router": _pad_lanes(w_router), "b_router": _pad_lanes(b_router[None, :]).reshape(LANES, 1)}

    xp = x_prompt.reshape(tp, d)
    xsm = jnp.transpose(x_sample, (1, 0, 2)).reshape(ts, d)
    outs = {k: [] for k in ("ssm_p", "conv_p", "conv_s", "re_p", "re_s", "im_p", "im_s")}
    ssm_s = None
    zero_state = jnp.zeros((nb, s5n), F32)

    for l in range(depth):
        o0 = 0
        o1 = d_in
        o2 = o1 + cdim
        o3 = o2 + SSD_HEADS
        o4 = o3 + s5w
        wl = w_in[l].astype(BF16)
        ws = (wl[:, o0:o1], wl[:, o1:o2], _pad_lanes(wl[:, o2:o3]), wl[:, o3:o4], wl[:, o4:])
        p = {"conv_w": conv_w[l], "conv_b": conv_b[l][None, :], "dt_bias": _pad_lanes(dt_bias[l][None, :]),
             "a_log": _pad_lanes(a_log[l][None, :]),
             "d_skip_e": jnp.repeat(d_ssd[l], SSD_HEAD_DIM)[None, :],
             "ssd_norm_w": ssd_norm_w[l][None, :], "head_expand": head_expand}
        chan_rows = lambda a, perm: jnp.transpose(a, perm).reshape(S5_GROUP_CH, s5n)
        ab, wb, c_re_t, c_im_t = _s5_params(
            s5_a_re[l].reshape(1, s5n), s5_a_im[l].reshape(1, s5n), jnp.repeat(s5_log_dt[l], S5_STATE)[None, :],
            chan_rows(s5_b_re[l], (2, 0, 1)), chan_rows(s5_b_im[l], (2, 0, 1)),
            chan_rows(s5_c_re[l], (1, 0, 2)), chan_rows(s5_c_im[l], (1, 0, 2)))
        sp = {"ab": ab, "wb": wb, "c_re_t": c_re_t, "c_im_t": c_im_t,
              "d": s5_d[l][None, :], "w_glu": w_glu[l].astype(BF16)}
        lw = {"w_out_a": w_out_a[l].astype(BF16), "w_out": w_out[l].astype(BF16), "norm2_w": norm2_w[l][None, :]}
        ew = {"w1": w1, "w3": w3, "w2": w2}
        final = l == depth - 1
        fw = final_norm_w[None, :]

        mod_p = mod[l, :nb].reshape(nb, 6, d)
        mod_p_spec = lambda tm: pl.BlockSpec((None, 6, d), lambda i: (i // (seq // tm), 0, 0))
        u, gate, yn, nconv, nssm = _inproj_ssd(xp, mod_p, norm1_w[l][None, :], ws, p, nb, seq, ROW_TILE)
        seq3 = lambda a: a.reshape((nb, seq) + a.shape[1:])
        nre, nim, x1, h2, meta, counts = _s5_merge(seq3(u), seq3(yn), seq3(gate), seq3(xp), mod_p, zero_state,
                                                   sp, lw, rw, S5_STEPS)
        meta = meta.reshape(tp, LANES)
        idx, rank = _meta_rows(meta)
        routed_p = {"h2": h2.reshape(tp, d // LANES, LANES), "idx": idx, "rank": rank, "counts": counts,
                    "x1": x1.reshape(tp, d), "meta": meta, "mod": mod_p, "tok_tile": MOE_TILE, "per_token": False,
                    "mod_spec_fn": mod_p_spec}
        outs["ssm_p"].append(nssm)
        outs["conv_p"].append(nconv)
        outs["re_p"].append(nre.reshape(nb, S5_GROUPS, S5_STATE))
        outs["im_p"].append(nim.reshape(nb, S5_GROUPS, S5_STATE))

        mod_s = jnp.transpose(mod[l, nb:].reshape(ns, 6, d), (1, 0, 2))
        mod_s_spec = lambda tm: pl.BlockSpec((6, tm, d), lambda i: (0, i % (ns // tm), 0))
        z, xbc, dtr, u, gate = _inproj(
            xsm, mod_s, norm1_w[l][None, :], ws, tm=ns, per_token=True, mod_spec=mod_s_spec(ns))
        conv0_tm = jnp.transpose(state_conv[l], (1, 0, 2)).reshape(k1 * ns, cdim)
        yn, nconv_tm, ssm_s = _ssd_sample(xbc, dtr, z, conv0_tm, state_ssm, ssm_s, l, p, ns, steps)
        yb, nre, nim = _s5(u, state_s5_re[l].reshape(ns, s5n), state_s5_im[l].reshape(ns, s5n), sp, ns, steps,
                           False)
        x1, h2, meta, counts = _merge_router(
            yn, yb, gate, xsm, mod_s, lw, rw, tm=ns, per_token=True, mod_spec=mod_s_spec(ns))
        idx, rank = _meta_rows(meta)
        routed_s = {"h2": h2, "idx": idx, "rank": rank, "counts": counts, "x1": x1, "meta": meta, "mod": mod_s,
                    "tok_tile": ns, "per_token": True, "mod_spec_fn": mod_s_spec}
        xp, xsm = _moe(routed_p, routed_s, fw, ew, l, final)
        outs["conv_s"].append(jnp.transpose(nconv_tm.reshape(k1, ns, cdim), (1, 0, 2)))
        outs["re_s"].append(nre.reshape(ns, S5_GROUPS, S5_STATE))
        outs["im_s"].append(nim.reshape(ns, S5_GROUPS, S5_STATE))

    y_prompt = xp.reshape(nb, seq, d)
    y_sample = jnp.transpose(xsm.reshape(steps, ns, d), (1, 0, 2))
    st = lambda k: jnp.stack(outs[k])
    return (y_prompt, y_sample, st("ssm_p"), ssm_s, st("conv_p"), st("conv_s"),
            st("re_p"), st("re_s"), st("im_p"), st("im_s"))
```

```python
import functools

import jax
import jax.numpy as jnp
from jax import lax
from jax.experimental import pallas as pl
from jax.experimental.pallas import tpu as pltpu

F32 = jnp.float32
BF16 = jnp.bfloat16
EPS = 1e-6

SSD_HEADS = 16
SSD_HEAD_DIM = 64
SSD_GROUPS = 2
SSD_STATE = 128
SSD_CONV = 4
SSD_CHUNK = 128
S5_GROUPS = 32
S5_GROUP_CH = 16
S5_STATE = 64
N_EXPERTS = 16
EXPERTS_PER_GROUP = 4
N_EXPERT_GROUPS = 4

LANES = 128
SUBLANES = 8
MXU_TILE = 256
VMEM_LIMIT_BYTES = 56 * 1024 * 1024

ROW_TILE = 512
MOE_TILE = 512
META_TILE = 2048
EXPERT_TILE = 512
S5_STEPS = 64
S5_SCAN_LANES = 512
ROUTER_SUBTILES = 2
SSD_SAMPLE_SEQS = 4


def _cparams(sem):
    return pltpu.CompilerParams(dimension_semantics=sem, vmem_limit_bytes=VMEM_LIMIT_BYTES)


def _bdot(a, b):
    return jnp.dot(a.astype(BF16), b.astype(BF16), preferred_element_type=F32)


def _bdot_nt(a, b):
    return lax.dot_general(a.astype(BF16), b.astype(BF16), (((1,), (1,)), ((), ())),
                           preferred_element_type=F32)


def _split3(x):
    hi = x.astype(BF16)
    r1 = x - hi.astype(F32)
    mid = r1.astype(BF16)
    lo = (r1 - mid.astype(F32)).astype(BF16)
    return hi, mid, lo


def _dot3_lhs(x, m3):
    return jnp.dot(jnp.concatenate(_split3(x), axis=1), m3, preferred_element_type=F32)


def _dot3_rhs(m, x):
    return jnp.dot(jnp.concatenate([m] * 3, axis=1), jnp.concatenate(_split3(x), axis=0),
                   preferred_element_type=F32)


def _sigmoid(x):
    return jax.nn.sigmoid(x)


def _silu(x):
    return x * _sigmoid(x)


def _softplus(x):
    return jnp.maximum(x, 0.0) + jnp.log1p(jnp.exp(-jnp.abs(x)))


def _gelu_tanh(x):
    c = 0.7978845608028654
    return 0.5 * x * (1.0 + jnp.tanh(c * (x + 0.044715 * (x * x * x))))


def _rms(x):
    return x * lax.rsqrt(jnp.mean(x * x, axis=-1, keepdims=True) + EPS)


def _ada_kernel(c_ref, w_ref, b_ref, o_ref):
    o_ref[0] = _bdot(_silu(c_ref[...]), w_ref[0]) + b_ref[0]


def _ada_mod(c_all, w_ada, b_ada):
    depth, d, n = w_ada.shape
    nseq = c_all.shape[0]
    tn = 1536
    return pl.pallas_call(
        _ada_kernel,
        grid=(depth, n // tn),
        in_specs=[pl.BlockSpec((nseq, d), lambda l, j: (0, 0)),
                  pl.BlockSpec((1, d, tn), lambda l, j: (l, 0, j)),
                  pl.BlockSpec((1, 1, tn), lambda l, j: (l, 0, j))],
        out_specs=pl.BlockSpec((1, nseq, tn), lambda l, j: (l, 0, j)),
        out_shape=jax.ShapeDtypeStruct((depth, nseq, n), F32),
        compiler_params=_cparams(("arbitrary", "arbitrary")),
        name="ada_mod",
    )(c_all, w_ada, b_ada.reshape(depth, 1, n))


def _mod_rows(mod_ref, j, per_token, rows_per_seq=None):
    if per_token:
        return mod_ref[j]
    if rows_per_seq is not None:
        m = mod_ref[:, j, :]
        return jnp.broadcast_to(m[:, None, :], (m.shape[0], rows_per_seq, m.shape[1])).reshape(-1, m.shape[1])
    return mod_ref[j:j + 1, :]


def _inproj_kernel(x_ref, mod_ref, nw_ref, wz_ref, wx_ref, wdt_ref, wu_ref, wg_ref,
                   z_ref, xbc_ref, dt_ref, u_ref, g_ref, *, per_token):
    sh = _mod_rows(mod_ref, 0, per_token)
    sc = _mod_rows(mod_ref, 1, per_token)
    h = (_rms(x_ref[...]) * nw_ref[...] * (1.0 + sc) + sh).astype(BF16)
    d = functools.partial(jnp.dot, preferred_element_type=F32)
    z_ref[...] = d(h, wz_ref[...])
    xbc_ref[...] = d(h, wx_ref[...])
    dt_ref[...] = d(h, wdt_ref[...])
    u_ref[...] = d(h, wu_ref[...])
    g_ref[...] = d(h, wg_ref[...])


def _const_spec(shape):
    nd = len(shape)
    return pl.BlockSpec(shape, lambda *_: (0,) * nd)


def _inproj(x, mod, nw, ws, *, tm, per_token, mod_spec):
    t, d = x.shape
    wz, wx, wdt, wu, wg = ws
    row = lambda n: pl.BlockSpec((tm, n), lambda i: (i, 0))
    return pl.pallas_call(
        functools.partial(_inproj_kernel, per_token=per_token),
        grid=(t // tm,),
        in_specs=[row(d), mod_spec, _const_spec((1, d)),
                  _const_spec(wz.shape), _const_spec(wx.shape), _const_spec(wdt.shape),
                  _const_spec(wu.shape), _const_spec(wg.shape)],
        out_specs=[row(wz.shape[1]), row(wx.shape[1]), row(wdt.shape[1]), row(wu.shape[1]), row(wg.shape[1])],
        out_shape=[jax.ShapeDtypeStruct((t, wz.shape[1]), F32),
                   jax.ShapeDtypeStruct((t, wx.shape[1]), F32),
                   jax.ShapeDtypeStruct((t, wdt.shape[1]), F32),
                   jax.ShapeDtypeStruct((t, wu.shape[1]), F32),
                   jax.ShapeDtypeStruct((t, wg.shape[1]), F32)],
        compiler_params=_cparams(("arbitrary",)),
        name="inproj",
    )(x, mod, nw, wz, wx, wdt, wu, wg)


def _gated_group_norm(y, z, nw):
    y = y * _silu(z)
    half = y.shape[1] // SSD_GROUPS
    parts = [_rms(y[:, g * half:(g + 1) * half]) for g in range(SSD_GROUPS)]
    return jnp.concatenate(parts, axis=1) * nw


def _ssd_chunk(first, last, xbc_ref, dt_ref, z_ref, cw_ref, cb_ref, dtb_ref, alog_ref, dsk_ref, nw_ref,
               e_ref, yn_ref, conv_ref, ssm_ref, ext_ref, st_ref):
    q = SSD_CHUNK
    d_in = SSD_HEADS * SSD_HEAD_DIM
    gw = SSD_STATE
    hpg = SSD_HEADS // SSD_GROUPS
    grows = hpg * SSD_HEAD_DIM

    if first is not None:
        @pl.when(first)
        def _():
            ext_ref[...] = jnp.zeros(ext_ref.shape, F32)
            st_ref[...] = jnp.zeros(st_ref.shape, F32)

    xbc = xbc_ref[...]
    sub = lax.broadcasted_iota(jnp.int32, (SUBLANES, xbc.shape[1]), 0)
    taps = [cw_ref[k:k + 1, :] for k in range(SSD_CONV)]
    prev_rolled = [pltpu.roll(ext_ref[...], j, axis=0) for j in range(1, SSD_CONV)]
    acc_tiles = []
    for r0 in range(0, q, SUBLANES):
        cur = xbc[r0:r0 + SUBLANES, :]
        rolled = [pltpu.roll(cur, j, axis=0) for j in range(1, SSD_CONV)]
        acc = cb_ref[...] + taps[SSD_CONV - 1] * cur
        for j in range(1, SSD_CONV):
            acc = acc + taps[SSD_CONV - 1 - j] * jnp.where(sub < j, prev_rolled[j - 1], rolled[j - 1])
        acc_tiles.append(acc)
        prev_rolled = rolled
    ext_ref[...] = xbc[q - SUBLANES:q, :]
    v = _silu(jnp.concatenate(acc_tiles, axis=0))
    xs = v[:, :d_in]
    bm = v[:, d_in:d_in + SSD_GROUPS * gw]
    cm = v[:, d_in + SSD_GROUPS * gw:]

    dt = _softplus(dt_ref[...] + dtb_ref[...])
    a = -jnp.exp(alog_ref[...])
    da = dt * a
    rid = lax.broadcasted_iota(jnp.int32, (q, q), 0)
    cid = lax.broadcasted_iota(jnp.int32, (q, q), 1)
    causal = rid >= cid
    tri = jnp.where(causal, 1.0, 0.0).astype(BF16)
    a_cs = _dot3_rhs(tri, da)
    a_cs_t = a_cs.T
    a_last = a_cs[q - 1:q, :]
    per_head = jnp.concatenate([dt, jnp.exp(a_last - a_cs) * dt, jnp.exp(a_cs)], axis=0)
    per_chan = _dot3_lhs(per_head, e_ref[...])
    x_dt = xs * per_chan[0:q]
    x_end = xs * per_chan[q:2 * q]
    eacs_e = per_chan[2 * q:3 * q]
    x_end_t = x_end.T.astype(BF16)
    lane = lax.broadcasted_iota(jnp.int32, (q, 2 * SSD_HEAD_DIM), 1)
    first = lane < SSD_HEAD_DIM

    y_diag = []
    y_off = []
    for g in range(SSD_GROUPS):
        bg = bm[:, g * gw:(g + 1) * gw].astype(BF16)
        cg = cm[:, g * gw:(g + 1) * gw].astype(BF16)
        cb = _bdot_nt(cg, bg)
        sg = st_ref[g * grows:(g + 1) * grows, :]
        y_off.append(_bdot_nt(cg, sg))
        for pair in range(hpg // 2):
            ms = []
            for r in (2 * pair, 2 * pair + 1):
                h = g * hpg + r
                diff = a_cs[:, h:h + 1] - a_cs_t[h:h + 1, :]
                decay = jnp.where(causal, jnp.exp(jnp.where(causal, diff, 0.0)), 0.0)
                ms.append((cb * decay).astype(BF16))
            h0 = g * hpg + 2 * pair
            xp = x_dt[:, h0 * SSD_HEAD_DIM:(h0 + 2) * SSD_HEAD_DIM]
            rhs = jnp.concatenate([jnp.where(first, xp, 0.0), jnp.where(first, 0.0, xp)], axis=0)
            y_diag.append(jnp.dot(jnp.concatenate(ms, axis=1), rhs.astype(BF16), preferred_element_type=F32))
        contrib = jnp.dot(x_end_t[g * grows:(g + 1) * grows, :], bg, preferred_element_type=F32)
        for r in range(hpg):
            h = g * hpg + r
            rows = slice(h * SSD_HEAD_DIM, (h + 1) * SSD_HEAD_DIM)
            dec = jnp.exp(a_cs_t[h:h + 1, q - 1:q])
            st_ref[rows, :] = st_ref[rows, :] * dec + contrib[r * SSD_HEAD_DIM:(r + 1) * SSD_HEAD_DIM, :]

    y = (jnp.concatenate(y_diag, axis=1) + jnp.concatenate(y_off, axis=1) * eacs_e
         + dsk_ref[...] * xs)
    yn_ref[...] = _gated_group_norm(y, z_ref[...], nw_ref[...]).astype(BF16)

    if last is not None:
        @pl.when(last)
        def _():
            conv_ref[0] = ext_ref[pl.ds(SUBLANES - (SSD_CONV - 1), SSD_CONV - 1), :]
            ssm_ref[0] = st_ref[...].reshape(SSD_HEADS, SSD_HEAD_DIM, SSD_STATE)


def _inproj_ssd_kernel(x_ref, mod_ref, nw1_ref, wz_ref, wx_ref, wdt_ref, wu_ref, wg_ref,
                       cw_ref, cb_ref, dtb_ref, alog_ref, dsk_ref, nw_ref, e_ref,
                       u_ref, g_ref, yn_ref, conv_ref, ssm_ref,
                       z_s, xbc_s, dt_s, ext_ref, st_ref):
    j = pl.program_id(1)
    sh = mod_ref[0:1, :]
    sc = mod_ref[1:2, :]
    h = (_rms(x_ref[...]) * nw1_ref[...] * (1.0 + sc) + sh).astype(BF16)
    d = functools.partial(jnp.dot, preferred_element_type=F32)
    z_s[...] = d(h, wz_ref[...])
    xbc_s[...] = d(h, wx_ref[...])
    dt_s[...] = d(h, wdt_ref[...])
    u_ref[...] = d(h, wu_ref[...])
    g_ref[...] = d(h, wg_ref[...])
    n_chunks = x_ref.shape[0] // SSD_CHUNK
    for k in range(n_chunks):
        rows = pl.ds(k * SSD_CHUNK, SSD_CHUNK)
        first = (j == 0) if k == 0 else None
        last = (j == pl.num_programs(1) - 1) if k == n_chunks - 1 else None
        _ssd_chunk(first, last, xbc_s.at[rows], dt_s.at[rows], z_s.at[rows], cw_ref, cb_ref, dtb_ref, alog_ref,
                   dsk_ref, nw_ref, e_ref, yn_ref.at[rows], conv_ref, ssm_ref, ext_ref, st_ref)


def _inproj_ssd(x, mod, nw1, ws, p, nb, seq, tm):
    t, d = x.shape
    wz, wx, wdt, wu, wg = ws
    cdim = wx.shape[1]
    d_in = wz.shape[1]
    tiles = seq // tm
    row = lambda n: pl.BlockSpec((tm, n), lambda b, j: (b * tiles + j, 0))
    consts = [nw1, wz, wx, wdt, wu, wg, p["conv_w"], p["conv_b"], p["dt_bias"], p["a_log"], p["d_skip_e"],
              p["ssd_norm_w"], p["head_expand"]]
    return pl.pallas_call(
        _inproj_ssd_kernel,
        grid=(nb, tiles),
        in_specs=[row(d), pl.BlockSpec((None, 6, d), lambda b, j: (b, 0, 0))]
                 + [_resident_spec(a.shape) for a in consts],
        out_specs=[row(wu.shape[1]), row(wg.shape[1]), row(d_in),
                   pl.BlockSpec((1, SSD_CONV - 1, cdim), lambda b, j: (b, 0, 0)),
                   pl.BlockSpec((1, SSD_HEADS, SSD_HEAD_DIM, SSD_STATE), lambda b, j: (b, 0, 0, 0))],
        out_shape=[jax.ShapeDtypeStruct((t, wu.shape[1]), F32),
                   jax.ShapeDtypeStruct((t, wg.shape[1]), F32),
                   jax.ShapeDtypeStruct((t, d_in), BF16),
                   jax.ShapeDtypeStruct((nb, SSD_CONV - 1, cdim), F32),
                   jax.ShapeDtypeStruct((nb, SSD_HEADS, SSD_HEAD_DIM, SSD_STATE), F32)],
        scratch_shapes=[pltpu.VMEM((tm, d_in), F32), pltpu.VMEM((tm, cdim), F32), pltpu.VMEM((tm, LANES), F32),
                        pltpu.VMEM((SUBLANES, cdim), F32),
                        pltpu.VMEM((d_in, SSD_STATE), F32)],
        compiler_params=_cparams(("arbitrary", "arbitrary")),
        name="inproj_ssd",
    )(x, mod, *consts)


def _ssd_sample_a_kernel(xbc_ref, conv0_ref, dt_ref, cw_ref, cb_ref, dtb_ref, alog_ref, dsk_ref, e_ref,
                         ydiag_ref, eacs_ref, wt_ref, bs_ref, c8_ref, dec_ref, nconv_ref, *, steps, nseq):
    d_in = SSD_HEADS * SSD_HEAD_DIM
    gw = SSD_STATE
    hpg = SSD_HEADS // SSD_GROUPS
    k1 = SSD_CONV - 1
    slab = lambda ref, i: ref[i * nseq:(i + 1) * nseq, :]
    xp = [slab(conv0_ref, i) for i in range(k1)] + [slab(xbc_ref, i) for i in range(steps)]
    for i in range(k1):
        nconv_ref[i * nseq:(i + 1) * nseq, :] = xp[steps + i]
    e = e_ref[...]
    a = -jnp.exp(alog_ref[...])
    head = lax.broadcasted_iota(jnp.int32, (nseq, LANES), 1)
    xs, bm, cm, dt, a_cs = [], [], [], [], []
    run = jnp.zeros((nseq, LANES), F32)
    for t in range(steps):
        acc = cb_ref[...]
        for k in range(SSD_CONV):
            acc = acc + cw_ref[k:k + 1, :] * xp[t + k]
        v = _silu(acc)
        xs.append(v[:, :d_in])
        bm.append(v[:, d_in:d_in + SSD_GROUPS * gw])
        cm.append(v[:, d_in + SSD_GROUPS * gw:])
        dt.append(_softplus(slab(dt_ref, t) + dtb_ref[...]))
        run = run + dt[t] * a
        a_cs.append(run)
    dec_ref[...] = jnp.exp(a_cs[steps - 1])
    for t in range(steps):
        y = dsk_ref[...] * xs[t]
        for s in range(t + 1):
            cbs = [jnp.sum(cm[t][:, g * gw:(g + 1) * gw] * bm[s][:, g * gw:(g + 1) * gw],
                           axis=-1, keepdims=True) for g in range(SSD_GROUPS)]
            coef = jnp.exp(a_cs[t] - a_cs[s]) * dt[s] * jnp.where(head < hpg, cbs[0], cbs[1])
            y = y + _dot3_lhs(coef, e) * xs[s]
        ydiag_ref[t * nseq:(t + 1) * nseq, :] = y
        eacs_ref[t * nseq:(t + 1) * nseq, :] = _dot3_lhs(jnp.exp(a_cs[t]), e)
        w = xs[t] * _dot3_lhs(jnp.exp(a_cs[steps - 1] - a_cs[t]) * dt[t], e)
        wt_ref[:, t * nseq:(t + 1) * nseq] = w.T.astype(BF16)
        bs_ref[t * nseq:(t + 1) * nseq, :] = bm[t]
        for g in range(SSD_GROUPS):
            j = g * steps + t
            c8_ref[j * nseq:(j + 1) * nseq, :] = cm[t][:, g * gw:(g + 1) * gw]


def _ssd_sample_b_kernel(dec_ref, h0_ref, wt_ref, bs_ref, c8_ref, *rest, steps, nseq, layer):
    yoff_ref, ssm_ref = rest[-2:]
    if layer:
        ssm_ref[0:layer] = rest[0][...]
    hpg = SSD_HEADS // SSD_GROUPS
    grows = hpg * SSD_HEAD_DIM
    gw = SSD_STATE
    rid = lax.broadcasted_iota(jnp.int32, (steps * nseq, gw), 0)
    for s in range(h0_ref.shape[0]):
        b = pl.program_id(0) * h0_ref.shape[0] + s
        h0 = h0_ref[s].reshape(SSD_HEADS * SSD_HEAD_DIM, SSD_STATE)
        c8 = c8_ref[pl.ds(b, SSD_GROUPS * steps, stride=nseq), :]
        yoff_ref[s] = _bdot_nt(c8, h0)
        mine = rid == b
        for t in range(1, steps):
            mine = jnp.logical_or(mine, rid == b + t * nseq)
        for g in range(SSD_GROUPS):
            bsel = jnp.where(mine, bs_ref[:, g * gw:(g + 1) * gw], 0.0).astype(BF16)
            contrib = jnp.dot(wt_ref[g * grows:(g + 1) * grows, :], bsel, preferred_element_type=F32)
            for r in range(hpg):
                h = g * hpg + r
                rows = slice(h * SSD_HEAD_DIM, (h + 1) * SSD_HEAD_DIM)
                ssm_ref[layer, s, h] = (h0[rows, :] * dec_ref[b, h]
                                        + contrib[r * SSD_HEAD_DIM:(r + 1) * SSD_HEAD_DIM, :])


def _ssd_sample_c_kernel(ydiag_ref, yoff_ref, eacs_ref, z_ref, nw_ref, yn_ref, *, steps, nseq):
    d_in = SSD_HEADS * SSD_HEAD_DIM
    half = d_in // SSD_GROUPS
    for t in range(steps):
        lo = yoff_ref[:, t * d_in:t * d_in + half]
        hi = yoff_ref[:, (steps + t) * d_in + half:(steps + t + 1) * d_in]
        rows = slice(t * nseq, (t + 1) * nseq)
        y = ydiag_ref[rows, :] + jnp.concatenate([lo, hi], axis=1) * eacs_ref[rows, :]
        yn_ref[rows, :] = _gated_group_norm(y, z_ref[rows, :], nw_ref[...]).astype(BF16)


def _ssd_sample(xbc, dt_raw, z, conv0_tm, ssm_all, ssm_done, layer, p, nseq, steps):
    t, cdim = xbc.shape
    d_in = z.shape[1]
    k1 = SSD_CONV - 1
    nj = SSD_GROUPS * steps
    a_out = pl.pallas_call(
        functools.partial(_ssd_sample_a_kernel, steps=steps, nseq=nseq),
        out_shape=[jax.ShapeDtypeStruct((t, d_in), F32),
                   jax.ShapeDtypeStruct((t, d_in), F32),
                   jax.ShapeDtypeStruct((d_in, t), BF16),
                   jax.ShapeDtypeStruct((t, SSD_GROUPS * SSD_STATE), F32),
                   jax.ShapeDtypeStruct((nj * nseq, SSD_STATE), F32),
                   jax.ShapeDtypeStruct((nseq, LANES), F32),
                   jax.ShapeDtypeStruct((k1 * nseq, cdim), F32)],
        compiler_params=pltpu.CompilerParams(vmem_limit_bytes=VMEM_LIMIT_BYTES),
        name="ssd_sample_a",
    )(xbc, conv0_tm, dt_raw, p["conv_w"], p["conv_b"], p["dt_bias"], p["a_log"], p["d_skip_e"],
      p["head_expand"])
    ydiag, eacs_e, wt, bs, c8, dec, nconv = a_out
    state_blk = (SSD_HEADS, SSD_HEAD_DIM, SSD_STATE)
    operands = [dec[:, :SSD_HEADS], ssm_all, wt, bs, c8]
    sb = SSD_SAMPLE_SEQS
    in_specs = [pl.BlockSpec(memory_space=pltpu.SMEM),
                pl.BlockSpec((None, sb) + state_blk, lambda b: (layer, b, 0, 0, 0)),
                _const_spec(wt.shape), _const_spec(bs.shape), _const_spec(c8.shape)]
    if layer:
        operands.append(ssm_done)
        in_specs.append(pl.BlockSpec((layer, sb) + state_blk, lambda b: (0, b, 0, 0, 0)))
    yoff, nssm = pl.pallas_call(
        functools.partial(_ssd_sample_b_kernel, steps=steps, nseq=nseq, layer=layer),
        grid=(nseq // sb,),
        in_specs=in_specs,
        out_specs=[pl.BlockSpec((sb, nj, d_in), lambda b: (b, 0, 0)),
                   pl.BlockSpec((layer + 1, sb) + state_blk, lambda b: (0, b, 0, 0, 0))],
        out_shape=[jax.ShapeDtypeStruct((nseq, nj, d_in), F32),
                   jax.ShapeDtypeStruct((layer + 1, nseq) + state_blk, F32)],
        compiler_params=_cparams(("arbitrary",)),
        name="ssd_sample_b",
    )(*operands)
    yn = pl.pallas_call(
        functools.partial(_ssd_sample_c_kernel, steps=steps, nseq=nseq),
        out_shape=jax.ShapeDtypeStruct((t, d_in), BF16),
        compiler_params=pltpu.CompilerParams(vmem_limit_bytes=VMEM_LIMIT_BYTES),
        name="ssd_sample_c",
    )(ydiag, yoff.reshape(nseq, nj * d_in), eacs_e, z, p["ssd_norm_w"])
    return yn, nconv, nssm


def _s5_param_kernel(are_ref, aim_ref, ldt_ref, bre_ref, bim_ref, cre_ref, cim_ref,
                     ab_ref, wb_ref, cret_ref, cimt_ref):
    n = are_ref.shape[1]
    w = wb_ref.shape[0]
    a_re = are_ref[...]
    a_im = aim_ref[...]
    dt = jnp.exp(ldt_ref[...])
    mag = jnp.exp(dt * a_re)
    ab_re = mag * jnp.cos(dt * a_im)
    ab_im = mag * jnp.sin(dt * a_im)
    den = a_re * a_re + a_im * a_im
    f_re = ((ab_re - 1.0) * a_re + ab_im * a_im) / den
    f_im = (ab_im * a_re - (ab_re - 1.0) * a_im) / den
    ab_ref[0:1, :] = ab_re
    ab_ref[1:2, :] = ab_im
    b_re = bre_ref[...]
    b_im = bim_ref[...]
    row_g = lax.shift_right_logical(lax.broadcasted_iota(jnp.int32, (w, n), 0), S5_GROUP_CH.bit_length() - 1)
    col_g = lax.shift_right_logical(lax.broadcasted_iota(jnp.int32, (w, n), 1), S5_STATE.bit_length() - 1)
    same = row_g == col_g

    def spread(rows):
        return jnp.where(same, jnp.concatenate([rows] * S5_GROUPS, axis=0), 0.0).astype(BF16)

    wb_ref[:, 0:n] = spread(f_re * b_re - f_im * b_im)
    wb_ref[:, n:2 * n] = spread(f_re * b_im + f_im * b_re)
    cret_ref[...] = spread(cre_ref[...])
    cimt_ref[...] = spread(cim_ref[...])


def _s5_params(a_re, a_im, ldt, bt_re, bt_im, ct_re, ct_im):
    n = a_re.shape[1]
    w = S5_GROUPS * S5_GROUP_CH
    return pl.pallas_call(
        _s5_param_kernel,
        out_shape=[jax.ShapeDtypeStruct((2, n), F32),
                   jax.ShapeDtypeStruct((w, 2 * n), BF16),
                   jax.ShapeDtypeStruct((w, n), BF16),
                   jax.ShapeDtypeStruct((w, n), BF16)],
        compiler_params=pltpu.CompilerParams(vmem_limit_bytes=VMEM_LIMIT_BYTES),
        name="s5_params",
    )(a_re, a_im, ldt, bt_re, bt_im, ct_re, ct_im)


def _s5_kernel(u_ref, h0re_ref, h0im_ref, ab_ref, wb_ref, cret_ref, cimt_ref, d_ref, wglu_ref,
               yb_ref, nre_ref, nim_ref, xh_ref, st_ref, *, rows_per_step, steps, batch_major):
    i = pl.program_id(0)
    n = ab_ref.shape[1]
    w = wb_ref.shape[0]
    r = rows_per_step

    @pl.when(i == 0)
    def _():
        st_ref[0] = h0re_ref[...]
        st_ref[1] = h0im_ref[...]

    if batch_major:
        u = pltpu.einshape("btd->(tb)d", u_ref[...])
    else:
        u = u_ref[...]
    ub = u.astype(BF16)
    for j in range(2 * n // MXU_TILE):
        c0 = (j * MXU_TILE) % n
        k0 = (c0 // S5_STATE * S5_GROUP_CH) // LANES * LANES
        cols = slice(j * MXU_TILE, (j + 1) * MXU_TILE)
        xh_ref[:, cols] = jnp.dot(ub[:, k0:k0 + LANES], wb_ref[k0:k0 + LANES, cols], preferred_element_type=F32)
    for c0 in range(0, n, S5_SCAN_LANES):
        re = slice(c0, c0 + S5_SCAN_LANES)
        im = slice(n + c0, n + c0 + S5_SCAN_LANES)
        a_re = jnp.broadcast_to(ab_ref[0:1, re], (r, S5_SCAN_LANES))
        a_im = jnp.broadcast_to(ab_ref[1:2, re], (r, S5_SCAN_LANES))

        def step(t, carry, re=re, im=im, a_re=a_re, a_im=a_im):
            h_re, h_im = carry
            rows = pl.ds(pl.multiple_of(t * r, r), r)
            n_re = a_re * h_re - a_im * h_im + xh_ref[rows, re]
            n_im = a_re * h_im + a_im * h_re + xh_ref[rows, im]
            xh_ref[rows, re] = n_re
            xh_ref[rows, im] = n_im
            return n_re, n_im

        carry = (st_ref[0, :, re], st_ref[1, :, re])
        if steps <= 8:
            for t in range(steps):
                carry = step(t, carry)
        else:
            carry = lax.fori_loop(0, steps, step, carry, unroll=4)
        st_ref[0, :, re] = carry[0]
        st_ref[1, :, re] = carry[1]

    n_ct = w // MXU_TILE
    ys = []
    for j in range(n_ct):
        rows = slice(j * MXU_TILE, (j + 1) * MXU_TILE)
        ks = slice(j * (n // n_ct), (j + 1) * (n // n_ct))
        ks_im = slice(n + j * (n // n_ct), n + (j + 1) * (n // n_ct))
        ys.append(_bdot_nt(xh_ref[:, ks], cret_ref[rows, ks]) - _bdot_nt(xh_ref[:, ks_im], cimt_ref[rows, ks]))
    y = jnp.concatenate(ys, axis=1) + d_ref[...] * u
    pre = _bdot(_gelu_tanh(y), wglu_ref[...])
    half = pre.shape[1] // 2
    yb = pre[:, :half] * _sigmoid(pre[:, half:])
    if batch_major:
        yb_ref[...] = pltpu.einshape("(tb)d->btd", yb, b=r)
    else:
        yb_ref[...] = yb

    @pl.when(i == pl.num_programs(0) - 1)
    def _():
        nre_ref[...] = st_ref[0]
        nim_ref[...] = st_ref[1]


def _s5(u, h0_re, h0_im, sp, rows_per_step, steps, batch_major):
    n = sp["ab"].shape[1]
    w = u.shape[-1]
    tm = rows_per_step * steps
    d_model = sp["w_glu"].shape[1] // 2
    if batch_major:
        nb, seq, _ = u.shape
        grid = (seq // steps,)
        u_spec = pl.BlockSpec((nb, steps, w), lambda i: (0, i, 0))
        yb_spec = pl.BlockSpec((nb, steps, d_model), lambda i: (0, i, 0))
        yb_shape = (nb, seq, d_model)
    else:
        grid = (u.shape[0] // tm,)
        u_spec = pl.BlockSpec((tm, w), lambda i: (i, 0))
        yb_spec = pl.BlockSpec((tm, d_model), lambda i: (i, 0))
        yb_shape = (u.shape[0], d_model)
    return pl.pallas_call(
        functools.partial(_s5_kernel, rows_per_step=rows_per_step, steps=steps, batch_major=batch_major),
        grid=grid,
        in_specs=[u_spec,
                  _const_spec(h0_re.shape), _const_spec(h0_im.shape), _const_spec(sp["ab"].shape),
                  _const_spec(sp["wb"].shape), _const_spec(sp["c_re_t"].shape), _const_spec(sp["c_im_t"].shape),
                  _const_spec((1, w)), _const_spec(sp["w_glu"].shape)],
        out_specs=[yb_spec, _const_spec(h0_re.shape), _const_spec(h0_im.shape)],
        out_shape=[jax.ShapeDtypeStruct(yb_shape, F32),
                   jax.ShapeDtypeStruct(h0_re.shape, F32),
                   jax.ShapeDtypeStruct(h0_im.shape, F32)],
        scratch_shapes=[pltpu.VMEM((tm, 2 * n), F32),
                        pltpu.VMEM((2, rows_per_step, n), F32)],
        compiler_params=_cparams(("arbitrary",)),
        name="s5_scan",
    )(u, h0_re, h0_im, sp["ab"], sp["wb"], sp["c_re_t"], sp["c_im_t"], sp["d"], sp["w_glu"])


def _merge_router_steps(i, row_refs, mod_ref, weight_refs, cnt_ref, run_ref, *, per_token, rows_per_seq):
    @pl.when(i == 0)
    def _():
        run_ref[...] = jnp.zeros(run_ref.shape, F32)

    lead = row_refs[3].shape[0]
    rows_each = 1 if rows_per_seq is None else rows_per_seq
    sub = lead // ROUTER_SUBTILES if (lead * rows_each) % (ROUTER_SUBTILES * LANES) == 0 else lead
    for r0 in range(0, lead, sub):
        rows = pl.ds(r0, sub)
        if per_token:
            mod = mod_ref.at[:, rows]
        elif rows_per_seq is not None:
            mod = mod_ref.at[rows]
        else:
            mod = mod_ref
        _route_rows(*[ref.at[rows] for ref in row_refs], mod, *weight_refs, run_ref,
                    per_token=per_token, rows_per_seq=rows_per_seq)

    @pl.when(i == pl.num_programs(0) - 1)
    def _():
        cnt_ref[...] = run_ref[...]


def _merge_router_kernel(yn_ref, yb_ref, gate_ref, x_ref, mod_ref, woa_ref, wo_ref, nw_ref, wr_ref, br_ref,
                         x1_ref, h2_ref, meta_ref, cnt_ref, run_ref, *, per_token):
    _merge_router_steps(pl.program_id(0), (yn_ref, yb_ref, gate_ref, x_ref, x1_ref, h2_ref, meta_ref), mod_ref,
                        (woa_ref, wo_ref, nw_ref, wr_ref, br_ref), cnt_ref, run_ref,
                        per_token=per_token, rows_per_seq=None)


def _rows2d(ref):
    v = ref[...]
    return v.reshape(-1, v.shape[-1]) if v.ndim == 3 else v


def _route_rows(yn_ref, yb_ref, gate_ref, x_ref, x1_ref, h2_ref, meta_ref, mod_ref,
                woa_ref, wo_ref, nw_ref, wr_ref, br_ref, run_ref, *, per_token, rows_per_seq):
    x = _rows2d(x_ref)
    tm, d = x.shape
    ya = jnp.dot(_rows2d(yn_ref), woa_ref[...], preferred_element_type=F32)
    gate = _sigmoid(_rows2d(gate_ref))
    merged = gate[:, :d] * ya + gate[:, d:] * _rows2d(yb_ref)
    mix = _bdot(merged, wo_ref[...])
    g1 = _mod_rows(mod_ref, 2, per_token, rows_per_seq)
    sh2 = _mod_rows(mod_ref, 3, per_token, rows_per_seq)
    sc2 = _mod_rows(mod_ref, 4, per_token, rows_per_seq)
    x1 = x + g1 * mix
    x1_ref[...] = x1.reshape(x1_ref.shape)
    h2 = _rms(x1) * nw_ref[...] * (1.0 + sc2) + sh2
    h2_ref[...] = h2.reshape(tm, d // LANES, LANES).reshape(h2_ref.shape)

    wr = wr_ref[...]
    wr_hi = wr.astype(BF16)
    wr_lo = (wr - wr_hi.astype(F32)).astype(BF16)
    h2_hi = h2.astype(BF16)
    h2_lo = (h2 - h2_hi.astype(F32)).astype(BF16)
    d = functools.partial(jnp.dot, preferred_element_type=F32)
    logits = d(h2_hi, wr_hi) + d(h2_lo, wr_hi) + d(h2_hi, wr_lo)
    lt = logits.T
    erow = lax.broadcasted_iota(jnp.int32, lt.shape, 0)
    lt = jnp.where(erow < N_EXPERTS, lt, -jnp.inf)
    ex = jnp.exp(lt - jnp.max(lt, axis=0, keepdims=True))
    scores = ex / jnp.sum(ex, axis=0, keepdims=True)
    sel = scores + br_ref[...]
    s = [sel[e:e + 1, :] for e in range(N_EXPERTS)]
    p = [scores[e:e + 1, :] for e in range(N_EXPERTS)]

    def group_top2_sum(vals):
        best = None
        for a in range(len(vals)):
            for b in range(a + 1, len(vals)):
                pair = vals[a] + vals[b]
                best = pair if best is None else jnp.maximum(best, pair)
        return best

    gs = [group_top2_sum(s[EXPERTS_PER_GROUP * g:EXPERTS_PER_GROUP * (g + 1)]) for g in range(N_EXPERT_GROUPS)]
    best = gs[0]
    bg = jnp.zeros(best.shape, jnp.int32)
    for g in range(1, N_EXPERT_GROUPS):
        better = gs[g] > best
        bg = jnp.where(better, g, bg)
        best = jnp.where(better, gs[g], best)

    def pick(rows, j):
        out = rows[j]
        for g in range(1, N_EXPERT_GROUPS):
            out = jnp.where(bg == g, rows[EXPERTS_PER_GROUP * g + j], out)
        return out

    cs = [pick(s, j) for j in range(EXPERTS_PER_GROUP)]
    cp = [pick(p, j) for j in range(EXPERTS_PER_GROUP)]

    def argmax_first(vals, skip):
        bv = None
        bi = None
        bw = None
        for j in range(len(vals)):
            v = vals[j] if skip is None else jnp.where(skip == j, -jnp.inf, vals[j])
            if bv is None:
                bv, bi, bw = v, jnp.zeros(v.shape, jnp.int32), cp[j]
            else:
                better = v > bv
                bi = jnp.where(better, j, bi)
                bw = jnp.where(better, cp[j], bw)
                bv = jnp.where(better, v, bv)
        return bi, bw

    i1, w1 = argmax_first(cs, None)
    i2, w2 = argmax_first(cs, i1)
    wsum = w1 + w2
    e1 = bg * EXPERTS_PER_GROUP + i1
    e2 = bg * EXPERTS_PER_GROUP + i2
    oh1 = jnp.where(erow == e1, 1.0, 0.0)
    oh2 = jnp.where(erow == e2, 1.0, 0.0)
    both = oh1 + oh2
    ta = lax.broadcasted_iota(jnp.int32, (tm, tm), 0)
    tb = lax.broadcasted_iota(jnp.int32, (tm, tm), 1)
    earlier = jnp.where(ta < tb, 1.0, 0.0).astype(BF16)
    before = jnp.dot(both.astype(BF16), earlier, preferred_element_type=F32) + run_ref[...]
    r1 = jnp.sum(oh1 * before, axis=0, keepdims=True)
    r2 = jnp.sum(oh2 * before, axis=0, keepdims=True)
    run_ref[...] = run_ref[...] + jnp.sum(both, axis=1, keepdims=True)

    record = [w1 / wsum, w2 / wsum, e1.astype(F32), e2.astype(F32), r1, r2]
    wrow = lax.broadcasted_iota(jnp.int32, lt.shape, 0)
    wmat = jnp.zeros(lt.shape, F32)
    for k, v in enumerate(record):
        wmat = jnp.where(wrow == k, v, wmat)
    meta_ref[...] = wmat.T.reshape(meta_ref.shape)


def _merge_router(yn, yb, gate, x, mod, lw, rw, *, tm, per_token, mod_spec):
    t, d = x.shape
    row = lambda n: pl.BlockSpec((tm, n), lambda i: (i, 0))
    return pl.pallas_call(
        functools.partial(_merge_router_kernel, per_token=per_token),
        grid=(t // tm,),
        in_specs=[row(d), row(d), row(2 * d), row(d), mod_spec,
                  _const_spec((d, d)), _const_spec((d, d)), _const_spec((1, d)),
                  _const_spec((d, LANES)), _const_spec((LANES, 1))],
        out_specs=[row(d), _row_tiles(tm, d), row(LANES), _const_spec((LANES, 1))],
        out_shape=[jax.ShapeDtypeStruct((t, d), F32),
                   jax.ShapeDtypeStruct((t, d // LANES, LANES), F32),
                   jax.ShapeDtypeStruct((t, LANES), F32),
                   jax.ShapeDtypeStruct((LANES, 1), F32)],
        scratch_shapes=[pltpu.VMEM((LANES, 1), F32)],
        compiler_params=_cparams(("arbitrary",)),
        name="merge_router",
    )(yn, yb, gate, x, mod, lw["w_out_a"], lw["w_out"], lw["norm2_w"], rw["w_router"], rw["b_router"])


def _s5_merge_kernel(u_ref, h0re_ref, h0im_ref, ab_ref, wb_ref, cret_ref, cimt_ref, d_ref, wglu_ref,
                     yn_ref, gate_ref, x_ref, mod_ref, woa_ref, wo_ref, nw_ref, wr_ref, br_ref,
                     nre_ref, nim_ref, x1_ref, h2_ref, meta_ref, cnt_ref,
                     xh_ref, st_ref, yb_ref, run_ref, *, steps):
    _s5_kernel(u_ref, h0re_ref, h0im_ref, ab_ref, wb_ref, cret_ref, cimt_ref, d_ref, wglu_ref,
               yb_ref, nre_ref, nim_ref, xh_ref, st_ref, rows_per_step=u_ref.shape[0], steps=steps, batch_major=True)
    _merge_router_steps(pl.program_id(0), (yn_ref, yb_ref, gate_ref, x_ref, x1_ref, h2_ref, meta_ref), mod_ref,
                        (woa_ref, wo_ref, nw_ref, wr_ref, br_ref), cnt_ref, run_ref,
                        per_token=False, rows_per_seq=steps)


def _resident_spec(shape):
    nd = len(shape)
    return pl.BlockSpec(shape, lambda *_: (0,) * nd, pipeline_mode=pl.Buffered(1))


def _s5_merge(u, yn, gate, x, mod, zero_state, sp, lw, rw, steps):
    nb, seq, w = u.shape
    d = x.shape[-1]
    n = sp["ab"].shape[1]
    blk = lambda *tail: pl.BlockSpec((nb, steps) + tail, lambda i: (0, i) + (0,) * len(tail))
    weights = [sp["ab"], sp["wb"], sp["c_re_t"], sp["c_im_t"], sp["d"], sp["w_glu"]]
    merge_w = [lw["w_out_a"], lw["w_out"], lw["norm2_w"], rw["w_router"], rw["b_router"]]
    return pl.pallas_call(
        functools.partial(_s5_merge_kernel, steps=steps),
        grid=(seq // steps,),
        in_specs=[blk(w), _resident_spec(zero_state.shape), _resident_spec(zero_state.shape)]
                 + [_resident_spec(a.shape) for a in weights]
                 + [blk(d), blk(2 * d), blk(d), _resident_spec(mod.shape)]
                 + [_resident_spec(a.shape) for a in merge_w],
        out_specs=[_const_spec(zero_state.shape), _const_spec(zero_state.shape),
                   blk(d), blk(d // LANES, LANES), blk(LANES), _const_spec((LANES, 1))],
        out_shape=[jax.ShapeDtypeStruct(zero_state.shape, F32),
                   jax.ShapeDtypeStruct(zero_state.shape, F32),
                   jax.ShapeDtypeStruct((nb, seq, d), F32),
                   jax.ShapeDtypeStruct((nb, seq, d // LANES, LANES), F32),
                   jax.ShapeDtypeStruct((nb, seq, LANES), F32),
                   jax.ShapeDtypeStruct((LANES, 1), F32)],
        scratch_shapes=[pltpu.VMEM((nb * steps, 2 * n), F32), pltpu.VMEM((2, nb, n), F32),
                        pltpu.VMEM((nb, steps, d), F32), pltpu.VMEM((LANES, 1), F32)],
        compiler_params=_cparams(("arbitrary",)),
        name="s5_merge_router",
    )(u, zero_state, zero_state, *weights, yn, gate, x, mod, *merge_w)


def _meta_rows_kernel(meta_ref, idx_ref, rank_ref):
    m = meta_ref[...].T
    idx_ref[...] = m[2:4, :].astype(jnp.int32)
    rank_ref[...] = m[4:6, :].astype(jnp.int32)


def _meta_rows(meta):
    t = meta.shape[0]
    tm = min(t, META_TILE)
    pair = pl.BlockSpec((2, tm), lambda i: (0, i))
    return pl.pallas_call(
        _meta_rows_kernel,
        grid=(t // tm,),
        in_specs=[pl.BlockSpec((tm, LANES), lambda i: (i, 0))],
        out_specs=[pair, pair],
        out_shape=[jax.ShapeDtypeStruct((2, t), jnp.int32), jax.ShapeDtypeStruct((2, t), jnp.int32)],
        compiler_params=_cparams(("arbitrary",)),
        name="moe_meta_rows",
    )(meta)


def _plan_kernel(cnta_ref, cntb_ref, idxa_ref, ranka_ref, idxb_ref, rankb_ref,
                 desta_ref, destb_ref, te_ref, ends_ref, nv_ref, *, tile, n_tiles):
    shift = tile.bit_length() - 1
    run = jnp.int32(0)
    starts = []
    for e in range(N_EXPERTS):
        starts.append(run)
        run = run + (((cnta_ref[e] + cntb_ref[e] + (tile - 1)) >> shift) << shift)
        ends_ref[e] = run
    n_valid = run >> shift
    nv_ref[0] = n_valid
    idx_a = idxa_ref[...]
    idx_b = idxb_ref[...]
    dest_a = ranka_ref[...]
    dest_b = rankb_ref[...]
    for e in range(N_EXPERTS):
        dest_a = dest_a + jnp.where(idx_a == e, starts[e], 0)
        dest_b = dest_b + jnp.where(idx_b == e, starts[e] + cnta_ref[e], 0)
    desta_ref[...] = dest_a
    destb_ref[...] = dest_b

    def tile_owner(j, carry):
        pos = jnp.minimum(j, n_valid - 1) * tile
        owner = jnp.int32(0)
        for e in range(N_EXPERTS - 1):
            owner = owner + jnp.where(ends_ref[e] <= pos, 1, 0)
        te_ref[j] = owner
        return carry

    lax.fori_loop(0, n_tiles, tile_owner, 0)


def _plan(cnt_a, cnt_b, idx_a, rank_a, idx_b, rank_b, tile, n_tiles):
    smem = pl.BlockSpec(memory_space=pltpu.SMEM)
    vmem = pl.BlockSpec(memory_space=pltpu.VMEM)
    return pl.pallas_call(
        functools.partial(_plan_kernel, tile=tile, n_tiles=n_tiles),
        in_specs=[smem, smem, vmem, vmem, vmem, vmem],
        out_specs=[vmem, vmem, smem, smem, smem],
        out_shape=[jax.ShapeDtypeStruct(idx_a.shape, jnp.int32),
                   jax.ShapeDtypeStruct(idx_b.shape, jnp.int32),
                   jax.ShapeDtypeStruct((n_tiles,), jnp.int32),
                   jax.ShapeDtypeStruct((N_EXPERTS,), jnp.int32),
                   jax.ShapeDtypeStruct((1,), jnp.int32)],
        name="moe_plan",
    )(cnt_a, cnt_b, idx_a, rank_a, idx_b, rank_b)


def _row_tiles(tm, d):
    return pl.BlockSpec((tm, d // LANES, LANES), lambda i: (i, 0, 0))


def _row_copy(src_ref, src_row, dst_ref, dst_row, sem):
    return pltpu.make_async_copy(src_ref.at[src_row], dst_ref.at[dst_row], sem)


def _each_row(tm, fn):
    def body(t, carry):
        for k in range(2):
            fn(t, k)
        return carry

    lax.fori_loop(0, tm, body, 0, unroll=8)


def _dispatch_kernel(ends_ref, dest_ref, prev_dest_ref, ha_ref, hb_ref, out_ref, zero_ref, stage_ref, sem, zsem,
                     *, tile, steps_a):
    i = pl.program_id(0)
    tm = ha_ref.shape[0]
    slot = i % 2

    @pl.when(i == 0)
    def _():
        zero_ref[...] = jnp.zeros(zero_ref.shape, F32)

        def clear(start):
            return pltpu.make_async_copy(zero_ref, out_ref.at[pl.ds(pl.multiple_of(start, tile), tile)], zsem)

        def used(e):
            return ends_ref[e] > (ends_ref[e - 1] if e else 0)

        def unused_tiles(fn):
            n_valid = ends_ref[N_EXPERTS - 1] >> (tile.bit_length() - 1)
            lax.fori_loop(n_valid, out_ref.shape[0] // tile, lambda j, c: (fn(clear(j * tile)), c)[1], 0)

        for e in range(N_EXPERTS):
            pl.when(used(e))(lambda e=e: clear(ends_ref[e] - tile).start())
        unused_tiles(lambda cp: cp.start())
        for e in range(N_EXPERTS):
            pl.when(used(e))(lambda e=e: clear(ends_ref[e] - tile).wait())
        unused_tiles(lambda cp: cp.wait())

    @pl.when(i < steps_a)
    def _():
        stage_ref[slot] = ha_ref[...]

    @pl.when(i >= steps_a)
    def _():
        stage_ref[slot] = hb_ref[...]

    _each_row(tm, lambda t, k: _row_copy(stage_ref.at[slot], t, out_ref, dest_ref[k * tm + t], sem.at[slot]).start(priority=k))

    @pl.when(i > 0)
    def _():
        _each_row(tm, lambda t, k: _row_copy(stage_ref.at[1 - slot], t, out_ref, prev_dest_ref[k * tm + t],
                                             sem.at[1 - slot]).wait())

    @pl.when(i == pl.num_programs(0) - 1)
    def _():
        _each_row(tm, lambda t, k: _row_copy(stage_ref.at[slot], t, out_ref, dest_ref[k * tm + t], sem.at[slot]).wait())


def _dispatch(ends, dest, h2_a, h2_b, n_rows, tm, tile):
    _, nt, lanes = h2_a.shape
    steps_a = h2_a.shape[0] // tm
    steps_b = h2_b.shape[0] // tm
    idx_spec = lambda fn: pl.BlockSpec((2 * tm,), fn, memory_space=pltpu.SMEM)
    rows = lambda fn: pl.BlockSpec((tm, nt, lanes), fn)
    return pl.pallas_call(
        functools.partial(_dispatch_kernel, tile=tile, steps_a=steps_a),
        grid=(steps_a + steps_b,),
        in_specs=[pl.BlockSpec(memory_space=pltpu.SMEM),
                  idx_spec(lambda i: (i,)), idx_spec(lambda i: (jnp.maximum(i - 1, 0),)),
                  rows(lambda i: (jnp.minimum(i, steps_a - 1), 0, 0)),
                  rows(lambda i: (jnp.maximum(i - steps_a, 0), 0, 0))],
        out_specs=pl.BlockSpec(memory_space=pl.ANY),
        out_shape=jax.ShapeDtypeStruct((n_rows, nt, lanes), F32),
        scratch_shapes=[pltpu.VMEM((tile, nt, lanes), F32), pltpu.VMEM((2, tm, nt, lanes), F32),
                        pltpu.SemaphoreType.DMA((2,)), pltpu.SemaphoreType.DMA(())],
        compiler_params=_cparams(("arbitrary",)),
        name="moe_dispatch",
    )(ends, dest, dest, h2_a, h2_b)


def _expert_kernel(te_ref, nv_ref, x_ref, w1_ref, w3_ref, w2_ref, y_ref, w1b_ref, w3b_ref, w2b_ref):
    j = pl.program_id(0)
    fresh = jnp.logical_or(j == 0, te_ref[j] != te_ref[jnp.maximum(j - 1, 0)])

    @pl.when(fresh)
    def _():
        w1b_ref[...] = w1_ref[...].astype(BF16)
        w3b_ref[...] = w3_ref[...].astype(BF16)
        w2b_ref[...] = w2_ref[...].astype(BF16)

    @pl.when(j < nv_ref[0])
    def _():
        tm, nt, lanes = x_ref.shape
        xb = x_ref[...].reshape(tm, nt * lanes).astype(BF16)
        a = jnp.dot(xb, w1b_ref[...], preferred_element_type=F32)
        b = jnp.dot(xb, w3b_ref[...], preferred_element_type=F32)
        y_ref[...] = _bdot(_silu(a) * b, w2b_ref[...]).reshape(tm, nt, lanes)

    @pl.when(j >= nv_ref[0])
    def _():
        y_ref[...] = jnp.zeros(y_ref.shape, F32)


def _experts(tile_expert, n_valid, xs, w1, w3, w2, layer, tm):
    n_rows, nt, lanes = xs.shape
    d = nt * lanes
    de = w1.shape[3]
    wspec = lambda r, c: pl.BlockSpec((None, None, r, c), lambda j, te, nv: (layer, te[j], 0, 0))
    grid_spec = pltpu.PrefetchScalarGridSpec(
        num_scalar_prefetch=2,
        grid=(n_rows // tm,),
        in_specs=[pl.BlockSpec((tm, nt, lanes), lambda j, te, nv: (jnp.minimum(j, nv[0] - 1), 0, 0)),
                  wspec(d, de), wspec(d, de), wspec(de, d)],
        out_specs=pl.BlockSpec((tm, nt, lanes), lambda j, te, nv: (j, 0, 0)),
        scratch_shapes=[pltpu.VMEM((d, de), BF16), pltpu.VMEM((d, de), BF16), pltpu.VMEM((de, d), BF16)],
    )
    return pl.pallas_call(
        _expert_kernel,
        grid_spec=grid_spec,
        out_shape=jax.ShapeDtypeStruct((n_rows, nt, lanes), F32),
        compiler_params=_cparams(("arbitrary",)),
        name="moe_experts",
    )(tile_expert, n_valid, xs, w1, w3, w2)


def _combine_kernel(dest_ref, next_dest_ref, ys_ref, x1_ref, wts_ref, mod_ref, fw_ref, out_ref, buf_ref, sem,
                    *, per_token, final_norm):
    i = pl.program_id(0)
    tm, d = x1_ref.shape
    slot = i % 2

    def gather(idx_ref, s):
        return lambda t, k: _row_copy(ys_ref, idx_ref[k * tm + t], buf_ref.at[s, k], t, sem.at[s])

    @pl.when(i == 0)
    def _():
        _each_row(tm, lambda t, k: gather(dest_ref, slot)(t, k).start(priority=k))

    @pl.when(i + 1 < pl.num_programs(0))
    def _():
        _each_row(tm, lambda t, k: gather(next_dest_ref, 1 - slot)(t, k).start(priority=k))

    _each_row(tm, lambda t, k: gather(dest_ref, slot)(t, k).wait())
    w = wts_ref[...]
    rows = lambda k: buf_ref[slot, k].reshape(tm, d)
    moe = w[:, 0:1] * rows(0) + w[:, 1:2] * rows(1)
    x2 = x1_ref[...] + _mod_rows(mod_ref, 5, per_token) * moe
    if final_norm:
        x2 = _rms(x2) * fw_ref[...]
    out_ref[...] = x2


def _combine(dest, ys, x1, wts, mod, fw, *, tm, per_token, mod_spec, final_norm):
    t, d = x1.shape
    row = lambda n: pl.BlockSpec((tm, n), lambda i: (i, 0))
    n_steps = t // tm
    idx_spec = lambda fn: pl.BlockSpec((2 * tm,), fn, memory_space=pltpu.SMEM)
    return pl.pallas_call(
        functools.partial(_combine_kernel, per_token=per_token, final_norm=final_norm),
        grid=(n_steps,),
        in_specs=[idx_spec(lambda i: (i,)), idx_spec(lambda i: (jnp.minimum(i + 1, n_steps - 1),)),
                  pl.BlockSpec(memory_space=pl.ANY),
                  row(d), row(LANES), mod_spec, _const_spec((1, d))],
        out_specs=row(d),
        out_shape=jax.ShapeDtypeStruct((t, d), F32),
        scratch_shapes=[pltpu.VMEM((2, 2, tm) + ys.shape[1:], F32), pltpu.SemaphoreType.DMA((2,))],
        compiler_params=_cparams(("arbitrary",)),
        name="moe_combine",
    )(dest, dest, ys, x1, wts, mod, fw)


def _step_table(dest, tm):
    return jnp.transpose(dest.reshape(2, dest.shape[1] // tm, tm), (1, 0, 2)).reshape(-1)


def _moe(ga, gb, fw, ew, layer, final_norm):
    tm = EXPERT_TILE
    t_all = ga["h2"].shape[0] + gb["h2"].shape[0]
    n_rows = ((2 * t_all + N_EXPERTS * (tm - 1)) // tm + 1) * tm
    cnt = lambda g: g["counts"][:N_EXPERTS, 0].astype(jnp.int32)
    dest_a, dest_b, tile_expert, ends, n_valid = _plan(cnt(ga), cnt(gb), ga["idx"], ga["rank"], gb["idx"], gb["rank"],
                                                       tm, n_rows // tm)
    dispatch_table = jnp.concatenate([_step_table(dest_a, MOE_TILE), _step_table(dest_b, MOE_TILE)])
    xs = _dispatch(ends, dispatch_table, ga["h2"], gb["h2"], n_rows, MOE_TILE, tm)
    ys = _experts(tile_expert, n_valid, xs, ew["w1"], ew["w3"], ew["w2"], layer, tm)
    outs = []
    for g, dest in ((ga, dest_a), (gb, dest_b)):
        tok = g["tok_tile"]
        outs.append(_combine(_step_table(dest, tok), ys, g["x1"], g["meta"], g["mod"], fw, tm=tok,
                             per_token=g["per_token"], mod_spec=g["mod_spec_fn"](tok), final_norm=final_norm))
    return outs


def _pad_lanes(a, n=LANES):
    return jnp.pad(a, [(0, 0)] * (a.ndim - 1) + [(0, n - a.shape[-1])])


def kernel(x_prompt, x_sample, state_ssm, state_conv, state_s5_re, state_s5_im, c_prompt, c_sample, w_in, conv_w, conv_b, dt_bias, a_log, d_ssd, ssd_norm_w, w_out_a, s5_a_re, s5_a_im, s5_log_dt, s5_b_re, s5_b_im, s5_c_re, s5_c_im, s5_d, w_glu, w_out, norm1_w, norm2_w, w_ada, b_ada, w_router, b_router, w1, w3, w2, final_norm_w):
    nb, seq, d = x_prompt.shape
    ns, steps, _ = x_sample.shape
    depth = w_in.shape[0]
    d_in = SSD_HEADS * SSD_HEAD_DIM
    cdim = conv_w.shape[2]
    s5w = S5_GROUPS * S5_GROUP_CH
    s5n = S5_GROUPS * S5_STATE
    k1 = SSD_CONV - 1
    tp = nb * seq
    ts = ns * steps

    mod = _ada_mod(jnp.concatenate([c_prompt, c_sample], axis=0), w_ada, b_ada)
    head_expand = jnp.tile(jnp.repeat(jnp.eye(LANES, SSD_HEADS, dtype=BF16), SSD_HEAD_DIM, axis=1), (3, 1))
    rw = {"w_router": _pad_lanes(w_router), "b_router": _pad_lanes(b_router[None, :]).reshape(LANES, 1)}

    xp = x_prompt.reshape(tp, d)
    xsm = jnp.transpose(x_sample, (1, 0, 2)).reshape(ts, d)
    outs = {k: [] for k in ("ssm_p", "conv_p", "conv_s", "re_p", "re_s", "im_p", "im_s")}
    ssm_s = None
    zero_state = jnp.zeros((nb, s5n), F32)

    for l in range(depth):
        o0 = 0
        o1 = d_in
        o2 = o1 + cdim
        o3 = o2 + SSD_HEADS
        o4 = o3 + s5w
        wl = w_in[l].astype(BF16)
        ws = (wl[:, o0:o1], wl[:, o1:o2], _pad_lanes(wl[:, o2:o3]), wl[:, o3:o4], wl[:, o4:])
        p = {"conv_w": conv_w[l], "conv_b": conv_b[l][None, :], "dt_bias": _pad_lanes(dt_bias[l][None, :]),
             "a_log": _pad_lanes(a_log[l][None, :]),
             "d_skip_e": jnp.repeat(d_ssd[l], SSD_HEAD_DIM)[None, :],
             "ssd_norm_w": ssd_norm_w[l][None, :], "head_expand": head_expand}
        chan_rows = lambda a, perm: jnp.transpose(a, perm).reshape(S5_GROUP_CH, s5n)
        ab, wb, c_re_t, c_im_t = _s5_params(
            s5_a_re[l].reshape(1, s5n), s5_a_im[l].reshape(1, s5n), jnp.repeat(s5_log_dt[l], S5_STATE)[None, :],
            chan_rows(s5_b_re[l], (2, 0, 1)), chan_rows(s5_b_im[l], (2, 0, 1)),
            chan_rows(s5_c_re[l], (1, 0, 2)), chan_rows(s5_c_im[l], (1, 0, 2)))
        sp = {"ab": ab, "wb": wb, "c_re_t": c_re_t, "c_im_t": c_im_t,
              "d": s5_d[l][None, :], "w_glu": w_glu[l].astype(BF16)}
        lw = {"w_out_a": w_out_a[l].astype(BF16), "w_out": w_out[l].astype(BF16), "norm2_w": norm2_w[l][None, :]}
        ew = {"w1": w1, "w3": w3, "w2": w2}
        final = l == depth - 1
        fw = final_norm_w[None, :]

        mod_p = mod[l, :nb].reshape(nb, 6, d)
        mod_p_spec = lambda tm: pl.BlockSpec((None, 6, d), lambda i: (i // (seq // tm), 0, 0))
        u, gate, yn, nconv, nssm = _inproj_ssd(xp, mod_p, norm1_w[l][None, :], ws, p, nb, seq, ROW_TILE)
        seq3 = lambda a: a.reshape((nb, seq) + a.shape[1:])
        nre, nim, x1, h2, meta, counts = _s5_merge(seq3(u), seq3(yn), seq3(gate), seq3(xp), mod_p, zero_state,
                                                   sp, lw, rw, S5_STEPS)
        meta = meta.reshape(tp, LANES)
        idx, rank = _meta_rows(meta)
        routed_p = {"h2": h2.reshape(tp, d // LANES, LANES), "idx": idx, "rank": rank, "counts": counts,
                    "x1": x1.reshape(tp, d), "meta": meta, "mod": mod_p, "tok_tile": MOE_TILE, "per_token": False,
                    "mod_spec_fn": mod_p_spec}
        outs["ssm_p"].append(nssm)
        outs["conv_p"].append(nconv)
        outs["re_p"].append(nre.reshape(nb, S5_GROUPS, S5_STATE))
        outs["im_p"].append(nim.reshape(nb, S5_GROUPS, S5_STATE))

        mod_s = jnp.transpose(mod[l, nb:].reshape(ns, 6, d), (1, 0, 2))
        mod_s_spec = lambda tm: pl.BlockSpec((6, tm, d), lambda i: (0, i % (ns // tm), 0))
        z, xbc, dtr, u, gate = _inproj(
            xsm, mod_s, norm1_w[l][None, :], ws, tm=ns, per_token=True, mod_spec=mod_s_spec(ns))
        conv0_tm = jnp.transpose(state_conv[l], (1, 0, 2)).reshape(k1 * ns, cdim)
        yn, nconv_tm, ssm_s = _ssd_sample(xbc, dtr, z, conv0_tm, state_ssm, ssm_s, l, p, ns, steps)
        yb, nre, nim = _s5(u, state_s5_re[l].reshape(ns, s5n), state_s5_im[l].reshape(ns, s5n), sp, ns, steps,
                           False)
        x1, h2, meta, counts = _merge_router(
            yn, yb, gate, xsm, mod_s, lw, rw, tm=ns, per_token=True, mod_spec=mod_s_spec(ns))
        idx, rank = _meta_rows(meta)
        routed_s = {"h2": h2, "idx": idx, "rank": rank, "counts": counts, "x1": x1, "meta": meta, "mod": mod_s,
                    "tok_tile": ns, "per_token": True, "mod_spec_fn": mod_s_spec}
        xp, xsm = _moe(routed_p, routed_s, fw, ew, l, final)
        outs["conv_s"].append(jnp.transpose(nconv_tm.reshape(k1, ns, cdim), (1, 0, 2)))
        outs["re_s"].append(nre.reshape(ns, S5_GROUPS, S5_STATE))
        outs["im_s"].append(nim.reshape(ns, S5_GROUPS, S5_STATE))

    y_prompt = xp.reshape(nb, seq, d)
    y_sample = jnp.transpose(xsm.reshape(steps, ns, d), (1, 0, 2))
    st = lambda k: jnp.stack(outs[k])
    return (y_prompt, y_sample, st("ssm_p"), ssm_s, st("conv_p"), st("conv_s"),
            st("re_p"), st("re_s"), st("im_p"), st("im_s"))
```
